```python
import jax, jax.numpy as jnp
from jax import lax
import numpy as np

D_MODEL = 1024
BATCH = 16
SEQ = 256
DEPTH = 1
DEC_BATCH = 2
DEC_SEQ = 1024
PAST_LEN = 512

GRID_W = 64
A_HEAD_DIM = 64
A_WIDTH = D_MODEL // 2
A_HEADS = A_WIDTH // A_HEAD_DIM
A_DECAY_RANK = 64
A_ICLR_RANK = 64
A_GATE_RANK = 128
B_HEAD_DIM = 128
B_WIDTH = D_MODEL // 2
B_HEADS = B_WIDTH // B_HEAD_DIM
B_CHUNK = 64
CONV_K = 3
N_GROUPS = 4
EXPERTS_PER_GROUP = 4
N_EXPERTS = N_GROUPS * EXPERTS_PER_GROUP
EXPERT_TOP_K = 2
D_EXPERT = D_MODEL // 4
IN_SPLITS = (A_WIDTH, A_WIDTH + B_WIDTH, A_WIDTH + 2 * B_WIDTH, A_WIDTH + 2 * B_WIDTH + D_MODEL)
IN_COLS = A_WIDTH + 2 * B_WIDTH + 2 * D_MODEL
NORM_EPS = 1e-6
RWKV_GN_EPS = 64e-5
MLSTM_GN_EPS = 1e-5

kernel_name = 'bidir_rwkv7_mlstm_hmoe_diffusion_step'


def rmsnorm(x, g):
    xf = x.astype(jnp.float32)
    return (xf * lax.rsqrt(jnp.mean(xf * xf, axis=-1, keepdims=True) + NORM_EPS)).astype(x.dtype) * g


def head_norm(y, eps):
    mu = jnp.mean(y, axis=-1, keepdims=True)
    var = jnp.mean(jnp.square(y - mu), axis=-1, keepdims=True)
    return (y - mu) * lax.rsqrt(var + eps)


def depthwise_conv(u, kernel, rows):
    Bn, T, C = u.shape
    img = u.reshape(Bn, rows, T // rows, C)
    out = lax.conv_general_dilated(img, kernel[:, :, None, :], window_strides=(1, 1), padding='SAME',
                                   dimension_numbers=('NHWC', 'HWIO', 'NHWC'), feature_group_count=C)
    return out.reshape(Bn, T, C)


def rwkv7_scan(r, decay, k, v, aa, bb, S0, reverse):
    def step(S, inp):
        r_t, w_t, k_t, v_t, a_t, b_t = inp
        sa = jnp.einsum('bhij,bhj->bhi', S, a_t)
        S = S * w_t[:, :, None, :] + sa[..., None] * b_t[:, :, None, :] + v_t[..., None] * k_t[:, :, None, :]
        return S, jnp.einsum('bhij,bhj->bhi', S, r_t)
    xs = tuple(jnp.moveaxis(t, 1, 0) for t in (r, decay, k, v, aa, bb))
    S, ys = lax.scan(step, S0, xs, reverse=reverse)
    return jnp.moveaxis(ys, 0, 1), S


def rwkv7_mix(xa, S0, p):
    Bn, T, _ = xa.shape
    hs = (Bn, T, A_HEADS, A_HEAD_DIM)
    xpad = jnp.pad(xa, ((0, 0), (1, 1), (0, 0)))
    xx = 0.5 * (xpad[:, :-2] + xpad[:, 2:]) - xa
    xr, xw, xk, xv, xi, xg = (xa + xx * p['rwkv_mu'][j] for j in range(6))
    r = (xr @ p['rwkv_w_r']).reshape(hs).astype(jnp.float32)
    k = (xk @ p['rwkv_w_k']).astype(jnp.float32)
    v = (xv @ p['rwkv_w_v']).reshape(hs).astype(jnp.float32)
    g = (jax.nn.sigmoid(xg @ p['rwkv_g1']) @ p['rwkv_g2']).astype(jnp.float32)
    kk = (k * p['rwkv_k_k']).reshape(hs)
    kk = kk / jnp.maximum(jnp.sqrt(jnp.sum(kk * kk, axis=-1, keepdims=True)), 1e-12)
    y_sum, bonus, states = 0.0, 0.0, []
    for d in range(2):
        w = -jax.nn.softplus(-(p['rwkv_w0'][d] + jnp.tanh(xw @ p['rwkv_w1'][d]) @ p['rwkv_w2'][d]).astype(jnp.float32)) - 0.5
        decay = jnp.exp(-jnp.exp(w)).reshape(hs)
        a = jax.nn.sigmoid((p['rwkv_a0'][d] + (xi @ p['rwkv_a1'][d]) @ p['rwkv_a2'][d]).astype(jnp.float32))
        kd = (k * (1.0 + (a - 1.0) * p['rwkv_k_a'])).reshape(hs)
        y, S = rwkv7_scan(r, decay, kd, v, -kk, kk * a.reshape(hs), S0[:, d], reverse=(d == 1))
        y_sum = y_sum + y
        bonus = bonus + jnp.sum(r * kd * p['rwkv_r_k'], axis=-1, keepdims=True) * v
        states.append(S)
    yn = head_norm(y_sum, RWKV_GN_EPS).reshape(Bn, T, A_WIDTH) * p['rwkv_ln_g'] + p['rwkv_ln_b']
    y = (yn + bonus.reshape(Bn, T, A_WIDTH)) * g
    return y.astype(xa.dtype), jnp.stack(states, axis=1)


def mlstm_chunkwise(q, k, v, log_i, log_f, C0, n0, m0):
    Bn, H, T, d = q.shape
    nc = T // B_CHUNK

    def chunks(a):
        return jnp.moveaxis(a.reshape(Bn, H, nc, B_CHUNK, *a.shape[3:]), 2, 0)

    causal = jnp.tril(jnp.ones((B_CHUNK, B_CHUNK), dtype=bool))

    def step(carry, inp):
        C, n, m = carry
        qc, kc, vc, ic, fc = inp
        b = jnp.cumsum(fc, axis=-1)
        logw = jnp.where(causal, b[..., :, None] - b[..., None, :] + ic[..., None, :], -jnp.inf)
        m_t = jnp.maximum(b + m[..., None], jnp.max(logw, axis=-1))
        inter = jnp.exp(b + m[..., None] - m_t)
        s = jnp.einsum('bhtk,bhsk->bhts', qc, kc) * jnp.exp(logw - m_t[..., None])
        num = inter[..., None] * jnp.einsum('bhtk,bhkv->bhtv', qc, C) + jnp.einsum('bhts,bhsv->bhtv', s, vc)
        den = inter * jnp.einsum('bhtk,bhk->bht', qc, n) + jnp.sum(s, axis=-1)
        h = num / jnp.maximum(jnp.abs(den), jnp.exp(-m_t))[..., None]
        m_new = m_t[..., -1]
        carry_decay = jnp.exp(b[..., -1] + m - m_new)
        kw = kc * jnp.exp(b[..., -1:] - b + ic - m_new[..., None])[..., None]
        C = carry_decay[..., None, None] * C + jnp.einsum('bhsk,bhsv->bhkv', kw, vc)
        n = carry_decay[..., None] * n + jnp.sum(kw, axis=2)
        return (C, n, m_new), h

    (C, n, m), hs = lax.scan(step, (C0, n0, m0), tuple(chunks(a) for a in (q, k, v, log_i, log_f)))
    return jnp.moveaxis(hs, 0, 2).reshape(Bn, H, T, d), C, n, m


def mlstm_mix(u, o_pre, C0, n0, m0, rows, p):
    Bn, T, _ = u.shape
    if rows is None:
        uc = depthwise_conv(u, p['mlstm_conv'][CONV_K // 2:CONV_K // 2 + 1], 1)
    else:
        uc = depthwise_conv(u, p['mlstm_conv'], rows)
    uc = jax.nn.silu(uc)
    uch = uc.reshape(Bn, T, B_HEADS, B_HEAD_DIM)
    q = jnp.einsum('bthd,hde->bhte', uch, p['mlstm_w_q']).astype(jnp.float32)
    k = jnp.einsum('bthd,hde->bhte', uch, p['mlstm_w_k']).astype(jnp.float32) * (B_HEAD_DIM ** -0.5)
    v = jnp.einsum('bthd,hde->bhte', u.reshape(Bn, T, B_HEADS, B_HEAD_DIM), p['mlstm_w_v']).astype(jnp.float32)
    h_sum, Cs, ns, ms = 0.0, [], [], []
    for d in range(2):
        log_i = (jnp.einsum('btc,ch->bht', uc, p['mlstm_w_i'][d]) + p['mlstm_b_i'][d][:, None]).astype(jnp.float32)
        log_f = jax.nn.log_sigmoid((jnp.einsum('btc,ch->bht', uc, p['mlstm_w_f'][d]) + p['mlstm_b_f'][d][:, None]).astype(jnp.float32))
        seq = (q, k, v, log_i, log_f)
        if d == 1:
            seq = tuple(jnp.flip(a, axis=2) for a in seq)
        h, C, n, m = mlstm_chunkwise(*seq, C0[:, d], n0[:, d], m0[:, d])
        if d == 1:
            h = jnp.flip(h, axis=2)
        h_sum = h_sum + h
        Cs.append(C); ns.append(n); ms.append(m)
    hn = jnp.moveaxis(head_norm(h_sum, MLSTM_GN_EPS), 1, 2).reshape(Bn, T, B_WIDTH).astype(u.dtype) * p['mlstm_ln_g']
    y = jax.nn.sigmoid(o_pre) * hn
    return y, jnp.stack(Cs, axis=1), jnp.stack(ns, axis=1), jnp.stack(ms, axis=1)


def hier_moe(h, p):
    Bn, T, D = h.shape
    t = h.reshape(Bn * T, D)
    g_logits = (t @ p['moe_w_group']).astype(jnp.float32) + p['moe_b_group']
    g_prob = jax.nn.softmax(g_logits, axis=-1)
    g_sel = jnp.argmax(g_logits, axis=-1)
    g_w = jnp.take_along_axis(g_prob, g_sel[:, None], axis=-1)
    e_logits = ((t @ p['moe_w_expert']).astype(jnp.float32) + p['moe_b_expert']).reshape(-1, N_GROUPS, EXPERTS_PER_GROUP)
    e_logits = jnp.take_along_axis(e_logits, g_sel[:, None, None], axis=1)[:, 0]
    top_p, top_i = lax.top_k(jax.nn.softmax(e_logits, axis=-1), EXPERT_TOP_K)
    top_w = g_w * top_p / jnp.sum(top_p, axis=-1, keepdims=True)
    expert_id = g_sel[:, None] * EXPERTS_PER_GROUP + top_i
    combine = jnp.sum(jax.nn.one_hot(expert_id, N_EXPERTS, dtype=jnp.float32) * top_w[..., None], axis=1)
    a = jnp.einsum('nd,edf->nef', t, p['moe_w1'])
    b = jnp.einsum('nd,edf->nef', t, p['moe_w3'])
    act = jax.nn.silu(a) * b * combine[..., None].astype(t.dtype)
    return jnp.einsum('nef,efd->nd', act, p['moe_w2']).reshape(Bn, T, D)


def trunk_layer(x, mod, s_rwkv, s_C, s_n, s_m, rows, p):
    shift1, scale1, gate1, shift2, scale2, gate2 = jnp.split(mod[:, None, :], 6, axis=-1)
    h = rmsnorm(x, p['norm1_g']) * (1 + scale1) + shift1
    xa, ub, o_pre, ga, gb = jnp.split(h @ p['w_in'], IN_SPLITS, axis=-1)
    ya, s_rwkv = rwkv7_mix(xa, s_rwkv, p)
    yb, s_C, s_n, s_m = mlstm_mix(ub, o_pre, s_C, s_n, s_m, rows, p)
    merged = jax.nn.sigmoid(ga) * (ya @ p['rwkv_w_proj']) + jax.nn.sigmoid(gb) * (yb @ p['mlstm_w_proj'])
    x = x + gate1 * (merged @ p['w_out'])
    h = rmsnorm(x, p['norm2_g']) * (1 + scale2) + shift2
    x = x + gate2 * hier_moe(h, p)
    return x, s_rwkv, s_C, s_n, s_m


def setup_inputs(seed: int = 0) -> dict:
    key = jax.random.key(seed)
    ks = iter(jax.random.split(key, 64))
    f32 = jnp.float32

    def nrm(shape, scale):
        return jax.random.normal(next(ks), shape, f32) * scale

    def uni(shape, lo, hi):
        return jax.random.uniform(next(ks), shape, f32, lo, hi)

    L, D = DEPTH, D_MODEL
    return {
        'x_prompt': nrm((BATCH, SEQ, D), 1.0),
        'x_sample': nrm((DEC_BATCH, DEC_SEQ, D), 1.0),
        'c': nrm((DEC_BATCH, D), 1.0),
        'c_ctx': nrm((D,), 1.0),
        'state_rwkv': nrm((DEC_BATCH, L, 2, A_HEADS, A_HEAD_DIM, A_HEAD_DIM), 0.1),
        'state_mlstm_C': nrm((DEC_BATCH, L, 2, B_HEADS, B_HEAD_DIM, B_HEAD_DIM), 0.1),
        'state_mlstm_n': nrm((DEC_BATCH, L, 2, B_HEADS, B_HEAD_DIM), 0.5),
        'state_mlstm_m': uni((DEC_BATCH, L, 2, B_HEADS), 0.0, 4.0),
        'w_mod': nrm((L, D, 6 * D), 0.3 * D ** -0.5),
        'b_mod': nrm((L, 6 * D), 0.01),
        'norm1_g': 1.0 + nrm((L, D), 0.01),
        'norm2_g': 1.0 + nrm((L, D), 0.01),
        'w_in': nrm((L, D, IN_COLS), D ** -0.5),
        'rwkv_mu': uni((L, 6, A_WIDTH), 0.0, 1.0),
        'rwkv_w_r': nrm((L, A_WIDTH, A_WIDTH), A_WIDTH ** -0.5),
        'rwkv_w_k': nrm((L, A_WIDTH, A_WIDTH), A_WIDTH ** -0.5),
        'rwkv_w_v': nrm((L, A_WIDTH, A_WIDTH), A_WIDTH ** -0.5),
        'rwkv_w0': uni((L, 2, A_WIDTH), -4.0, 0.0),
        'rwkv_w1': nrm((L, 2, A_WIDTH, A_DECAY_RANK), A_WIDTH ** -0.5),
        'rwkv_w2': nrm((L, 2, A_DECAY_RANK, A_WIDTH), 0.5 * A_DECAY_RANK ** -0.5),
        'rwkv_a0': nrm((L, 2, A_WIDTH), 0.5),
        'rwkv_a1': nrm((L, 2, A_WIDTH, A_ICLR_RANK), A_WIDTH ** -0.5),
        'rwkv_a2': nrm((L, 2, A_ICLR_RANK, A_WIDTH), 0.5 * A_ICLR_RANK ** -0.5),
        'rwkv_g1': nrm((L, A_WIDTH, A_GATE_RANK), A_WIDTH ** -0.5),
        'rwkv_g2': nrm((L, A_GATE_RANK, A_WIDTH), A_GATE_RANK ** -0.5),
        'rwkv_k_k': 1.0 + nrm((L, A_WIDTH), 0.1),
        'rwkv_k_a': 1.0 + nrm((L, A_WIDTH), 0.1),
        'rwkv_r_k': nrm((L, A_HEADS, A_HEAD_DIM), 0.1),
        'rwkv_ln_g': 1.0 + nrm((L, A_WIDTH), 0.01),
        'rwkv_ln_b': nrm((L, A_WIDTH), 0.01),
        'rwkv_w_proj': nrm((L, A_WIDTH, D), A_WIDTH ** -0.5),
        'mlstm_conv': nrm((L, CONV_K, CONV_K, B_WIDTH), 1.0 / CONV_K),
        'mlstm_w_q': nrm((L, B_HEADS, B_HEAD_DIM, B_HEAD_DIM), B_HEAD_DIM ** -0.5),
        'mlstm_w_k': nrm((L, B_HEADS, B_HEAD_DIM, B_HEAD_DIM), B_HEAD_DIM ** -0.5),
        'mlstm_w_v': nrm((L, B_HEADS, B_HEAD_DIM, B_HEAD_DIM), B_HEAD_DIM ** -0.5),
        'mlstm_w_i': nrm((L, 2, B_WIDTH, B_HEADS), 0.1 * B_WIDTH ** -0.5),
        'mlstm_b_i': nrm((L, 2, B_HEADS), 0.1),
        'mlstm_w_f': nrm((L, 2, B_WIDTH, B_HEADS), 0.1 * B_WIDTH ** -0.5),
        'mlstm_b_f': uni((L, 2, B_HEADS), 3.0, 6.0),
        'mlstm_ln_g': 1.0 + nrm((L, B_WIDTH), 0.01),
        'mlstm_w_proj': nrm((L, B_WIDTH, D), B_WIDTH ** -0.5),
        'w_out': nrm((L, D, D), D ** -0.5),
        'moe_w_group': nrm((L, D, N_GROUPS), D ** -0.5),
        'moe_b_group': nrm((L, N_GROUPS), 0.01),
        'moe_w_expert': nrm((L, D, N_EXPERTS), D ** -0.5),
        'moe_b_expert': nrm((L, N_EXPERTS), 0.01),
        'moe_w1': nrm((L, N_EXPERTS, D, D_EXPERT), D ** -0.5),
        'moe_w3': nrm((L, N_EXPERTS, D, D_EXPERT), D ** -0.5),
        'moe_w2': nrm((L, N_EXPERTS, D_EXPERT, D), D_EXPERT ** -0.5),
        'final_norm_g': 1.0 + nrm((D,), 0.01),
    }


def reference(x_prompt, x_sample, c, c_ctx, state_rwkv, state_mlstm_C, state_mlstm_n, state_mlstm_m,
              w_mod, b_mod, norm1_g, norm2_g, w_in,
              rwkv_mu, rwkv_w_r, rwkv_w_k, rwkv_w_v, rwkv_w0, rwkv_w1, rwkv_w2, rwkv_a0, rwkv_a1, rwkv_a2,
              rwkv_g1, rwkv_g2, rwkv_k_k, rwkv_k_a, rwkv_r_k, rwkv_ln_g, rwkv_ln_b, rwkv_w_proj,
              mlstm_conv, mlstm_w_q, mlstm_w_k, mlstm_w_v, mlstm_w_i, mlstm_b_i, mlstm_w_f, mlstm_b_f,
              mlstm_ln_g, mlstm_w_proj, w_out,
              moe_w_group, moe_b_group, moe_w_expert, moe_b_expert, moe_w1, moe_w3, moe_w2,
              final_norm_g):
    f32 = jnp.float32
    rows = x_sample.shape[1] // GRID_W
    Bp = x_prompt.shape[0]
    zero_rwkv = jnp.zeros((Bp, 2, A_HEADS, A_HEAD_DIM, A_HEAD_DIM), f32)
    zero_C = jnp.zeros((Bp, 2, B_HEADS, B_HEAD_DIM, B_HEAD_DIM), f32)
    zero_n = jnp.zeros((Bp, 2, B_HEADS, B_HEAD_DIM), f32)
    zero_m = jnp.zeros((Bp, 2, B_HEADS), f32)
    new_rwkv, new_C, new_n, new_m = [], [], [], []
    xp, xs = x_prompt, x_sample
    for l in range(DEPTH):
        p = {
            'norm1_g': norm1_g[l], 'norm2_g': norm2_g[l], 'w_in': w_in[l],
            'rwkv_mu': rwkv_mu[l], 'rwkv_w_r': rwkv_w_r[l], 'rwkv_w_k': rwkv_w_k[l], 'rwkv_w_v': rwkv_w_v[l],
            'rwkv_w0': rwkv_w0[l], 'rwkv_w1': rwkv_w1[l], 'rwkv_w2': rwkv_w2[l],
            'rwkv_a0': rwkv_a0[l], 'rwkv_a1': rwkv_a1[l], 'rwkv_a2': rwkv_a2[l],
            'rwkv_g1': rwkv_g1[l], 'rwkv_g2': rwkv_g2[l], 'rwkv_k_k': rwkv_k_k[l], 'rwkv_k_a': rwkv_k_a[l],
            'rwkv_r_k': rwkv_r_k[l], 'rwkv_ln_g': rwkv_ln_g[l], 'rwkv_ln_b': rwkv_ln_b[l],
            'rwkv_w_proj': rwkv_w_proj[l],
            'mlstm_conv': mlstm_conv[l], 'mlstm_w_q': mlstm_w_q[l], 'mlstm_w_k': mlstm_w_k[l],
            'mlstm_w_v': mlstm_w_v[l], 'mlstm_w_i': mlstm_w_i[l], 'mlstm_b_i': mlstm_b_i[l],
            'mlstm_w_f': mlstm_w_f[l], 'mlstm_b_f': mlstm_b_f[l], 'mlstm_ln_g': mlstm_ln_g[l],
            'mlstm_w_proj': mlstm_w_proj[l], 'w_out': w_out[l],
            'moe_w_group': moe_w_group[l], 'moe_b_group': moe_b_group[l],
            'moe_w_expert': moe_w_expert[l], 'moe_b_expert': moe_b_expert[l],
            'moe_w1': moe_w1[l], 'moe_w3': moe_w3[l], 'moe_w2': moe_w2[l],
        }
        mod_ctx = jax.nn.silu(c_ctx)[None, :] @ w_mod[l] + b_mod[l]
        mod_lat = jax.nn.silu(c) @ w_mod[l] + b_mod[l]
        xp, s_r, s_C, s_n, s_m = trunk_layer(xp, mod_ctx, zero_rwkv, zero_C, zero_n, zero_m, None, p)
        new_rwkv.append(s_r); new_C.append(s_C); new_n.append(s_n); new_m.append(s_m)
        xs, _, _, _, _ = trunk_layer(xs, mod_lat, state_rwkv[:, l].astype(f32), state_mlstm_C[:, l].astype(f32),
                                     state_mlstm_n[:, l].astype(f32), state_mlstm_m[:, l].astype(f32), rows, p)
    y_prompt = rmsnorm(xp, final_norm_g)
    y_sample = rmsnorm(xs, final_norm_g)
    dt = x_prompt.dtype
    return (y_prompt, y_sample, jnp.stack(new_rwkv, axis=1).astype(dt), jnp.stack(new_C, axis=1).astype(dt),
            jnp.stack(new_n, axis=1).astype(dt), jnp.stack(new_m, axis=1).astype(dt))
```

```python
import functools

import jax
import jax.numpy as jnp
from jax import lax
from jax.experimental import pallas as pl
from jax.experimental.pallas import tpu as pltpu

F32 = jnp.float32
BF16 = jnp.bfloat16

D_MODEL = 1024
GRID_W = 64
A_HEAD_DIM = 64
A_WIDTH = 512
A_HEADS = 8
B_HEAD_DIM = 128
B_WIDTH = 512
B_HEADS = 4
CHUNK = 64
ROW_TILE = 256
N_GROUPS = 4
EXPERTS_PER_GROUP = 4
N_EXPERTS = 16
D_EXPERT = 256
NORM_EPS = 1e-6
RWKV_GN_EPS = 64e-5
MLSTM_GN_EPS = 1e-5
LANES = 128
CONV_PAD = 72
VMEM_LIMIT = 56 * 1024 * 1024


def _mm(a, b):
    return jnp.dot(a.astype(BF16), b.astype(BF16), preferred_element_type=F32)


def _mmh(a, b):
    return jnp.dot(a, b, precision=lax.Precision.HIGHEST, preferred_element_type=F32)


def _mmh_nt(a, b):
    return lax.dot_general(a, b, (((1,), (1,)), ((), ())), precision=lax.Precision.HIGHEST,
                           preferred_element_type=F32)


def _mmh_tn(a, b):
    return lax.dot_general(a, b, (((0,), (0,)), ((), ())), precision=lax.Precision.HIGHEST,
                           preferred_element_type=F32)


def _mm_nt(a, b):
    return lax.dot_general(a.astype(BF16), b.astype(BF16), (((1,), (1,)), ((), ())),
                           preferred_element_type=F32)


def _mm_tn(a, b):
    return lax.dot_general(a.astype(BF16), b.astype(BF16), (((0,), (0,)), ((), ())),
                           preferred_element_type=F32)


def _sigmoid(x):
    return 1.0 / (1.0 + jnp.exp(-x))


def _silu(x):
    return x * _sigmoid(x)


def _softplus(x):
    return jnp.maximum(x, 0.0) + jnp.log(1.0 + jnp.exp(-jnp.abs(x)))


def _iota(shape, dim):
    return lax.broadcasted_iota(jnp.int32, shape, dim)


def _tri(reverse, inclusive, n=CHUNK):
    row, col = _iota((n, n), 0), _iota((n, n), 1)
    if reverse:
        return (col >= row) if inclusive else (col > row)
    return (col <= row) if inclusive else (col < row)


def _same_block(n, blk):
    sh = blk.bit_length() - 1
    row, col = _iota((n, n), 0), _iota((n, n), 1)
    return lax.shift_right_logical(row, sh) == lax.shift_right_logical(col, sh)


def _rmsnorm(x, g):
    return x * lax.rsqrt(jnp.mean(x * x, axis=-1, keepdims=True) + NORM_EPS) * g


def _cparams(sem):
    return pltpu.CompilerParams(dimension_semantics=sem, vmem_limit_bytes=VMEM_LIMIT)


def _full(shape):
    nd = len(shape)
    return pl.BlockSpec(shape, lambda *_: (0,) * nd)


def _mod_kernel(c_ref, w_ref, b_ref, o_ref):
    o_ref[...] = _mm(_silu(c_ref[...]), w_ref[...]) + b_ref[...]


def _mod(cvec, w_mod, b_mod):
    rows, d = cvec.shape
    n = w_mod.shape[1]
    tn = 1536
    return pl.pallas_call(
        _mod_kernel,
        grid=(n // tn,),
        in_specs=[_full((rows, d)), pl.BlockSpec((d, tn), lambda j: (0, j)),
                  pl.BlockSpec((1, tn), lambda j: (0, j))],
        out_specs=pl.BlockSpec((rows, tn), lambda j: (0, j)),
        out_shape=jax.ShapeDtypeStruct((rows, n), F32),
        compiler_params=_cparams(("arbitrary",)),
        name="mod",
    )(cvec, w_mod, b_mod)


def _in_kernel(x_ref, mod_ref, g_ref, w_ref, xa_ref, ub_ref, op_ref, ga_ref, gb_ref):
    d = D_MODEL
    shift, scale = mod_ref[:, 0:d], mod_ref[:, d:2 * d]
    h = (_rmsnorm(x_ref[...], g_ref[...]) * (1.0 + scale) + shift).astype(BF16)
    col = 0
    for ref in (xa_ref, ub_ref, op_ref, ga_ref, gb_ref):
        n = ref.shape[-1]
        ref[...] = jnp.dot(h, w_ref[:, col:col + n], preferred_element_type=F32)
        col += n


def _in_proj(x, mod, g1, w_in, tm):
    b, t, d = x.shape
    per = t // tm
    widths = (A_WIDTH, B_WIDTH, B_WIDTH, D_MODEL, D_MODEL)
    tok = lambda n: pl.BlockSpec((None, tm, n), lambda i: (i // per, i % per, 0))
    return pl.pallas_call(
        _in_kernel,
        grid=(b * per,),
        in_specs=[tok(d), pl.BlockSpec((None, 1, 6 * d), lambda i: (i // per, 0, 0)),
                  _full((1, d)), _full(w_in.shape)],
        out_specs=[tok(n) for n in widths],
        out_shape=[jax.ShapeDtypeStruct((b, t, n), F32) for n in widths],
        compiler_params=_cparams(("arbitrary",)),
        name="in_proj",
    )(x, mod, g1, w_in)


def _unit_lower_inverse(n_mat):
    c = CHUNK
    eye = (_iota((c, c), 0) == _iota((c, c), 1)).astype(F32)
    nd = jnp.where(_same_block(c, 8), n_mat, 0.0)
    n2 = _mmh(nd, nd)
    n4 = _mmh(n2, n2)
    t = eye + nd
    t = t + _mmh(t, n2)
    t = t + _mmh(t, n4)
    blk = 8
    while blk < c:
        off = jnp.where(_same_block(c, 2 * blk) & jnp.logical_not(_same_block(c, blk)), n_mat, 0.0)
        t = t + _mmh(t, _mmh(off, t))
        blk *= 2
    return t


def _rwkv_kernel(x_ref, s0_ref, mu_ref, wr_ref, wk_ref, wv_ref, w0_ref, w1_ref, w2_ref,
                 a0_ref, a1_ref, a2_ref, g1_ref, g2_ref, kk_ref, ka_ref, rk_ref, lng_ref, lnb_ref,
                 y_ref, s_ref,
                 xx_s, r_s, v_s, kk_s, lw_s, kd_s, bb_s, g_s, bonus_s, y_s):
    t_len = x_ref.shape[0]
    n_chunks = t_len // CHUNK
    n_tiles = t_len // ROW_TILE
    hd = A_HEAD_DIM
    x = x_ref[...]
    row = _iota(x.shape, 0)
    prev = jnp.where(row == 0, 0.0, pltpu.roll(x, 1, 0))
    nxt = jnp.where(row == t_len - 1, 0.0, pltpu.roll(x, t_len - 1, 0))
    xx_s[...] = 0.5 * (prev + nxt) - x
    head_ones = _same_block(A_WIDTH, hd).astype(F32)

    def project_tile(ti, carry):
        rows = pl.ds(pl.multiple_of(ti * ROW_TILE, ROW_TILE), ROW_TILE)
        x, xx = x_ref[rows, :], xx_s[rows, :]
        xr, xw, xk, xv, xi, xg = (x + xx * mu_ref[j:j + 1, :] for j in range(6))
        r = _mm(xr, wr_ref[...])
        k = _mm(xk, wk_ref[...])
        v = _mm(xv, wv_ref[...])
        g_s[rows, :] = _mm(_sigmoid(_mm(xg, g1_ref[...])), g2_ref[...])
        kk = k * kk_ref[...]
        kk = kk / jnp.maximum(jnp.sqrt(_mmh(kk * kk, head_ones)), 1e-12)
        r_s[rows, :] = r
        v_s[rows, :] = v
        kk_s[rows, :] = kk
        bonus = jnp.zeros_like(x)
        for d in range(2):
            zw = w0_ref[d:d + 1, :] + _mm(jnp.tanh(_mm(xw, w1_ref[d])), w2_ref[d])
            lw_s[d, rows, :] = -jnp.exp(-_softplus(-zw) - 0.5)
            a = _sigmoid(a0_ref[d:d + 1, :] + _mm(_mm(xi, a1_ref[d]), a2_ref[d]))
            kd = k * (1.0 + (a - 1.0) * ka_ref[...])
            kd_s[d, rows, :] = kd
            bb_s[d, rows, :] = kk * a
            bonus = bonus + _mmh(r * kd * rk_ref[...], head_ones) * v
        bonus_s[rows, :] = bonus
        y_s[rows, :] = jnp.zeros_like(x)
        return carry

    lax.fori_loop(0, n_tiles, project_tile, 0)
    s_ref[...] = s0_ref[...]

    for d in range(2):
        reverse = d == 1
        tri_incl = _tri(reverse, True).astype(F32)
        strict, incl = _tri(reverse, False), _tri(reverse, True)
        last = 0 if reverse else CHUNK - 1

        def chunk_body(ci, carry, d=d, reverse=reverse, tri_incl=tri_incl, strict=strict, incl=incl, last=last):
            cpos = (n_chunks - 1 - ci) if reverse else ci
            rows = pl.ds(pl.multiple_of(cpos * CHUNK, CHUNK), CHUNK)
            lw = lw_s[d, rows, :]
            cum = _mmh(tri_incl, lw)
            tot = cum[last:last + 1, :]
            rc, vc, kkc = r_s[rows, :], v_s[rows, :], kk_s[rows, :]
            kdc, bbc = kd_s[d, rows, :], bb_s[d, rows, :]
            e_neg = jnp.exp(-cum)
            e_tail = jnp.exp(tot - cum)
            at = -kkc * jnp.exp(cum - lw)
            rt = rc * jnp.exp(cum)
            bt, kt = bbc * e_neg, kdc * e_neg
            bp, kp = bbc * e_tail, kdc * e_tail
            pc = jnp.exp(tot)
            for h in range(A_HEADS):
                ln = slice(h * hd, (h + 1) * hd)
                s_h = s_ref[d, h]
                lhs = jnp.concatenate([at[:, ln], rt[:, ln]], axis=0)
                sb = _mmh_nt(lhs, bt[:, ln])
                sk = _mmh_nt(lhs, kt[:, ln])
                a_ab = jnp.where(strict, sb[:CHUNK], 0.0)
                a_ak = jnp.where(strict, sk[:CHUNK], 0.0)
                a_rb = jnp.where(incl, sb[CHUNK:], 0.0)
                a_rk = jnp.where(incl, sk[CHUNK:], 0.0)
                from_state = _mmh_nt(lhs, s_h)
                v_h = vc[:, ln]
                u = _mmh(_unit_lower_inverse(a_ab), from_state[:CHUNK] + _mmh(a_ak, v_h))
                y = from_state[CHUNK:] + _mmh(a_rb, u) + _mmh(a_rk, v_h)
                y_s[rows, ln] = y_s[rows, ln] + y
                s_ref[d, h] = s_h * pc[:, ln] + _mmh_tn(u, bp[:, ln]) + _mmh_tn(v_h, kp[:, ln])
            return carry

        lax.fori_loop(0, n_chunks, chunk_body, 0)

    def finish_tile(ti, carry):
        rows = pl.ds(pl.multiple_of(ti * ROW_TILE, ROW_TILE), ROW_TILE)
        ys = y_s[rows, :]
        cen = ys - _mmh(ys, head_ones) * (1.0 / hd)
        var = _mmh(cen * cen, head_ones) * (1.0 / hd)
        yn = cen * lax.rsqrt(var + RWKV_GN_EPS) * lng_ref[...] + lnb_ref[...]
        y_ref[rows, :] = (yn + bonus_s[rows, :]) * g_s[rows, :]
        return carry

    lax.fori_loop(0, n_tiles, finish_tile, 0)


def _rwkv(xa, s0, p):
    b, t, w = xa.shape
    names = ("mu", "wr", "wk", "wv", "w0", "w1", "w2", "a0", "a1", "a2", "g1", "g2",
             "k_k", "k_a", "r_k", "ln_g", "ln_b")
    weights = [p["rwkv_" + n] for n in names]
    seq = pl.BlockSpec((None, t, w), lambda i: (i, 0, 0))
    st = pl.BlockSpec((None, 2, A_HEADS, A_HEAD_DIM, A_HEAD_DIM), lambda i: (i, 0, 0, 0, 0))
    tw = lambda lead=(): pltpu.VMEM(lead + (t, w), F32)
    return pl.pallas_call(
        _rwkv_kernel,
        grid=(b,),
        in_specs=[seq, st] + [_full(x.shape) for x in weights],
        out_specs=[seq, st],
        out_shape=[jax.ShapeDtypeStruct((b, t, w), F32), jax.ShapeDtypeStruct(s0.shape, F32)],
        scratch_shapes=[tw(), tw(), tw(), tw(), tw((2,)), tw((2,)), tw((2,)), tw(), tw(), tw()],
        compiler_params=_cparams(("arbitrary",)),
        name="rwkv",
    )(xa, s0, *weights)


def _mlstm_kernel(u_ref, op_ref, c0_ref, n0_ref, m0_ref, conv_ref, wq_ref, wk_ref, wv_ref,
                  wg_ref, bg_ref, wgt_ref, bgt_ref, lng_ref,
                  y_ref, c_ref, n_ref, m_ref,
                  pad_s, q_s, k_s, v_s, gc_s, gr_s, h_s, *, conv2d):
    t_len = u_ref.shape[0]
    n_chunks = t_len // CHUNK
    hd = B_HEAD_DIM
    u = u_ref[...]
    pad_s[0:CONV_PAD, :] = jnp.zeros((CONV_PAD, B_WIDTH), F32)
    pad_s[CONV_PAD + t_len:CONV_PAD + t_len + CONV_PAD, :] = jnp.zeros((CONV_PAD, B_WIDTH), F32)
    pad_s[CONV_PAD:CONV_PAD + t_len, :] = u
    col = _iota(u.shape, 0) & (GRID_W - 1)
    conv = jnp.zeros_like(u)
    for kh in range(3):
        if not conv2d and kh != 1:
            continue
        for kw in range(3):
            shift = (kh - 1) * GRID_W + (kw - 1)
            term = pad_s[CONV_PAD + shift:CONV_PAD + shift + t_len, :] * conv_ref[kh * 3 + kw:kh * 3 + kw + 1, :]
            if conv2d and kw != 1:
                src = col + (kw - 1)
                term = jnp.where((src >= 0) & (src < GRID_W), term, 0.0)
            conv = conv + term
    uc = _silu(conv)
    for h in range(B_HEADS):
        ln = slice(h * hd, (h + 1) * hd)
        q_s[:, ln] = _mm(uc[:, ln], wq_ref[h])
        k_s[:, ln] = _mm(uc[:, ln], wk_ref[h]) * (hd ** -0.5)
        v_s[:, ln] = _mm(u[:, ln], wv_ref[h])
    gcol = _mm(uc, wg_ref[...]) + bg_ref[...]
    gcol = jnp.where(_iota(gcol.shape, 1) >= 8, -_softplus(-gcol), gcol)
    gc_s[...] = gcol
    grow = _mm_nt(wgt_ref[...], uc) + bgt_ref[:, 0:1]
    grow = jnp.where(_iota(grow.shape, 0) >= 8, -_softplus(-grow), grow)
    for c in range(n_chunks):
        gr_s[c] = grow[:, c * CHUNK:(c + 1) * CHUNK]
    h_s[...] = jnp.zeros_like(u)
    c_ref[...] = c0_ref[...]
    n_ref[...] = n0_ref[...]
    m_ref[...] = m0_ref[...]

    for d in range(2):
        reverse = d == 1
        tri_incl = _tri(reverse, True).astype(F32)
        incl = _tri(reverse, True)
        last = 0 if reverse else CHUNK - 1

        def chunk_body(ci, carry, d=d, reverse=reverse, tri_incl=tri_incl, incl=incl, last=last):
            cpos = (n_chunks - 1 - ci) if reverse else ci
            rows = pl.ds(pl.multiple_of(cpos * CHUNK, CHUNK), CHUNK)
            gc = gc_s[rows, :]
            gr = gr_s[cpos]
            cum_c = _mmh(tri_incl, gc)
            cum_r = _mmh_nt(gr, tri_incl)
            for h in range(B_HEADS):
                ln = slice(h * hd, (h + 1) * hd)
                li, lf = d * 4 + h, 8 + d * 4 + h
                b_col, i_col = cum_c[:, lf:lf + 1], gc[:, li:li + 1]
                b_row, i_row = cum_r[lf:lf + 1, :], gr[li:li + 1, :]
                m_prev = m_ref[d, h:h + 1, 0:1]
                qc, kc, vc = q_s[rows, ln], k_s[rows, ln], v_s[rows, ln]
                logw = jnp.where(incl, b_col - b_row + i_row, -jnp.inf)
                m_t = jnp.maximum(b_col + m_prev, jnp.max(logw, axis=1, keepdims=True))
                inter = jnp.exp(b_col + m_prev - m_t)
                s = _mm_nt(qc, kc) * jnp.exp(logw - m_t)
                c_h = c_ref[d, h]
                n_h = n_ref[d, h:h + 1, :]
                num = inter * _mm(qc, c_h) + _mm(s, vc)
                den = inter * jnp.sum(qc * n_h, axis=1, keepdims=True) + jnp.sum(s, axis=1, keepdims=True)
                h_s[rows, ln] = h_s[rows, ln] + num / jnp.maximum(jnp.abs(den), jnp.exp(-m_t))
                m_new = m_t[last:last + 1, :]
                b_last = b_col[last:last + 1, :]
                decay = jnp.exp(b_last + m_prev - m_new)
                kw = kc * jnp.exp(b_last - b_col + i_col - m_new)
                c_ref[d, h] = decay * c_h + _mm_tn(kw, vc)
                n_ref[d, h:h + 1, :] = decay * n_h + jnp.sum(kw, axis=0, keepdims=True)
                m_ref[d, h:h + 1, :] = jnp.broadcast_to(m_new, (1, LANES))
            return carry

        lax.fori_loop(0, n_chunks, chunk_body, 0)

    for h in range(B_HEADS):
        ln = slice(h * hd, (h + 1) * hd)
        hh = h_s[:, ln]
        cen = hh - jnp.mean(hh, axis=1, keepdims=True)
        var = jnp.mean(cen * cen, axis=1, keepdims=True)
        y_ref[:, ln] = _sigmoid(op_ref[:, ln]) * (cen * lax.rsqrt(var + MLSTM_GN_EPS) * lng_ref[:, ln])


def _mlstm(ub, o_pre, c0, n0, m0, p, conv2d):
    b, t, w = ub.shape
    names = ("conv", "wq", "wk", "wv", "wg", "bg", "wgt", "bgt", "ln_g")
    weights = [p["mlstm_" + n] for n in names]
    seq = pl.BlockSpec((None, t, w), lambda i: (i, 0, 0))
    cst = pl.BlockSpec((None, 2, B_HEADS, B_HEAD_DIM, B_HEAD_DIM), lambda i: (i, 0, 0, 0, 0))
    vst = pl.BlockSpec((None, 2, B_HEADS, LANES), lambda i: (i, 0, 0, 0))
    return pl.pallas_call(
        functools.partial(_mlstm_kernel, conv2d=conv2d),
        grid=(b,),
        in_specs=[seq, seq, cst, vst, vst] + [_full(x.shape) for x in weights],
        out_specs=[seq, cst, vst, vst],
        out_shape=[jax.ShapeDtypeStruct((b, t, w), F32), jax.ShapeDtypeStruct(c0.shape, F32),
                   jax.ShapeDtypeStruct(n0.shape, F32), jax.ShapeDtypeStruct(m0.shape, F32)],
        scratch_shapes=[pltpu.VMEM((t + 2 * CONV_PAD, w), F32), pltpu.VMEM((t, w), F32),
                        pltpu.VMEM((t, w), F32), pltpu.VMEM((t, w), F32), pltpu.VMEM((t, LANES), F32),
                        pltpu.VMEM((t // CHUNK, 16, CHUNK), F32), pltpu.VMEM((t, w), F32)],
        compiler_params=_cparams(("arbitrary",)),
        name="mlstm",
    )(ub, o_pre, c0, n0, m0, *weights)


def _first_lane_of_max(val, lane, valid):
    masked = jnp.where(valid, val, -jnp.inf)
    best = jnp.max(masked, axis=1, keepdims=True)
    idx = jnp.min(jnp.where(valid & (masked == best), lane, LANES), axis=1, keepdims=True)
    return best, idx


def _merge_kernel(x_ref, ya_ref, yb_ref, ga_ref, gb_ref, mod_ref, wpa_ref, wpb_ref, wo_ref, g2_ref,
                  wr_ref, br_ref, x1_ref, comb_ref):
    d = D_MODEL
    gate1 = mod_ref[:, 2 * d:3 * d]
    shift2, scale2 = mod_ref[:, 3 * d:4 * d], mod_ref[:, 4 * d:5 * d]
    merged = (_sigmoid(ga_ref[...]) * _mm(ya_ref[...], wpa_ref[...])
              + _sigmoid(gb_ref[...]) * _mm(yb_ref[...], wpb_ref[...]))
    x1 = x_ref[...] + gate1 * _mm(merged, wo_ref[...])
    x1_ref[...] = x1
    h2 = _rmsnorm(x1, g2_ref[...]) * (1.0 + scale2) + shift2
    logits = _mm(h2, wr_ref[...]) + br_ref[...]
    lane = _iota(logits.shape, 1)
    is_group = lane < N_GROUPS
    g_max, g_sel = _first_lane_of_max(logits, lane, is_group)
    g_w = 1.0 / jnp.sum(jnp.where(is_group, jnp.exp(logits - g_max), 0.0), axis=1, keepdims=True)
    expert = lane - N_GROUPS
    in_group = (expert >= 0) & (expert < N_EXPERTS) & (lax.shift_right_arithmetic(expert, 2) == g_sel)
    e_max = jnp.max(jnp.where(in_group, logits, -jnp.inf), axis=1, keepdims=True)
    e_exp = jnp.where(in_group, jnp.exp(logits - e_max), 0.0)
    prob = e_exp / jnp.sum(e_exp, axis=1, keepdims=True)
    p1, i1 = _first_lane_of_max(prob, lane, in_group)
    p2, i2 = _first_lane_of_max(prob, lane, in_group & (lane != i1))
    denom = p1 + p2
    comb = jnp.where(lane == i1, g_w * p1 / denom, 0.0) + jnp.where(lane == i2, g_w * p2 / denom, 0.0)
    comb_ref[...] = comb


def _merge(x, ya, yb, ga, gb, mod, p, tm):
    b, t, d = x.shape
    per = t // tm
    tok = lambda n: pl.BlockSpec((None, tm, n), lambda i: (i // per, i % per, 0))
    weights = [p["rwkv_w_proj"], p["mlstm_w_proj"], p["w_out"], p["norm2_g"], p["moe_w_router"], p["moe_b_router"]]
    return pl.pallas_call(
        _merge_kernel,
        grid=(b * per,),
        in_specs=[tok(d), tok(A_WIDTH), tok(B_WIDTH), tok(d), tok(d),
                  pl.BlockSpec((None, 1, 6 * d), lambda i: (i // per, 0, 0))] + [_full(w.shape) for w in weights],
        out_specs=[tok(d), tok(LANES)],
        out_shape=[jax.ShapeDtypeStruct((b, t, d), F32), jax.ShapeDtypeStruct((b, t, LANES), F32)],
        compiler_params=_cparams(("arbitrary",)),
        name="merge",
    )(x, ya, yb, ga, gb, mod, *weights)


def _moe_kernel(x1_ref, comb_ref, mod_ref, g2_ref, w1_ref, w3_ref, w2_ref, gf_ref, y_ref, h2_s, acc_s):
    d = D_MODEL
    e = pl.program_id(1)

    @pl.when(e == 0)
    def _():
        shift2, scale2 = mod_ref[:, 3 * d:4 * d], mod_ref[:, 4 * d:5 * d]
        h2_s[...] = (_rmsnorm(x1_ref[...], g2_ref[...]) * (1.0 + scale2) + shift2).astype(BF16)
        acc_s[...] = jnp.zeros_like(acc_s)

    comb = comb_ref[...]
    w_e = jnp.sum(jnp.where(_iota(comb.shape, 1) == e + N_GROUPS, comb, 0.0), axis=1, keepdims=True)
    h2 = h2_s[...]
    a = jnp.dot(h2, w1_ref[...], preferred_element_type=F32)
    b = jnp.dot(h2, w3_ref[...], preferred_element_type=F32)
    acc_s[...] += _mm(_silu(a) * b * w_e, w2_ref[...])

    @pl.when(e == N_EXPERTS - 1)
    def _():
        gate2 = mod_ref[:, 5 * d:6 * d]
        y_ref[...] = _rmsnorm(x1_ref[...] + gate2 * acc_s[...], gf_ref[...])


def _moe(x1, comb, mod, p, tm):
    b, t, d = x1.shape
    per = t // tm
    tok = lambda n: pl.BlockSpec((None, tm, n), lambda i, e: (i // per, i % per, 0))
    return pl.pallas_call(
        _moe_kernel,
        grid=(b * per, N_EXPERTS),
        in_specs=[tok(d), tok(LANES), pl.BlockSpec((None, 1, 6 * d), lambda i, e: (i // per, 0, 0)),
                  _full((1, d)),
                  pl.BlockSpec((None, d, D_EXPERT), lambda i, e: (e, 0, 0)),
                  pl.BlockSpec((None, d, D_EXPERT), lambda i, e: (e, 0, 0)),
                  pl.BlockSpec((None, D_EXPERT, d), lambda i, e: (e, 0, 0)),
                  _full((1, d))],
        out_specs=tok(d),
        out_shape=jax.ShapeDtypeStruct((b, t, d), F32),
        scratch_shapes=[pltpu.VMEM((tm, d), BF16), pltpu.VMEM((tm, d), F32)],
        compiler_params=_cparams(("arbitrary", "arbitrary")),
        name="moe",
    )(x1, comb, mod, p["norm2_g"], p["moe_w1"], p["moe_w3"], p["moe_w2"], p["final_norm_g"])


def _trunk(x, mod, s_rwkv, s_c, s_n, s_m, conv2d, p, tm):
    xa, ub, o_pre, ga, gb = _in_proj(x, mod, p["norm1_g"], p["w_in"], tm)
    ya, s_rwkv = _rwkv(xa, s_rwkv, p)
    m_in = jnp.broadcast_to(s_m[..., None], s_m.shape + (LANES,))
    yb, s_c, s_n, s_m = _mlstm(ub, o_pre, s_c, s_n, m_in, p, conv2d)
    x1, comb = _merge(x, ya, yb, ga, gb, mod, p, tm)
    y = _moe(x1, comb, mod, p, tm)
    return y, s_rwkv, s_c, s_n, s_m[..., 0]


def kernel(x_prompt, x_sample, c, c_ctx, state_rwkv, state_mlstm_C, state_mlstm_n, state_mlstm_m, w_mod, b_mod, norm1_g, norm2_g, w_in, rwkv_mu, rwkv_w_r, rwkv_w_k, rwkv_w_v, rwkv_w0, rwkv_w1, rwkv_w2, rwkv_a0, rwkv_a1, rwkv_a2, rwkv_g1, rwkv_g2, rwkv_k_k, rwkv_k_a, rwkv_r_k, rwkv_ln_g, rwkv_ln_b, rwkv_w_proj, mlstm_conv, mlstm_w_q, mlstm_w_k, mlstm_w_v, mlstm_w_i, mlstm_b_i, mlstm_w_f, mlstm_b_f, mlstm_ln_g, mlstm_w_proj, w_out, moe_w_group, moe_b_group, moe_w_expert, moe_b_expert, moe_w1, moe_w3, moe_w2, final_norm_g):
    assert w_mod.shape[0] == 1, "single trunk layer"
    l = 0
    bp, dec = x_prompt.shape[0], x_sample.shape[0]
    bf = lambda w: w.astype(BF16)
    row = lambda w: w.reshape(1, -1).astype(F32)
    wg = jnp.concatenate([mlstm_w_i[l, 0], mlstm_w_i[l, 1], mlstm_w_f[l, 0], mlstm_w_f[l, 1]], axis=1)
    bg = jnp.concatenate([mlstm_b_i[l, 0], mlstm_b_i[l, 1], mlstm_b_f[l, 0], mlstm_b_f[l, 1]])
    wg = jnp.pad(wg, ((0, 0), (0, LANES - wg.shape[1])))
    bg = jnp.pad(bg, (0, LANES - bg.shape[0]))
    w_router = jnp.pad(jnp.concatenate([moe_w_group[l], moe_w_expert[l]], axis=1),
                       ((0, 0), (0, LANES - N_GROUPS - N_EXPERTS)))
    b_router = jnp.pad(jnp.concatenate([moe_b_group[l], moe_b_expert[l]]), (0, LANES - N_GROUPS - N_EXPERTS))
    p = {
        "norm1_g": row(norm1_g[l]), "norm2_g": row(norm2_g[l]), "w_in": bf(w_in[l]),
        "rwkv_mu": rwkv_mu[l], "rwkv_wr": bf(rwkv_w_r[l]), "rwkv_wk": bf(rwkv_w_k[l]), "rwkv_wv": bf(rwkv_w_v[l]),
        "rwkv_w0": rwkv_w0[l], "rwkv_w1": bf(rwkv_w1[l]), "rwkv_w2": bf(rwkv_w2[l]),
        "rwkv_a0": rwkv_a0[l], "rwkv_a1": bf(rwkv_a1[l]), "rwkv_a2": bf(rwkv_a2[l]),
        "rwkv_g1": bf(rwkv_g1[l]), "rwkv_g2": bf(rwkv_g2[l]),
        "rwkv_k_k": row(rwkv_k_k[l]), "rwkv_k_a": row(rwkv_k_a[l]), "rwkv_r_k": row(rwkv_r_k[l]),
        "rwkv_ln_g": row(rwkv_ln_g[l]), "rwkv_ln_b": row(rwkv_ln_b[l]),
        "rwkv_w_proj": bf(rwkv_w_proj[l]),
        "mlstm_conv": mlstm_conv[l].reshape(9, B_WIDTH),
        "mlstm_wq": bf(mlstm_w_q[l]), "mlstm_wk": bf(mlstm_w_k[l]), "mlstm_wv": bf(mlstm_w_v[l]),
        "mlstm_wg": bf(wg), "mlstm_bg": row(bg),
        "mlstm_wgt": bf(wg[:, :16].T), "mlstm_bgt": jnp.broadcast_to(bg[:16, None], (16, LANES)),
        "mlstm_ln_g": row(mlstm_ln_g[l]), "mlstm_w_proj": bf(mlstm_w_proj[l]), "w_out": bf(w_out[l]),
        "moe_w_router": bf(w_router), "moe_b_router": row(b_router),
        "moe_w1": bf(moe_w1[l]), "moe_w3": bf(moe_w3[l]), "moe_w2": bf(moe_w2[l]),
        "final_norm_g": row(final_norm_g),
    }
    cvec = jnp.concatenate([c_ctx[None, :], c, jnp.zeros((8 - 1 - dec, D_MODEL), F32)], axis=0)
    mod = _mod(cvec, w_mod[l], b_mod[l].reshape(1, -1))
    mod_ctx = jnp.broadcast_to(mod[0:1][:, None, :], (bp, 1, 6 * D_MODEL))
    mod_lat = mod[1:1 + dec][:, None, :]

    zeros = lambda *s: jnp.zeros((bp,) + s, F32)
    yp, n_rwkv, n_c, n_n, n_m = _trunk(
        x_prompt, mod_ctx, zeros(2, A_HEADS, A_HEAD_DIM, A_HEAD_DIM), zeros(2, B_HEADS, B_HEAD_DIM, B_HEAD_DIM),
        zeros(2, B_HEADS, B_HEAD_DIM), zeros(2, B_HEADS), False, p, 256)
    ys, _, _, _, _ = _trunk(
        x_sample, mod_lat, state_rwkv[:, l].astype(F32), state_mlstm_C[:, l].astype(F32),
        state_mlstm_n[:, l].astype(F32), state_mlstm_m[:, l].astype(F32), True, p, 512)
    dt = x_prompt.dtype
    return (yp, ys, n_rwkv[:, None].astype(dt), n_c[:, None].astype(dt), n_n[:, None].astype(dt),
            n_m[:, None].astype(dt))
```

```python
import functools

import jax
import jax.numpy as jnp
from jax import lax
from jax.experimental import pallas as pl
from jax.experimental.pallas import tpu as pltpu

F32 = jnp.float32
BF16 = jnp.bfloat16

D_MODEL = 1024
GRID_W = 64
A_HEAD_DIM = 64
A_WIDTH = 512
A_HEADS = 8
B_HEAD_DIM = 128
B_WIDTH = 512
B_HEADS = 4
CHUNK = 64
ROW_TILE = 256
N_GROUPS = 4
EXPERTS_PER_GROUP = 4
N_EXPERTS = 16
D_EXPERT = 256
NORM_EPS = 1e-6
RWKV_GN_EPS = 64e-5
MLSTM_GN_EPS = 1e-5
LANES = 128
CONV_PAD = 72
VMEM_LIMIT = 56 * 1024 * 1024


def _mm(a, b):
    return jnp.dot(a.astype(BF16), b.astype(BF16), preferred_element_type=F32)


def _mmh(a, b):
    return jnp.dot(a, b, precision=lax.Precision.HIGHEST, preferred_element_type=F32)


def _mmh_nt(a, b):
    return lax.dot_general(a, b, (((1,), (1,)), ((), ())), precision=lax.Precision.HIGHEST,
                           preferred_element_type=F32)


def _mmh_tn(a, b):
    return lax.dot_general(a, b, (((0,), (0,)), ((), ())), precision=lax.Precision.HIGHEST,
                           preferred_element_type=F32)


_NN, _NT, _TN = ((1,), (0,)), ((1,), (1,)), ((0,), (0,))


def _split(a):
    hi = a.astype(BF16)
    return hi, (a - hi.astype(F32)).astype(BF16)


def _mm3(a, b, dims=_NN):
    (ah, al), (bh, bl) = a, b
    dot = lambda p, q: lax.dot_general(p, q, (dims, ((), ())), preferred_element_type=F32)
    return dot(ah, bh) + dot(ah, bl) + dot(al, bh)


def _mm_nt(a, b):
    return lax.dot_general(a.astype(BF16), b.astype(BF16), (((1,), (1,)), ((), ())),
                           preferred_element_type=F32)


def _mm_tn(a, b):
    return lax.dot_general(a.astype(BF16), b.astype(BF16), (((0,), (0,)), ((), ())),
                           preferred_element_type=F32)


def _sigmoid(x):
    return 1.0 / (1.0 + jnp.exp(-x))


def _silu(x):
    return x * _sigmoid(x)


def _softplus(x):
    return jnp.maximum(x, 0.0) + jnp.log(1.0 + jnp.exp(-jnp.abs(x)))


def _iota(shape, dim):
    return lax.broadcasted_iota(jnp.int32, shape, dim)


def _tri(reverse, inclusive, n=CHUNK):
    row, col = _iota((n, n), 0), _iota((n, n), 1)
    if reverse:
        return (col >= row) if inclusive else (col > row)
    return (col <= row) if inclusive else (col < row)


def _same_block(n, blk):
    sh = blk.bit_length() - 1
    row, col = _iota((n, n), 0), _iota((n, n), 1)
    return lax.shift_right_logical(row, sh) == lax.shift_right_logical(col, sh)


def _rmsnorm(x, g):
    return x * lax.rsqrt(jnp.mean(x * x, axis=-1, keepdims=True) + NORM_EPS) * g


def _cparams(sem):
    return pltpu.CompilerParams(dimension_semantics=sem, vmem_limit_bytes=VMEM_LIMIT)


def _full(shape):
    nd = len(shape)
    return pl.BlockSpec(shape, lambda *_: (0,) * nd)


def _mod_kernel(c_ref, w_ref, b_ref, o_ref):
    o_ref[...] = _mm(_silu(c_ref[...]), w_ref[...]) + b_ref[...]


def _mod(cvec, w_mod, b_mod):
    rows, d = cvec.shape
    n = w_mod.shape[1]
    tn = 1536
    return pl.pallas_call(
        _mod_kernel,
        grid=(n // tn,),
        in_specs=[_full((rows, d)), pl.BlockSpec((d, tn), lambda j: (0, j)),
                  pl.BlockSpec((1, tn), lambda j: (0, j))],
        out_specs=pl.BlockSpec((rows, tn), lambda j: (0, j)),
        out_shape=jax.ShapeDtypeStruct((rows, n), F32),
        compiler_params=_cparams(("arbitrary",)),
        name="mod",
    )(cvec, w_mod, b_mod)


def _in_kernel(x_ref, mod_ref, g_ref, w_ref, xa_ref, ub_ref, op_ref, ga_ref, gb_ref):
    d = D_MODEL
    shift, scale = mod_ref[:, 0:d], mod_ref[:, d:2 * d]
    h = (_rmsnorm(x_ref[...], g_ref[...]) * (1.0 + scale) + shift).astype(BF16)
    col = 0
    for ref in (xa_ref, ub_ref, op_ref, ga_ref, gb_ref):
        n = ref.shape[-1]
        ref[...] = jnp.dot(h, w_ref[:, col:col + n], preferred_element_type=F32)
        col += n


def _in_proj(x, mod, g1, w_in, tm):
    b, t, d = x.shape
    per = t // tm
    widths = (A_WIDTH, B_WIDTH, B_WIDTH, D_MODEL, D_MODEL)
    tok = lambda n: pl.BlockSpec((None, tm, n), lambda i: (i // per, i % per, 0))
    return pl.pallas_call(
        _in_kernel,
        grid=(b * per,),
        in_specs=[tok(d), pl.BlockSpec((None, 1, 6 * d), lambda i: (i // per, 0, 0)),
                  _full((1, d)), _full(w_in.shape)],
        out_specs=[tok(n) for n in widths],
        out_shape=[jax.ShapeDtypeStruct((b, t, n), F32) for n in widths],
        compiler_params=_cparams(("arbitrary",)),
        name="in_proj",
    )(x, mod, g1, w_in)


def _unit_tri_inverses(mats):
    c = CHUNK
    eye = (_iota((c, c), 0) == _iota((c, c), 1)).astype(F32)
    diag8 = _same_block(c, 8)
    nd = [jnp.where(diag8, m, 0.0) for m in mats]
    nd_s = [_split(x) for x in nd]
    n2_s = [_split(_mm3(s, s)) for s in nd_s]
    n4_s = [_split(_mm3(s, s)) for s in n2_s]
    t = [eye + x for x in nd]
    t = [x + _mm3(_split(x), s) for x, s in zip(t, n2_s)]
    t = [x + _mm3(_split(x), s) for x, s in zip(t, n4_s)]
    blk = 8
    while blk < c:
        sel = _same_block(c, 2 * blk) & jnp.logical_not(_same_block(c, blk))
        t_s = [_split(x) for x in t]
        w_s = [_split(_mm3(_split(jnp.where(sel, m, 0.0)), s)) for m, s in zip(mats, t_s)]
        t = [x + _mm3(s, w) for x, s, w in zip(t, t_s, w_s)]
        blk *= 2
    return t


def _rwkv_kernel(x_ref, s0_ref, mu_ref, wr_ref, wk_ref, wv_ref, w0_ref, w1_ref, w2_ref,
                 a0_ref, a1_ref, a2_ref, g1_ref, g2_ref, kk_ref, ka_ref, rk_ref, lng_ref, lnb_ref,
                 y_ref, s_ref,
                 r_s, v_s, kk_s, cum_s, cx_s, kd_s, bb_s, g_s, bonus_s, y_s):
    t_len = x_ref.shape[0]
    n_chunks = t_len // CHUNK
    n_tiles = t_len // ROW_TILE
    hd = A_HEAD_DIM
    x = x_ref[...]
    row = _iota(x.shape, 0)
    prev = jnp.where(row == 0, 0.0, pltpu.roll(x, 1, 0))
    nxt = jnp.where(row == t_len - 1, 0.0, pltpu.roll(x, t_len - 1, 0))
    y_s[...] = 0.5 * (prev + nxt) - x
    head_ones = _same_block(A_WIDTH, hd).astype(F32)
    chunk_diag = _same_block(ROW_TILE, CHUNK)
    run_sum = [(chunk_diag & _tri(d == 1, True, ROW_TILE)).astype(F32) for d in range(2)]

    def project_tile(ti, carry):
        rows = pl.ds(pl.multiple_of(ti * ROW_TILE, ROW_TILE), ROW_TILE)
        x, xx = x_ref[rows, :], y_s[rows, :]
        xr, xw, xk, xv, xi, xg = (x + xx * mu_ref[j:j + 1, :] for j in range(6))
        r = _mm(xr, wr_ref[...])
        k = _mm(xk, wk_ref[...])
        v = _mm(xv, wv_ref[...])
        g_s[rows, :] = _mm(_sigmoid(_mm(xg, g1_ref[...])), g2_ref[...])
        kk = k * kk_ref[...]
        kk = kk / jnp.maximum(jnp.sqrt(_mmh(kk * kk, head_ones)), 1e-12)
        r_s[rows, :] = r
        v_s[rows, :] = v
        kk_s[rows, :] = kk
        bonus = jnp.zeros_like(x)
        for d in range(2):
            zw = w0_ref[d:d + 1, :] + _mm(jnp.tanh(_mm(xw, w1_ref[d])), w2_ref[d])
            lw = -jnp.exp(-_softplus(-zw) - 0.5)
            cum = _mmh(run_sum[d], lw)
            cum_s[d, rows, :] = cum
            cx_s[d, rows, :] = cum - lw
            a = _sigmoid(a0_ref[d:d + 1, :] + _mm(_mm(xi, a1_ref[d]), a2_ref[d]))
            kd = k * (1.0 + (a - 1.0) * ka_ref[...])
            kd_s[d, rows, :] = kd
            bb_s[d, rows, :] = kk * a
            bonus = bonus + _mmh(r * kd * rk_ref[...], head_ones) * v
        bonus_s[rows, :] = bonus
        y_s[rows, :] = jnp.zeros_like(x)
        return carry

    lax.fori_loop(0, n_tiles, project_tile, 0)
    s_ref[...] = s0_ref[...]

    strict = [_tri(d == 1, False) for d in range(2)]
    incl = [_tri(d == 1, True) for d in range(2)]

    def chunk_body(ci, carry):
        chains = []
        for d in range(2):
            cpos = (n_chunks - 1 - ci) if d == 1 else ci
            last = 0 if d == 1 else CHUNK - 1
            rows = pl.ds(pl.multiple_of(cpos * CHUNK, CHUNK), CHUNK)
            cum, cx = cum_s[d, rows, :], cx_s[d, rows, :]
            tot = cum[last:last + 1, :]
            kdc, bbc = kd_s[d, rows, :], bb_s[d, rows, :]
            e_neg = jnp.exp(-cum)
            e_tail = jnp.exp(tot - cum)
            at = -kk_s[rows, :] * jnp.exp(cx)
            rt = r_s[rows, :] * jnp.exp(cum)
            bt, kt = bbc * e_neg, kdc * e_neg
            bp, kp = bbc * e_tail, kdc * e_tail
            pc = jnp.exp(tot)
            vc = v_s[rows, :]
            for h in range(A_HEADS):
                ln = slice(h * hd, (h + 1) * hd)
                chains.append(dict(d=d, h=h, rows=rows, ln=ln, v=vc[:, ln], pc=pc[:, ln],
                                   lhs=jnp.concatenate([at[:, ln], rt[:, ln]], axis=0),
                                   bt=bt[:, ln], kt=kt[:, ln], bp=bp[:, ln], kp=kp[:, ln]))
        for c in chains:
            lhs_s = _split(c["lhs"])
            c["sb"] = _mm3(lhs_s, _split(c["bt"]), _NT)
            c["sk"] = _mm3(lhs_s, _split(c["kt"]), _NT)
            c["s"] = s_ref[c["d"], c["h"]]
            c["fs"] = _mm_nt(c["lhs"], c["s"])
        for c in chains:
            c["x"] = c["fs"][:CHUNK] + _mm(jnp.where(strict[c["d"]], c["sk"][:CHUNK], 0.0), c["v"])
        inv = _unit_tri_inverses([jnp.where(strict[c["d"]], c["sb"][:CHUNK], 0.0) for c in chains])
        for c, t in zip(chains, inv):
            c["u"] = _mm3(_split(t), _split(c["x"]))
        for c in chains:
            d, rows, ln = c["d"], c["rows"], c["ln"]
            a_rb = jnp.where(incl[d], c["sb"][CHUNK:], 0.0)
            a_rk = jnp.where(incl[d], c["sk"][CHUNK:], 0.0)
            y = c["fs"][CHUNK:] + _mm(a_rb, c["u"]) + _mm(a_rk, c["v"])
            y_s[rows, ln] = y_s[rows, ln] + y
            s_ref[d, c["h"]] = c["s"] * c["pc"] + _mm_tn(c["u"], c["bp"]) + _mm_tn(c["v"], c["kp"])
        return carry

    lax.fori_loop(0, n_chunks, chunk_body, 0)

    def finish_tile(ti, carry):
        rows = pl.ds(pl.multiple_of(ti * ROW_TILE, ROW_TILE), ROW_TILE)
        ys = y_s[rows, :]
        cen = ys - _mmh(ys, head_ones) * (1.0 / hd)
        var = _mmh(cen * cen, head_ones) * (1.0 / hd)
        yn = cen * lax.rsqrt(var + RWKV_GN_EPS) * lng_ref[...] + lnb_ref[...]
        y_ref[rows, :] = (yn + bonus_s[rows, :]) * g_s[rows, :]
        return carry

    lax.fori_loop(0, n_tiles, finish_tile, 0)


def _rwkv(xa, s0, p):
    b, t, w = xa.shape
    names = ("mu", "wr", "wk", "wv", "w0", "w1", "w2", "a0", "a1", "a2", "g1", "g2",
             "k_k", "k_a", "r_k", "ln_g", "ln_b")
    weights = [p["rwkv_" + n] for n in names]
    seq = pl.BlockSpec((None, t, w), lambda i: (i, 0, 0))
    st = pl.BlockSpec((None, 2, A_HEADS, A_HEAD_DIM, A_HEAD_DIM), lambda i: (i, 0, 0, 0, 0))
    tw = lambda lead=(): pltpu.VMEM(lead + (t, w), F32)
    return pl.pallas_call(
        _rwkv_kernel,
        grid=(b,),
        in_specs=[seq, st] + [_full(x.shape) for x in weights],
        out_specs=[seq, st],
        out_shape=[jax.ShapeDtypeStruct((b, t, w), F32), jax.ShapeDtypeStruct(s0.shape, F32)],
        scratch_shapes=[tw(), tw(), tw(), tw((2,)), tw((2,)), tw((2,)), tw((2,)), tw(), tw(), tw()],
        compiler_params=_cparams(("arbitrary",)),
        name="rwkv",
    )(xa, s0, *weights)


def _mlstm_kernel(u_ref, op_ref, c0_ref, n0_ref, m0_ref, conv_ref, wq_ref, wk_ref, wv_ref,
                  wg_ref, bg_ref, wgt_ref, bgt_ref, lng_ref,
                  y_ref, c_ref, n_ref, m_ref,
                  pad_s, q_s, k_s, v_s, gc_s, gr_s, h_s, *, conv2d):
    t_len = u_ref.shape[0]
    n_chunks = t_len // CHUNK
    hd = B_HEAD_DIM
    u = u_ref[...]
    pad_s[0:CONV_PAD, :] = jnp.zeros((CONV_PAD, B_WIDTH), F32)
    pad_s[CONV_PAD + t_len:CONV_PAD + t_len + CONV_PAD, :] = jnp.zeros((CONV_PAD, B_WIDTH), F32)
    pad_s[CONV_PAD:CONV_PAD + t_len, :] = u
    col = _iota(u.shape, 0) & (GRID_W - 1)
    conv = jnp.zeros_like(u)
    for kh in range(3):
        if not conv2d and kh != 1:
            continue
        for kw in range(3):
            shift = (kh - 1) * GRID_W + (kw - 1)
            term = pad_s[CONV_PAD + shift:CONV_PAD + shift + t_len, :] * conv_ref[kh * 3 + kw:kh * 3 + kw + 1, :]
            if conv2d and kw != 1:
                src = col + (kw - 1)
                term = jnp.where((src >= 0) & (src < GRID_W), term, 0.0)
            conv = conv + term
    uc = _silu(conv)
    for h in range(B_HEADS):
        ln = slice(h * hd, (h + 1) * hd)
        q_s[:, ln] = _mm(uc[:, ln], wq_ref[h])
        k_s[:, ln] = _mm(uc[:, ln], wk_ref[h]) * (hd ** -0.5)
        v_s[:, ln] = _mm(u[:, ln], wv_ref[h])
    gcol = _mm(uc, wg_ref[...]) + bg_ref[...]
    gcol = jnp.where(_iota(gcol.shape, 1) >= 8, -_softplus(-gcol), gcol)
    gc_s[...] = gcol
    grow = _mm_nt(wgt_ref[...], uc) + bgt_ref[:, 0:1]
    grow = jnp.where(_iota(grow.shape, 0) >= 8, -_softplus(-grow), grow)
    for c in range(n_chunks):
        gr_s[c] = grow[:, c * CHUNK:(c + 1) * CHUNK]
    h_s[...] = jnp.zeros_like(u)
    c_ref[...] = c0_ref[...]
    n_ref[...] = n0_ref[...]
    m_ref[...] = m0_ref[...]

    for d in range(2):
        reverse = d == 1
        tri_incl = _tri(reverse, True).astype(F32)
        incl = _tri(reverse, True)
        last = 0 if reverse else CHUNK - 1

        def chunk_body(ci, carry, d=d, reverse=reverse, tri_incl=tri_incl, incl=incl, last=last):
            cpos = (n_chunks - 1 - ci) if reverse else ci
            rows = pl.ds(pl.multiple_of(cpos * CHUNK, CHUNK), CHUNK)
            gc = gc_s[rows, :]
            gr = gr_s[cpos]
            cum_c = _mmh(tri_incl, gc)
            cum_r = _mmh_nt(gr, tri_incl)
            for h in range(B_HEADS):
                ln = slice(h * hd, (h + 1) * hd)
                li, lf = d * 4 + h, 8 + d * 4 + h
                b_col, i_col = cum_c[:, lf:lf + 1], gc[:, li:li + 1]
                b_row, i_row = cum_r[lf:lf + 1, :], gr[li:li + 1, :]
                m_prev = m_ref[d, h:h + 1, 0:1]
                qc, kc, vc = q_s[rows, ln], k_s[rows, ln], v_s[rows, ln]
                logw = jnp.where(incl, b_col - b_row + i_row, -jnp.inf)
                m_t = jnp.maximum(b_col + m_prev, jnp.max(logw, axis=1, keepdims=True))
                inter = jnp.exp(b_col + m_prev - m_t)
                s = _mm_nt(qc, kc) * jnp.exp(logw - m_t)
                c_h = c_ref[d, h]
                n_h = n_ref[d, h:h + 1, :]
                num = inter * _mm(qc, c_h) + _mm(s, vc)
                den = inter * jnp.sum(qc * n_h, axis=1, keepdims=True) + jnp.sum(s, axis=1, keepdims=True)
                h_s[rows, ln] = h_s[rows, ln] + num / jnp.maximum(jnp.abs(den), jnp.exp(-m_t))
                m_new = m_t[last:last + 1, :]
                b_last = b_col[last:last + 1, :]
                decay = jnp.exp(b_last + m_prev - m_new)
                kw = kc * jnp.exp(b_last - b_col + i_col - m_new)
                c_ref[d, h] = decay * c_h + _mm_tn(kw, vc)
                n_ref[d, h:h + 1, :] = decay * n_h + jnp.sum(kw, axis=0, keepdims=True)
                m_ref[d, h:h + 1, :] = jnp.broadcast_to(m_new, (1, LANES))
            return carry

        lax.fori_loop(0, n_chunks, chunk_body, 0)

    for h in range(B_HEADS):
        ln = slice(h * hd, (h + 1) * hd)
        hh = h_s[:, ln]
        cen = hh - jnp.mean(hh, axis=1, keepdims=True)
        var = jnp.mean(cen * cen, axis=1, keepdims=True)
        y_ref[:, ln] = _sigmoid(op_ref[:, ln]) * (cen * lax.rsqrt(var + MLSTM_GN_EPS) * lng_ref[:, ln])


def _mlstm(ub, o_pre, c0, n0, m0, p, conv2d):
    b, t, w = ub.shape
    names = ("conv", "wq", "wk", "wv", "wg", "bg", "wgt", "bgt", "ln_g")
    weights = [p["mlstm_" + n] for n in names]
    seq = pl.BlockSpec((None, t, w), lambda i: (i, 0, 0))
    cst = pl.BlockSpec((None, 2, B_HEADS, B_HEAD_DIM, B_HEAD_DIM), lambda i: (i, 0, 0, 0, 0))
    vst = pl.BlockSpec((None, 2, B_HEADS, LANES), lambda i: (i, 0, 0, 0))
    return pl.pallas_call(
        functools.partial(_mlstm_kernel, conv2d=conv2d),
        grid=(b,),
        in_specs=[seq, seq, cst, vst, vst] + [_full(x.shape) for x in weights],
        out_specs=[seq, cst, vst, vst],
        out_shape=[jax.ShapeDtypeStruct((b, t, w), F32), jax.ShapeDtypeStruct(c0.shape, F32),
                   jax.ShapeDtypeStruct(n0.shape, F32), jax.ShapeDtypeStruct(m0.shape, F32)],
        scratch_shapes=[pltpu.VMEM((t + 2 * CONV_PAD, w), F32), pltpu.VMEM((t, w), F32),
                        pltpu.VMEM((t, w), F32), pltpu.VMEM((t, w), F32), pltpu.VMEM((t, LANES), F32),
                        pltpu.VMEM((t // CHUNK, 16, CHUNK), F32), pltpu.VMEM((t, w), F32)],
        compiler_params=_cparams(("arbitrary",)),
        name="mlstm",
    )(ub, o_pre, c0, n0, m0, *weights)


def _first_lane_of_max(val, lane, valid):
    masked = jnp.where(valid, val, -jnp.inf)
    best = jnp.max(masked, axis=1, keepdims=True)
    idx = jnp.min(jnp.where(valid & (masked == best), lane, LANES), axis=1, keepdims=True)
    return best, idx


def _merge_kernel(x_ref, ya_ref, yb_ref, ga_ref, gb_ref, mod_ref, wpa_ref, wpb_ref, wo_ref, g2_ref,
                  wr_ref, br_ref, x1_ref, comb_ref):
    d = D_MODEL
    gate1 = mod_ref[:, 2 * d:3 * d]
    shift2, scale2 = mod_ref[:, 3 * d:4 * d], mod_ref[:, 4 * d:5 * d]
    merged = (_sigmoid(ga_ref[...]) * _mm(ya_ref[...], wpa_ref[...])
              + _sigmoid(gb_ref[...]) * _mm(yb_ref[...], wpb_ref[...]))
    x1 = x_ref[...] + gate1 * _mm(merged, wo_ref[...])
    x1_ref[...] = x1
    h2 = _rmsnorm(x1, g2_ref[...]) * (1.0 + scale2) + shift2
    logits = _mm(h2, wr_ref[...]) + br_ref[...]
    lane = _iota(logits.shape, 1)
    is_group = lane < N_GROUPS
    g_max, g_sel = _first_lane_of_max(logits, lane, is_group)
    g_w = 1.0 / jnp.sum(jnp.where(is_group, jnp.exp(logits - g_max), 0.0), axis=1, keepdims=True)
    expert = lane - N_GROUPS
    in_group = (expert >= 0) & (expert < N_EXPERTS) & (lax.shift_right_arithmetic(expert, 2) == g_sel)
    e_max = jnp.max(jnp.where(in_group, logits, -jnp.inf), axis=1, keepdims=True)
    e_exp = jnp.where(in_group, jnp.exp(logits - e_max), 0.0)
    prob = e_exp / jnp.sum(e_exp, axis=1, keepdims=True)
    p1, i1 = _first_lane_of_max(prob, lane, in_group)
    p2, i2 = _first_lane_of_max(prob, lane, in_group & (lane != i1))
    denom = p1 + p2
    comb = jnp.where(lane == i1, g_w * p1 / denom, 0.0) + jnp.where(lane == i2, g_w * p2 / denom, 0.0)
    comb_ref[...] = comb


def _merge(x, ya, yb, ga, gb, mod, p, tm):
    b, t, d = x.shape
    per = t // tm
    tok = lambda n: pl.BlockSpec((None, tm, n), lambda i: (i // per, i % per, 0))
    weights = [p["rwkv_w_proj"], p["mlstm_w_proj"], p["w_out"], p["norm2_g"], p["moe_w_router"], p["moe_b_router"]]
    return pl.pallas_call(
        _merge_kernel,
        grid=(b * per,),
        in_specs=[tok(d), tok(A_WIDTH), tok(B_WIDTH), tok(d), tok(d),
                  pl.BlockSpec((None, 1, 6 * d), lambda i: (i // per, 0, 0))] + [_full(w.shape) for w in weights],
        out_specs=[tok(d), tok(LANES)],
        out_shape=[jax.ShapeDtypeStruct((b, t, d), F32), jax.ShapeDtypeStruct((b, t, LANES), F32)],
        compiler_params=_cparams(("arbitrary",)),
        name="merge",
    )(x, ya, yb, ga, gb, mod, *weights)


def _moe_kernel(x1_ref, comb_ref, mod_ref, g2_ref, w1_ref, w3_ref, w2_ref, gf_ref, y_ref, h2_s, acc_s):
    d = D_MODEL
    e = pl.program_id(1)

    @pl.when(e == 0)
    def _():
        shift2, scale2 = mod_ref[:, 3 * d:4 * d], mod_ref[:, 4 * d:5 * d]
        h2_s[...] = (_rmsnorm(x1_ref[...], g2_ref[...]) * (1.0 + scale2) + shift2).astype(BF16)
        acc_s[...] = jnp.zeros_like(acc_s)

    comb = comb_ref[...]
    w_e = jnp.sum(jnp.where(_iota(comb.shape, 1) == e + N_GROUPS, comb, 0.0), axis=1, keepdims=True)
    h2 = h2_s[...]
    a = jnp.dot(h2, w1_ref[...], preferred_element_type=F32)
    b = jnp.dot(h2, w3_ref[...], preferred_element_type=F32)
    acc_s[...] += _mm(_silu(a) * b * w_e, w2_ref[...])

    @pl.when(e == N_EXPERTS - 1)
    def _():
        gate2 = mod_ref[:, 5 * d:6 * d]
        y_ref[...] = _rmsnorm(x1_ref[...] + gate2 * acc_s[...], gf_ref[...])


def _moe(x1, comb, mod, p, tm):
    b, t, d = x1.shape
    per = t // tm
    tok = lambda n: pl.BlockSpec((None, tm, n), lambda i, e: (i // per, i % per, 0))
    return pl.pallas_call(
        _moe_kernel,
        grid=(b * per, N_EXPERTS),
        in_specs=[tok(d), tok(LANES), pl.BlockSpec((None, 1, 6 * d), lambda i, e: (i // per, 0, 0)),
                  _full((1, d)),
                  pl.BlockSpec((None, d, D_EXPERT), lambda i, e: (e, 0, 0)),
                  pl.BlockSpec((None, d, D_EXPERT), lambda i, e: (e, 0, 0)),
                  pl.BlockSpec((None, D_EXPERT, d), lambda i, e: (e, 0, 0)),
                  _full((1, d))],
        out_specs=tok(d),
        out_shape=jax.ShapeDtypeStruct((b, t, d), F32),
        scratch_shapes=[pltpu.VMEM((tm, d), BF16), pltpu.VMEM((tm, d), F32)],
        compiler_params=_cparams(("arbitrary", "arbitrary")),
        name="moe",
    )(x1, comb, mod, p["norm2_g"], p["moe_w1"], p["moe_w3"], p["moe_w2"], p["final_norm_g"])


def _trunk(x, mod, s_rwkv, s_c, s_n, s_m, conv2d, p, tm):
    xa, ub, o_pre, ga, gb = _in_proj(x, mod, p["norm1_g"], p["w_in"], tm)
    ya, s_rwkv = _rwkv(xa, s_rwkv, p)
    m_in = jnp.broadcast_to(s_m[..., None], s_m.shape + (LANES,))
    yb, s_c, s_n, s_m = _mlstm(ub, o_pre, s_c, s_n, m_in, p, conv2d)
    x1, comb = _merge(x, ya, yb, ga, gb, mod, p, tm)
    y = _moe(x1, comb, mod, p, tm)
    return y, s_rwkv, s_c, s_n, s_m[..., 0]


def kernel(x_prompt, x_sample, c, c_ctx, state_rwkv, state_mlstm_C, state_mlstm_n, state_mlstm_m, w_mod, b_mod, norm1_g, norm2_g, w_in, rwkv_mu, rwkv_w_r, rwkv_w_k, rwkv_w_v, rwkv_w0, rwkv_w1, rwkv_w2, rwkv_a0, rwkv_a1, rwkv_a2, rwkv_g1, rwkv_g2, rwkv_k_k, rwkv_k_a, rwkv_r_k, rwkv_ln_g, rwkv_ln_b, rwkv_w_proj, mlstm_conv, mlstm_w_q, mlstm_w_k, mlstm_w_v, mlstm_w_i, mlstm_b_i, mlstm_w_f, mlstm_b_f, mlstm_ln_g, mlstm_w_proj, w_out, moe_w_group, moe_b_group, moe_w_expert, moe_b_expert, moe_w1, moe_w3, moe_w2, final_norm_g):
    assert w_mod.shape[0] == 1, "single trunk layer"
    l = 0
    bp, dec = x_prompt.shape[0], x_sample.shape[0]
    bf = lambda w: w.astype(BF16)
    row = lambda w: w.reshape(1, -1).astype(F32)
    wg = jnp.concatenate([mlstm_w_i[l, 0], mlstm_w_i[l, 1], mlstm_w_f[l, 0], mlstm_w_f[l, 1]], axis=1)
    bg = jnp.concatenate([mlstm_b_i[l, 0], mlstm_b_i[l, 1], mlstm_b_f[l, 0], mlstm_b_f[l, 1]])
    wg = jnp.pad(wg, ((0, 0), (0, LANES - wg.shape[1])))
    bg = jnp.pad(bg, (0, LANES - bg.shape[0]))
    w_router = jnp.pad(jnp.concatenate([moe_w_group[l], moe_w_expert[l]], axis=1),
                       ((0, 0), (0, LANES - N_GROUPS - N_EXPERTS)))
    b_router = jnp.pad(jnp.concatenate([moe_b_group[l], moe_b_expert[l]]), (0, LANES - N_GROUPS - N_EXPERTS))
    p = {
        "norm1_g": row(norm1_g[l]), "norm2_g": row(norm2_g[l]), "w_in": bf(w_in[l]),
        "rwkv_mu": rwkv_mu[l], "rwkv_wr": bf(rwkv_w_r[l]), "rwkv_wk": bf(rwkv_w_k[l]), "rwkv_wv": bf(rwkv_w_v[l]),
        "rwkv_w0": rwkv_w0[l], "rwkv_w1": bf(rwkv_w1[l]), "rwkv_w2": bf(rwkv_w2[l]),
        "rwkv_a0": rwkv_a0[l], "rwkv_a1": bf(rwkv_a1[l]), "rwkv_a2": bf(rwkv_a2[l]),
        "rwkv_g1": bf(rwkv_g1[l]), "rwkv_g2": bf(rwkv_g2[l]),
        "rwkv_k_k": row(rwkv_k_k[l]), "rwkv_k_a": row(rwkv_k_a[l]), "rwkv_r_k": row(rwkv_r_k[l]),
        "rwkv_ln_g": row(rwkv_ln_g[l]), "rwkv_ln_b": row(rwkv_ln_b[l]),
        "rwkv_w_proj": bf(rwkv_w_proj[l]),
        "mlstm_conv": mlstm_conv[l].reshape(9, B_WIDTH),
        "mlstm_wq": bf(mlstm_w_q[l]), "mlstm_wk": bf(mlstm_w_k[l]), "mlstm_wv": bf(mlstm_w_v[l]),
        "mlstm_wg": bf(wg), "mlstm_bg": row(bg),
        "mlstm_wgt": bf(wg[:, :16].T), "mlstm_bgt": jnp.broadcast_to(bg[:16, None], (16, LANES)),
        "mlstm_ln_g": row(mlstm_ln_g[l]), "mlstm_w_proj": bf(mlstm_w_proj[l]), "w_out": bf(w_out[l]),
        "moe_w_router": bf(w_router), "moe_b_router": row(b_router),
        "moe_w1": bf(moe_w1[l]), "moe_w3": bf(moe_w3[l]), "moe_w2": bf(moe_w2[l]),
        "final_norm_g": row(final_norm_g),
    }
    cvec = jnp.concatenate([c_ctx[None, :], c, jnp.zeros((8 - 1 - dec, D_MODEL), F32)], axis=0)
    mod = _mod(cvec, w_mod[l], b_mod[l].reshape(1, -1))
    mod_ctx = jnp.broadcast_to(mod[0:1][:, None, :], (bp, 1, 6 * D_MODEL))
    mod_lat = mod[1:1 + dec][:, None, :]

    zeros = lambda *s: jnp.zeros((bp,) + s, F32)
    yp, n_rwkv, n_c, n_n, n_m = _trunk(
        x_prompt, mod_ctx, zeros(2, A_HEADS, A_HEAD_DIM, A_HEAD_DIM), zeros(2, B_HEADS, B_HEAD_DIM, B_HEAD_DIM),
        zeros(2, B_HEADS, B_HEAD_DIM), zeros(2, B_HEADS), False, p, 256)
    ys, _, _, _, _ = _trunk(
        x_sample, mod_lat, state_rwkv[:, l].astype(F32), state_mlstm_C[:, l].astype(F32),
        state_mlstm_n[:, l].astype(F32), state_mlstm_m[:, l].astype(F32), True, p, 512)
    dt = x_prompt.dtype
    return (yp, ys, n_rwkv[:, None].astype(dt), n_c[:, None].astype(dt), n_n[:, None].astype(dt),
            n_m[:, None].astype(dt))
```

```python
import functools

import jax
import jax.numpy as jnp
from jax import lax
from jax.experimental import pallas as pl
from jax.experimental.pallas import tpu as pltpu

F32 = jnp.float32
BF16 = jnp.bfloat16

D_MODEL = 1024
GRID_W = 64
A_HEAD_DIM = 64
A_WIDTH = 512
A_HEADS = 8
B_HEAD_DIM = 128
B_WIDTH = 512
B_HEADS = 4
CHUNK = 64
ROW_TILE = 256
N_GROUPS = 4
EXPERTS_PER_GROUP = 4
N_EXPERTS = 16
D_EXPERT = 256
NORM_EPS = 1e-6
RWKV_GN_EPS = 64e-5
MLSTM_GN_EPS = 1e-5
LANES = 128
CONV_PAD = 72
VMEM_LIMIT = 56 * 1024 * 1024


def _mm(a, b):
    return jnp.dot(a.astype(BF16), b.astype(BF16), preferred_element_type=F32)


_NN, _NT, _TN = ((1,), (0,)), ((1,), (1,)), ((0,), (0,))


def _split(a):
    hi = a.astype(BF16)
    return hi, (a - hi.astype(F32)).astype(BF16)


def _split3(a):
    hi = a.astype(BF16)
    rest = a - hi.astype(F32)
    lo = rest.astype(BF16)
    return hi, lo, (rest - lo.astype(F32)).astype(BF16)


def _sum01(a, b, dims=_NN):
    dot = lambda p, q: lax.dot_general(p, q, (dims, ((), ())), preferred_element_type=F32)
    terms = [dot(x, b) for x in a] if isinstance(a, tuple) else [dot(a, x) for x in b]
    return functools.reduce(lambda p, q: p + q, terms)


def _mm3(a, b, dims=_NN):
    (ah, al), (bh, bl) = a, b
    dot = lambda p, q: lax.dot_general(p, q, (dims, ((), ())), preferred_element_type=F32)
    return dot(ah, bh) + dot(ah, bl) + dot(al, bh)


def _mm_nt(a, b):
    return lax.dot_general(a.astype(BF16), b.astype(BF16), (((1,), (1,)), ((), ())),
                           preferred_element_type=F32)


def _mm_tn(a, b):
    return lax.dot_general(a.astype(BF16), b.astype(BF16), (((0,), (0,)), ((), ())),
                           preferred_element_type=F32)


def _sigmoid(x):
    return 1.0 / (1.0 + jnp.exp(-x))


def _silu(x):
    return x * _sigmoid(x)


def _softplus(x):
    return jnp.maximum(x, 0.0) + jnp.log(1.0 + jnp.exp(-jnp.abs(x)))


def _iota(shape, dim):
    return lax.broadcasted_iota(jnp.int32, shape, dim)


def _tri(reverse, inclusive, n=CHUNK):
    row, col = _iota((n, n), 0), _iota((n, n), 1)
    if reverse:
        return (col >= row) if inclusive else (col > row)
    return (col <= row) if inclusive else (col < row)


def _same_block(n, blk):
    sh = blk.bit_length() - 1
    row, col = _iota((n, n), 0), _iota((n, n), 1)
    return lax.shift_right_logical(row, sh) == lax.shift_right_logical(col, sh)


def _rmsnorm(x, g):
    return x * lax.rsqrt(jnp.mean(x * x, axis=-1, keepdims=True) + NORM_EPS) * g


def _cparams(sem):
    return pltpu.CompilerParams(dimension_semantics=sem, vmem_limit_bytes=VMEM_LIMIT)


def _full(shape):
    nd = len(shape)
    return pl.BlockSpec(shape, lambda *_: (0,) * nd)


def _mod_kernel(c_ref, w_ref, b_ref, o_ref):
    o_ref[...] = _mm(_silu(c_ref[...]), w_ref[...]) + b_ref[...]


def _mod(cvec, w_mod, b_mod):
    rows, d = cvec.shape
    n = w_mod.shape[1]
    tn = 1536
    return pl.pallas_call(
        _mod_kernel,
        grid=(n // tn,),
        in_specs=[_full((rows, d)), pl.BlockSpec((d, tn), lambda j: (0, j)),
                  pl.BlockSpec((1, tn), lambda j: (0, j))],
        out_specs=pl.BlockSpec((rows, tn), lambda j: (0, j)),
        out_shape=jax.ShapeDtypeStruct((rows, n), F32),
        compiler_params=_cparams(("arbitrary",)),
        name="mod",
    )(cvec, w_mod, b_mod)


def _in_kernel(x_ref, mod_ref, g_ref, w_ref, xa_ref, ub_ref, op_ref, ga_ref, gb_ref):
    d = D_MODEL
    shift, scale = mod_ref[:, 0:d], mod_ref[:, d:2 * d]
    h = (_rmsnorm(x_ref[...], g_ref[...]) * (1.0 + scale) + shift).astype(BF16)
    col = 0
    for ref in (xa_ref, ub_ref, op_ref, ga_ref, gb_ref):
        n = ref.shape[-1]
        ref[...] = jnp.dot(h, w_ref[:, col:col + n], preferred_element_type=F32)
        col += n


def _in_proj(x, mod, g1, w_in, tm):
    b, t, d = x.shape
    per = t // tm
    widths = (A_WIDTH, B_WIDTH, B_WIDTH, D_MODEL, D_MODEL)
    tok = lambda n: pl.BlockSpec((None, tm, n), lambda i: (i // per, i % per, 0))
    return pl.pallas_call(
        _in_kernel,
        grid=(b * per,),
        in_specs=[tok(d), pl.BlockSpec((None, 1, 6 * d), lambda i: (i // per, 0, 0)),
                  _full((1, d)), _full(w_in.shape)],
        out_specs=[tok(n) for n in widths],
        out_shape=[jax.ShapeDtypeStruct((b, t, n), F32) for n in widths],
        compiler_params=_cparams(("arbitrary",)),
        name="in_proj",
    )(x, mod, g1, w_in)


def _unit_tri_inverses(mats):
    c = CHUNK
    eye = (_iota((c, c), 0) == _iota((c, c), 1)).astype(F32)
    diag8 = _same_block(c, 8)
    nd = [jnp.where(diag8, m, 0.0) for m in mats]
    nd_s = [_split(x) for x in nd]
    n2_s = [_split(_mm3(s, s)) for s in nd_s]
    n4_s = [_split(_mm3(s, s)) for s in n2_s]
    t = [eye + x for x in nd]
    t = [x + _mm3(_split(x), s) for x, s in zip(t, n2_s)]
    t = [x + _mm3(_split(x), s) for x, s in zip(t, n4_s)]
    blk = 8
    while blk < c:
        sel = _same_block(c, 2 * blk) & jnp.logical_not(_same_block(c, blk))
        t_s = [_split(x) for x in t]
        w_s = [_split(_mm3(_split(jnp.where(sel, m, 0.0)), s)) for m, s in zip(mats, t_s)]
        t = [x + _mm3(s, w) for x, s, w in zip(t, t_s, w_s)]
        blk *= 2
    return t


def _rwkv_kernel(x_ref, s0_ref, mu_ref, wr_ref, wk_ref, wv_ref, w0_ref, w1_ref, w2_ref,
                 a0_ref, a1_ref, a2_ref, g1_ref, g2_ref, kk_ref, ka_ref, rk_ref, lng_ref, lnb_ref,
                 y_ref, s_ref,
                 r_s, v_s, kk_s, cum_s, cx_s, kd_s, bb_s, g_s, bonus_s, y_s):
    t_len = x_ref.shape[0]
    n_chunks = t_len // CHUNK
    n_tiles = t_len // ROW_TILE
    hd = A_HEAD_DIM
    x = x_ref[...]
    row = _iota(x.shape, 0)
    prev = jnp.where(row == 0, 0.0, pltpu.roll(x, 1, 0))
    nxt = jnp.where(row == t_len - 1, 0.0, pltpu.roll(x, t_len - 1, 0))
    y_s[...] = 0.5 * (prev + nxt) - x
    head_ones = _same_block(A_WIDTH, hd).astype(BF16)
    chunk_diag = _same_block(ROW_TILE, CHUNK)
    run_sum = [(chunk_diag & _tri(d == 1, True, ROW_TILE)).astype(BF16) for d in range(2)]

    def project_tile(ti, carry):
        rows = pl.ds(pl.multiple_of(ti * ROW_TILE, ROW_TILE), ROW_TILE)
        x, xx = x_ref[rows, :], y_s[rows, :]
        xr, xw, xk, xv, xi, xg = (x + xx * mu_ref[j:j + 1, :] for j in range(6))
        r = _mm(xr, wr_ref[...])
        k = _mm(xk, wk_ref[...])
        v = _mm(xv, wv_ref[...])
        g_s[rows, :] = _mm(_sigmoid(_mm(xg, g1_ref[...])), g2_ref[...])
        kk = k * kk_ref[...]
        kk = kk / jnp.maximum(jnp.sqrt(_sum01(_split3(kk * kk), head_ones)), 1e-12)
        r_s[rows, :] = r
        v_s[rows, :] = v
        kk_s[rows, :] = kk
        bonus = jnp.zeros_like(x)
        for d in range(2):
            zw = w0_ref[d:d + 1, :] + _mm(jnp.tanh(_mm(xw, w1_ref[d])), w2_ref[d])
            lw = -jnp.exp(-_softplus(-zw) - 0.5)
            cum = _sum01(run_sum[d], _split3(lw))
            cum_s[d, rows, :] = cum
            cx_s[d, rows, :] = cum - lw
            a = _sigmoid(a0_ref[d:d + 1, :] + _mm(_mm(xi, a1_ref[d]), a2_ref[d]))
            kd = k * (1.0 + (a - 1.0) * ka_ref[...])
            kd_s[d, rows, :] = kd
            bb_s[d, rows, :] = kk * a
            bonus = bonus + _sum01(_split3(r * kd * rk_ref[...]), head_ones) * v
        bonus_s[rows, :] = bonus
        y_s[rows, :] = jnp.zeros_like(x)
        return carry

    lax.fori_loop(0, n_tiles, project_tile, 0)
    s_ref[...] = s0_ref[...]

    strict = [_tri(d == 1, False) for d in range(2)]
    incl = [_tri(d == 1, True) for d in range(2)]

    def chunk_body(ci, carry):
        chains = []
        for d in range(2):
            cpos = (n_chunks - 1 - ci) if d == 1 else ci
            last = 0 if d == 1 else CHUNK - 1
            rows = pl.ds(pl.multiple_of(cpos * CHUNK, CHUNK), CHUNK)
            cum, cx = cum_s[d, rows, :], cx_s[d, rows, :]
            tot = cum[last:last + 1, :]
            kdc, bbc = kd_s[d, rows, :], bb_s[d, rows, :]
            e_neg = jnp.exp(-cum)
            e_tail = jnp.exp(tot - cum)
            at = -kk_s[rows, :] * jnp.exp(cx)
            rt = r_s[rows, :] * jnp.exp(cum)
            bt, kt = bbc * e_neg, kdc * e_neg
            bp, kp = bbc * e_tail, kdc * e_tail
            pc = jnp.exp(tot)
            vc = v_s[rows, :]
            for h in range(A_HEADS):
                ln = slice(h * hd, (h + 1) * hd)
                chains.append(dict(d=d, h=h, rows=rows, ln=ln, v=vc[:, ln], pc=pc[:, ln],
                                   lhs=jnp.concatenate([at[:, ln], rt[:, ln]], axis=0),
                                   bt=bt[:, ln], kt=kt[:, ln], bp=bp[:, ln], kp=kp[:, ln]))
        for c in chains:
            lhs_s = _split(c["lhs"])
            c["sb"] = _mm3(lhs_s, _split(c["bt"]), _NT)
            c["sk"] = _mm3(lhs_s, _split(c["kt"]), _NT)
            c["s"] = s_ref[c["d"], c["h"]]
            c["fs"] = _mm_nt(c["lhs"], c["s"])
        for c in chains:
            c["x"] = c["fs"][:CHUNK] + _mm(jnp.where(strict[c["d"]], c["sk"][:CHUNK], 0.0), c["v"])
        inv = _unit_tri_inverses([jnp.where(strict[c["d"]], c["sb"][:CHUNK], 0.0) for c in chains])
        for c, t in zip(chains, inv):
            c["u"] = _mm3(_split(t), _split(c["x"]))
        for c in chains:
            d, rows, ln = c["d"], c["rows"], c["ln"]
            a_rb = jnp.where(incl[d], c["sb"][CHUNK:], 0.0)
            a_rk = jnp.where(incl[d], c["sk"][CHUNK:], 0.0)
            y = c["fs"][CHUNK:] + _mm(a_rb, c["u"]) + _mm(a_rk, c["v"])
            y_s[rows, ln] = y_s[rows, ln] + y
            s_ref[d, c["h"]] = c["s"] * c["pc"] + _mm_tn(c["u"], c["bp"]) + _mm_tn(c["v"], c["kp"])
        return carry

    lax.fori_loop(0, n_chunks, chunk_body, 0)

    def finish_tile(ti, carry):
        rows = pl.ds(pl.multiple_of(ti * ROW_TILE, ROW_TILE), ROW_TILE)
        ys = y_s[rows, :]
        cen = ys - _sum01(_split3(ys), head_ones) * (1.0 / hd)
        var = _sum01(_split3(cen * cen), head_ones) * (1.0 / hd)
        yn = cen * lax.rsqrt(var + RWKV_GN_EPS) * lng_ref[...] + lnb_ref[...]
        y_ref[rows, :] = (yn + bonus_s[rows, :]) * g_s[rows, :]
        return carry

    lax.fori_loop(0, n_tiles, finish_tile, 0)


def _rwkv(xa, s0, p):
    b, t, w = xa.shape
    names = ("mu", "wr", "wk", "wv", "w0", "w1", "w2", "a0", "a1", "a2", "g1", "g2",
             "k_k", "k_a", "r_k", "ln_g", "ln_b")
    weights = [p["rwkv_" + n] for n in names]
    seq = pl.BlockSpec((None, t, w), lambda i: (i, 0, 0))
    st = pl.BlockSpec((None, 2, A_HEADS, A_HEAD_DIM, A_HEAD_DIM), lambda i: (i, 0, 0, 0, 0))
    tw = lambda lead=(): pltpu.VMEM(lead + (t, w), F32)
    return pl.pallas_call(
        _rwkv_kernel,
        grid=(b,),
        in_specs=[seq, st] + [_full(x.shape) for x in weights],
        out_specs=[seq, st],
        out_shape=[jax.ShapeDtypeStruct((b, t, w), F32), jax.ShapeDtypeStruct(s0.shape, F32)],
        scratch_shapes=[tw(), tw(), tw(), tw((2,)), tw((2,)), tw((2,)), tw((2,)), tw(), tw(), tw()],
        compiler_params=_cparams(("arbitrary",)),
        name="rwkv",
    )(xa, s0, *weights)


def _mlstm_kernel(u_ref, op_ref, c0_ref, n0_ref, m0_ref, conv_ref, wq_ref, wk_ref, wv_ref,
                  wg_ref, bg_ref, wgt_ref, bgt_ref, lng_ref,
                  y_ref, c_ref, n_ref, m_ref,
                  pad_s, q_s, k_s, v_s, gc_s, cc_s, gr_s, cr_s, h_s, *, conv2d):
    t_len = u_ref.shape[0]
    n_chunks = t_len // CHUNK
    hd = B_HEAD_DIM
    u = u_ref[...]
    pad_s[0:CONV_PAD, :] = jnp.zeros((CONV_PAD, B_WIDTH), F32)
    pad_s[CONV_PAD + t_len:CONV_PAD + t_len + CONV_PAD, :] = jnp.zeros((CONV_PAD, B_WIDTH), F32)
    pad_s[CONV_PAD:CONV_PAD + t_len, :] = u
    col = _iota(u.shape, 0) & (GRID_W - 1)
    conv = jnp.zeros_like(u)
    for kh in range(3):
        if not conv2d and kh != 1:
            continue
        for kw in range(3):
            shift = (kh - 1) * GRID_W + (kw - 1)
            term = pad_s[CONV_PAD + shift:CONV_PAD + shift + t_len, :] * conv_ref[kh * 3 + kw:kh * 3 + kw + 1, :]
            if conv2d and kw != 1:
                src = col + (kw - 1)
                term = jnp.where((src >= 0) & (src < GRID_W), term, 0.0)
            conv = conv + term
    uc = _silu(conv)
    for h in range(B_HEADS):
        ln = slice(h * hd, (h + 1) * hd)
        q_s[:, ln] = _mm(uc[:, ln], wq_ref[h])
        k_s[:, ln] = _mm(uc[:, ln], wk_ref[h]) * (hd ** -0.5)
        v_s[:, ln] = _mm(u[:, ln], wv_ref[h])
    gcol = _mm(uc, wg_ref[...]) + bg_ref[...]
    gcol = jnp.where(_iota(gcol.shape, 1) >= 8, -_softplus(-gcol), gcol)
    gc_s[...] = gcol
    grow = _mm_nt(wgt_ref[...], uc) + bgt_ref[:, 0:1]
    grow = jnp.where(_iota(grow.shape, 0) >= 8, -_softplus(-grow), grow)
    chunk_diag = _same_block(ROW_TILE, CHUNK)
    run_sum = [(chunk_diag & _tri(d == 1, True, ROW_TILE)).astype(BF16) for d in range(2)]
    per_tile = ROW_TILE // CHUNK
    for ti in range(t_len // ROW_TILE):
        sl = slice(ti * ROW_TILE, (ti + 1) * ROW_TILE)
        gt, rt = _split3(gcol[sl, :]), _split3(grow[:, sl])
        backward_lane = (_iota((ROW_TILE, LANES), 1) & 4) != 0
        cc_s[sl, :] = jnp.where(backward_lane, _sum01(run_sum[1], gt, _NN), _sum01(run_sum[0], gt, _NN))
        backward_row = (_iota((16, ROW_TILE), 0) & 4) != 0
        cum_r = jnp.where(backward_row, _sum01(rt, run_sum[1], _NT), _sum01(rt, run_sum[0], _NT))
        for c in range(per_tile):
            gr_s[ti * per_tile + c] = grow[:, ti * ROW_TILE + c * CHUNK:ti * ROW_TILE + (c + 1) * CHUNK]
            cr_s[ti * per_tile + c] = cum_r[:, c * CHUNK:(c + 1) * CHUNK]
    h_s[...] = jnp.zeros_like(u)
    c_ref[...] = c0_ref[...]
    n_ref[...] = n0_ref[...]
    m_ref[...] = m0_ref[...]
    incl = [_tri(d == 1, True) for d in range(2)]

    def chunk_body(ci, carry):
        chains = []
        for d in range(2):
            cpos = (n_chunks - 1 - ci) if d == 1 else ci
            last = 0 if d == 1 else CHUNK - 1
            rows = pl.ds(pl.multiple_of(cpos * CHUNK, CHUNK), CHUNK)
            gc, cc, gr, cr = gc_s[rows, :], cc_s[rows, :], gr_s[cpos], cr_s[cpos]
            for h in range(B_HEADS):
                ln = slice(h * hd, (h + 1) * hd)
                li, lf = d * 4 + h, 8 + d * 4 + h
                chains.append(dict(d=d, h=h, rows=rows, ln=ln, last=last,
                                   b_col=cc[:, lf:lf + 1], i_col=gc[:, li:li + 1],
                                   b_row=cr[lf:lf + 1, :], i_row=gr[li:li + 1, :],
                                   m=m_ref[d, h:h + 1, 0:1], q=q_s[rows, ln], k=k_s[rows, ln], v=v_s[rows, ln],
                                   c=c_ref[d, h], n=n_ref[d, h:h + 1, :]))
        for c in chains:
            c["qk"] = _mm_nt(c["q"], c["k"])
            c["qc"] = _mm(c["q"], c["c"])
            c["logw"] = jnp.where(incl[c["d"]], c["b_col"] - c["b_row"] + c["i_row"], -jnp.inf)
        for c in chains:
            c["m_t"] = jnp.maximum(c["b_col"] + c["m"], jnp.max(c["logw"], axis=1, keepdims=True))
        for c in chains:
            c["s"] = c["qk"] * jnp.exp(c["logw"] - c["m_t"])
            c["inter"] = jnp.exp(c["b_col"] + c["m"] - c["m_t"])
        for c in chains:
            c["sv"] = _mm(c["s"], c["v"])
            last = c["last"]
            m_new = c["m_t"][last:last + 1, :]
            b_last = c["b_col"][last:last + 1, :]
            c["m_new"] = m_new
            c["decay"] = jnp.exp(b_last + c["m"] - m_new)
            c["kw"] = c["k"] * jnp.exp(b_last - c["b_col"] + c["i_col"] - m_new)
        for c in chains:
            c["kv"] = _mm_tn(c["kw"], c["v"])
            c["den"] = (c["inter"] * jnp.sum(c["q"] * c["n"], axis=1, keepdims=True)
                        + jnp.sum(c["s"], axis=1, keepdims=True))
        for c in chains:
            d, h, rows, ln = c["d"], c["h"], c["rows"], c["ln"]
            num = c["inter"] * c["qc"] + c["sv"]
            h_s[rows, ln] = h_s[rows, ln] + num / jnp.maximum(jnp.abs(c["den"]), jnp.exp(-c["m_t"]))
            c_ref[d, h] = c["decay"] * c["c"] + c["kv"]
            n_ref[d, h:h + 1, :] = c["decay"] * c["n"] + jnp.sum(c["kw"], axis=0, keepdims=True)
            m_ref[d, h:h + 1, :] = jnp.broadcast_to(c["m_new"], (1, LANES))
        return carry

    lax.fori_loop(0, n_chunks, chunk_body, 0)

    for h in range(B_HEADS):
        ln = slice(h * hd, (h + 1) * hd)
        hh = h_s[:, ln]
        cen = hh - jnp.mean(hh, axis=1, keepdims=True)
        var = jnp.mean(cen * cen, axis=1, keepdims=True)
        y_ref[:, ln] = _sigmoid(op_ref[:, ln]) * (cen * lax.rsqrt(var + MLSTM_GN_EPS) * lng_ref[:, ln])


def _mlstm(ub, o_pre, c0, n0, m0, p, conv2d):
    b, t, w = ub.shape
    names = ("conv", "wq", "wk", "wv", "wg", "bg", "wgt", "bgt", "ln_g")
    weights = [p["mlstm_" + n] for n in names]
    seq = pl.BlockSpec((None, t, w), lambda i: (i, 0, 0))
    cst = pl.BlockSpec((None, 2, B_HEADS, B_HEAD_DIM, B_HEAD_DIM), lambda i: (i, 0, 0, 0, 0))
    vst = pl.BlockSpec((None, 2, B_HEADS, LANES), lambda i: (i, 0, 0, 0))
    return pl.pallas_call(
        functools.partial(_mlstm_kernel, conv2d=conv2d),
        grid=(b,),
        in_specs=[seq, seq, cst, vst, vst] + [_full(x.shape) for x in weights],
        out_specs=[seq, cst, vst, vst],
        out_shape=[jax.ShapeDtypeStruct((b, t, w), F32), jax.ShapeDtypeStruct(c0.shape, F32),
                   jax.ShapeDtypeStruct(n0.shape, F32), jax.ShapeDtypeStruct(m0.shape, F32)],
        scratch_shapes=[pltpu.VMEM((t + 2 * CONV_PAD, w), F32), pltpu.VMEM((t, w), F32),
                        pltpu.VMEM((t, w), F32), pltpu.VMEM((t, w), F32),
                        pltpu.VMEM((t, LANES), F32), pltpu.VMEM((t, LANES), F32),
                        pltpu.VMEM((t // CHUNK, 16, CHUNK), F32), pltpu.VMEM((t // CHUNK, 16, CHUNK), F32),
                        pltpu.VMEM((t, w), F32)],
        compiler_params=_cparams(("arbitrary",)),
        name="mlstm",
    )(ub, o_pre, c0, n0, m0, *weights)


def _first_lane_of_max(val, lane, valid):
    masked = jnp.where(valid, val, -jnp.inf)
    best = jnp.max(masked, axis=1, keepdims=True)
    idx = jnp.min(jnp.where(valid & (masked == best), lane, LANES), axis=1, keepdims=True)
    return best, idx


def _merge_kernel(x_ref, ya_ref, yb_ref, ga_ref, gb_ref, mod_ref, wpa_ref, wpb_ref, wo_ref, g2_ref,
                  wr_ref, br_ref, x1_ref, comb_ref):
    d = D_MODEL
    gate1 = mod_ref[:, 2 * d:3 * d]
    shift2, scale2 = mod_ref[:, 3 * d:4 * d], mod_ref[:, 4 * d:5 * d]
    merged = (_sigmoid(ga_ref[...]) * _mm(ya_ref[...], wpa_ref[...])
              + _sigmoid(gb_ref[...]) * _mm(yb_ref[...], wpb_ref[...]))
    x1 = x_ref[...] + gate1 * _mm(merged, wo_ref[...])
    x1_ref[...] = x1
    h2 = _rmsnorm(x1, g2_ref[...]) * (1.0 + scale2) + shift2
    logits = _mm(h2, wr_ref[...]) + br_ref[...]
    lane = _iota(logits.shape, 1)
    is_group = lane < N_GROUPS
    g_max, g_sel = _first_lane_of_max(logits, lane, is_group)
    g_w = 1.0 / jnp.sum(jnp.where(is_group, jnp.exp(logits - g_max), 0.0), axis=1, keepdims=True)
    expert = lane - N_GROUPS
    in_group = (expert >= 0) & (expert < N_EXPERTS) & (lax.shift_right_arithmetic(expert, 2) == g_sel)
    e_max = jnp.max(jnp.where(in_group, logits, -jnp.inf), axis=1, keepdims=True)
    e_exp = jnp.where(in_group, jnp.exp(logits - e_max), 0.0)
    prob = e_exp / jnp.sum(e_exp, axis=1, keepdims=True)
    p1, i1 = _first_lane_of_max(prob, lane, in_group)
    p2, i2 = _first_lane_of_max(prob, lane, in_group & (lane != i1))
    denom = p1 + p2
    comb = jnp.where(lane == i1, g_w * p1 / denom, 0.0) + jnp.where(lane == i2, g_w * p2 / denom, 0.0)
    comb_ref[...] = comb


def _merge(x, ya, yb, ga, gb, mod, p, tm):
    b, t, d = x.shape
    per = t // tm
    tok = lambda n: pl.BlockSpec((None, tm, n), lambda i: (i // per, i % per, 0))
    weights = [p["rwkv_w_proj"], p["mlstm_w_proj"], p["w_out"], p["norm2_g"], p["moe_w_router"], p["moe_b_router"]]
    return pl.pallas_call(
        _merge_kernel,
        grid=(b * per,),
        in_specs=[tok(d), tok(A_WIDTH), tok(B_WIDTH), tok(d), tok(d),
                  pl.BlockSpec((None, 1, 6 * d), lambda i: (i // per, 0, 0))] + [_full(w.shape) for w in weights],
        out_specs=[tok(d), tok(LANES)],
        out_shape=[jax.ShapeDtypeStruct((b, t, d), F32), jax.ShapeDtypeStruct((b, t, LANES), F32)],
        compiler_params=_cparams(("arbitrary",)),
        name="merge",
    )(x, ya, yb, ga, gb, mod, *weights)


def _moe_kernel(x1_ref, comb_ref, mod_ref, g2_ref, w1_ref, w3_ref, w2_ref, gf_ref, y_ref, h2_s, acc_s):
    d = D_MODEL
    e = pl.program_id(1)

    @pl.when(e == 0)
    def _():
        shift2, scale2 = mod_ref[:, 3 * d:4 * d], mod_ref[:, 4 * d:5 * d]
        h2_s[...] = (_rmsnorm(x1_ref[...], g2_ref[...]) * (1.0 + scale2) + shift2).astype(BF16)
        acc_s[...] = jnp.zeros_like(acc_s)

    comb = comb_ref[...]
    w_e = jnp.sum(jnp.where(_iota(comb.shape, 1) == e + N_GROUPS, comb, 0.0), axis=1, keepdims=True)
    h2 = h2_s[...]
    a = jnp.dot(h2, w1_ref[...], preferred_element_type=F32)
    b = jnp.dot(h2, w3_ref[...], preferred_element_type=F32)
    acc_s[...] += _mm(_silu(a) * b * w_e, w2_ref[...])

    @pl.when(e == N_EXPERTS - 1)
    def _():
        gate2 = mod_ref[:, 5 * d:6 * d]
        y_ref[...] = _rmsnorm(x1_ref[...] + gate2 * acc_s[...], gf_ref[...])


def _moe(x1, comb, mod, p, tm):
    b, t, d = x1.shape
    per = t // tm
    tok = lambda n: pl.BlockSpec((None, tm, n), lambda i, e: (i // per, i % per, 0))
    return pl.pallas_call(
        _moe_kernel,
        grid=(b * per, N_EXPERTS),
        in_specs=[tok(d), tok(LANES), pl.BlockSpec((None, 1, 6 * d), lambda i, e: (i // per, 0, 0)),
                  _full((1, d)),
                  pl.BlockSpec((None, d, D_EXPERT), lambda i, e: (e, 0, 0)),
                  pl.BlockSpec((None, d, D_EXPERT), lambda i, e: (e, 0, 0)),
                  pl.BlockSpec((None, D_EXPERT, d), lambda i, e: (e, 0, 0)),
                  _full((1, d))],
        out_specs=tok(d),
        out_shape=jax.ShapeDtypeStruct((b, t, d), F32),
        scratch_shapes=[pltpu.VMEM((tm, d), BF16), pltpu.VMEM((tm, d), F32)],
        compiler_params=_cparams(("arbitrary", "arbitrary")),
        name="moe",
    )(x1, comb, mod, p["norm2_g"], p["moe_w1"], p["moe_w3"], p["moe_w2"], p["final_norm_g"])


def _trunk(x, mod, s_rwkv, s_c, s_n, s_m, conv2d, p, tm):
    xa, ub, o_pre, ga, gb = _in_proj(x, mod, p["norm1_g"], p["w_in"], tm)
    ya, s_rwkv = _rwkv(xa, s_rwkv, p)
    m_in = jnp.broadcast_to(s_m[..., None], s_m.shape + (LANES,))
    yb, s_c, s_n, s_m = _mlstm(ub, o_pre, s_c, s_n, m_in, p, conv2d)
    x1, comb = _merge(x, ya, yb, ga, gb, mod, p, tm)
    y = _moe(x1, comb, mod, p, tm)
    return y, s_rwkv, s_c, s_n, s_m[..., 0]


def kernel(x_prompt, x_sample, c, c_ctx, state_rwkv, state_mlstm_C, state_mlstm_n, state_mlstm_m, w_mod, b_mod, norm1_g, norm2_g, w_in, rwkv_mu, rwkv_w_r, rwkv_w_k, rwkv_w_v, rwkv_w0, rwkv_w1, rwkv_w2, rwkv_a0, rwkv_a1, rwkv_a2, rwkv_g1, rwkv_g2, rwkv_k_k, rwkv_k_a, rwkv_r_k, rwkv_ln_g, rwkv_ln_b, rwkv_w_proj, mlstm_conv, mlstm_w_q, mlstm_w_k, mlstm_w_v, mlstm_w_i, mlstm_b_i, mlstm_w_f, mlstm_b_f, mlstm_ln_g, mlstm_w_proj, w_out, moe_w_group, moe_b_group, moe_w_expert, moe_b_expert, moe_w1, moe_w3, moe_w2, final_norm_g):
    assert w_mod.shape[0] == 1, "single trunk layer"
    l = 0
    bp, dec = x_prompt.shape[0], x_sample.shape[0]
    bf = lambda w: w.astype(BF16)
    row = lambda w: w.reshape(1, -1).astype(F32)
    wg = jnp.concatenate([mlstm_w_i[l, 0], mlstm_w_i[l, 1], mlstm_w_f[l, 0], mlstm_w_f[l, 1]], axis=1)
    bg = jnp.concatenate([mlstm_b_i[l, 0], mlstm_b_i[l, 1], mlstm_b_f[l, 0], mlstm_b_f[l, 1]])
    wg = jnp.pad(wg, ((0, 0), (0, LANES - wg.shape[1])))
    bg = jnp.pad(bg, (0, LANES - bg.shape[0]))
    w_router = jnp.pad(jnp.concatenate([moe_w_group[l], moe_w_expert[l]], axis=1),
                       ((0, 0), (0, LANES - N_GROUPS - N_EXPERTS)))
    b_router = jnp.pad(jnp.concatenate([moe_b_group[l], moe_b_expert[l]]), (0, LANES - N_GROUPS - N_EXPERTS))
    p = {
        "norm1_g": row(norm1_g[l]), "norm2_g": row(norm2_g[l]), "w_in": bf(w_in[l]),
        "rwkv_mu": rwkv_mu[l], "rwkv_wr": bf(rwkv_w_r[l]), "rwkv_wk": bf(rwkv_w_k[l]), "rwkv_wv": bf(rwkv_w_v[l]),
        "rwkv_w0": rwkv_w0[l], "rwkv_w1": bf(rwkv_w1[l]), "rwkv_w2": bf(rwkv_w2[l]),
        "rwkv_a0": rwkv_a0[l], "rwkv_a1": bf(rwkv_a1[l]), "rwkv_a2": bf(rwkv_a2[l]),
        "rwkv_g1": bf(rwkv_g1[l]), "rwkv_g2": bf(rwkv_g2[l]),
        "rwkv_k_k": row(rwkv_k_k[l]), "rwkv_k_a": row(rwkv_k_a[l]), "rwkv_r_k": row(rwkv_r_k[l]),
        "rwkv_ln_g": row(rwkv_ln_g[l]), "rwkv_ln_b": row(rwkv_ln_b[l]),
        "rwkv_w_proj": bf(rwkv_w_proj[l]),
        "mlstm_conv": mlstm_conv[l].reshape(9, B_WIDTH),
        "mlstm_wq": bf(mlstm_w_q[l]), "mlstm_wk": bf(mlstm_w_k[l]), "mlstm_wv": bf(mlstm_w_v[l]),
        "mlstm_wg": bf(wg), "mlstm_bg": row(bg),
        "mlstm_wgt": bf(wg[:, :16].T), "mlstm_bgt": jnp.broadcast_to(bg[:16, None], (16, LANES)),
        "mlstm_ln_g": row(mlstm_ln_g[l]), "mlstm_w_proj": bf(mlstm_w_proj[l]), "w_out": bf(w_out[l]),
        "moe_w_router": bf(w_router), "moe_b_router": row(b_router),
        "moe_w1": bf(moe_w1[l]), "moe_w3": bf(moe_w3[l]), "moe_w2": bf(moe_w2[l]),
        "final_norm_g": row(final_norm_g),
    }
    cvec = jnp.concatenate([c_ctx[None, :], c, jnp.zeros((8 - 1 - dec, D_MODEL), F32)], axis=0)
    mod = _mod(cvec, w_mod[l], b_mod[l].reshape(1, -1))
    mod_ctx = jnp.broadcast_to(mod[0:1][:, None, :], (bp, 1, 6 * D_MODEL))
    mod_lat = mod[1:1 + dec][:, None, :]

    zeros = lambda *s: jnp.zeros((bp,) + s, F32)
    yp, n_rwkv, n_c, n_n, n_m = _trunk(
        x_prompt, mod_ctx, zeros(2, A_HEADS, A_HEAD_DIM, A_HEAD_DIM), zeros(2, B_HEADS, B_HEAD_DIM, B_HEAD_DIM),
        zeros(2, B_HEADS, B_HEAD_DIM), zeros(2, B_HEADS), False, p, 256)
    ys, _, _, _, _ = _trunk(
        x_sample, mod_lat, state_rwkv[:, l].astype(F32), state_mlstm_C[:, l].astype(F32),
        state_mlstm_n[:, l].astype(F32), state_mlstm_m[:, l].astype(F32), True, p, 512)
    dt = x_prompt.dtype
    return (yp, ys, n_rwkv[:, None].astype(dt), n_c[:, None].astype(dt), n_n[:, None].astype(dt),
            n_m[:, None].astype(dt))
```

```python
import functools

import jax
import jax.numpy as jnp
from jax import lax
from jax.experimental import pallas as pl
from jax.experimental.pallas import tpu as pltpu

F32 = jnp.float32
BF16 = jnp.bfloat16

D_MODEL = 1024
GRID_W = 64
A_HEAD_DIM = 64
A_WIDTH = 512
A_HEADS = 8
B_HEAD_DIM = 128
B_WIDTH = 512
B_HEADS = 4
CHUNK = 64
ROW_TILE = 256
N_GROUPS = 4
EXPERTS_PER_GROUP = 4
N_EXPERTS = 16
D_EXPERT = 256
GROUP_LANE = N_GROUPS + N_EXPERTS
NORM_EPS = 1e-6
RWKV_GN_EPS = 64e-5
MLSTM_GN_EPS = 1e-5
LANES = 128
CONV_PAD = 72
VMEM_LIMIT = 56 * 1024 * 1024


def _mm(a, b):
    return jnp.dot(a.astype(BF16), b.astype(BF16), preferred_element_type=F32)


_NN, _NT, _TN = ((1,), (0,)), ((1,), (1,)), ((0,), (0,))


def _split(a):
    hi = a.astype(BF16)
    return hi, (a - hi.astype(F32)).astype(BF16)


def _split3(a):
    hi = a.astype(BF16)
    rest = a - hi.astype(F32)
    lo = rest.astype(BF16)
    return hi, lo, (rest - lo.astype(F32)).astype(BF16)


def _sum01(a, b, dims=_NN):
    dot = lambda p, q: lax.dot_general(p, q, (dims, ((), ())), preferred_element_type=F32)
    terms = [dot(x, b) for x in a] if isinstance(a, tuple) else [dot(a, x) for x in b]
    return functools.reduce(lambda p, q: p + q, terms)


def _mm3(a, b, dims=_NN):
    (ah, al), (bh, bl) = a, b
    dot = lambda p, q: lax.dot_general(p, q, (dims, ((), ())), preferred_element_type=F32)
    return dot(ah, bh) + dot(ah, bl) + dot(al, bh)


def _mm_nt(a, b):
    return lax.dot_general(a.astype(BF16), b.astype(BF16), (((1,), (1,)), ((), ())),
                           preferred_element_type=F32)


def _mm_tn(a, b):
    return lax.dot_general(a.astype(BF16), b.astype(BF16), (((0,), (0,)), ((), ())),
                           preferred_element_type=F32)


def _sigmoid(x):
    return 1.0 / (1.0 + jnp.exp(-x))


def _silu(x):
    return x * _sigmoid(x)


def _softplus(x):
    return jnp.maximum(x, 0.0) + jnp.log(1.0 + jnp.exp(-jnp.abs(x)))


def _iota(shape, dim):
    return lax.broadcasted_iota(jnp.int32, shape, dim)


def _tri(reverse, inclusive, n=CHUNK):
    row, col = _iota((n, n), 0), _iota((n, n), 1)
    if reverse:
        return (col >= row) if inclusive else (col > row)
    return (col <= row) if inclusive else (col < row)


def _same_block(n, blk):
    sh = blk.bit_length() - 1
    row, col = _iota((n, n), 0), _iota((n, n), 1)
    return lax.shift_right_logical(row, sh) == lax.shift_right_logical(col, sh)


def _rmsnorm(x, g):
    return x * lax.rsqrt(jnp.mean(x * x, axis=-1, keepdims=True) + NORM_EPS) * g


def _cparams(sem):
    return pltpu.CompilerParams(dimension_semantics=sem, vmem_limit_bytes=VMEM_LIMIT)


def _full(shape):
    nd = len(shape)
    return pl.BlockSpec(shape, lambda *_: (0,) * nd)


def _mod_kernel(c_ref, w_ref, b_ref, o_ref):
    o_ref[...] = _mm(_silu(c_ref[...]), w_ref[...]) + b_ref[...]


def _mod(cvec, w_mod, b_mod):
    rows, d = cvec.shape
    n = w_mod.shape[1]
    tn = 1536
    return pl.pallas_call(
        _mod_kernel,
        grid=(n // tn,),
        in_specs=[_full((rows, d)), pl.BlockSpec((d, tn), lambda j: (0, j)),
                  pl.BlockSpec((1, tn), lambda j: (0, j))],
        out_specs=pl.BlockSpec((rows, tn), lambda j: (0, j)),
        out_shape=jax.ShapeDtypeStruct((rows, n), F32),
        compiler_params=_cparams(("arbitrary",)),
        name="mod",
    )(cvec, w_mod, b_mod)


def _in_kernel(x_ref, mod_ref, g_ref, w_ref, xa_ref, ub_ref, op_ref, ga_ref, gb_ref):
    d = D_MODEL
    shift, scale = mod_ref[:, 0:d], mod_ref[:, d:2 * d]
    h = (_rmsnorm(x_ref[...], g_ref[...]) * (1.0 + scale) + shift).astype(BF16)
    col = 0
    for ref in (xa_ref, ub_ref, op_ref, ga_ref, gb_ref):
        n = ref.shape[-1]
        ref[...] = jnp.dot(h, w_ref[:, col:col + n], preferred_element_type=F32)
        col += n


def _in_proj(x, mod, g1, w_in, tm):
    b, t, d = x.shape
    per = t // tm
    widths = (A_WIDTH, B_WIDTH, B_WIDTH, D_MODEL, D_MODEL)
    tok = lambda n: pl.BlockSpec((None, tm, n), lambda i: (i // per, i % per, 0))
    return pl.pallas_call(
        _in_kernel,
        grid=(b * per,),
        in_specs=[tok(d), pl.BlockSpec((None, 1, 6 * d), lambda i: (i // per, 0, 0)),
                  _full((1, d)), _full(w_in.shape)],
        out_specs=[tok(n) for n in widths],
        out_shape=[jax.ShapeDtypeStruct((b, t, n), F32) for n in widths],
        compiler_params=_cparams(("arbitrary",)),
        name="in_proj",
    )(x, mod, g1, w_in)


def _unit_tri_inverses(mats):
    c = CHUNK
    eye = (_iota((c, c), 0) == _iota((c, c), 1)).astype(F32)
    diag8 = _same_block(c, 8)
    nd = [jnp.where(diag8, m, 0.0) for m in mats]
    nd_s = [_split(x) for x in nd]
    n2_s = [_split(_mm3(s, s)) for s in nd_s]
    n4_s = [_split(_mm3(s, s)) for s in n2_s]
    t = [eye + x for x in nd]
    t = [x + _mm3(_split(x), s) for x, s in zip(t, n2_s)]
    t = [x + _mm3(_split(x), s) for x, s in zip(t, n4_s)]
    blk = 8
    while blk < c:
        sel = _same_block(c, 2 * blk) & jnp.logical_not(_same_block(c, blk))
        t_s = [_split(x) for x in t]
        w_s = [_split(_mm3(_split(jnp.where(sel, m, 0.0)), s)) for m, s in zip(mats, t_s)]
        t = [x + _mm3(s, w) for x, s, w in zip(t, t_s, w_s)]
        blk *= 2
    return t


def _rwkv_kernel(x_ref, s0_ref, mu_ref, wr_ref, wk_ref, wv_ref, w0_ref, w1_ref, w2_ref,
                 a0_ref, a1_ref, a2_ref, g1_ref, g2_ref, kk_ref, ka_ref, rk_ref, lng_ref, lnb_ref,
                 y_ref, s_ref,
                 r_s, v_s, kk_s, cum_s, cx_s, kd_s, bb_s, g_s, bonus_s, y_s):
    t_len = x_ref.shape[0]
    n_chunks = t_len // CHUNK
    n_tiles = t_len // ROW_TILE
    hd = A_HEAD_DIM
    x = x_ref[...]
    row = _iota(x.shape, 0)
    prev = jnp.where(row == 0, 0.0, pltpu.roll(x, 1, 0))
    nxt = jnp.where(row == t_len - 1, 0.0, pltpu.roll(x, t_len - 1, 0))
    y_s[...] = 0.5 * (prev + nxt) - x
    head_ones = _same_block(A_WIDTH, hd).astype(BF16)
    chunk_diag = _same_block(ROW_TILE, CHUNK)
    run_sum = [(chunk_diag & _tri(d == 1, True, ROW_TILE)).astype(BF16) for d in range(2)]

    def project_tile(ti, carry):
        rows = pl.ds(pl.multiple_of(ti * ROW_TILE, ROW_TILE), ROW_TILE)
        x, xx = x_ref[rows, :], y_s[rows, :]
        xr, xw, xk, xv, xi, xg = (x + xx * mu_ref[j:j + 1, :] for j in range(6))
        r = _mm(xr, wr_ref[...])
        k = _mm(xk, wk_ref[...])
        v = _mm(xv, wv_ref[...])
        g_s[rows, :] = _mm(_sigmoid(_mm(xg, g1_ref[...])), g2_ref[...])
        kk = k * kk_ref[...]
        kk = kk / jnp.maximum(jnp.sqrt(_sum01(_split3(kk * kk), head_ones)), 1e-12)
        r_s[rows, :] = r
        v_s[rows, :] = v
        kk_s[rows, :] = kk
        bonus = jnp.zeros_like(x)
        for d in range(2):
            zw = w0_ref[d:d + 1, :] + _mm(jnp.tanh(_mm(xw, w1_ref[d])), w2_ref[d])
            lw = -jnp.exp(-_softplus(-zw) - 0.5)
            cum = _sum01(run_sum[d], _split3(lw))
            cum_s[d, rows, :] = cum
            cx_s[d, rows, :] = cum - lw
            a = _sigmoid(a0_ref[d:d + 1, :] + _mm(_mm(xi, a1_ref[d]), a2_ref[d]))
            kd = k * (1.0 + (a - 1.0) * ka_ref[...])
            kd_s[d, rows, :] = kd
            bb_s[d, rows, :] = kk * a
            bonus = bonus + _sum01(_split3(r * kd * rk_ref[...]), head_ones) * v
        bonus_s[rows, :] = bonus
        y_s[rows, :] = jnp.zeros_like(x)
        return carry

    lax.fori_loop(0, n_tiles, project_tile, 0)
    s_ref[...] = s0_ref[...]

    strict = [_tri(d == 1, False) for d in range(2)]
    incl = [_tri(d == 1, True) for d in range(2)]

    def chunk_body(ci, carry):
        chains = []
        for d in range(2):
            cpos = (n_chunks - 1 - ci) if d == 1 else ci
            last = 0 if d == 1 else CHUNK - 1
            rows = pl.ds(pl.multiple_of(cpos * CHUNK, CHUNK), CHUNK)
            cum, cx = cum_s[d, rows, :], cx_s[d, rows, :]
            tot = cum[last:last + 1, :]
            kdc, bbc = kd_s[d, rows, :], bb_s[d, rows, :]
            e_neg = jnp.exp(-cum)
            e_tail = jnp.exp(tot - cum)
            at = -kk_s[rows, :] * jnp.exp(cx)
            rt = r_s[rows, :] * jnp.exp(cum)
            bt, kt = bbc * e_neg, kdc * e_neg
            bp, kp = bbc * e_tail, kdc * e_tail
            pc = jnp.exp(tot)
            vc = v_s[rows, :]
            for h in range(A_HEADS):
                ln = slice(h * hd, (h + 1) * hd)
                chains.append(dict(d=d, h=h, rows=rows, ln=ln, v=vc[:, ln], pc=pc[:, ln],
                                   lhs=jnp.concatenate([at[:, ln], rt[:, ln]], axis=0),
                                   bt=bt[:, ln], kt=kt[:, ln], bp=bp[:, ln], kp=kp[:, ln]))
        for c in chains:
            lhs_s = _split(c["lhs"])
            c["sb"] = _mm3(lhs_s, _split(c["bt"]), _NT)
            c["sk"] = _mm3(lhs_s, _split(c["kt"]), _NT)
            c["s"] = s_ref[c["d"], c["h"]]
            c["fs"] = _mm_nt(c["lhs"], c["s"])
        for c in chains:
            c["x"] = c["fs"][:CHUNK] + _mm(jnp.where(strict[c["d"]], c["sk"][:CHUNK], 0.0), c["v"])
        inv = _unit_tri_inverses([jnp.where(strict[c["d"]], c["sb"][:CHUNK], 0.0) for c in chains])
        for c, t in zip(chains, inv):
            c["u"] = _mm3(_split(t), _split(c["x"]))
        for c in chains:
            d, rows, ln = c["d"], c["rows"], c["ln"]
            a_rb = jnp.where(incl[d], c["sb"][CHUNK:], 0.0)
            a_rk = jnp.where(incl[d], c["sk"][CHUNK:], 0.0)
            y = c["fs"][CHUNK:] + _mm(a_rb, c["u"]) + _mm(a_rk, c["v"])
            y_s[rows, ln] = y_s[rows, ln] + y
            s_ref[d, c["h"]] = c["s"] * c["pc"] + _mm_tn(c["u"], c["bp"]) + _mm_tn(c["v"], c["kp"])
        return carry

    lax.fori_loop(0, n_chunks, chunk_body, 0)

    def finish_tile(ti, carry):
        rows = pl.ds(pl.multiple_of(ti * ROW_TILE, ROW_TILE), ROW_TILE)
        ys = y_s[rows, :]
        cen = ys - _sum01(_split3(ys), head_ones) * (1.0 / hd)
        var = _sum01(_split3(cen * cen), head_ones) * (1.0 / hd)
        yn = cen * lax.rsqrt(var + RWKV_GN_EPS) * lng_ref[...] + lnb_ref[...]
        y_ref[rows, :] = (yn + bonus_s[rows, :]) * g_s[rows, :]
        return carry

    lax.fori_loop(0, n_tiles, finish_tile, 0)


def _rwkv(xa, s0, p):
    b, t, w = xa.shape
    names = ("mu", "wr", "wk", "wv", "w0", "w1", "w2", "a0", "a1", "a2", "g1", "g2",
             "k_k", "k_a", "r_k", "ln_g", "ln_b")
    weights = [p["rwkv_" + n] for n in names]
    seq = pl.BlockSpec((None, t, w), lambda i: (i, 0, 0))
    st = pl.BlockSpec((None, 2, A_HEADS, A_HEAD_DIM, A_HEAD_DIM), lambda i: (i, 0, 0, 0, 0))
    tw = lambda lead=(): pltpu.VMEM(lead + (t, w), F32)
    return pl.pallas_call(
        _rwkv_kernel,
        grid=(b,),
        in_specs=[seq, st] + [_full(x.shape) for x in weights],
        out_specs=[seq, st],
        out_shape=[jax.ShapeDtypeStruct((b, t, w), F32), jax.ShapeDtypeStruct(s0.shape, F32)],
        scratch_shapes=[tw(), tw(), tw(), tw((2,)), tw((2,)), tw((2,)), tw((2,)), tw(), tw(), tw()],
        compiler_params=_cparams(("arbitrary",)),
        name="rwkv",
    )(xa, s0, *weights)


def _mlstm_kernel(u_ref, op_ref, c0_ref, n0_ref, m0_ref, conv_ref, wq_ref, wk_ref, wv_ref,
                  wg_ref, bg_ref, wgt_ref, bgt_ref, lng_ref,
                  y_ref, c_ref, n_ref, m_ref,
                  pad_s, q_s, k_s, v_s, gc_s, cc_s, gr_s, cr_s, h_s, *, conv2d):
    t_len = u_ref.shape[0]
    n_chunks = t_len // CHUNK
    hd = B_HEAD_DIM
    u = u_ref[...]
    pad_s[0:CONV_PAD, :] = jnp.zeros((CONV_PAD, B_WIDTH), F32)
    pad_s[CONV_PAD + t_len:CONV_PAD + t_len + CONV_PAD, :] = jnp.zeros((CONV_PAD, B_WIDTH), F32)
    pad_s[CONV_PAD:CONV_PAD + t_len, :] = u
    col = _iota(u.shape, 0) & (GRID_W - 1)
    conv = jnp.zeros_like(u)
    for kh in range(3):
        if not conv2d and kh != 1:
            continue
        for kw in range(3):
            shift = (kh - 1) * GRID_W + (kw - 1)
            term = pad_s[CONV_PAD + shift:CONV_PAD + shift + t_len, :] * conv_ref[kh * 3 + kw:kh * 3 + kw + 1, :]
            if conv2d and kw != 1:
                src = col + (kw - 1)
                term = jnp.where((src >= 0) & (src < GRID_W), term, 0.0)
            conv = conv + term
    uc = _silu(conv)
    for h in range(B_HEADS):
        ln = slice(h * hd, (h + 1) * hd)
        q_s[:, ln] = _mm(uc[:, ln], wq_ref[h])
        k_s[:, ln] = _mm(uc[:, ln], wk_ref[h]) * (hd ** -0.5)
        v_s[:, ln] = _mm(u[:, ln], wv_ref[h])
    gcol = _mm(uc, wg_ref[...]) + bg_ref[...]
    gcol = jnp.where(_iota(gcol.shape, 1) >= 8, -_softplus(-gcol), gcol)
    gc_s[...] = gcol
    grow = _mm_nt(wgt_ref[...], uc) + bgt_ref[:, 0:1]
    grow = jnp.where(_iota(grow.shape, 0) >= 8, -_softplus(-grow), grow)
    chunk_diag = _same_block(ROW_TILE, CHUNK)
    run_sum = [(chunk_diag & _tri(d == 1, True, ROW_TILE)).astype(BF16) for d in range(2)]
    per_tile = ROW_TILE // CHUNK
    for ti in range(t_len // ROW_TILE):
        sl = slice(ti * ROW_TILE, (ti + 1) * ROW_TILE)
        gt, rt = _split3(gcol[sl, :]), _split3(grow[:, sl])
        backward_lane = (_iota((ROW_TILE, LANES), 1) & 4) != 0
        cc_s[sl, :] = jnp.where(backward_lane, _sum01(run_sum[1], gt, _NN), _sum01(run_sum[0], gt, _NN))
        backward_row = (_iota((16, ROW_TILE), 0) & 4) != 0
        cum_r = jnp.where(backward_row, _sum01(rt, run_sum[1], _NT), _sum01(rt, run_sum[0], _NT))
        for c in range(per_tile):
            gr_s[ti * per_tile + c] = grow[:, ti * ROW_TILE + c * CHUNK:ti * ROW_TILE + (c + 1) * CHUNK]
            cr_s[ti * per_tile + c] = cum_r[:, c * CHUNK:(c + 1) * CHUNK]
    h_s[...] = jnp.zeros_like(u)
    c_ref[...] = c0_ref[...]
    n_ref[...] = n0_ref[...]
    m_ref[...] = m0_ref[...]
    incl = [_tri(d == 1, True) for d in range(2)]

    def chunk_body(ci, carry):
        chains = []
        for d in range(2):
            cpos = (n_chunks - 1 - ci) if d == 1 else ci
            last = 0 if d == 1 else CHUNK - 1
            rows = pl.ds(pl.multiple_of(cpos * CHUNK, CHUNK), CHUNK)
            gc, cc, gr, cr = gc_s[rows, :], cc_s[rows, :], gr_s[cpos], cr_s[cpos]
            for h in range(B_HEADS):
                ln = slice(h * hd, (h + 1) * hd)
                li, lf = d * 4 + h, 8 + d * 4 + h
                chains.append(dict(d=d, h=h, rows=rows, ln=ln, last=last,
                                   b_col=cc[:, lf:lf + 1], i_col=gc[:, li:li + 1],
                                   b_row=cr[lf:lf + 1, :], i_row=gr[li:li + 1, :],
                                   m=m_ref[d, h:h + 1, 0:1], q=q_s[rows, ln], k=k_s[rows, ln], v=v_s[rows, ln],
                                   c=c_ref[d, h], n=n_ref[d, h:h + 1, :]))
        for c in chains:
            c["qk"] = _mm_nt(c["q"], c["k"])
            c["qc"] = _mm(c["q"], c["c"])
            c["logw"] = jnp.where(incl[c["d"]], c["b_col"] - c["b_row"] + c["i_row"], -jnp.inf)
        for c in chains:
            c["m_t"] = jnp.maximum(c["b_col"] + c["m"], jnp.max(c["logw"], axis=1, keepdims=True))
        for c in chains:
            c["s"] = c["qk"] * jnp.exp(c["logw"] - c["m_t"])
            c["inter"] = jnp.exp(c["b_col"] + c["m"] - c["m_t"])
        for c in chains:
            c["sv"] = _mm(c["s"], c["v"])
            last = c["last"]
            m_new = c["m_t"][last:last + 1, :]
            b_last = c["b_col"][last:last + 1, :]
            c["m_new"] = m_new
            c["decay"] = jnp.exp(b_last + c["m"] - m_new)
            c["kw"] = c["k"] * jnp.exp(b_last - c["b_col"] + c["i_col"] - m_new)
        for c in chains:
            c["kv"] = _mm_tn(c["kw"], c["v"])
            c["den"] = (c["inter"] * jnp.sum(c["q"] * c["n"], axis=1, keepdims=True)
                        + jnp.sum(c["s"], axis=1, keepdims=True))
        for c in chains:
            d, h, rows, ln = c["d"], c["h"], c["rows"], c["ln"]
            num = c["inter"] * c["qc"] + c["sv"]
            h_s[rows, ln] = h_s[rows, ln] + num / jnp.maximum(jnp.abs(c["den"]), jnp.exp(-c["m_t"]))
            c_ref[d, h] = c["decay"] * c["c"] + c["kv"]
            n_ref[d, h:h + 1, :] = c["decay"] * c["n"] + jnp.sum(c["kw"], axis=0, keepdims=True)
            m_ref[d, h:h + 1, :] = jnp.broadcast_to(c["m_new"], (1, LANES))
        return carry

    lax.fori_loop(0, n_chunks, chunk_body, 0)

    for h in range(B_HEADS):
        ln = slice(h * hd, (h + 1) * hd)
        hh = h_s[:, ln]
        cen = hh - jnp.mean(hh, axis=1, keepdims=True)
        var = jnp.mean(cen * cen, axis=1, keepdims=True)
        y_ref[:, ln] = _sigmoid(op_ref[:, ln]) * (cen * lax.rsqrt(var + MLSTM_GN_EPS) * lng_ref[:, ln])


def _mlstm(ub, o_pre, c0, n0, m0, p, conv2d):
    b, t, w = ub.shape
    names = ("conv", "wq", "wk", "wv", "wg", "bg", "wgt", "bgt", "ln_g")
    weights = [p["mlstm_" + n] for n in names]
    seq = pl.BlockSpec((None, t, w), lambda i: (i, 0, 0))
    cst = pl.BlockSpec((None, 2, B_HEADS, B_HEAD_DIM, B_HEAD_DIM), lambda i: (i, 0, 0, 0, 0))
    vst = pl.BlockSpec((None, 2, B_HEADS, LANES), lambda i: (i, 0, 0, 0))
    return pl.pallas_call(
        functools.partial(_mlstm_kernel, conv2d=conv2d),
        grid=(b,),
        in_specs=[seq, seq, cst, vst, vst] + [_full(x.shape) for x in weights],
        out_specs=[seq, cst, vst, vst],
        out_shape=[jax.ShapeDtypeStruct((b, t, w), F32), jax.ShapeDtypeStruct(c0.shape, F32),
                   jax.ShapeDtypeStruct(n0.shape, F32), jax.ShapeDtypeStruct(m0.shape, F32)],
        scratch_shapes=[pltpu.VMEM((t + 2 * CONV_PAD, w), F32), pltpu.VMEM((t, w), F32),
                        pltpu.VMEM((t, w), F32), pltpu.VMEM((t, w), F32),
                        pltpu.VMEM((t, LANES), F32), pltpu.VMEM((t, LANES), F32),
                        pltpu.VMEM((t // CHUNK, 16, CHUNK), F32), pltpu.VMEM((t // CHUNK, 16, CHUNK), F32),
                        pltpu.VMEM((t, w), F32)],
        compiler_params=_cparams(("arbitrary",)),
        name="mlstm",
    )(ub, o_pre, c0, n0, m0, *weights)


def _first_lane_of_max(val, lane, valid):
    masked = jnp.where(valid, val, -jnp.inf)
    best = jnp.max(masked, axis=1, keepdims=True)
    idx = jnp.min(jnp.where(valid & (masked == best), lane, LANES), axis=1, keepdims=True)
    return best, idx


def _merge_kernel(x_ref, ya_ref, yb_ref, ga_ref, gb_ref, mod_ref, wpa_ref, wpb_ref, wo_ref, g2_ref,
                  wr_ref, br_ref, xc_ref):
    d = D_MODEL
    gate1 = mod_ref[:, 2 * d:3 * d]
    shift2, scale2 = mod_ref[:, 3 * d:4 * d], mod_ref[:, 4 * d:5 * d]
    merged = (_sigmoid(ga_ref[...]) * _mm(ya_ref[...], wpa_ref[...])
              + _sigmoid(gb_ref[...]) * _mm(yb_ref[...], wpb_ref[...]))
    x1 = x_ref[...] + gate1 * _mm(merged, wo_ref[...])
    xc_ref[:, 0:d] = x1
    h2 = _rmsnorm(x1, g2_ref[...]) * (1.0 + scale2) + shift2
    logits = _mm(h2, wr_ref[...]) + br_ref[...]
    lane = _iota(logits.shape, 1)
    is_group = lane < N_GROUPS
    g_max, g_sel = _first_lane_of_max(logits, lane, is_group)
    g_w = 1.0 / jnp.sum(jnp.where(is_group, jnp.exp(logits - g_max), 0.0), axis=1, keepdims=True)
    expert = lane - N_GROUPS
    in_group = (expert >= 0) & (expert < N_EXPERTS) & (lax.shift_right_arithmetic(expert, EXPERTS_PER_GROUP.bit_length() - 1) == g_sel)
    e_max = jnp.max(jnp.where(in_group, logits, -jnp.inf), axis=1, keepdims=True)
    e_exp = jnp.where(in_group, jnp.exp(logits - e_max), 0.0)
    prob = e_exp / jnp.sum(e_exp, axis=1, keepdims=True)
    p1, i1 = _first_lane_of_max(prob, lane, in_group)
    p2, i2 = _first_lane_of_max(prob, lane, in_group & (lane != i1))
    denom = p1 + p2
    comb = jnp.where(lane == i1, g_w * p1 / denom, 0.0) + jnp.where(lane == i2, g_w * p2 / denom, 0.0)
    xc_ref[:, d:d + LANES] = jnp.where(lane == GROUP_LANE, g_sel.astype(F32), comb)


def _merge(x, ya, yb, ga, gb, mod, p, tm):
    b, t, d = x.shape
    per = t // tm
    tok = lambda n: pl.BlockSpec((None, tm, n), lambda i: (i // per, i % per, 0))
    weights = [p["rwkv_w_proj"], p["mlstm_w_proj"], p["w_out"], p["norm2_g"], p["moe_w_router"], p["moe_b_router"]]
    return pl.pallas_call(
        _merge_kernel,
        grid=(b * per,),
        in_specs=[tok(d), tok(A_WIDTH), tok(B_WIDTH), tok(d), tok(d),
                  pl.BlockSpec((None, 1, 6 * d), lambda i: (i // per, 0, 0))] + [_full(w.shape) for w in weights],
        out_specs=tok(d + LANES),
        out_shape=jax.ShapeDtypeStruct((b, t, d + LANES), F32),
        compiler_params=_cparams(("arbitrary",)),
        name="merge",
    )(x, ya, yb, ga, gb, mod, *weights)


def _route(gsel, tb):
    s, l = gsel.shape
    n_buckets = s * N_GROUPS
    max_tiles = s * (l // tb + N_GROUPS - 1)
    onehot = (gsel[..., None] == jnp.arange(N_GROUPS, dtype=jnp.int32)).astype(jnp.int32)
    rank = jnp.sum((jnp.cumsum(onehot, axis=1) - onehot) * onehot, axis=-1)
    n_tiles = ((jnp.sum(onehot, axis=1) + tb - 1) // tb).reshape(n_buckets)
    ends = jnp.cumsum(n_tiles)
    bucket = jnp.arange(s, dtype=jnp.int32)[:, None] * N_GROUPS + gsel
    pos = (ends - n_tiles)[bucket] * tb + rank
    tok = jnp.broadcast_to(jnp.arange(l, dtype=jnp.int32), (s, l))
    row_src = jnp.full((max_tiles * tb,), -1, jnp.int32).at[pos.reshape(-1)].set(tok.reshape(-1))
    tile_bucket = jnp.sum(jnp.arange(max_tiles, dtype=jnp.int32)[:, None] >= ends[None, :], axis=1)
    tile_bucket = jnp.minimum(tile_bucket, n_buckets - 1).astype(jnp.int32)
    return row_src, tile_bucket // N_GROUPS, tile_bucket % N_GROUPS, ends[-1:].astype(jnp.int32)


def _moe_kernel(src_ref, seg_ref, grp_ref, used_ref,
                xc_hbm, mod_ref, g2_ref, w1_ref, w3_ref, w2_ref, gf_ref, dump_in_hbm,
                y_hbm, dump_hbm, gbuf, obuf, gsem, ssem, *, tb):
    d = D_MODEL
    q = pl.program_id(0)
    n_used = used_ref[0]
    slot = lax.rem(q, 2)

    def start_gather(tile, sl):
        seg = seg_ref[tile]

        def body(r, carry):
            idx = jnp.maximum(src_ref[tile * tb + r], 0)
            pltpu.make_async_copy(xc_hbm.at[seg, pl.ds(idx, 1), :], gbuf.at[sl, pl.ds(r, 1), :],
                                  gsem.at[sl]).start()
            return carry

        lax.fori_loop(0, tb, body, 0, unroll=8)

    def wait_gather(sl):
        pltpu.make_async_copy(xc_hbm.at[0, pl.ds(0, tb), :], gbuf.at[sl], gsem.at[sl]).wait()

    def start_scatter(tile, sl):
        seg = seg_ref[tile]

        def body(r, carry):
            idx = src_ref[tile * tb + r]

            @pl.when(idx >= 0)
            def _():
                pltpu.make_async_copy(obuf.at[sl, pl.ds(r, 1), :], y_hbm.at[seg, pl.ds(idx, 1), :],
                                      ssem.at[sl]).start()

            @pl.when(idx < 0)
            def _():
                pltpu.make_async_copy(obuf.at[sl, pl.ds(r, 1), :], dump_hbm.at[sl, pl.ds(r, 1), :],
                                      ssem.at[sl]).start()

            return carry

        lax.fori_loop(0, tb, body, 0, unroll=8)

    def wait_scatter(sl):
        pltpu.make_async_copy(obuf.at[sl], dump_hbm.at[sl], ssem.at[sl]).wait()

    @pl.when(q < n_used)
    def _():
        @pl.when(q == 0)
        def _():
            start_gather(0, 0)

        @pl.when(q + 1 < n_used)
        def _():
            start_gather(q + 1, 1 - slot)

        wait_gather(slot)

        @pl.when(q >= 2)
        def _():
            wait_scatter(slot)

        rows = gbuf[slot]
        x1, comb = rows[:, 0:d], rows[:, d:d + LANES]
        shift2, scale2, gate2 = mod_ref[:, 3 * d:4 * d], mod_ref[:, 4 * d:5 * d], mod_ref[:, 5 * d:6 * d]
        h2 = (_rmsnorm(x1, g2_ref[...]) * (1.0 + scale2) + shift2).astype(BF16)
        first_lane = N_GROUPS + grp_ref[q] * EXPERTS_PER_GROUP
        lane = _iota(comb.shape, 1)
        acc = jnp.zeros((tb, d), F32)
        for e in range(EXPERTS_PER_GROUP):
            w_e = jnp.sum(jnp.where(lane == first_lane + e, comb, 0.0), axis=1, keepdims=True)
            a = jnp.dot(h2, w1_ref[e], preferred_element_type=F32)
            b = jnp.dot(h2, w3_ref[e], preferred_element_type=F32)
            acc = acc + _mm(_silu(a) * b * w_e, w2_ref[e])
        obuf[slot] = _rmsnorm(x1 + gate2 * acc, gf_ref[...])
        start_scatter(q, slot)

        @pl.when(q == n_used - 1)
        def _():
            @pl.when(q >= 1)
            def _():
                wait_scatter(1 - slot)

            wait_scatter(slot)


def _moe(xc, mod, p, tb):
    s, l, width = xc.shape
    d = D_MODEL
    gsel = xc[:, :, d + GROUP_LANE].astype(jnp.int32)
    row_src, tile_seg, tile_grp, n_used = _route(gsel, tb)
    max_tiles = tile_seg.shape[0]
    grouped = lambda w: w.reshape((N_GROUPS, EXPERTS_PER_GROUP) + w.shape[1:])
    w1, w3, w2 = grouped(p["moe_w1"]), grouped(p["moe_w3"]), grouped(p["moe_w2"])
    const = lambda shape: pl.BlockSpec(shape, lambda q, *_: (0,) * len(shape))
    by_group = lambda w: pl.BlockSpec((None,) + w.shape[1:], lambda q, src, seg, grp, used: (grp[q], 0, 0, 0))
    hbm = pl.BlockSpec(memory_space=pl.ANY)
    y, _ = pl.pallas_call(
        functools.partial(_moe_kernel, tb=tb),
        grid_spec=pltpu.PrefetchScalarGridSpec(
            num_scalar_prefetch=4,
            grid=(max_tiles,),
            in_specs=[hbm, pl.BlockSpec((None, 1, 6 * d), lambda q, src, seg, grp, used: (seg[q], 0, 0)),
                      const((1, d)), by_group(w1), by_group(w3), by_group(w2), const((1, d)), hbm],
            out_specs=[hbm, hbm],
            scratch_shapes=[pltpu.VMEM((2, tb, width), F32), pltpu.VMEM((2, tb, d), F32),
                            pltpu.SemaphoreType.DMA((2,)), pltpu.SemaphoreType.DMA((2,))]),
        out_shape=[jax.ShapeDtypeStruct((s, l, d), F32), jax.ShapeDtypeStruct((2, tb, d), F32)],
        input_output_aliases={11: 1},
        compiler_params=_cparams(("arbitrary",)),
        name="moe",
    )(row_src, tile_seg, tile_grp, n_used, xc, mod, p["norm2_g"], w1, w3, w2, p["final_norm_g"],
      jnp.zeros((2, tb, d), F32))
    return y


def _trunk(x, mod, mod_seg, s_rwkv, s_c, s_n, s_m, conv2d, p, tm, tb):
    b, t, d = x.shape
    xa, ub, o_pre, ga, gb = _in_proj(x, mod, p["norm1_g"], p["w_in"], tm)
    ya, s_rwkv = _rwkv(xa, s_rwkv, p)
    m_in = jnp.broadcast_to(s_m[..., None], s_m.shape + (LANES,))
    yb, s_c, s_n, s_m = _mlstm(ub, o_pre, s_c, s_n, m_in, p, conv2d)
    xc = _merge(x, ya, yb, ga, gb, mod, p, tm)
    n_seg = mod_seg.shape[0]
    y = _moe(xc.reshape(n_seg, b * t // n_seg, d + LANES), mod_seg, p, tb)
    return y.reshape(b, t, d), s_rwkv, s_c, s_n, s_m[..., 0]


def kernel(x_prompt, x_sample, c, c_ctx, state_rwkv, state_mlstm_C, state_mlstm_n, state_mlstm_m, w_mod, b_mod, norm1_g, norm2_g, w_in, rwkv_mu, rwkv_w_r, rwkv_w_k, rwkv_w_v, rwkv_w0, rwkv_w1, rwkv_w2, rwkv_a0, rwkv_a1, rwkv_a2, rwkv_g1, rwkv_g2, rwkv_k_k, rwkv_k_a, rwkv_r_k, rwkv_ln_g, rwkv_ln_b, rwkv_w_proj, mlstm_conv, mlstm_w_q, mlstm_w_k, mlstm_w_v, mlstm_w_i, mlstm_b_i, mlstm_w_f, mlstm_b_f, mlstm_ln_g, mlstm_w_proj, w_out, moe_w_group, moe_b_group, moe_w_expert, moe_b_expert, moe_w1, moe_w3, moe_w2, final_norm_g):
    assert w_mod.shape[0] == 1, "single trunk layer"
    l = 0
    bp, dec = x_prompt.shape[0], x_sample.shape[0]
    bf = lambda w: w.astype(BF16)
    row = lambda w: w.reshape(1, -1).astype(F32)
    wg = jnp.concatenate([mlstm_w_i[l, 0], mlstm_w_i[l, 1], mlstm_w_f[l, 0], mlstm_w_f[l, 1]], axis=1)
    bg = jnp.concatenate([mlstm_b_i[l, 0], mlstm_b_i[l, 1], mlstm_b_f[l, 0], mlstm_b_f[l, 1]])
    wg = jnp.pad(wg, ((0, 0), (0, LANES - wg.shape[1])))
    bg = jnp.pad(bg, (0, LANES - bg.shape[0]))
    w_router = jnp.pad(jnp.concatenate([moe_w_group[l], moe_w_expert[l]], axis=1),
                       ((0, 0), (0, LANES - N_GROUPS - N_EXPERTS)))
    b_router = jnp.pad(jnp.concatenate([moe_b_group[l], moe_b_expert[l]]), (0, LANES - N_GROUPS - N_EXPERTS))
    p = {
        "norm1_g": row(norm1_g[l]), "norm2_g": row(norm2_g[l]), "w_in": bf(w_in[l]),
        "rwkv_mu": rwkv_mu[l], "rwkv_wr": bf(rwkv_w_r[l]), "rwkv_wk": bf(rwkv_w_k[l]), "rwkv_wv": bf(rwkv_w_v[l]),
        "rwkv_w0": rwkv_w0[l], "rwkv_w1": bf(rwkv_w1[l]), "rwkv_w2": bf(rwkv_w2[l]),
        "rwkv_a0": rwkv_a0[l], "rwkv_a1": bf(rwkv_a1[l]), "rwkv_a2": bf(rwkv_a2[l]),
        "rwkv_g1": bf(rwkv_g1[l]), "rwkv_g2": bf(rwkv_g2[l]),
        "rwkv_k_k": row(rwkv_k_k[l]), "rwkv_k_a": row(rwkv_k_a[l]), "rwkv_r_k": row(rwkv_r_k[l]),
        "rwkv_ln_g": row(rwkv_ln_g[l]), "rwkv_ln_b": row(rwkv_ln_b[l]),
        "rwkv_w_proj": bf(rwkv_w_proj[l]),
        "mlstm_conv": mlstm_conv[l].reshape(9, B_WIDTH),
        "mlstm_wq": bf(mlstm_w_q[l]), "mlstm_wk": bf(mlstm_w_k[l]), "mlstm_wv": bf(mlstm_w_v[l]),
        "mlstm_wg": bf(wg), "mlstm_bg": row(bg),
        "mlstm_wgt": bf(wg[:, :16].T), "mlstm_bgt": jnp.broadcast_to(bg[:16, None], (16, LANES)),
        "mlstm_ln_g": row(mlstm_ln_g[l]), "mlstm_w_proj": bf(mlstm_w_proj[l]), "w_out": bf(w_out[l]),
        "moe_w_router": bf(w_router), "moe_b_router": row(b_router),
        "moe_w1": bf(moe_w1[l]), "moe_w3": bf(moe_w3[l]), "moe_w2": bf(moe_w2[l]),
        "final_norm_g": row(final_norm_g),
    }
    cvec = jnp.concatenate([c_ctx[None, :], c, jnp.zeros((8 - 1 - dec, D_MODEL), F32)], axis=0)
    mod = _mod(cvec, w_mod[l], b_mod[l].reshape(1, -1))
    mod_ctx = jnp.broadcast_to(mod[0:1][:, None, :], (bp, 1, 6 * D_MODEL))
    mod_lat = mod[1:1 + dec][:, None, :]

    zeros = lambda *s: jnp.zeros((bp,) + s, F32)
    yp, n_rwkv, n_c, n_n, n_m = _trunk(
        x_prompt, mod_ctx, mod[0:1][:, None, :],
        zeros(2, A_HEADS, A_HEAD_DIM, A_HEAD_DIM), zeros(2, B_HEADS, B_HEAD_DIM, B_HEAD_DIM),
        zeros(2, B_HEADS, B_HEAD_DIM), zeros(2, B_HEADS), False, p, 256, 512)
    ys, _, _, _, _ = _trunk(
        x_sample, mod_lat, mod_lat, state_rwkv[:, l].astype(F32), state_mlstm_C[:, l].astype(F32),
        state_mlstm_n[:, l].astype(F32), state_mlstm_m[:, l].astype(F32), True, p, 512, 256)
    dt = x_prompt.dtype
    return (yp, ys, n_rwkv[:, None].astype(dt), n_c[:, None].astype(dt), n_n[:, None].astype(dt),
            n_m[:, None].astype(dt))
```

```python
import functools

import jax
import jax.numpy as jnp
from jax import lax
from jax.experimental import pallas as pl
from jax.experimental.pallas import tpu as pltpu

F32 = jnp.float32
BF16 = jnp.bfloat16

D_MODEL = 1024
GRID_W = 64
A_HEAD_DIM = 64
A_WIDTH = 512
A_HEADS = 8
B_HEAD_DIM = 128
B_WIDTH = 512
B_HEADS = 4
CHUNK = 64
ROW_TILE = 256
N_GROUPS = 4
EXPERTS_PER_GROUP = 4
N_EXPERTS = 16
D_EXPERT = 256
GROUP_LANE = N_GROUPS + N_EXPERTS
NORM_EPS = 1e-6
RWKV_GN_EPS = 64e-5
MLSTM_GN_EPS = 1e-5
LANES = 128
CONV_PAD = 72
VMEM_LIMIT = 56 * 1024 * 1024


def _mm(a, b):
    return jnp.dot(a.astype(BF16), b.astype(BF16), preferred_element_type=F32)


_NN, _NT, _TN = ((1,), (0,)), ((1,), (1,)), ((0,), (0,))


def _split(a):
    hi = a.astype(BF16)
    return hi, (a - hi.astype(F32)).astype(BF16)


def _split3(a):
    hi = a.astype(BF16)
    rest = a - hi.astype(F32)
    lo = rest.astype(BF16)
    return hi, lo, (rest - lo.astype(F32)).astype(BF16)


def _sum01(a, b, dims=_NN):
    dot = lambda p, q: lax.dot_general(p, q, (dims, ((), ())), preferred_element_type=F32)
    terms = [dot(x, b) for x in a] if isinstance(a, tuple) else [dot(a, x) for x in b]
    return functools.reduce(lambda p, q: p + q, terms)


def _mm3(a, b, dims=_NN):
    (ah, al), (bh, bl) = a, b
    dot = lambda p, q: lax.dot_general(p, q, (dims, ((), ())), preferred_element_type=F32)
    return dot(ah, bh) + dot(ah, bl) + dot(al, bh)


def _mm_nt(a, b):
    return lax.dot_general(a.astype(BF16), b.astype(BF16), (((1,), (1,)), ((), ())),
                           preferred_element_type=F32)


def _mm_tn(a, b):
    return lax.dot_general(a.astype(BF16), b.astype(BF16), (((0,), (0,)), ((), ())),
                           preferred_element_type=F32)


def _sigmoid(x):
    return 1.0 / (1.0 + jnp.exp(-x))


def _silu(x):
    return x * _sigmoid(x)


def _softplus(x):
    return jnp.maximum(x, 0.0) + jnp.log(1.0 + jnp.exp(-jnp.abs(x)))


def _iota(shape, dim):
    return lax.broadcasted_iota(jnp.int32, shape, dim)


def _tri(reverse, inclusive, n=CHUNK):
    row, col = _iota((n, n), 0), _iota((n, n), 1)
    if reverse:
        return (col >= row) if inclusive else (col > row)
    return (col <= row) if inclusive else (col < row)


def _same_block(n, blk):
    sh = blk.bit_length() - 1
    row, col = _iota((n, n), 0), _iota((n, n), 1)
    return lax.shift_right_logical(row, sh) == lax.shift_right_logical(col, sh)


def _rmsnorm(x, g):
    return x * lax.rsqrt(jnp.mean(x * x, axis=-1, keepdims=True) + NORM_EPS) * g


def _cparams(sem):
    return pltpu.CompilerParams(dimension_semantics=sem, vmem_limit_bytes=VMEM_LIMIT)


def _full(shape):
    nd = len(shape)
    return pl.BlockSpec(shape, lambda *_: (0,) * nd)


def _mod_kernel(c_ref, w_ref, b_ref, o_ref):
    o_ref[...] = _mm(_silu(c_ref[...]), w_ref[...]) + b_ref[...]


def _mod(cvec, w_mod, b_mod):
    rows, d = cvec.shape
    n = w_mod.shape[1]
    tn = 1536
    return pl.pallas_call(
        _mod_kernel,
        grid=(n // tn,),
        in_specs=[_full((rows, d)), pl.BlockSpec((d, tn), lambda j: (0, j)),
                  pl.BlockSpec((1, tn), lambda j: (0, j))],
        out_specs=pl.BlockSpec((rows, tn), lambda j: (0, j)),
        out_shape=jax.ShapeDtypeStruct((rows, n), F32),
        compiler_params=_cparams(("arbitrary",)),
        name="mod",
    )(cvec, w_mod, b_mod)


def _in_kernel(x_ref, mod_ref, g_ref, w_ref, xa_ref, ub_ref, op_ref, ga_ref, gb_ref):
    d = D_MODEL
    shift, scale = mod_ref[:, 0:d], mod_ref[:, d:2 * d]
    h = (_rmsnorm(x_ref[...], g_ref[...]) * (1.0 + scale) + shift).astype(BF16)
    col = 0
    for ref in (xa_ref, ub_ref, op_ref, ga_ref, gb_ref):
        n = ref.shape[-1]
        ref[...] = jnp.dot(h, w_ref[:, col:col + n], preferred_element_type=F32)
        col += n


def _in_proj(x, mod, g1, w_in, tm):
    b, t, d = x.shape
    per = t // tm
    widths = (A_WIDTH, B_WIDTH, B_WIDTH, D_MODEL, D_MODEL)
    tok = lambda n: pl.BlockSpec((None, tm, n), lambda i: (i // per, i % per, 0))
    return pl.pallas_call(
        _in_kernel,
        grid=(b * per,),
        in_specs=[tok(d), pl.BlockSpec((None, 1, 6 * d), lambda i: (i // per, 0, 0)),
                  _full((1, d)), _full(w_in.shape)],
        out_specs=[tok(n) for n in widths],
        out_shape=[jax.ShapeDtypeStruct((b, t, n), F32) for n in widths],
        compiler_params=_cparams(("arbitrary",)),
        name="in_proj",
    )(x, mod, g1, w_in)


def _pair_diag(x):
    first = (_iota(x.shape, 1) < A_HEAD_DIM).astype(x.dtype)
    return jnp.concatenate([x * first, x * (1 - first)], axis=0)


def _pair_mm3(lhs, rhs, dims=_NN):
    rows = lhs[0].shape[0]
    rh, rl = _split(rhs)
    parts = [_split(l) for l in lhs]
    dot = lambda a, b: lax.dot_general(a, b, (dims, ((), ())), preferred_element_type=F32)
    top = dot(jnp.concatenate([t for part in parts for t in part], axis=0), _pair_diag(rh))
    low = dot(jnp.concatenate([part[0] for part in parts], axis=0), _pair_diag(rl))
    return [top[2 * k * rows:(2 * k + 1) * rows] + top[(2 * k + 1) * rows:(2 * k + 2) * rows]
            + low[k * rows:(k + 1) * rows] for k in range(len(lhs))]


def _pair_index(shape):
    return _iota(shape, 0), _iota(shape, 1) & (A_HEAD_DIM - 1)


def _pair_tri_inverses(mats):
    t_idx, s_idx = _pair_index((CHUNK, LANES))
    same = lambda blk: (lax.shift_right_logical(t_idx, blk.bit_length() - 1)
                        == lax.shift_right_logical(s_idx, blk.bit_length() - 1))
    eye = (t_idx == s_idx).astype(F32)
    nd = [jnp.where(same(8), m, 0.0) for m in mats]
    n2 = [_pair_mm3([x], x)[0] for x in nd]
    t = [eye + x for x in nd]
    both = [_pair_mm3([a, b], a) for a, b in zip(n2, t)]
    t = [x + r[1] for x, r in zip(t, both)]
    t = [x + _pair_mm3([x], r[0])[0] for x, r in zip(t, both)]
    blk = 8
    while blk < CHUNK:
        sel = same(2 * blk) & jnp.logical_not(same(blk))
        w = [_pair_mm3([jnp.where(sel, m, 0.0)], x)[0] for m, x in zip(mats, t)]
        t = [x + _pair_mm3([x], y)[0] for x, y in zip(t, w)]
        blk *= 2
    return t


def _rwkv_kernel(x_ref, s0_ref, mu_ref, wr_ref, wk_ref, wv_ref, w0_ref, w1_ref, w2_ref,
                 a0_ref, a1_ref, a2_ref, g1_ref, g2_ref, kk_ref, ka_ref, rk_ref, lng_ref, lnb_ref,
                 y_ref, s_ref,
                 r_s, v_s, kk_s, cum_s, cx_s, kd_s, bb_s, g_s, bonus_s, y_s):
    t_len = x_ref.shape[0]
    n_chunks = t_len // CHUNK
    n_tiles = t_len // ROW_TILE
    hd = A_HEAD_DIM
    x = x_ref[...]
    row = _iota(x.shape, 0)
    prev = jnp.where(row == 0, 0.0, pltpu.roll(x, 1, 0))
    nxt = jnp.where(row == t_len - 1, 0.0, pltpu.roll(x, t_len - 1, 0))
    y_s[...] = 0.5 * (prev + nxt) - x
    head_ones = _same_block(A_WIDTH, hd).astype(BF16)
    chunk_diag = _same_block(ROW_TILE, CHUNK)
    run_sum = [(chunk_diag & _tri(d == 1, True, ROW_TILE)).astype(BF16) for d in range(2)]

    def project_tile(ti, carry):
        rows = pl.ds(pl.multiple_of(ti * ROW_TILE, ROW_TILE), ROW_TILE)
        x, xx = x_ref[rows, :], y_s[rows, :]
        xr, xw, xk, xv, xi, xg = (x + xx * mu_ref[j:j + 1, :] for j in range(6))
        r = _mm(xr, wr_ref[...])
        k = _mm(xk, wk_ref[...])
        v = _mm(xv, wv_ref[...])
        g_s[rows, :] = _mm(_sigmoid(_mm(xg, g1_ref[...])), g2_ref[...])
        kk = k * kk_ref[...]
        kk = kk / jnp.maximum(jnp.sqrt(_sum01(_split3(kk * kk), head_ones)), 1e-12)
        r_s[rows, :] = r
        v_s[rows, :] = v
        kk_s[rows, :] = kk
        kd_sum = jnp.zeros_like(x)
        for d in range(2):
            zw = w0_ref[d:d + 1, :] + _mm(jnp.tanh(_mm(xw, w1_ref[d])), w2_ref[d])
            lw = -jnp.exp(-_softplus(-zw) - 0.5)
            cum = _sum01(run_sum[d], _split3(lw))
            cum_s[d, rows, :] = cum
            cx_s[d, rows, :] = cum - lw
            a = _sigmoid(a0_ref[d:d + 1, :] + _mm(_mm(xi, a1_ref[d]), a2_ref[d]))
            kd = k * (1.0 + (a - 1.0) * ka_ref[...])
            kd_s[d, rows, :] = kd
            bb_s[d, rows, :] = kk * a
            kd_sum = kd_sum + kd
        bonus_s[rows, :] = _sum01(_split3(r * kd_sum * rk_ref[...]), head_ones) * v
        y_s[rows, :] = jnp.zeros_like(x)
        return carry

    lax.fori_loop(0, n_tiles, project_tile, 0)
    s_ref[...] = s0_ref[...]

    t_idx, s_idx = _pair_index((CHUNK, LANES))
    strict = [s_idx < t_idx, s_idx > t_idx]
    incl = [s_idx <= t_idx, s_idx >= t_idx]
    first_head = _iota((CHUNK, LANES), 1) < hd

    def chunk_body(ci, carry):
        chains = []
        for d in range(2):
            cpos = (n_chunks - 1 - ci) if d == 1 else ci
            last = 0 if d == 1 else CHUNK - 1
            rows = pl.ds(pl.multiple_of(cpos * CHUNK, CHUNK), CHUNK)
            cum, cx = cum_s[d, rows, :], cx_s[d, rows, :]
            tot = cum[last:last + 1, :]
            kdc, bbc = kd_s[d, rows, :], bb_s[d, rows, :]
            e_neg = jnp.exp(-cum)
            e_tail = jnp.exp(tot - cum)
            at = -kk_s[rows, :] * jnp.exp(cx)
            rt = r_s[rows, :] * jnp.exp(cum)
            bt, kt = bbc * e_neg, kdc * e_neg
            bp, kp = bbc * e_tail, kdc * e_tail
            pc = jnp.exp(tot)
            vc = v_s[rows, :]
            for p in range(A_HEADS // 2):
                ln = slice(p * LANES, (p + 1) * LANES)
                chains.append(dict(d=d, p=p, rows=rows, ln=ln, v=vc[:, ln], pc=pc[:, ln],
                                   lhs=jnp.concatenate([at[:, ln], rt[:, ln]], axis=0),
                                   bt=bt[:, ln], kt=kt[:, ln], bp=bp[:, ln], kp=kp[:, ln]))
        for c in chains:
            c["sb"] = _pair_mm3([c["lhs"]], c["bt"], _NT)[0]
            c["sk"] = _pair_mm3([c["lhs"]], c["kt"], _NT)[0]
            c["s"] = s_ref[c["d"], c["p"]]
            c["fs"] = _mm_nt(c["lhs"], _pair_diag(c["s"].astype(BF16)))
            c["v_diag"] = _pair_diag(c["v"].astype(BF16))
        for c in chains:
            a_ak = jnp.where(strict[c["d"]], c["sk"][:CHUNK], 0.0)
            c["x"] = c["fs"][:CHUNK] + _mm(a_ak, c["v_diag"])
        inv = _pair_tri_inverses([jnp.where(strict[c["d"]], c["sb"][:CHUNK], 0.0) for c in chains])
        for c, t in zip(chains, inv):
            c["u"] = _pair_mm3([t], c["x"])[0]
        for c in chains:
            d, rows, ln, u = c["d"], c["rows"], c["ln"], c["u"]
            a_r = jnp.concatenate([jnp.where(incl[d], c["sb"][CHUNK:], 0.0),
                                   jnp.where(incl[d], c["sk"][CHUNK:], 0.0)], axis=1)
            uv_diag = jnp.concatenate([_pair_diag(u.astype(BF16)), c["v_diag"]], axis=0)
            y_s[rows, ln] = y_s[rows, ln] + c["fs"][CHUNK:] + _mm(a_r, uv_diag)
            g = _mm_tn(jnp.concatenate([u, c["v"]], axis=0), jnp.concatenate([c["bp"], c["kp"]], axis=0))
            s_ref[d, c["p"]] = c["s"] * c["pc"] + jnp.where(first_head, g[:CHUNK], g[CHUNK:])
        return carry

    lax.fori_loop(0, n_chunks, chunk_body, 0)

    def finish_tile(ti, carry):
        rows = pl.ds(pl.multiple_of(ti * ROW_TILE, ROW_TILE), ROW_TILE)
        ys = y_s[rows, :]
        cen = ys - _sum01(_split3(ys), head_ones) * (1.0 / hd)
        var = _sum01(_split3(cen * cen), head_ones) * (1.0 / hd)
        yn = cen * lax.rsqrt(var + RWKV_GN_EPS) * lng_ref[...] + lnb_ref[...]
        y_ref[rows, :] = (yn + bonus_s[rows, :]) * g_s[rows, :]
        return carry

    lax.fori_loop(0, n_tiles, finish_tile, 0)


def _rwkv(xa, s0, p):
    b, t, w = xa.shape
    names = ("mu", "wr", "wk", "wv", "w0", "w1", "w2", "a0", "a1", "a2", "g1", "g2",
             "k_k", "k_a", "r_k", "ln_g", "ln_b")
    weights = [p["rwkv_" + n] for n in names]
    seq = pl.BlockSpec((None, t, w), lambda i: (i, 0, 0))
    pairs, hd = A_HEADS // 2, A_HEAD_DIM
    st = pl.BlockSpec((None, 2, pairs, hd, LANES), lambda i: (i, 0, 0, 0, 0))
    tw = lambda lead=(): pltpu.VMEM(lead + (t, w), F32)
    s0_pairs = s0.reshape(b, 2, pairs, 2, hd, hd).transpose(0, 1, 2, 4, 3, 5).reshape(b, 2, pairs, hd, LANES)
    y, s_pairs = pl.pallas_call(
        _rwkv_kernel,
        grid=(b,),
        in_specs=[seq, st] + [_full(x.shape) for x in weights],
        out_specs=[seq, st],
        out_shape=[jax.ShapeDtypeStruct((b, t, w), F32), jax.ShapeDtypeStruct(s0_pairs.shape, F32)],
        scratch_shapes=[tw(), tw(), tw(), tw((2,)), tw((2,)), tw((2,)), tw((2,)), tw(), tw(), tw()],
        compiler_params=_cparams(("arbitrary",)),
        name="rwkv",
    )(xa, s0_pairs, *weights)
    s_out = s_pairs.reshape(b, 2, pairs, hd, 2, hd).transpose(0, 1, 2, 4, 3, 5).reshape(s0.shape)
    return y, s_out


def _mlstm_kernel(u_ref, op_ref, c0_ref, n0_ref, m0_ref, conv_ref, wq_ref, wk_ref, wv_ref,
                  wg_ref, bg_ref, wgt_ref, bgt_ref, lng_ref,
                  y_ref, c_ref, n_ref, m_ref,
                  pad_s, q_s, k_s, v_s, gc_s, cc_s, gr_s, cr_s, h_s, *, conv2d):
    t_len = u_ref.shape[0]
    n_chunks = t_len // CHUNK
    hd = B_HEAD_DIM
    u = u_ref[...]
    pad_s[0:CONV_PAD, :] = jnp.zeros((CONV_PAD, B_WIDTH), F32)
    pad_s[CONV_PAD + t_len:CONV_PAD + t_len + CONV_PAD, :] = jnp.zeros((CONV_PAD, B_WIDTH), F32)
    pad_s[CONV_PAD:CONV_PAD + t_len, :] = u
    col = _iota(u.shape, 0) & (GRID_W - 1)
    conv = jnp.zeros_like(u)
    for kh in range(3):
        if not conv2d and kh != 1:
            continue
        for kw in range(3):
            shift = (kh - 1) * GRID_W + (kw - 1)
            term = pad_s[CONV_PAD + shift:CONV_PAD + shift + t_len, :] * conv_ref[kh * 3 + kw:kh * 3 + kw + 1, :]
            if conv2d and kw != 1:
                src = col + (kw - 1)
                term = jnp.where((src >= 0) & (src < GRID_W), term, 0.0)
            conv = conv + term
    uc = _silu(conv)
    for h in range(B_HEADS):
        ln = slice(h * hd, (h + 1) * hd)
        q_s[:, ln] = _mm(uc[:, ln], wq_ref[h])
        k_s[:, ln] = _mm(uc[:, ln], wk_ref[h]) * (hd ** -0.5)
        v_s[:, ln] = _mm(u[:, ln], wv_ref[h])
    gcol = _mm(uc, wg_ref[...]) + bg_ref[...]
    gcol = jnp.where(_iota(gcol.shape, 1) >= 8, -_softplus(-gcol), gcol)
    gc_s[...] = gcol
    grow = _mm_nt(wgt_ref[...], uc) + bgt_ref[:, 0:1]
    grow = jnp.where(_iota(grow.shape, 0) >= 8, -_softplus(-grow), grow)
    chunk_diag = _same_block(ROW_TILE, CHUNK)
    run_sum = [(chunk_diag & _tri(d == 1, True, ROW_TILE)).astype(BF16) for d in range(2)]
    per_tile = ROW_TILE // CHUNK
    for ti in range(t_len // ROW_TILE):
        sl = slice(ti * ROW_TILE, (ti + 1) * ROW_TILE)
        gt, rt = _split3(gcol[sl, :]), _split3(grow[:, sl])
        backward_lane = (_iota((ROW_TILE, LANES), 1) & 4) != 0
        cc_s[sl, :] = jnp.where(backward_lane, _sum01(run_sum[1], gt, _NN), _sum01(run_sum[0], gt, _NN))
        backward_row = (_iota((16, ROW_TILE), 0) & 4) != 0
        cum_r = jnp.where(backward_row, _sum01(rt, run_sum[1], _NT), _sum01(rt, run_sum[0], _NT))
        for c in range(per_tile):
            gr_s[ti * per_tile + c] = grow[:, ti * ROW_TILE + c * CHUNK:ti * ROW_TILE + (c + 1) * CHUNK]
            cr_s[ti * per_tile + c] = cum_r[:, c * CHUNK:(c + 1) * CHUNK]
    h_s[...] = jnp.zeros_like(u)
    c_ref[...] = c0_ref[...]
    n_ref[...] = n0_ref[...]
    m_ref[...] = m0_ref[...]
    incl = [_tri(d == 1, True) for d in range(2)]

    def chunk_body(ci, carry):
        chains = []
        for d in range(2):
            cpos = (n_chunks - 1 - ci) if d == 1 else ci
            last = 0 if d == 1 else CHUNK - 1
            rows = pl.ds(pl.multiple_of(cpos * CHUNK, CHUNK), CHUNK)
            gc, cc, gr, cr = gc_s[rows, :], cc_s[rows, :], gr_s[cpos], cr_s[cpos]
            for h in range(B_HEADS):
                ln = slice(h * hd, (h + 1) * hd)
                li, lf = d * 4 + h, 8 + d * 4 + h
                chains.append(dict(d=d, h=h, rows=rows, ln=ln, last=last,
                                   b_col=cc[:, lf:lf + 1], i_col=gc[:, li:li + 1],
                                   b_row=cr[lf:lf + 1, :], i_row=gr[li:li + 1, :],
                                   m=m_ref[d, h:h + 1, 0:1], q=q_s[rows, ln], k=k_s[rows, ln], v=v_s[rows, ln],
                                   c=c_ref[d, h], n=n_ref[d, h:h + 1, :]))
        for c in chains:
            c["qk"] = _mm_nt(c["q"], c["k"])
            c["qc"] = _mm(c["q"], c["c"])
            c["logw"] = jnp.where(incl[c["d"]], c["b_col"] - c["b_row"] + c["i_row"], -jnp.inf)
        for c in chains:
            c["m_t"] = jnp.maximum(c["b_col"] + c["m"], jnp.max(c["logw"], axis=1, keepdims=True))
        for c in chains:
            c["s"] = c["qk"] * jnp.exp(c["logw"] - c["m_t"])
            c["inter"] = jnp.exp(c["b_col"] + c["m"] - c["m_t"])
        for c in chains:
            c["sv"] = _mm(c["s"], c["v"])
            last = c["last"]
            m_new = c["m_t"][last:last + 1, :]
            b_last = c["b_col"][last:last + 1, :]
            c["m_new"] = m_new
            c["decay"] = jnp.exp(b_last + c["m"] - m_new)
            c["kw"] = c["k"] * jnp.exp(b_last - c["b_col"] + c["i_col"] - m_new)
        for c in chains:
            c["kv"] = _mm_tn(c["kw"], c["v"])
            c["den"] = (c["inter"] * jnp.sum(c["q"] * c["n"], axis=1, keepdims=True)
                        + jnp.sum(c["s"], axis=1, keepdims=True))
        for c in chains:
            d, h, rows, ln = c["d"], c["h"], c["rows"], c["ln"]
            num = c["inter"] * c["qc"] + c["sv"]
            h_s[rows, ln] = h_s[rows, ln] + num / jnp.maximum(jnp.abs(c["den"]), jnp.exp(-c["m_t"]))
            c_ref[d, h] = c["decay"] * c["c"] + c["kv"]
            n_ref[d, h:h + 1, :] = c["decay"] * c["n"] + jnp.sum(c["kw"], axis=0, keepdims=True)
            m_ref[d, h:h + 1, :] = jnp.broadcast_to(c["m_new"], (1, LANES))
        return carry

    lax.fori_loop(0, n_chunks, chunk_body, 0)

    for h in range(B_HEADS):
        ln = slice(h * hd, (h + 1) * hd)
        hh = h_s[:, ln]
        cen = hh - jnp.mean(hh, axis=1, keepdims=True)
        var = jnp.mean(cen * cen, axis=1, keepdims=True)
        y_ref[:, ln] = _sigmoid(op_ref[:, ln]) * (cen * lax.rsqrt(var + MLSTM_GN_EPS) * lng_ref[:, ln])


def _mlstm(ub, o_pre, c0, n0, m0, p, conv2d):
    b, t, w = ub.shape
    names = ("conv", "wq", "wk", "wv", "wg", "bg", "wgt", "bgt", "ln_g")
    weights = [p["mlstm_" + n] for n in names]
    seq = pl.BlockSpec((None, t, w), lambda i: (i, 0, 0))
    cst = pl.BlockSpec((None, 2, B_HEADS, B_HEAD_DIM, B_HEAD_DIM), lambda i: (i, 0, 0, 0, 0))
    vst = pl.BlockSpec((None, 2, B_HEADS, LANES), lambda i: (i, 0, 0, 0))
    return pl.pallas_call(
        functools.partial(_mlstm_kernel, conv2d=conv2d),
        grid=(b,),
        in_specs=[seq, seq, cst, vst, vst] + [_full(x.shape) for x in weights],
        out_specs=[seq, cst, vst, vst],
        out_shape=[jax.ShapeDtypeStruct((b, t, w), F32), jax.ShapeDtypeStruct(c0.shape, F32),
                   jax.ShapeDtypeStruct(n0.shape, F32), jax.ShapeDtypeStruct(m0.shape, F32)],
        scratch_shapes=[pltpu.VMEM((t + 2 * CONV_PAD, w), F32), pltpu.VMEM((t, w), F32),
                        pltpu.VMEM((t, w), F32), pltpu.VMEM((t, w), F32),
                        pltpu.VMEM((t, LANES), F32), pltpu.VMEM((t, LANES), F32),
                        pltpu.VMEM((t // CHUNK, 16, CHUNK), F32), pltpu.VMEM((t // CHUNK, 16, CHUNK), F32),
                        pltpu.VMEM((t, w), F32)],
        compiler_params=_cparams(("arbitrary",)),
        name="mlstm",
    )(ub, o_pre, c0, n0, m0, *weights)


def _first_lane_of_max(val, lane, valid):
    masked = jnp.where(valid, val, -jnp.inf)
    best = jnp.max(masked, axis=1, keepdims=True)
    idx = jnp.min(jnp.where(valid & (masked == best), lane, LANES), axis=1, keepdims=True)
    return best, idx


def _merge_kernel(x_ref, ya_ref, yb_ref, ga_ref, gb_ref, mod_ref, wpa_ref, wpb_ref, wo_ref, g2_ref,
                  wr_ref, br_ref, xc_ref):
    d = D_MODEL
    gate1 = mod_ref[:, 2 * d:3 * d]
    shift2, scale2 = mod_ref[:, 3 * d:4 * d], mod_ref[:, 4 * d:5 * d]
    merged = (_sigmoid(ga_ref[...]) * _mm(ya_ref[...], wpa_ref[...])
              + _sigmoid(gb_ref[...]) * _mm(yb_ref[...], wpb_ref[...]))
    x1 = x_ref[...] + gate1 * _mm(merged, wo_ref[...])
    xc_ref[:, 0:d] = x1
    h2 = _rmsnorm(x1, g2_ref[...]) * (1.0 + scale2) + shift2
    logits = _mm(h2, wr_ref[...]) + br_ref[...]
    lane = _iota(logits.shape, 1)
    is_group = lane < N_GROUPS
    g_max, g_sel = _first_lane_of_max(logits, lane, is_group)
    g_w = 1.0 / jnp.sum(jnp.where(is_group, jnp.exp(logits - g_max), 0.0), axis=1, keepdims=True)
    expert = lane - N_GROUPS
    in_group = (expert >= 0) & (expert < N_EXPERTS) & (lax.shift_right_arithmetic(expert, EXPERTS_PER_GROUP.bit_length() - 1) == g_sel)
    e_max = jnp.max(jnp.where(in_group, logits, -jnp.inf), axis=1, keepdims=True)
    e_exp = jnp.where(in_group, jnp.exp(logits - e_max), 0.0)
    prob = e_exp / jnp.sum(e_exp, axis=1, keepdims=True)
    p1, i1 = _first_lane_of_max(prob, lane, in_group)
    p2, i2 = _first_lane_of_max(prob, lane, in_group & (lane != i1))
    denom = p1 + p2
    comb = jnp.where(lane == i1, g_w * p1 / denom, 0.0) + jnp.where(lane == i2, g_w * p2 / denom, 0.0)
    xc_ref[:, d:d + LANES] = jnp.where(lane == GROUP_LANE, g_sel.astype(F32), comb)


def _merge(x, ya, yb, ga, gb, mod, p, tm):
    b, t, d = x.shape
    per = t // tm
    tok = lambda n: pl.BlockSpec((None, tm, n), lambda i: (i // per, i % per, 0))
    weights = [p["rwkv_w_proj"], p["mlstm_w_proj"], p["w_out"], p["norm2_g"], p["moe_w_router"], p["moe_b_router"]]
    return pl.pallas_call(
        _merge_kernel,
        grid=(b * per,),
        in_specs=[tok(d), tok(A_WIDTH), tok(B_WIDTH), tok(d), tok(d),
                  pl.BlockSpec((None, 1, 6 * d), lambda i: (i // per, 0, 0))] + [_full(w.shape) for w in weights],
        out_specs=tok(d + LANES),
        out_shape=jax.ShapeDtypeStruct((b, t, d + LANES), F32),
        compiler_params=_cparams(("arbitrary",)),
        name="merge",
    )(x, ya, yb, ga, gb, mod, *weights)


def _route(gsel, tb):
    s, l = gsel.shape
    n_buckets = s * N_GROUPS
    max_tiles = s * (l // tb + N_GROUPS - 1)
    onehot = (gsel[..., None] == jnp.arange(N_GROUPS, dtype=jnp.int32)).astype(jnp.int32)
    rank = jnp.sum((jnp.cumsum(onehot, axis=1) - onehot) * onehot, axis=-1)
    n_tiles = ((jnp.sum(onehot, axis=1) + tb - 1) // tb).reshape(n_buckets)
    ends = jnp.cumsum(n_tiles)
    first_tile = jnp.sum(onehot * (ends - n_tiles).reshape(s, 1, N_GROUPS), axis=-1)
    pos = first_tile * tb + rank
    tok = jnp.broadcast_to(jnp.arange(l, dtype=jnp.int32), (s, l))
    row_src = jnp.full((max_tiles * tb,), -1, jnp.int32).at[pos.reshape(-1)].set(tok.reshape(-1))
    tile_bucket = jnp.sum(jnp.arange(max_tiles, dtype=jnp.int32)[:, None] >= ends[None, :], axis=1)
    tile_bucket = jnp.minimum(tile_bucket, n_buckets - 1).astype(jnp.int32)
    return row_src, tile_bucket // N_GROUPS, tile_bucket % N_GROUPS, ends[-1:].astype(jnp.int32)


def _moe_kernel(src_ref, seg_ref, grp_ref, used_ref,
                xc_hbm, mod_ref, g2_ref, w1_ref, w3_ref, w2_ref, gf_ref, dump_in_hbm,
                y_hbm, dump_hbm, gbuf, obuf, gsem, ssem, *, tb):
    d = D_MODEL
    q = pl.program_id(0)
    n_used = used_ref[0]
    slot = lax.rem(q, 2)

    def start_gather(tile, sl):
        seg = seg_ref[tile]

        def body(r, carry):
            idx = jnp.maximum(src_ref[tile * tb + r], 0)
            pltpu.make_async_copy(xc_hbm.at[seg, pl.ds(idx, 1), :], gbuf.at[sl, pl.ds(r, 1), :],
                                  gsem.at[sl]).start()
            return carry

        lax.fori_loop(0, tb, body, 0, unroll=8)

    def wait_gather(sl):
        pltpu.make_async_copy(xc_hbm.at[0, pl.ds(0, tb), :], gbuf.at[sl], gsem.at[sl]).wait()

    def start_scatter(tile, sl):
        seg = seg_ref[tile]

        def body(r, carry):
            idx = src_ref[tile * tb + r]

            @pl.when(idx >= 0)
            def _():
                pltpu.make_async_copy(obuf.at[sl, pl.ds(r, 1), :], y_hbm.at[seg, pl.ds(idx, 1), :],
                                      ssem.at[sl]).start()

            @pl.when(idx < 0)
            def _():
                pltpu.make_async_copy(obuf.at[sl, pl.ds(r, 1), :], dump_hbm.at[sl, pl.ds(r, 1), :],
                                      ssem.at[sl]).start()

            return carry

        lax.fori_loop(0, tb, body, 0, unroll=8)

    def wait_scatter(sl):
        pltpu.make_async_copy(obuf.at[sl], dump_hbm.at[sl], ssem.at[sl]).wait()

    @pl.when(q < n_used)
    def _():
        @pl.when(q == 0)
        def _():
            start_gather(0, 0)

        @pl.when(q + 1 < n_used)
        def _():
            start_gather(q + 1, 1 - slot)

        wait_gather(slot)

        @pl.when(q >= 2)
        def _():
            wait_scatter(slot)

        rows = gbuf[slot]
        x1, comb = rows[:, 0:d], rows[:, d:d + LANES]
        shift2, scale2, gate2 = mod_ref[:, 3 * d:4 * d], mod_ref[:, 4 * d:5 * d], mod_ref[:, 5 * d:6 * d]
        h2 = (_rmsnorm(x1, g2_ref[...]) * (1.0 + scale2) + shift2).astype(BF16)
        first_lane = N_GROUPS + grp_ref[q] * EXPERTS_PER_GROUP
        lane = _iota(comb.shape, 1)
        acc = jnp.zeros((tb, d), F32)
        for e in range(EXPERTS_PER_GROUP):
            w_e = jnp.sum(jnp.where(lane == first_lane + e, comb, 0.0), axis=1, keepdims=True)
            a = jnp.dot(h2, w1_ref[e], preferred_element_type=F32)
            b = jnp.dot(h2, w3_ref[e], preferred_element_type=F32)
            acc = acc + _mm(_silu(a) * b * w_e, w2_ref[e])
        obuf[slot] = _rmsnorm(x1 + gate2 * acc, gf_ref[...])
        start_scatter(q, slot)

        @pl.when(q == n_used - 1)
        def _():
            @pl.when(q >= 1)
            def _():
                wait_scatter(1 - slot)

            wait_scatter(slot)


def _moe(xc, mod, p, tb):
    s, l, width = xc.shape
    d = D_MODEL
    gsel = xc[:, :, d + GROUP_LANE].astype(jnp.int32)
    row_src, tile_seg, tile_grp, n_used = _route(gsel, tb)
    max_tiles = tile_seg.shape[0]
    grouped = lambda w: w.reshape((N_GROUPS, EXPERTS_PER_GROUP) + w.shape[1:])
    w1, w3, w2 = grouped(p["moe_w1"]), grouped(p["moe_w3"]), grouped(p["moe_w2"])
    const = lambda shape: pl.BlockSpec(shape, lambda q, *_: (0,) * len(shape))
    by_group = lambda w: pl.BlockSpec((None,) + w.shape[1:], lambda q, src, seg, grp, used: (grp[q], 0, 0, 0))
    hbm = pl.BlockSpec(memory_space=pl.ANY)
    y, _ = pl.pallas_call(
        functools.partial(_moe_kernel, tb=tb),
        grid_spec=pltpu.PrefetchScalarGridSpec(
            num_scalar_prefetch=4,
            grid=(max_tiles,),
            in_specs=[hbm, pl.BlockSpec((None, 1, 6 * d), lambda q, src, seg, grp, used: (seg[q], 0, 0)),
                      const((1, d)), by_group(w1), by_group(w3), by_group(w2), const((1, d)), hbm],
            out_specs=[hbm, hbm],
            scratch_shapes=[pltpu.VMEM((2, tb, width), F32), pltpu.VMEM((2, tb, d), F32),
                            pltpu.SemaphoreType.DMA((2,)), pltpu.SemaphoreType.DMA((2,))]),
        out_shape=[jax.ShapeDtypeStruct((s, l, d), F32), jax.ShapeDtypeStruct((2, tb, d), F32)],
        input_output_aliases={11: 1},
        compiler_params=_cparams(("arbitrary",)),
        name="moe",
    )(row_src, tile_seg, tile_grp, n_used, xc, mod, p["norm2_g"], w1, w3, w2, p["final_norm_g"],
      jnp.zeros((2, tb, d), F32))
    return y


def _trunk(x, mod, mod_seg, s_rwkv, s_c, s_n, s_m, conv2d, p, tm, tb):
    b, t, d = x.shape
    xa, ub, o_pre, ga, gb = _in_proj(x, mod, p["norm1_g"], p["w_in"], tm)
    ya, s_rwkv = _rwkv(xa, s_rwkv, p)
    m_in = jnp.broadcast_to(s_m[..., None], s_m.shape + (LANES,))
    yb, s_c, s_n, s_m = _mlstm(ub, o_pre, s_c, s_n, m_in, p, conv2d)
    xc = _merge(x, ya, yb, ga, gb, mod, p, tm)
    n_seg = mod_seg.shape[0]
    y = _moe(xc.reshape(n_seg, b * t // n_seg, d + LANES), mod_seg, p, tb)
    return y.reshape(b, t, d), s_rwkv, s_c, s_n, s_m[..., 0]


def kernel(x_prompt, x_sample, c, c_ctx, state_rwkv, state_mlstm_C, state_mlstm_n, state_mlstm_m, w_mod, b_mod, norm1_g, norm2_g, w_in, rwkv_mu, rwkv_w_r, rwkv_w_k, rwkv_w_v, rwkv_w0, rwkv_w1, rwkv_w2, rwkv_a0, rwkv_a1, rwkv_a2, rwkv_g1, rwkv_g2, rwkv_k_k, rwkv_k_a, rwkv_r_k, rwkv_ln_g, rwkv_ln_b, rwkv_w_proj, mlstm_conv, mlstm_w_q, mlstm_w_k, mlstm_w_v, mlstm_w_i, mlstm_b_i, mlstm_w_f, mlstm_b_f, mlstm_ln_g, mlstm_w_proj, w_out, moe_w_group, moe_b_group, moe_w_expert, moe_b_expert, moe_w1, moe_w3, moe_w2, final_norm_g):
    assert w_mod.shape[0] == 1, "single trunk layer"
    l = 0
    bp, dec = x_prompt.shape[0], x_sample.shape[0]
    bf = lambda w: w.astype(BF16)
    row = lambda w: w.reshape(1, -1).astype(F32)
    wg = jnp.concatenate([mlstm_w_i[l, 0], mlstm_w_i[l, 1], mlstm_w_f[l, 0], mlstm_w_f[l, 1]], axis=1)
    bg = jnp.concatenate([mlstm_b_i[l, 0], mlstm_b_i[l, 1], mlstm_b_f[l, 0], mlstm_b_f[l, 1]])
    wg = jnp.pad(wg, ((0, 0), (0, LANES - wg.shape[1])))
    bg = jnp.pad(bg, (0, LANES - bg.shape[0]))
    w_router = jnp.pad(jnp.concatenate([moe_w_group[l], moe_w_expert[l]], axis=1),
                       ((0, 0), (0, LANES - N_GROUPS - N_EXPERTS)))
    b_router = jnp.pad(jnp.concatenate([moe_b_group[l], moe_b_expert[l]]), (0, LANES - N_GROUPS - N_EXPERTS))
    p = {
        "norm1_g": row(norm1_g[l]), "norm2_g": row(norm2_g[l]), "w_in": bf(w_in[l]),
        "rwkv_mu": rwkv_mu[l], "rwkv_wr": bf(rwkv_w_r[l]), "rwkv_wk": bf(rwkv_w_k[l]), "rwkv_wv": bf(rwkv_w_v[l]),
        "rwkv_w0": rwkv_w0[l], "rwkv_w1": bf(rwkv_w1[l]), "rwkv_w2": bf(rwkv_w2[l]),
        "rwkv_a0": rwkv_a0[l], "rwkv_a1": bf(rwkv_a1[l]), "rwkv_a2": bf(rwkv_a2[l]),
        "rwkv_g1": bf(rwkv_g1[l]), "rwkv_g2": bf(rwkv_g2[l]),
        "rwkv_k_k": row(rwkv_k_k[l]), "rwkv_k_a": row(rwkv_k_a[l]), "rwkv_r_k": row(rwkv_r_k[l]),
        "rwkv_ln_g": row(rwkv_ln_g[l]), "rwkv_ln_b": row(rwkv_ln_b[l]),
        "rwkv_w_proj": bf(rwkv_w_proj[l]),
        "mlstm_conv": mlstm_conv[l].reshape(9, B_WIDTH),
        "mlstm_wq": bf(mlstm_w_q[l]), "mlstm_wk": bf(mlstm_w_k[l]), "mlstm_wv": bf(mlstm_w_v[l]),
        "mlstm_wg": bf(wg), "mlstm_bg": row(bg),
        "mlstm_wgt": bf(wg[:, :16].T), "mlstm_bgt": jnp.broadcast_to(bg[:16, None], (16, LANES)),
        "mlstm_ln_g": row(mlstm_ln_g[l]), "mlstm_w_proj": bf(mlstm_w_proj[l]), "w_out": bf(w_out[l]),
        "moe_w_router": bf(w_router), "moe_b_router": row(b_router),
        "moe_w1": bf(moe_w1[l]), "moe_w3": bf(moe_w3[l]), "moe_w2": bf(moe_w2[l]),
        "final_norm_g": row(final_norm_g),
    }
    cvec = jnp.concatenate([c_ctx[None, :], c, jnp.zeros((8 - 1 - dec, D_MODEL), F32)], axis=0)
    mod = _mod(cvec, w_mod[l], b_mod[l].reshape(1, -1))
    mod_ctx = jnp.broadcast_to(mod[0:1][:, None, :], (bp, 1, 6 * D_MODEL))
    mod_lat = mod[1:1 + dec][:, None, :]

    zeros = lambda *s: jnp.zeros((bp,) + s, F32)
    yp, n_rwkv, n_c, n_n, n_m = _trunk(
        x_prompt, mod_ctx, mod[0:1][:, None, :],
        zeros(2, A_HEADS, A_HEAD_DIM, A_HEAD_DIM), zeros(2, B_HEADS, B_HEAD_DIM, B_HEAD_DIM),
        zeros(2, B_HEADS, B_HEAD_DIM), zeros(2, B_HEADS), False, p, 256, 512)
    ys, _, _, _, _ = _trunk(
        x_sample, mod_lat, mod_lat, state_rwkv[:, l].astype(F32), state_mlstm_C[:, l].astype(F32),
        state_mlstm_n[:, l].astype(F32), state_mlstm_m[:, l].astype(F32), True, p, 512, 256)
    dt = x_prompt.dtype
    return (yp, ys, n_rwkv[:, None].astype(dt), n_c[:, None].astype(dt), n_n[:, None].astype(dt),
            n_m[:, None].astype(dt))
```

```python
import functools

import jax
import jax.numpy as jnp
from jax import lax
from jax.experimental import pallas as pl
from jax.experimental.pallas import tpu as pltpu

F32 = jnp.float32
BF16 = jnp.bfloat16

D_MODEL = 1024
GRID_W = 64
A_HEAD_DIM = 64
A_WIDTH = 512
A_HEADS = 8
B_HEAD_DIM = 128
B_WIDTH = 512
B_HEADS = 4
CHUNK = 64
ROW_TILE = 256
N_GROUPS = 4
EXPERTS_PER_GROUP = 4
N_EXPERTS = 16
D_EXPERT = 256
GROUP_LANE = N_GROUPS + N_EXPERTS
NORM_EPS = 1e-6
RWKV_GN_EPS = 64e-5
MLSTM_GN_EPS = 1e-5
LANES = 128
CONV_PAD = 72
VMEM_LIMIT = 56 * 1024 * 1024


def _mm(a, b):
    return jnp.dot(a.astype(BF16), b.astype(BF16), preferred_element_type=F32)


_NN, _NT, _TN = ((1,), (0,)), ((1,), (1,)), ((0,), (0,))


def _split(a):
    hi = a.astype(BF16)
    return hi, (a - hi.astype(F32)).astype(BF16)


def _split3(a):
    hi = a.astype(BF16)
    rest = a - hi.astype(F32)
    lo = rest.astype(BF16)
    return hi, lo, (rest - lo.astype(F32)).astype(BF16)


def _sum01(a, b, dims=_NN):
    dot = lambda p, q: lax.dot_general(p, q, (dims, ((), ())), preferred_element_type=F32)
    terms = [dot(x, b) for x in a] if isinstance(a, tuple) else [dot(a, x) for x in b]
    return functools.reduce(lambda p, q: p + q, terms)


def _mm3(a, b, dims=_NN):
    (ah, al), (bh, bl) = a, b
    dot = lambda p, q: lax.dot_general(p, q, (dims, ((), ())), preferred_element_type=F32)
    return dot(ah, bh) + dot(ah, bl) + dot(al, bh)


def _mm_nt(a, b):
    return lax.dot_general(a.astype(BF16), b.astype(BF16), (((1,), (1,)), ((), ())),
                           preferred_element_type=F32)


def _mm_tn(a, b):
    return lax.dot_general(a.astype(BF16), b.astype(BF16), (((0,), (0,)), ((), ())),
                           preferred_element_type=F32)


def _sigmoid(x):
    return 1.0 / (1.0 + jnp.exp(-x))


def _silu(x):
    return x * _sigmoid(x)


def _softplus(x):
    return jnp.maximum(x, 0.0) + jnp.log(1.0 + jnp.exp(-jnp.abs(x)))


def _iota(shape, dim):
    return lax.broadcasted_iota(jnp.int32, shape, dim)


def _tri(reverse, inclusive, n=CHUNK):
    row, col = _iota((n, n), 0), _iota((n, n), 1)
    if reverse:
        return (col >= row) if inclusive else (col > row)
    return (col <= row) if inclusive else (col < row)


def _same_block(n, blk):
    sh = blk.bit_length() - 1
    row, col = _iota((n, n), 0), _iota((n, n), 1)
    return lax.shift_right_logical(row, sh) == lax.shift_right_logical(col, sh)


def _rmsnorm(x, g):
    return x * lax.rsqrt(jnp.mean(x * x, axis=-1, keepdims=True) + NORM_EPS) * g


def _cparams(sem):
    return pltpu.CompilerParams(dimension_semantics=sem, vmem_limit_bytes=VMEM_LIMIT)


def _full(shape):
    nd = len(shape)
    return pl.BlockSpec(shape, lambda *_: (0,) * nd)


def _mod_kernel(c_ref, w_ref, b_ref, o_ref):
    o_ref[...] = _mm(_silu(c_ref[...]), w_ref[...]) + b_ref[...]


def _mod(cvec, w_mod, b_mod):
    rows, d = cvec.shape
    n = w_mod.shape[1]
    tn = 1536
    return pl.pallas_call(
        _mod_kernel,
        grid=(n // tn,),
        in_specs=[_full((rows, d)), pl.BlockSpec((d, tn), lambda j: (0, j)),
                  pl.BlockSpec((1, tn), lambda j: (0, j))],
        out_specs=pl.BlockSpec((rows, tn), lambda j: (0, j)),
        out_shape=jax.ShapeDtypeStruct((rows, n), F32),
        compiler_params=_cparams(("arbitrary",)),
        name="mod",
    )(cvec, w_mod, b_mod)


W_IN_BLOCK = 512


def _modulated_norm1(x_ref, mod_ref, g_ref):
    d = D_MODEL
    shift, scale = mod_ref[:, 0:d], mod_ref[:, d:2 * d]
    return (_rmsnorm(x_ref[...], g_ref[...]) * (1.0 + scale) + shift).astype(BF16)


def _w_in_blocks(w_in, first, count):
    specs = [pl.BlockSpec((w_in.shape[0], W_IN_BLOCK), lambda *_, j=first + k: (0, j)) for k in range(count)]
    return specs, [w_in] * count


def _in_kernel(x_ref, mod_ref, g_ref, wa_ref, wu_ref, wo_ref, xa_ref, ub_ref, op_ref):
    h = _modulated_norm1(x_ref, mod_ref, g_ref)
    for ref, w_ref in ((xa_ref, wa_ref), (ub_ref, wu_ref), (op_ref, wo_ref)):
        ref[...] = _mm(h, w_ref[...])


def _in_proj(x, mod, g1, w_in, tm):
    b, t, d = x.shape
    per = t // tm
    widths = (A_WIDTH, B_WIDTH, B_WIDTH)
    assert all(n == W_IN_BLOCK for n in widths)
    tok = lambda n: pl.BlockSpec((None, tm, n), lambda i: (i // per, i % per, 0))
    w_specs, w_args = _w_in_blocks(w_in, 0, 3)
    return pl.pallas_call(
        _in_kernel,
        grid=(b * per,),
        in_specs=[tok(d), pl.BlockSpec((None, 1, 6 * d), lambda i: (i // per, 0, 0)), _full((1, d))] + w_specs,
        out_specs=[tok(n) for n in widths],
        out_shape=[jax.ShapeDtypeStruct((b, t, n), F32) for n in widths],
        compiler_params=_cparams(("arbitrary",)),
        name="in_proj",
    )(x, mod, g1, *w_args)


def _pair_diag(x):
    first = (_iota(x.shape, 1) < A_HEAD_DIM).astype(x.dtype)
    return jnp.concatenate([x * first, x * (1 - first)], axis=0)


def _pair_mm3(lhs, rhs, dims=_NN):
    rows = lhs[0].shape[0]
    rh, rl = _split(rhs)
    parts = [_split(l) for l in lhs]
    dot = lambda a, b: lax.dot_general(a, b, (dims, ((), ())), preferred_element_type=F32)
    top = dot(jnp.concatenate([t for part in parts for t in part], axis=0), _pair_diag(rh))
    low = dot(jnp.concatenate([part[0] for part in parts], axis=0), _pair_diag(rl))
    return [top[2 * k * rows:(2 * k + 1) * rows] + top[(2 * k + 1) * rows:(2 * k + 2) * rows]
            + low[k * rows:(k + 1) * rows] for k in range(len(lhs))]


def _pair_index(shape):
    return _iota(shape, 0), _iota(shape, 1) & (A_HEAD_DIM - 1)


def _pair_tri_inverses(mats):
    t_idx, s_idx = _pair_index((CHUNK, LANES))
    same = lambda blk: (lax.shift_right_logical(t_idx, blk.bit_length() - 1)
                        == lax.shift_right_logical(s_idx, blk.bit_length() - 1))
    eye = (t_idx == s_idx).astype(F32)
    nd = [jnp.where(same(8), m, 0.0) for m in mats]
    n2 = [_pair_mm3([x], x)[0] for x in nd]
    t = [eye + x for x in nd]
    both = [_pair_mm3([a, b], a) for a, b in zip(n2, t)]
    t = [x + r[1] for x, r in zip(t, both)]
    t = [x + _pair_mm3([x], r[0])[0] for x, r in zip(t, both)]
    blk = 8
    while blk < CHUNK:
        sel = same(2 * blk) & jnp.logical_not(same(blk))
        w = [_pair_mm3([jnp.where(sel, m, 0.0)], x)[0] for m, x in zip(mats, t)]
        t = [x + _pair_mm3([x], y)[0] for x, y in zip(t, w)]
        blk *= 2
    return t


def _rwkv_kernel(x_ref, s0_ref, mu_ref, wr_ref, wk_ref, wv_ref, w0_ref, w1_ref, w2_ref,
                 a0_ref, a1_ref, a2_ref, g1_ref, g2_ref, kk_ref, ka_ref, rk_ref, lng_ref, lnb_ref,
                 y_ref, s_ref,
                 r_s, v_s, kk_s, cum_s, cx_s, kd_s, bb_s, g_s, bonus_s, y_s, sp_s, *, seq_len):
    total = x_ref.shape[0]
    n_seq = total // seq_len
    n_chunks = seq_len // CHUNK
    n_tiles = total // ROW_TILE
    hd = A_HEAD_DIM
    x = x_ref[...]
    row = _iota(x.shape, 0) & (seq_len - 1)
    prev = jnp.where(row == 0, 0.0, pltpu.roll(x, 1, 0))
    nxt = jnp.where(row == seq_len - 1, 0.0, pltpu.roll(x, total - 1, 0))
    y_s[...] = 0.5 * (prev + nxt) - x
    head_ones = _same_block(A_WIDTH, hd).astype(BF16)
    chunk_diag = _same_block(ROW_TILE, CHUNK)
    run_sum = [(chunk_diag & _tri(d == 1, True, ROW_TILE)).astype(BF16) for d in range(2)]

    def project_tile(ti, carry):
        rows = pl.ds(pl.multiple_of(ti * ROW_TILE, ROW_TILE), ROW_TILE)
        x, xx = x_ref[rows, :], y_s[rows, :]
        xr, xw, xk, xv, xi, xg = (x + xx * mu_ref[j:j + 1, :] for j in range(6))
        r = _mm(xr, wr_ref[...])
        k = _mm(xk, wk_ref[...])
        v = _mm(xv, wv_ref[...])
        g_s[rows, :] = _mm(_sigmoid(_mm(xg, g1_ref[...])), g2_ref[...])
        kk = k * kk_ref[...]
        kk = kk / jnp.maximum(jnp.sqrt(_sum01(_split3(kk * kk), head_ones)), 1e-12)
        r_s[rows, :] = r
        v_s[rows, :] = v
        kk_s[rows, :] = kk
        kd_sum = jnp.zeros_like(x)
        for d in range(2):
            zw = w0_ref[d:d + 1, :] + _mm(jnp.tanh(_mm(xw, w1_ref[d])), w2_ref[d])
            lw = -jnp.exp(-_softplus(-zw) - 0.5)
            cum = _sum01(run_sum[d], _split3(lw))
            cum_s[d, rows, :] = cum
            cx_s[d, rows, :] = cum - lw
            a = _sigmoid(a0_ref[d:d + 1, :] + _mm(_mm(xi, a1_ref[d]), a2_ref[d]))
            kd = k * (1.0 + (a - 1.0) * ka_ref[...])
            kd_s[d, rows, :] = kd
            bb_s[d, rows, :] = kk * a
            kd_sum = kd_sum + kd
        bonus_s[rows, :] = _sum01(_split3(r * kd_sum * rk_ref[...]), head_ones) * v
        y_s[rows, :] = jnp.zeros_like(x)
        return carry

    lax.fori_loop(0, n_tiles, project_tile, 0)
    for n in range(n_seq):
        for d in range(2):
            for p in range(A_HEADS // 2):
                sp_s[n, d, p] = jnp.concatenate([s0_ref[n, d, 2 * p], s0_ref[n, d, 2 * p + 1]], axis=1)

    t_idx, s_idx = _pair_index((CHUNK, LANES))
    strict = [s_idx < t_idx, s_idx > t_idx]
    incl = [s_idx <= t_idx, s_idx >= t_idx]
    first_head = _iota((CHUNK, LANES), 1) < hd

    def chunk_body(ci, carry):
        chains = []
        for n, d in [(n, d) for n in range(n_seq) for d in range(2)]:
            cpos = n * n_chunks + ((n_chunks - 1 - ci) if d == 1 else ci)
            last = 0 if d == 1 else CHUNK - 1
            rows = pl.ds(pl.multiple_of(cpos * CHUNK, CHUNK), CHUNK)
            cum, cx = cum_s[d, rows, :], cx_s[d, rows, :]
            tot = cum[last:last + 1, :]
            kdc, bbc = kd_s[d, rows, :], bb_s[d, rows, :]
            e_neg = jnp.exp(-cum)
            e_tail = jnp.exp(tot - cum)
            at = -kk_s[rows, :] * jnp.exp(cx)
            rt = r_s[rows, :] * jnp.exp(cum)
            bt, kt = bbc * e_neg, kdc * e_neg
            bp, kp = bbc * e_tail, kdc * e_tail
            pc = jnp.exp(tot)
            vc = v_s[rows, :]
            for p in range(A_HEADS // 2):
                ln = slice(p * LANES, (p + 1) * LANES)
                chains.append(dict(d=d, st=(n, d, p), rows=rows, ln=ln, v=vc[:, ln], pc=pc[:, ln],
                                   lhs=jnp.concatenate([at[:, ln], rt[:, ln]], axis=0),
                                   bt=bt[:, ln], kt=kt[:, ln], bp=bp[:, ln], kp=kp[:, ln]))
        for c in chains:
            c["sb"] = _pair_mm3([c["lhs"]], c["bt"], _NT)[0]
            c["sk"] = _pair_mm3([c["lhs"]], c["kt"], _NT)[0]
            c["s"] = sp_s[c["st"]]
            c["fs"] = _mm_nt(c["lhs"], _pair_diag(c["s"].astype(BF16)))
            c["v_diag"] = _pair_diag(c["v"].astype(BF16))
        for c in chains:
            a_ak = jnp.where(strict[c["d"]], c["sk"][:CHUNK], 0.0)
            c["x"] = c["fs"][:CHUNK] + _mm(a_ak, c["v_diag"])
        inv = _pair_tri_inverses([jnp.where(strict[c["d"]], c["sb"][:CHUNK], 0.0) for c in chains])
        for c, t in zip(chains, inv):
            c["u"] = _pair_mm3([t], c["x"])[0]
        for c in chains:
            d, rows, ln, u = c["d"], c["rows"], c["ln"], c["u"]
            a_r = jnp.concatenate([jnp.where(incl[d], c["sb"][CHUNK:], 0.0),
                                   jnp.where(incl[d], c["sk"][CHUNK:], 0.0)], axis=1)
            uv_diag = jnp.concatenate([_pair_diag(u.astype(BF16)), c["v_diag"]], axis=0)
            y_s[rows, ln] = y_s[rows, ln] + c["fs"][CHUNK:] + _mm(a_r, uv_diag)
            g = _mm_tn(jnp.concatenate([u, c["v"]], axis=0), jnp.concatenate([c["bp"], c["kp"]], axis=0))
            sp_s[c["st"]] = c["s"] * c["pc"] + jnp.where(first_head, g[:CHUNK], g[CHUNK:])
        return carry

    lax.fori_loop(0, n_chunks, chunk_body, 0)
    for n in range(n_seq):
        for d in range(2):
            for p in range(A_HEADS // 2):
                s_ref[n, d, 2 * p] = sp_s[n, d, p, :, 0:hd]
                s_ref[n, d, 2 * p + 1] = sp_s[n, d, p, :, hd:2 * hd]

    def finish_tile(ti, carry):
        rows = pl.ds(pl.multiple_of(ti * ROW_TILE, ROW_TILE), ROW_TILE)
        ys = y_s[rows, :]
        cen = ys - _sum01(_split3(ys), head_ones) * (1.0 / hd)
        var = _sum01(_split3(cen * cen), head_ones) * (1.0 / hd)
        yn = cen * lax.rsqrt(var + RWKV_GN_EPS) * lng_ref[...] + lnb_ref[...]
        y_ref[rows, :] = (yn + bonus_s[rows, :]) * g_s[rows, :]
        return carry

    lax.fori_loop(0, n_tiles, finish_tile, 0)


def _rwkv(xa, s0, p, n_seq):
    b, t, w = xa.shape
    assert t & (t - 1) == 0 and b % n_seq == 0
    names = ("mu", "wr", "wk", "wv", "w0", "w1", "w2", "a0", "a1", "a2", "g1", "g2",
             "k_k", "k_a", "r_k", "ln_g", "ln_b")
    weights = [p["rwkv_" + n] for n in names]
    rows = n_seq * t
    seq = pl.BlockSpec((None, rows, w), lambda i: (i, 0, 0))
    st = pl.BlockSpec((n_seq, 2, A_HEADS, A_HEAD_DIM, A_HEAD_DIM), lambda i: (i, 0, 0, 0, 0))
    tw = lambda lead=(): pltpu.VMEM(lead + (rows, w), F32)
    y, s_out = pl.pallas_call(
        functools.partial(_rwkv_kernel, seq_len=t),
        grid=(b // n_seq,),
        in_specs=[seq, st] + [_full(x.shape) for x in weights],
        out_specs=[seq, st],
        out_shape=[jax.ShapeDtypeStruct((b // n_seq, rows, w), F32), jax.ShapeDtypeStruct(s0.shape, F32)],
        scratch_shapes=[tw(), tw(), tw(), tw((2,)), tw((2,)), tw((2,)), tw((2,)), tw(), tw(), tw(),
                        pltpu.VMEM((n_seq, 2, A_HEADS // 2, A_HEAD_DIM, LANES), F32)],
        compiler_params=_cparams(("arbitrary",)),
        name="rwkv",
    )(xa.reshape(b // n_seq, rows, w), s0, *weights)
    return y.reshape(b, t, w), s_out


def _mlstm_kernel(u_ref, op_ref, c0_ref, n0_ref, m0_ref, conv_ref, wq_ref, wk_ref, wv_ref,
                  wg_ref, bg_ref, wgt_ref, bgt_ref, lng_ref,
                  y_ref, c_ref, n_ref, m_ref,
                  pad_s, q_s, k_s, v_s, gc_s, cc_s, gr_s, cr_s, h_s, *, conv2d):
    t_len = u_ref.shape[0]
    n_chunks = t_len // CHUNK
    hd = B_HEAD_DIM
    u = u_ref[...]
    pad_s[0:CONV_PAD, :] = jnp.zeros((CONV_PAD, B_WIDTH), F32)
    pad_s[CONV_PAD + t_len:CONV_PAD + t_len + CONV_PAD, :] = jnp.zeros((CONV_PAD, B_WIDTH), F32)
    pad_s[CONV_PAD:CONV_PAD + t_len, :] = u
    col = _iota(u.shape, 0) & (GRID_W - 1)
    conv = jnp.zeros_like(u)
    for kh in range(3):
        if not conv2d and kh != 1:
            continue
        for kw in range(3):
            shift = (kh - 1) * GRID_W + (kw - 1)
            term = pad_s[CONV_PAD + shift:CONV_PAD + shift + t_len, :] * conv_ref[kh * 3 + kw:kh * 3 + kw + 1, :]
            if conv2d and kw != 1:
                src = col + (kw - 1)
                term = jnp.where((src >= 0) & (src < GRID_W), term, 0.0)
            conv = conv + term
    uc = _silu(conv)
    for h in range(B_HEADS):
        ln = slice(h * hd, (h + 1) * hd)
        q_s[:, ln] = _mm(uc[:, ln], wq_ref[h])
        k_s[:, ln] = _mm(uc[:, ln], wk_ref[h]) * (hd ** -0.5)
        v_s[:, ln] = _mm(u[:, ln], wv_ref[h])
    gcol = _mm(uc, wg_ref[...]) + bg_ref[...]
    gcol = jnp.where(_iota(gcol.shape, 1) >= 8, -_softplus(-gcol), gcol)
    gc_s[...] = gcol
    grow = _mm_nt(wgt_ref[...], uc) + bgt_ref[:, 0:1]
    grow = jnp.where(_iota(grow.shape, 0) >= 8, -_softplus(-grow), grow)
    chunk_diag = _same_block(ROW_TILE, CHUNK)
    run_sum = [(chunk_diag & _tri(d == 1, True, ROW_TILE)).astype(BF16) for d in range(2)]
    per_tile = ROW_TILE // CHUNK
    for ti in range(t_len // ROW_TILE):
        sl = slice(ti * ROW_TILE, (ti + 1) * ROW_TILE)
        gt, rt = _split3(gcol[sl, :]), _split3(grow[:, sl])
        backward_lane = (_iota((ROW_TILE, LANES), 1) & 4) != 0
        cc_s[sl, :] = jnp.where(backward_lane, _sum01(run_sum[1], gt, _NN), _sum01(run_sum[0], gt, _NN))
        backward_row = (_iota((16, ROW_TILE), 0) & 4) != 0
        cum_r = jnp.where(backward_row, _sum01(rt, run_sum[1], _NT), _sum01(rt, run_sum[0], _NT))
        for c in range(per_tile):
            gr_s[ti * per_tile + c] = grow[:, ti * ROW_TILE + c * CHUNK:ti * ROW_TILE + (c + 1) * CHUNK]
            cr_s[ti * per_tile + c] = cum_r[:, c * CHUNK:(c + 1) * CHUNK]
    h_s[...] = jnp.zeros_like(u)
    c_ref[...] = c0_ref[...]
    n_ref[...] = n0_ref[...]
    m_ref[...] = m0_ref[...]
    incl = [_tri(d == 1, True) for d in range(2)]

    def chunk_body(ci, carry):
        chains = []
        for d in range(2):
            cpos = (n_chunks - 1 - ci) if d == 1 else ci
            last = 0 if d == 1 else CHUNK - 1
            rows = pl.ds(pl.multiple_of(cpos * CHUNK, CHUNK), CHUNK)
            gc, cc, gr, cr = gc_s[rows, :], cc_s[rows, :], gr_s[cpos], cr_s[cpos]
            for h in range(B_HEADS):
                ln = slice(h * hd, (h + 1) * hd)
                li, lf = d * 4 + h, 8 + d * 4 + h
                chains.append(dict(d=d, h=h, rows=rows, ln=ln, last=last,
                                   b_col=cc[:, lf:lf + 1], i_col=gc[:, li:li + 1],
                                   b_row=cr[lf:lf + 1, :], i_row=gr[li:li + 1, :],
                                   m=m_ref[d, h:h + 1, 0:1], q=q_s[rows, ln], k=k_s[rows, ln], v=v_s[rows, ln],
                                   c=c_ref[d, h], n=n_ref[d, h:h + 1, :]))
        for c in chains:
            c["qk"] = _mm_nt(c["q"], c["k"])
            c["qc"] = _mm(c["q"], c["c"])
            c["logw"] = jnp.where(incl[c["d"]], c["b_col"] - c["b_row"] + c["i_row"], -jnp.inf)
        for c in chains:
            c["m_t"] = jnp.maximum(c["b_col"] + c["m"], jnp.max(c["logw"], axis=1, keepdims=True))
        for c in chains:
            c["s"] = c["qk"] * jnp.exp(c["logw"] - c["m_t"])
            c["inter"] = jnp.exp(c["b_col"] + c["m"] - c["m_t"])
        for c in chains:
            c["sv"] = _mm(c["s"], c["v"])
            last = c["last"]
            m_new = c["m_t"][last:last + 1, :]
            b_last = c["b_col"][last:last + 1, :]
            c["m_new"] = m_new
            c["decay"] = jnp.exp(b_last + c["m"] - m_new)
            c["kw"] = c["k"] * jnp.exp(b_last - c["b_col"] + c["i_col"] - m_new)
        for c in chains:
            c["kv"] = _mm_tn(c["kw"], c["v"])
            c["den"] = (c["inter"] * jnp.sum(c["q"] * c["n"], axis=1, keepdims=True)
                        + jnp.sum(c["s"], axis=1, keepdims=True))
        for c in chains:
            d, h, rows, ln = c["d"], c["h"], c["rows"], c["ln"]
            num = c["inter"] * c["qc"] + c["sv"]
            h_s[rows, ln] = h_s[rows, ln] + num / jnp.maximum(jnp.abs(c["den"]), jnp.exp(-c["m_t"]))
            c_ref[d, h] = c["decay"] * c["c"] + c["kv"]
            n_ref[d, h:h + 1, :] = c["decay"] * c["n"] + jnp.sum(c["kw"], axis=0, keepdims=True)
            m_ref[d, h:h + 1, :] = jnp.broadcast_to(c["m_new"], (1, LANES))
        return carry

    lax.fori_loop(0, n_chunks, chunk_body, 0)

    for h in range(B_HEADS):
        ln = slice(h * hd, (h + 1) * hd)
        hh = h_s[:, ln]
        cen = hh - jnp.mean(hh, axis=1, keepdims=True)
        var = jnp.mean(cen * cen, axis=1, keepdims=True)
        y_ref[:, ln] = _sigmoid(op_ref[:, ln]) * (cen * lax.rsqrt(var + MLSTM_GN_EPS) * lng_ref[:, ln])


def _mlstm(ub, o_pre, c0, n0, m0, p, conv2d):
    b, t, w = ub.shape
    names = ("conv", "wq", "wk", "wv", "wg", "bg", "wgt", "bgt", "ln_g")
    weights = [p["mlstm_" + n] for n in names]
    seq = pl.BlockSpec((None, t, w), lambda i: (i, 0, 0))
    cst = pl.BlockSpec((None, 2, B_HEADS, B_HEAD_DIM, B_HEAD_DIM), lambda i: (i, 0, 0, 0, 0))
    vst = pl.BlockSpec((None, 2, B_HEADS, LANES), lambda i: (i, 0, 0, 0))
    return pl.pallas_call(
        functools.partial(_mlstm_kernel, conv2d=conv2d),
        grid=(b,),
        in_specs=[seq, seq, cst, vst, vst] + [_full(x.shape) for x in weights],
        out_specs=[seq, cst, vst, vst],
        out_shape=[jax.ShapeDtypeStruct((b, t, w), F32), jax.ShapeDtypeStruct(c0.shape, F32),
                   jax.ShapeDtypeStruct(n0.shape, F32), jax.ShapeDtypeStruct(m0.shape, F32)],
        scratch_shapes=[pltpu.VMEM((t + 2 * CONV_PAD, w), F32), pltpu.VMEM((t, w), F32),
                        pltpu.VMEM((t, w), F32), pltpu.VMEM((t, w), F32),
                        pltpu.VMEM((t, LANES), F32), pltpu.VMEM((t, LANES), F32),
                        pltpu.VMEM((t // CHUNK, 16, CHUNK), F32), pltpu.VMEM((t // CHUNK, 16, CHUNK), F32),
                        pltpu.VMEM((t, w), F32)],
        compiler_params=_cparams(("arbitrary",)),
        name="mlstm",
    )(ub, o_pre, c0, n0, m0, *weights)


def _first_lane_of_max(val, lane, valid):
    masked = jnp.where(valid, val, -jnp.inf)
    best = jnp.max(masked, axis=1, keepdims=True)
    idx = jnp.min(jnp.where(valid & (masked == best), lane, LANES), axis=1, keepdims=True)
    return best, idx


def _merge_kernel(x_ref, ya_ref, yb_ref, mod_ref, g1_ref, wga0_ref, wga1_ref, wgb0_ref, wgb1_ref,
                  wpa_ref, wpb_ref, wo_ref, g2_ref, wr_ref, br_ref, xc_ref):
    d = D_MODEL
    gate1 = mod_ref[:, 2 * d:3 * d]
    shift2, scale2 = mod_ref[:, 3 * d:4 * d], mod_ref[:, 4 * d:5 * d]
    h1 = _modulated_norm1(x_ref, mod_ref, g1_ref)
    pa, pb = _mm(ya_ref[...], wpa_ref[...]), _mm(yb_ref[...], wpb_ref[...])
    halves = []
    for j, (wga_ref, wgb_ref) in enumerate(((wga0_ref, wgb0_ref), (wga1_ref, wgb1_ref))):
        cols = slice(j * W_IN_BLOCK, (j + 1) * W_IN_BLOCK)
        halves.append(_sigmoid(_mm(h1, wga_ref[...])) * pa[:, cols] + _sigmoid(_mm(h1, wgb_ref[...])) * pb[:, cols])
    merged = jnp.concatenate(halves, axis=1)
    x1 = x_ref[...] + gate1 * _mm(merged, wo_ref[...])
    xc_ref[:, 0:d] = x1
    h2 = _rmsnorm(x1, g2_ref[...]) * (1.0 + scale2) + shift2
    logits = _mm(h2, wr_ref[...]) + br_ref[...]
    lane = _iota(logits.shape, 1)
    is_group = lane < N_GROUPS
    g_max, g_sel = _first_lane_of_max(logits, lane, is_group)
    g_w = 1.0 / jnp.sum(jnp.where(is_group, jnp.exp(logits - g_max), 0.0), axis=1, keepdims=True)
    expert = lane - N_GROUPS
    in_group = (expert >= 0) & (expert < N_EXPERTS) & (lax.shift_right_arithmetic(expert, EXPERTS_PER_GROUP.bit_length() - 1) == g_sel)
    e_max = jnp.max(jnp.where(in_group, logits, -jnp.inf), axis=1, keepdims=True)
    e_exp = jnp.where(in_group, jnp.exp(logits - e_max), 0.0)
    prob = e_exp / jnp.sum(e_exp, axis=1, keepdims=True)
    p1, i1 = _first_lane_of_max(prob, lane, in_group)
    p2, i2 = _first_lane_of_max(prob, lane, in_group & (lane != i1))
    denom = p1 + p2
    comb = jnp.where(lane == i1, g_w * p1 / denom, 0.0) + jnp.where(lane == i2, g_w * p2 / denom, 0.0)
    xc_ref[:, d:d + LANES] = jnp.where(lane == GROUP_LANE, g_sel.astype(F32), comb)


def _merge(x, ya, yb, mod, p, tm):
    b, t, d = x.shape
    per = t // tm
    assert d == 2 * W_IN_BLOCK
    tok = lambda n: pl.BlockSpec((None, tm, n), lambda i: (i // per, i % per, 0))
    gate_specs, gate_args = _w_in_blocks(p["w_in"], 3, 4)
    weights = [p["rwkv_w_proj"], p["mlstm_w_proj"], p["w_out"], p["norm2_g"], p["moe_w_router"], p["moe_b_router"]]
    return pl.pallas_call(
        _merge_kernel,
        grid=(b * per,),
        in_specs=[tok(d), tok(A_WIDTH), tok(B_WIDTH), pl.BlockSpec((None, 1, 6 * d), lambda i: (i // per, 0, 0)),
                  _full((1, d))] + gate_specs + [_full(w.shape) for w in weights],
        out_specs=tok(d + LANES),
        out_shape=jax.ShapeDtypeStruct((b, t, d + LANES), F32),
        compiler_params=_cparams(("arbitrary",)),
        name="merge",
    )(x, ya, yb, mod, p["norm1_g"], *gate_args, *weights)


def _route(gsel, tb):
    s, l = gsel.shape
    n_buckets = s * N_GROUPS
    max_tiles = s * (l // tb + N_GROUPS - 1)
    onehot = (gsel[..., None] == jnp.arange(N_GROUPS, dtype=jnp.int32)).astype(jnp.int32)
    rank = jnp.sum((jnp.cumsum(onehot, axis=1) - onehot) * onehot, axis=-1)
    n_tiles = ((jnp.sum(onehot, axis=1) + tb - 1) // tb).reshape(n_buckets)
    ends = jnp.cumsum(n_tiles)
    first_tile = jnp.sum(onehot * (ends - n_tiles).reshape(s, 1, N_GROUPS), axis=-1)
    pos = first_tile * tb + rank
    tok = jnp.arange(s * l, dtype=jnp.int32)
    row_src = jnp.full((max_tiles * tb,), -1, jnp.int32).at[pos.reshape(-1)].set(tok)
    tile_bucket = jnp.sum(jnp.arange(max_tiles, dtype=jnp.int32)[:, None] >= ends[None, :], axis=1)
    tile_bucket = jnp.minimum(tile_bucket, n_buckets - 1).astype(jnp.int32)
    return row_src, tile_bucket // N_GROUPS, tile_bucket % N_GROUPS, ends[-1:].astype(jnp.int32)


def _moe_kernel(gsrc_ref, sdst_ref, seg_ref, grp_ref, used_ref,
                xc_hbm, mod_ref, g2_ref, w1_ref, w3_ref, w2_ref, gf_ref, dump_in_hbm,
                y_hbm, dump_hbm, gbuf, obuf, gsem, ssem, *, tb):
    d = D_MODEL
    q = pl.program_id(0)
    n_used = used_ref[0]

    def start_gather(tile, sl):
        def body(i, carry):
            r0 = pl.multiple_of(i * 8, 8)
            for u in range(8):
                idx = gsrc_ref[tile * tb + r0 + u]
                pltpu.make_async_copy(xc_hbm.at[pl.ds(idx, 1), :], gbuf.at[sl, pl.ds(r0 + u, 1), :],
                                      gsem.at[sl]).start()
            return carry

        lax.fori_loop(0, tb // 8, body, 0)

    def wait_gather(sl):
        pltpu.make_async_copy(xc_hbm.at[pl.ds(0, tb), :], gbuf.at[sl], gsem.at[sl]).wait()

    def start_scatter(tile, sl):
        def body(i, carry):
            r0 = pl.multiple_of(i * 8, 8)
            for u in range(8):
                idx = sdst_ref[tile * tb + r0 + u]

                @pl.when(idx >= 0)
                def _():
                    pltpu.make_async_copy(obuf.at[sl, pl.ds(r0 + u, 1), :], y_hbm.at[pl.ds(idx, 1), :],
                                          ssem.at[sl]).start()

                @pl.when(idx < 0)
                def _():
                    pltpu.make_async_copy(obuf.at[sl, pl.ds(r0 + u, 1), :], dump_hbm.at[sl, pl.ds(r0 + u, 1), :],
                                          ssem.at[sl]).start()

            return carry

        lax.fori_loop(0, tb // 8, body, 0)

    def wait_scatter(sl):
        pltpu.make_async_copy(obuf.at[sl], dump_hbm.at[sl], ssem.at[sl]).wait()

    def step(slot):
        @pl.when(q == 0)
        def _():
            start_gather(0, 0)

        @pl.when(q + 1 < n_used)
        def _():
            start_gather(q + 1, 1 - slot)

        wait_gather(slot)

        @pl.when(q >= 2)
        def _():
            wait_scatter(slot)

        rows = gbuf[slot]
        x1, comb = rows[:, 0:d], rows[:, d:d + LANES]
        shift2, scale2, gate2 = mod_ref[:, 3 * d:4 * d], mod_ref[:, 4 * d:5 * d], mod_ref[:, 5 * d:6 * d]
        h2 = (_rmsnorm(x1, g2_ref[...]) * (1.0 + scale2) + shift2).astype(BF16)
        first_lane = N_GROUPS + grp_ref[q] * EXPERTS_PER_GROUP
        lane = _iota(comb.shape, 1)
        acc = jnp.zeros((tb, d), F32)
        for e in range(EXPERTS_PER_GROUP):
            w_e = jnp.sum(jnp.where(lane == first_lane + e, comb, 0.0), axis=1, keepdims=True)
            a = _mm(h2, w1_ref[e])
            b = _mm(h2, w3_ref[e])
            acc = acc + _mm(_silu(a) * b * w_e, w2_ref[e])
        obuf[slot] = _rmsnorm(x1 + gate2 * acc, gf_ref[...])
        start_scatter(q, slot)

        @pl.when(q == n_used - 1)
        def _():
            @pl.when(q >= 1)
            def _():
                wait_scatter(1 - slot)

            wait_scatter(slot)

    for slot in range(2):
        @pl.when((q < n_used) & (lax.rem(q, 2) == slot))
        def _(slot=slot):
            step(slot)


def _moe(xc, mod, p, tb):
    s, l, width = xc.shape
    d = D_MODEL
    gsel = xc[:, :, d + GROUP_LANE].astype(jnp.int32)
    row_src, tile_seg, tile_grp, n_used = _route(gsel, tb)
    max_tiles = tile_seg.shape[0]
    grouped = lambda w: w.reshape((N_GROUPS, EXPERTS_PER_GROUP) + w.shape[1:])
    w1, w3, w2 = grouped(p["moe_w1"]), grouped(p["moe_w3"]), grouped(p["moe_w2"])
    const = lambda shape: pl.BlockSpec(shape, lambda q, *_: (0,) * len(shape))
    by_group = lambda w: pl.BlockSpec((None,) + w.shape[1:], lambda q, gs, sd, seg, grp, used: (grp[q], 0, 0, 0))
    hbm = pl.BlockSpec(memory_space=pl.ANY)
    y, _ = pl.pallas_call(
        functools.partial(_moe_kernel, tb=tb),
        grid_spec=pltpu.PrefetchScalarGridSpec(
            num_scalar_prefetch=5,
            grid=(max_tiles,),
            in_specs=[hbm, pl.BlockSpec((None, 1, 6 * d), lambda q, gs, sd, seg, grp, used: (seg[q], 0, 0)),
                      const((1, d)), by_group(w1), by_group(w3), by_group(w2), const((1, d)), hbm],
            out_specs=[hbm, hbm],
            scratch_shapes=[pltpu.VMEM((2, tb, width), F32), pltpu.VMEM((2, tb, d), F32),
                            pltpu.SemaphoreType.DMA((2,)), pltpu.SemaphoreType.DMA((2,))]),
        out_shape=[jax.ShapeDtypeStruct((s * l, d), F32), jax.ShapeDtypeStruct((2, tb, d), F32)],
        input_output_aliases={12: 1},
        compiler_params=_cparams(("arbitrary",)),
        name="moe",
    )(jnp.maximum(row_src, 0), row_src, tile_seg, tile_grp, n_used, xc.reshape(s * l, width), mod,
      p["norm2_g"], w1, w3, w2, p["final_norm_g"], jnp.zeros((2, tb, d), F32))
    return y.reshape(s, l, d)


def _trunk(x, mod, mod_seg, s_rwkv, s_c, s_n, s_m, conv2d, p, tm, tb, rwkv_seqs):
    b, t, d = x.shape
    xa, ub, o_pre = _in_proj(x, mod, p["norm1_g"], p["w_in"], tm)
    ya, s_rwkv = _rwkv(xa, s_rwkv, p, rwkv_seqs)
    m_in = jnp.broadcast_to(s_m[..., None], s_m.shape + (LANES,))
    yb, s_c, s_n, s_m = _mlstm(ub, o_pre, s_c, s_n, m_in, p, conv2d)
    xc = _merge(x, ya, yb, mod, p, tm)
    n_seg = mod_seg.shape[0]
    y = _moe(xc.reshape(n_seg, b * t // n_seg, d + LANES), mod_seg, p, tb)
    return y.reshape(b, t, d), s_rwkv, s_c, s_n, s_m[..., 0]


def kernel(x_prompt, x_sample, c, c_ctx, state_rwkv, state_mlstm_C, state_mlstm_n, state_mlstm_m, w_mod, b_mod, norm1_g, norm2_g, w_in, rwkv_mu, rwkv_w_r, rwkv_w_k, rwkv_w_v, rwkv_w0, rwkv_w1, rwkv_w2, rwkv_a0, rwkv_a1, rwkv_a2, rwkv_g1, rwkv_g2, rwkv_k_k, rwkv_k_a, rwkv_r_k, rwkv_ln_g, rwkv_ln_b, rwkv_w_proj, mlstm_conv, mlstm_w_q, mlstm_w_k, mlstm_w_v, mlstm_w_i, mlstm_b_i, mlstm_w_f, mlstm_b_f, mlstm_ln_g, mlstm_w_proj, w_out, moe_w_group, moe_b_group, moe_w_expert, moe_b_expert, moe_w1, moe_w3, moe_w2, final_norm_g):
    assert w_mod.shape[0] == 1, "single trunk layer"
    l = 0
    bp, dec = x_prompt.shape[0], x_sample.shape[0]
    bf = lambda w: w.astype(BF16)
    row = lambda w: w.reshape(1, -1).astype(F32)
    wg = jnp.concatenate([mlstm_w_i[l, 0], mlstm_w_i[l, 1], mlstm_w_f[l, 0], mlstm_w_f[l, 1]], axis=1)
    bg = jnp.concatenate([mlstm_b_i[l, 0], mlstm_b_i[l, 1], mlstm_b_f[l, 0], mlstm_b_f[l, 1]])
    wg = jnp.pad(wg, ((0, 0), (0, LANES - wg.shape[1])))
    bg = jnp.pad(bg, (0, LANES - bg.shape[0]))
    w_router = jnp.pad(jnp.concatenate([moe_w_group[l], moe_w_expert[l]], axis=1),
                       ((0, 0), (0, LANES - N_GROUPS - N_EXPERTS)))
    b_router = jnp.pad(jnp.concatenate([moe_b_group[l], moe_b_expert[l]]), (0, LANES - N_GROUPS - N_EXPERTS))
    p = {
        "norm1_g": row(norm1_g[l]), "norm2_g": row(norm2_g[l]), "w_in": w_in[l],
        "rwkv_mu": rwkv_mu[l], "rwkv_wr": bf(rwkv_w_r[l]), "rwkv_wk": bf(rwkv_w_k[l]), "rwkv_wv": bf(rwkv_w_v[l]),
        "rwkv_w0": rwkv_w0[l], "rwkv_w1": bf(rwkv_w1[l]), "rwkv_w2": bf(rwkv_w2[l]),
        "rwkv_a0": rwkv_a0[l], "rwkv_a1": bf(rwkv_a1[l]), "rwkv_a2": bf(rwkv_a2[l]),
        "rwkv_g1": bf(rwkv_g1[l]), "rwkv_g2": bf(rwkv_g2[l]),
        "rwkv_k_k": row(rwkv_k_k[l]), "rwkv_k_a": row(rwkv_k_a[l]), "rwkv_r_k": row(rwkv_r_k[l]),
        "rwkv_ln_g": row(rwkv_ln_g[l]), "rwkv_ln_b": row(rwkv_ln_b[l]),
        "rwkv_w_proj": rwkv_w_proj[l],
        "mlstm_conv": mlstm_conv[l].reshape(9, B_WIDTH),
        "mlstm_wq": bf(mlstm_w_q[l]), "mlstm_wk": bf(mlstm_w_k[l]), "mlstm_wv": bf(mlstm_w_v[l]),
        "mlstm_wg": bf(wg), "mlstm_bg": row(bg),
        "mlstm_wgt": bf(wg[:, :16].T), "mlstm_bgt": jnp.broadcast_to(bg[:16, None], (16, LANES)),
        "mlstm_ln_g": row(mlstm_ln_g[l]), "mlstm_w_proj": mlstm_w_proj[l], "w_out": w_out[l],
        "moe_w_router": bf(w_router), "moe_b_router": row(b_router),
        "moe_w1": moe_w1[l], "moe_w3": moe_w3[l], "moe_w2": moe_w2[l],
        "final_norm_g": row(final_norm_g),
    }
    cvec = jnp.concatenate([c_ctx[None, :], c, jnp.zeros((8 - 1 - dec, D_MODEL), F32)], axis=0)
    mod = _mod(cvec, w_mod[l], b_mod[l].reshape(1, -1))
    mod_ctx = jnp.broadcast_to(mod[0:1][:, None, :], (bp, 1, 6 * D_MODEL))
    mod_lat = mod[1:1 + dec][:, None, :]

    zeros = lambda *s: jnp.zeros((bp,) + s, F32)
    yp, n_rwkv, n_c, n_n, n_m = _trunk(
        x_prompt, mod_ctx, mod[0:1][:, None, :],
        zeros(2, A_HEADS, A_HEAD_DIM, A_HEAD_DIM), zeros(2, B_HEADS, B_HEAD_DIM, B_HEAD_DIM),
        zeros(2, B_HEADS, B_HEAD_DIM), zeros(2, B_HEADS), False, p, 256, 512, 2)
    ys, _, _, _, _ = _trunk(
        x_sample, mod_lat, mod_lat, state_rwkv[:, l].astype(F32), state_mlstm_C[:, l].astype(F32),
        state_mlstm_n[:, l].astype(F32), state_mlstm_m[:, l].astype(F32), True, p, 512, 256, 1)
    dt = x_prompt.dtype
    return (yp, ys, n_rwkv[:, None].astype(dt), n_c[:, None].astype(dt), n_n[:, None].astype(dt),
            n_m[:, None].astype(dt))
```

```python
import functools

import jax
import jax.numpy as jnp
from jax import lax
from jax.experimental import pallas as pl
from jax.experimental.pallas import tpu as pltpu

F32 = jnp.float32
BF16 = jnp.bfloat16

D_MODEL = 1024
GRID_W = 64
A_HEAD_DIM = 64
A_WIDTH = 512
A_HEADS = 8
B_HEAD_DIM = 128
B_WIDTH = 512
B_HEADS = 4
CHUNK = 64
ROW_TILE = 256
N_GROUPS = 4
EXPERTS_PER_GROUP = 4
N_EXPERTS = 16
D_EXPERT = 256
GROUP_LANE = N_GROUPS + N_EXPERTS
NORM_EPS = 1e-6
RWKV_GN_EPS = 64e-5
MLSTM_GN_EPS = 1e-5
LANES = 128
CONV_PAD = 72
VMEM_LIMIT = 56 * 1024 * 1024


def _mm(a, b):
    return jnp.dot(a.astype(BF16), b.astype(BF16), preferred_element_type=F32)


_NN, _NT, _TN = ((1,), (0,)), ((1,), (1,)), ((0,), (0,))


def _split(a):
    hi = a.astype(BF16)
    return hi, (a - hi.astype(F32)).astype(BF16)


def _split3(a):
    hi = a.astype(BF16)
    rest = a - hi.astype(F32)
    lo = rest.astype(BF16)
    return hi, lo, (rest - lo.astype(F32)).astype(BF16)


def _sum01(a, b, dims=_NN):
    dot = lambda p, q: lax.dot_general(p, q, (dims, ((), ())), preferred_element_type=F32)
    terms = [dot(x, b) for x in a] if isinstance(a, tuple) else [dot(a, x) for x in b]
    return functools.reduce(lambda p, q: p + q, terms)


def _mm3(a, b, dims=_NN):
    (ah, al), (bh, bl) = a, b
    dot = lambda p, q: lax.dot_general(p, q, (dims, ((), ())), preferred_element_type=F32)
    return dot(ah, bh) + dot(ah, bl) + dot(al, bh)


def _mm_nt(a, b):
    return lax.dot_general(a.astype(BF16), b.astype(BF16), (((1,), (1,)), ((), ())),
                           preferred_element_type=F32)


def _mm_tn(a, b):
    return lax.dot_general(a.astype(BF16), b.astype(BF16), (((0,), (0,)), ((), ())),
                           preferred_element_type=F32)


def _sigmoid(x):
    return 1.0 / (1.0 + jnp.exp(-x))


def _silu(x):
    return x * _sigmoid(x)


def _softplus(x):
    return jnp.maximum(x, 0.0) + jnp.log(1.0 + jnp.exp(-jnp.abs(x)))


def _iota(shape, dim):
    return lax.broadcasted_iota(jnp.int32, shape, dim)


def _tri(reverse, inclusive, n=CHUNK):
    row, col = _iota((n, n), 0), _iota((n, n), 1)
    if reverse:
        return (col >= row) if inclusive else (col > row)
    return (col <= row) if inclusive else (col < row)


def _same_block(n, blk):
    sh = blk.bit_length() - 1
    row, col = _iota((n, n), 0), _iota((n, n), 1)
    return lax.shift_right_logical(row, sh) == lax.shift_right_logical(col, sh)


def _rmsnorm(x, g):
    return x * lax.rsqrt(jnp.mean(x * x, axis=-1, keepdims=True) + NORM_EPS) * g


def _cparams(sem):
    return pltpu.CompilerParams(dimension_semantics=sem, vmem_limit_bytes=VMEM_LIMIT)


def _full(shape):
    nd = len(shape)
    return pl.BlockSpec(shape, lambda *_: (0,) * nd)


def _mod_kernel(c_ref, w_ref, b_ref, o_ref):
    o_ref[...] = _mm(_silu(c_ref[...]), w_ref[...]) + b_ref[...]


def _mod(cvec, w_mod, b_mod):
    rows, d = cvec.shape
    n = w_mod.shape[1]
    tn = 1536
    return pl.pallas_call(
        _mod_kernel,
        grid=(n // tn,),
        in_specs=[_full((rows, d)), pl.BlockSpec((d, tn), lambda j: (0, j)),
                  pl.BlockSpec((1, tn), lambda j: (0, j))],
        out_specs=pl.BlockSpec((rows, tn), lambda j: (0, j)),
        out_shape=jax.ShapeDtypeStruct((rows, n), F32),
        compiler_params=_cparams(("arbitrary",)),
        name="mod",
    )(cvec, w_mod, b_mod)


W_IN_BLOCK = 512


def _modulated_norm1(x_ref, mod_ref, g_ref):
    d = D_MODEL
    shift, scale = mod_ref[:, 0:d], mod_ref[:, d:2 * d]
    return (_rmsnorm(x_ref[...], g_ref[...]) * (1.0 + scale) + shift).astype(BF16)


def _w_in_blocks(w_in, first, count):
    specs = [pl.BlockSpec((w_in.shape[0], W_IN_BLOCK), lambda *_, j=first + k: (0, j)) for k in range(count)]
    return specs, [w_in] * count


def _in_kernel(x_ref, mod_ref, g_ref, wa_ref, wu_ref, wo_ref, xa_ref, ub_ref, op_ref):
    h = _modulated_norm1(x_ref, mod_ref, g_ref)
    for ref, w_ref in ((xa_ref, wa_ref), (ub_ref, wu_ref), (op_ref, wo_ref)):
        ref[...] = _mm(h, w_ref[...])


def _in_proj(x, mod, g1, w_in, tm):
    b, t, d = x.shape
    per = t // tm
    widths = (A_WIDTH, B_WIDTH, B_WIDTH)
    assert all(n == W_IN_BLOCK for n in widths)
    tok = lambda n: pl.BlockSpec((None, tm, n), lambda i: (i // per, i % per, 0))
    w_specs, w_args = _w_in_blocks(w_in, 0, 3)
    return pl.pallas_call(
        _in_kernel,
        grid=(b * per,),
        in_specs=[tok(d), pl.BlockSpec((None, 1, 6 * d), lambda i: (i // per, 0, 0)), _full((1, d))] + w_specs,
        out_specs=[tok(n) for n in widths],
        out_shape=[jax.ShapeDtypeStruct((b, t, n), F32) for n in widths],
        compiler_params=_cparams(("arbitrary",)),
        name="in_proj",
    )(x, mod, g1, *w_args)


def _pair_diag(x):
    first = (_iota(x.shape, 1) < A_HEAD_DIM).astype(x.dtype)
    return jnp.concatenate([x * first, x * (1 - first)], axis=0)


def _pair_mm3(lhs, rhs, dims=_NN):
    rows = lhs[0].shape[0]
    rh, rl = _split(rhs)
    parts = [_split(l) for l in lhs]
    dot = lambda a, b: lax.dot_general(a, b, (dims, ((), ())), preferred_element_type=F32)
    top = dot(jnp.concatenate([t for part in parts for t in part], axis=0), _pair_diag(rh))
    low = dot(jnp.concatenate([part[0] for part in parts], axis=0), _pair_diag(rl))
    return [top[2 * k * rows:(2 * k + 1) * rows] + top[(2 * k + 1) * rows:(2 * k + 2) * rows]
            + low[k * rows:(k + 1) * rows] for k in range(len(lhs))]


def _pair_index(shape):
    return _iota(shape, 0), _iota(shape, 1) & (A_HEAD_DIM - 1)


def _pair_tri_inverses(mats):
    t_idx, s_idx = _pair_index((CHUNK, LANES))
    same = lambda blk: (lax.shift_right_logical(t_idx, blk.bit_length() - 1)
                        == lax.shift_right_logical(s_idx, blk.bit_length() - 1))
    eye = (t_idx == s_idx).astype(F32)
    nd = [jnp.where(same(8), m, 0.0) for m in mats]
    n2 = [_pair_mm3([x], x)[0] for x in nd]
    t = [eye + x for x in nd]
    both = [_pair_mm3([a, b], a) for a, b in zip(n2, t)]
    t = [x + r[1] for x, r in zip(t, both)]
    t = [x + _pair_mm3([x], r[0])[0] for x, r in zip(t, both)]
    blk = 8
    while blk < CHUNK:
        sel = same(2 * blk) & jnp.logical_not(same(blk))
        w = [_pair_mm3([jnp.where(sel, m, 0.0)], x)[0] for m, x in zip(mats, t)]
        t = [x + _pair_mm3([x], y)[0] for x, y in zip(t, w)]
        blk *= 2
    return t


def _rwkv_kernel(x_ref, s0_ref, mu_ref, wr_ref, wk_ref, wv_ref, w0_ref, w1_ref, w2_ref,
                 a0_ref, a1_ref, a2_ref, g1_ref, g2_ref, kk_ref, ka_ref, rk_ref, lng_ref, lnb_ref,
                 y_ref, s_ref,
                 r_s, v_s, kk_s, cum_s, cx_s, kd_s, bb_s, g_s, bonus_s, y_s, sp_s, *, seq_len):
    total = x_ref.shape[0]
    n_seq = total // seq_len
    n_chunks = seq_len // CHUNK
    n_tiles = total // ROW_TILE
    hd = A_HEAD_DIM
    x = x_ref[...]
    row = _iota(x.shape, 0) & (seq_len - 1)
    prev = jnp.where(row == 0, 0.0, pltpu.roll(x, 1, 0))
    nxt = jnp.where(row == seq_len - 1, 0.0, pltpu.roll(x, total - 1, 0))
    y_s[...] = 0.5 * (prev + nxt) - x
    head_ones = _same_block(A_WIDTH, hd).astype(BF16)
    chunk_diag = _same_block(ROW_TILE, CHUNK)
    run_sum = [(chunk_diag & _tri(d == 1, True, ROW_TILE)).astype(BF16) for d in range(2)]

    def project_tile(ti, carry):
        rows = pl.ds(pl.multiple_of(ti * ROW_TILE, ROW_TILE), ROW_TILE)
        x, xx = x_ref[rows, :], y_s[rows, :]
        xr, xw, xk, xv, xi, xg = (x + xx * mu_ref[j:j + 1, :] for j in range(6))
        r = _mm(xr, wr_ref[...])
        k = _mm(xk, wk_ref[...])
        v = _mm(xv, wv_ref[...])
        g_s[rows, :] = _mm(_sigmoid(_mm(xg, g1_ref[...])), g2_ref[...])
        kk = k * kk_ref[...]
        kk = kk / jnp.maximum(jnp.sqrt(_sum01(_split3(kk * kk), head_ones)), 1e-12)
        r_s[rows, :] = r
        v_s[rows, :] = v
        kk_s[rows, :] = kk
        kd_sum = jnp.zeros_like(x)
        for d in range(2):
            zw = w0_ref[d:d + 1, :] + _mm(jnp.tanh(_mm(xw, w1_ref[d])), w2_ref[d])
            lw = -jnp.exp(-_softplus(-zw) - 0.5)
            cum = _sum01(run_sum[d], _split3(lw))
            cum_s[d, rows, :] = cum
            cx_s[d, rows, :] = cum - lw
            a = _sigmoid(a0_ref[d:d + 1, :] + _mm(_mm(xi, a1_ref[d]), a2_ref[d]))
            kd = k * (1.0 + (a - 1.0) * ka_ref[...])
            kd_s[d, rows, :] = kd
            bb_s[d, rows, :] = kk * a
            kd_sum = kd_sum + kd
        bonus_s[rows, :] = _sum01(_split3(r * kd_sum * rk_ref[...]), head_ones) * v
        y_s[rows, :] = jnp.zeros_like(x)
        return carry

    lax.fori_loop(0, n_tiles, project_tile, 0)
    for n in range(n_seq):
        for d in range(2):
            for p in range(A_HEADS // 2):
                sp_s[n, d, p] = (jnp.zeros((hd, LANES), F32) if s0_ref is None else
                                 jnp.concatenate([s0_ref[n, d, 2 * p], s0_ref[n, d, 2 * p + 1]], axis=1))

    t_idx, s_idx = _pair_index((CHUNK, LANES))
    strict = [s_idx < t_idx, s_idx > t_idx]
    incl = [s_idx <= t_idx, s_idx >= t_idx]
    first_head = _iota((CHUNK, LANES), 1) < hd

    def chunk_body(ci, carry):
        chains = []
        for n, d in [(n, d) for n in range(n_seq) for d in range(2)]:
            cpos = n * n_chunks + ((n_chunks - 1 - ci) if d == 1 else ci)
            last = 0 if d == 1 else CHUNK - 1
            rows = pl.ds(pl.multiple_of(cpos * CHUNK, CHUNK), CHUNK)
            cum, cx = cum_s[d, rows, :], cx_s[d, rows, :]
            tot = cum[last:last + 1, :]
            kdc, bbc = kd_s[d, rows, :], bb_s[d, rows, :]
            e_neg = jnp.exp(-cum)
            e_tail = jnp.exp(tot - cum)
            at = -kk_s[rows, :] * jnp.exp(cx)
            rt = r_s[rows, :] * jnp.exp(cum)
            bt, kt = bbc * e_neg, kdc * e_neg
            bp, kp = bbc * e_tail, kdc * e_tail
            pc = jnp.exp(tot)
            vc = v_s[rows, :]
            for p in range(A_HEADS // 2):
                ln = slice(p * LANES, (p + 1) * LANES)
                chains.append(dict(d=d, st=(n, d, p), rows=rows, ln=ln, v=vc[:, ln], pc=pc[:, ln],
                                   lhs=jnp.concatenate([at[:, ln], rt[:, ln]], axis=0),
                                   bt=bt[:, ln], kt=kt[:, ln], bp=bp[:, ln], kp=kp[:, ln]))
        for c in chains:
            c["sb"] = _pair_mm3([c["lhs"]], c["bt"], _NT)[0]
            c["sk"] = _pair_mm3([c["lhs"]], c["kt"], _NT)[0]
            c["s"] = sp_s[c["st"]]
            c["fs"] = _mm_nt(c["lhs"], _pair_diag(c["s"].astype(BF16)))
            c["v_diag"] = _pair_diag(c["v"].astype(BF16))
        for c in chains:
            a_ak = jnp.where(strict[c["d"]], c["sk"][:CHUNK], 0.0)
            c["x"] = c["fs"][:CHUNK] + _mm(a_ak, c["v_diag"])
        inv = _pair_tri_inverses([jnp.where(strict[c["d"]], c["sb"][:CHUNK], 0.0) for c in chains])
        for c, t in zip(chains, inv):
            c["u"] = _pair_mm3([t], c["x"])[0]
        for c in chains:
            d, rows, ln, u = c["d"], c["rows"], c["ln"], c["u"]
            a_r = jnp.concatenate([jnp.where(incl[d], c["sb"][CHUNK:], 0.0),
                                   jnp.where(incl[d], c["sk"][CHUNK:], 0.0)], axis=1)
            uv_diag = jnp.concatenate([_pair_diag(u.astype(BF16)), c["v_diag"]], axis=0)
            y_s[rows, ln] = y_s[rows, ln] + c["fs"][CHUNK:] + _mm(a_r, uv_diag)
            g = _mm_tn(jnp.concatenate([u, c["v"]], axis=0), jnp.concatenate([c["bp"], c["kp"]], axis=0))
            sp_s[c["st"]] = c["s"] * c["pc"] + jnp.where(first_head, g[:CHUNK], g[CHUNK:])
        return carry

    lax.fori_loop(0, n_chunks, chunk_body, 0)
    for n in range(n_seq):
        for d in range(2):
            for p in range(A_HEADS // 2):
                s_ref[n, d, 2 * p] = sp_s[n, d, p, :, 0:hd]
                s_ref[n, d, 2 * p + 1] = sp_s[n, d, p, :, hd:2 * hd]

    def finish_tile(ti, carry):
        rows = pl.ds(pl.multiple_of(ti * ROW_TILE, ROW_TILE), ROW_TILE)
        ys = y_s[rows, :]
        cen = ys - _sum01(_split3(ys), head_ones) * (1.0 / hd)
        var = _sum01(_split3(cen * cen), head_ones) * (1.0 / hd)
        yn = cen * lax.rsqrt(var + RWKV_GN_EPS) * lng_ref[...] + lnb_ref[...]
        y_ref[rows, :] = (yn + bonus_s[rows, :]) * g_s[rows, :]
        return carry

    lax.fori_loop(0, n_tiles, finish_tile, 0)


def _rwkv(xa, s0, p, n_seq):
    b, t, w = xa.shape
    assert t & (t - 1) == 0 and b % n_seq == 0
    names = ("mu", "wr", "wk", "wv", "w0", "w1", "w2", "a0", "a1", "a2", "g1", "g2",
             "k_k", "k_a", "r_k", "ln_g", "ln_b")
    weights = [p["rwkv_" + n] for n in names]
    rows = n_seq * t
    seq = pl.BlockSpec((None, rows, w), lambda i: (i, 0, 0))
    st = pl.BlockSpec((n_seq, 2, A_HEADS, A_HEAD_DIM, A_HEAD_DIM), lambda i: (i, 0, 0, 0, 0))
    tw = lambda lead=(): pltpu.VMEM(lead + (rows, w), F32)
    kernel = functools.partial(_rwkv_kernel, seq_len=t)
    if s0 is None:
        kernel, state_specs, states = _zero_state_kernel(kernel, 1, 1), [], ()
    else:
        state_specs, states = [st], (s0,)
    y, s_out = pl.pallas_call(
        kernel,
        grid=(b // n_seq,),
        in_specs=[seq] + state_specs + [_full(x.shape) for x in weights],
        out_specs=[seq, st],
        out_shape=[jax.ShapeDtypeStruct((b // n_seq, rows, w), F32),
                   jax.ShapeDtypeStruct((b, 2, A_HEADS, A_HEAD_DIM, A_HEAD_DIM), F32)],
        scratch_shapes=[tw(), tw(), tw(), tw((2,)), tw((2,)), tw((2,)), tw((2,)), tw(), tw(), tw(),
                        pltpu.VMEM((n_seq, 2, A_HEADS // 2, A_HEAD_DIM, LANES), F32)],
        compiler_params=_cparams(("arbitrary",)),
        name="rwkv",
    )(xa.reshape(b // n_seq, rows, w), *states, *weights)
    return y.reshape(b, t, w), s_out


def _mlstm_kernel(u_ref, op_ref, c0_ref, n0_ref, m0_ref, conv_ref, wq_ref, wk_ref, wv_ref,
                  wg_ref, bg_ref, wgt_ref, bgt_ref, lng_ref,
                  y_ref, c_ref, n_ref, m_ref,
                  pad_s, q_s, k_s, v_s, dd_s, cr_s, h_s, *, conv2d, seq_len):
    t_len = u_ref.shape[0]
    n_seq = t_len // seq_len
    n_chunks = seq_len // CHUNK
    hd = B_HEAD_DIM
    u = u_ref[...]
    pad_s[0:CONV_PAD, :] = jnp.zeros((CONV_PAD, B_WIDTH), F32)
    pad_s[CONV_PAD + t_len:CONV_PAD + t_len + CONV_PAD, :] = jnp.zeros((CONV_PAD, B_WIDTH), F32)
    pad_s[CONV_PAD:CONV_PAD + t_len, :] = u
    pos = _iota(u.shape, 0) & (seq_len - 1)
    col = pos & (GRID_W - 1)
    conv = jnp.zeros_like(u)
    for kh in range(3):
        if not conv2d and kh != 1:
            continue
        for kw in range(3):
            shift = (kh - 1) * GRID_W + (kw - 1)
            term = pad_s[CONV_PAD + shift:CONV_PAD + shift + t_len, :] * conv_ref[kh * 3 + kw:kh * 3 + kw + 1, :]
            if conv2d and kw != 1:
                src = col + (kw - 1)
                term = jnp.where((src >= 0) & (src < GRID_W), term, 0.0)
            if n_seq > 1 and shift != 0:
                term = jnp.where((pos + shift >= 0) & (pos + shift < seq_len), term, 0.0)
            conv = conv + term
    uc = _silu(conv)
    for h in range(B_HEADS):
        ln = slice(h * hd, (h + 1) * hd)
        q_s[:, ln] = _mm(uc[:, ln], wq_ref[h])
        k_s[:, ln] = _mm(uc[:, ln], wk_ref[h]) * (hd ** -0.5)
        v_s[:, ln] = _mm(u[:, ln], wv_ref[h])
    gcol = _mm(uc, wg_ref[...]) + bg_ref[...]
    gcol = jnp.where(_iota(gcol.shape, 1) >= 8, -_softplus(-gcol), gcol)
    grow = _mm_nt(wgt_ref[...], uc) + bgt_ref[:, 0:1]
    grow = jnp.where(_iota(grow.shape, 0) >= 8, -_softplus(-grow), grow)
    chunk_diag = _same_block(ROW_TILE, CHUNK)
    run_sum = [(chunk_diag & _tri(d == 1, True, ROW_TILE)).astype(BF16) for d in range(2)]
    per_tile = ROW_TILE // CHUNK
    for ti in range(t_len // ROW_TILE):
        sl = slice(ti * ROW_TILE, (ti + 1) * ROW_TILE)
        gt, rt = _split3(gcol[sl, :]), _split3(grow[:, sl])
        backward_lane = (_iota((ROW_TILE, LANES), 1) & 4) != 0
        cum_c = jnp.where(backward_lane, _sum01(run_sum[1], gt, _NN), _sum01(run_sum[0], gt, _NN))
        dd_s[sl, :] = pltpu.roll(cum_c, LANES - 8, 1) - gcol[sl, :]
        backward_row = (_iota((16, ROW_TILE), 0) & 4) != 0
        cum_r = jnp.where(backward_row, _sum01(rt, run_sum[1], _NT), _sum01(rt, run_sum[0], _NT))
        for c in range(per_tile):
            cr_s[ti * per_tile + c] = cum_r[:, c * CHUNK:(c + 1) * CHUNK]
    h_s[...] = jnp.zeros_like(u)
    for ref, init in ((c_ref, c0_ref), (n_ref, n0_ref), (m_ref, m0_ref)):
        ref[...] = jnp.zeros(ref.shape, F32) if init is None else init[...]
    row_i, col_i = _iota((CHUNK, CHUNK), 0), _iota((CHUNK, CHUNK), 1)
    incl_t = [row_i <= col_i, row_i >= col_i]
    eye = row_i == col_i
    pick = [(_iota((LANES, B_HEADS * LANES), 0)
             == d * B_HEADS + lax.shift_right_logical(_iota((LANES, B_HEADS * LANES), 1), LANES.bit_length() - 1)
             ).astype(BF16) for d in range(2)]
    spread = (lax.shift_right_logical(_iota((2 * CHUNK, 2 * LANES), 0), CHUNK.bit_length() - 1)
              == lax.shift_right_logical(_iota((2 * CHUNK, 2 * LANES), 1), LANES.bit_length() - 1)).astype(BF16)
    ones_bf = jnp.ones((CHUNK, LANES), BF16)

    def chunk_body(ci, carry):
        chains = []
        for n, d in [(n, d) for n in range(n_seq) for d in range(2)]:
            cpos = n * n_chunks + ((n_chunks - 1 - ci) if d == 1 else ci)
            last = 0 if d == 1 else CHUNK - 1
            rows = pl.ds(pl.multiple_of(cpos * CHUNK, CHUNK), CHUNK)
            cr = cr_s[cpos]
            d_col = _sum01(_split3(dd_s[rows, :]), pick[d])
            for h in range(B_HEADS):
                ln = slice(h * hd, (h + 1) * hd)
                lf = 8 + d * 4 + h
                chains.append(dict(d=d, st=(n, d, h), vec=(n, d, slice(h, h + 1)), rows=rows, ln=ln, last=last,
                                   b_row=cr[lf:lf + 1, :], d_col=d_col[:, ln],
                                   m=m_ref[n, d, h:h + 1, 0:1], q=q_s[rows, ln], k=k_s[rows, ln], v=v_s[rows, ln],
                                   c=c_ref[n, d, h], n=n_ref[n, d, h:h + 1, :]))
        for c in chains:
            c["kq"] = _mm_nt(c["k"], c["q"])
            c["qc"] = _mm(c["q"], c["c"])
            c["qn"] = _mm_nt(c["q"], jnp.broadcast_to(c["n"], (LANES, hd)))
            logw = jnp.where(incl_t[c["d"]], c["b_row"] - c["d_col"][:, 0:CHUNK], -jnp.inf)
            c["m_row"] = jnp.maximum(c["b_row"] + c["m"], jnp.max(logw, axis=0, keepdims=True))
            c["logw"] = logw
        for c in chains:
            inter_row = jnp.exp(c["b_row"] + c["m"] - c["m_row"])
            diag = jnp.concatenate([jnp.where(eye, c["m_row"], 0.0), jnp.where(eye, inter_row, 0.0)], axis=1)
            cols = _sum01(_split3(diag), spread)
            c["m_col"], c["inter"] = cols[:, 0:LANES], cols[:, LANES:2 * LANES]
            c["s"] = c["kq"] * jnp.exp(c["logw"] - c["m_row"])
        for c in chains:
            s_hi, s_lo = _split(c["s"])
            sv = lax.dot_general(s_hi, jnp.concatenate([c["v"].astype(BF16), ones_bf], axis=1),
                                 (_TN, ((), ())), preferred_element_type=F32)
            c["sv"] = sv[:, 0:hd]
            c["den"] = (c["inter"] * c["qn"] + sv[:, hd:hd + LANES]
                        + lax.dot_general(s_lo, ones_bf, (_TN, ((), ())), preferred_element_type=F32))
            last = c["last"]
            c["m_new"] = c["m_row"][:, last:last + 1]
            b_last = c["b_row"][:, last:last + 1]
            c["decay"] = jnp.exp(b_last + c["m"] - c["m_new"])
            c["kw"] = c["k"] * jnp.exp(b_last - c["m_new"] - c["d_col"])
        for c in chains:
            c["kv"] = _mm_tn(c["kw"], c["v"])
        for c in chains:
            rows, ln = c["rows"], c["ln"]
            num = c["inter"] * c["qc"] + c["sv"]
            h_s[rows, ln] = h_s[rows, ln] + num / jnp.maximum(jnp.abs(c["den"]), jnp.exp(-c["m_col"]))
            c_ref[c["st"]] = c["decay"] * c["c"] + c["kv"]
            n_ref[c["vec"]] = c["decay"] * c["n"] + jnp.sum(c["kw"], axis=0, keepdims=True)
            m_ref[c["vec"]] = jnp.broadcast_to(c["m_new"], (1, LANES))
        return carry

    lax.fori_loop(0, n_chunks, chunk_body, 0)

    for h in range(B_HEADS):
        ln = slice(h * hd, (h + 1) * hd)
        hh = h_s[:, ln]
        cen = hh - jnp.mean(hh, axis=1, keepdims=True)
        var = jnp.mean(cen * cen, axis=1, keepdims=True)
        y_ref[:, ln] = _sigmoid(op_ref[:, ln]) * (cen * lax.rsqrt(var + MLSTM_GN_EPS) * lng_ref[:, ln])


def _zero_state_kernel(kernel, n_data, n_state):
    def wrapped(*refs, **kw):
        return kernel(*refs[:n_data], *([None] * n_state), *refs[n_data:], **kw)
    return wrapped


def _mlstm(ub, o_pre, states, p, conv2d, n_seq):
    b, t, w = ub.shape
    assert t & (t - 1) == 0 and b % n_seq == 0
    names = ("conv", "wq", "wk", "wv", "wg", "bg", "wgt", "bgt", "ln_g")
    weights = [p["mlstm_" + n] for n in names]
    rows, steps = n_seq * t, b // n_seq
    seq = pl.BlockSpec((None, rows, w), lambda i: (i, 0, 0))
    cst = pl.BlockSpec((n_seq, 2, B_HEADS, B_HEAD_DIM, B_HEAD_DIM), lambda i: (i, 0, 0, 0, 0))
    vst = pl.BlockSpec((n_seq, 2, B_HEADS, LANES), lambda i: (i, 0, 0, 0))
    kernel = functools.partial(_mlstm_kernel, conv2d=conv2d, seq_len=t)
    if states is None:
        kernel, state_specs, states = _zero_state_kernel(kernel, 2, 3), [], ()
    else:
        state_specs = [cst, vst, vst]
    y, c_out, n_out, m_out = pl.pallas_call(
        kernel,
        grid=(steps,),
        in_specs=[seq, seq] + state_specs + [_full(x.shape) for x in weights],
        out_specs=[seq, cst, vst, vst],
        out_shape=[jax.ShapeDtypeStruct((steps, rows, w), F32),
                   jax.ShapeDtypeStruct((b, 2, B_HEADS, B_HEAD_DIM, B_HEAD_DIM), F32),
                   jax.ShapeDtypeStruct((b, 2, B_HEADS, LANES), F32),
                   jax.ShapeDtypeStruct((b, 2, B_HEADS, LANES), F32)],
        scratch_shapes=[pltpu.VMEM((rows + 2 * CONV_PAD, w), F32), pltpu.VMEM((rows, w), F32),
                        pltpu.VMEM((rows, w), F32), pltpu.VMEM((rows, w), F32),
                        pltpu.VMEM((rows, LANES), F32), pltpu.VMEM((rows // CHUNK, 16, CHUNK), F32),
                        pltpu.VMEM((rows, w), F32)],
        compiler_params=_cparams(("arbitrary",)),
        name="mlstm",
    )(ub.reshape(steps, rows, w), o_pre.reshape(steps, rows, w), *states, *weights)
    return y.reshape(b, t, w), c_out, n_out, m_out


def _first_lane_of_max(val, lane, valid):
    masked = jnp.where(valid, val, -jnp.inf)
    best = jnp.max(masked, axis=1, keepdims=True)
    idx = jnp.min(jnp.where(valid & (masked == best), lane, LANES), axis=1, keepdims=True)
    return best, idx


def _merge_kernel(x_ref, ya_ref, yb_ref, mod_ref, g1_ref, wga0_ref, wga1_ref, wgb0_ref, wgb1_ref,
                  wpa_ref, wpb_ref, wo_ref, g2_ref, wr_ref, br_ref, xc_ref):
    d = D_MODEL
    gate1 = mod_ref[:, 2 * d:3 * d]
    shift2, scale2 = mod_ref[:, 3 * d:4 * d], mod_ref[:, 4 * d:5 * d]
    h1 = _modulated_norm1(x_ref, mod_ref, g1_ref)
    pa, pb = _mm(ya_ref[...], wpa_ref[...]), _mm(yb_ref[...], wpb_ref[...])
    halves = []
    for j, (wga_ref, wgb_ref) in enumerate(((wga0_ref, wgb0_ref), (wga1_ref, wgb1_ref))):
        cols = slice(j * W_IN_BLOCK, (j + 1) * W_IN_BLOCK)
        halves.append(_sigmoid(_mm(h1, wga_ref[...])) * pa[:, cols] + _sigmoid(_mm(h1, wgb_ref[...])) * pb[:, cols])
    merged = jnp.concatenate(halves, axis=1)
    x1 = x_ref[...] + gate1 * _mm(merged, wo_ref[...])
    xc_ref[:, 0:d] = x1
    h2 = _rmsnorm(x1, g2_ref[...]) * (1.0 + scale2) + shift2
    logits = _mm(h2, wr_ref[...]) + br_ref[...]
    lane = _iota(logits.shape, 1)
    is_group = lane < N_GROUPS
    g_max, g_sel = _first_lane_of_max(logits, lane, is_group)
    g_w = 1.0 / jnp.sum(jnp.where(is_group, jnp.exp(logits - g_max), 0.0), axis=1, keepdims=True)
    expert = lane - N_GROUPS
    in_group = (expert >= 0) & (expert < N_EXPERTS) & (lax.shift_right_arithmetic(expert, EXPERTS_PER_GROUP.bit_length() - 1) == g_sel)
    e_max = jnp.max(jnp.where(in_group, logits, -jnp.inf), axis=1, keepdims=True)
    e_exp = jnp.where(in_group, jnp.exp(logits - e_max), 0.0)
    prob = e_exp / jnp.sum(e_exp, axis=1, keepdims=True)
    p1, i1 = _first_lane_of_max(prob, lane, in_group)
    p2, i2 = _first_lane_of_max(prob, lane, in_group & (lane != i1))
    denom = p1 + p2
    comb = jnp.where(lane == i1, g_w * p1 / denom, 0.0) + jnp.where(lane == i2, g_w * p2 / denom, 0.0)
    xc_ref[:, d:d + LANES] = jnp.where(lane == GROUP_LANE, g_sel.astype(F32), comb)


def _merge(x, ya, yb, mod, p, tm):
    b, t, d = x.shape
    per = t // tm
    assert d == 2 * W_IN_BLOCK
    tok = lambda n: pl.BlockSpec((None, tm, n), lambda i: (i // per, i % per, 0))
    gate_specs, gate_args = _w_in_blocks(p["w_in"], 3, 4)
    weights = [p["rwkv_w_proj"], p["mlstm_w_proj"], p["w_out"], p["norm2_g"], p["moe_w_router"], p["moe_b_router"]]
    return pl.pallas_call(
        _merge_kernel,
        grid=(b * per,),
        in_specs=[tok(d), tok(A_WIDTH), tok(B_WIDTH), pl.BlockSpec((None, 1, 6 * d), lambda i: (i // per, 0, 0)),
                  _full((1, d))] + gate_specs + [_full(w.shape) for w in weights],
        out_specs=tok(d + LANES),
        out_shape=jax.ShapeDtypeStruct((b, t, d + LANES), F32),
        compiler_params=_cparams(("arbitrary",)),
        name="merge",
    )(x, ya, yb, mod, p["norm1_g"], *gate_args, *weights)


def _route(gsel, tb):
    s, l = gsel.shape
    n_buckets = s * N_GROUPS
    max_tiles = s * (l // tb + N_GROUPS - 1)
    onehot = (gsel[..., None] == jnp.arange(N_GROUPS, dtype=jnp.int32)).astype(jnp.int32)
    rank = jnp.sum((jnp.cumsum(onehot, axis=1) - onehot) * onehot, axis=-1)
    n_tiles = ((jnp.sum(onehot, axis=1) + tb - 1) // tb).reshape(n_buckets)
    ends = jnp.cumsum(n_tiles)
    first_tile = jnp.sum(onehot * (ends - n_tiles).reshape(s, 1, N_GROUPS), axis=-1)
    pos = first_tile * tb + rank
    tok = jnp.arange(s * l, dtype=jnp.int32)
    row_src = jnp.full((max_tiles * tb,), -1, jnp.int32).at[pos.reshape(-1)].set(tok)
    tile_bucket = jnp.sum(jnp.arange(max_tiles, dtype=jnp.int32)[:, None] >= ends[None, :], axis=1)
    tile_bucket = jnp.minimum(tile_bucket, n_buckets - 1).astype(jnp.int32)
    return row_src, tile_bucket // N_GROUPS, tile_bucket % N_GROUPS, ends[-1:].astype(jnp.int32)


def _moe_kernel(gsrc_ref, sdst_ref, seg_ref, grp_ref, used_ref,
                xc_hbm, mod_ref, g2_ref, w1_ref, w3_ref, w2_ref, gf_ref, dump_in_hbm,
                y_hbm, dump_hbm, gbuf, obuf, gsem, ssem, *, tb):
    d = D_MODEL
    q = pl.program_id(0)
    n_used = used_ref[0]

    def start_gather(tile, sl):
        def body(i, carry):
            r0 = pl.multiple_of(i * 8, 8)
            for u in range(8):
                idx = gsrc_ref[tile * tb + r0 + u]
                pltpu.make_async_copy(xc_hbm.at[pl.ds(idx, 1), :], gbuf.at[sl, pl.ds(r0 + u, 1), :],
                                      gsem.at[sl]).start()
            return carry

        lax.fori_loop(0, tb // 8, body, 0)

    def wait_gather(sl):
        pltpu.make_async_copy(xc_hbm.at[pl.ds(0, tb), :], gbuf.at[sl], gsem.at[sl]).wait()

    def start_scatter(tile, sl):
        def body(i, carry):
            r0 = pl.multiple_of(i * 8, 8)
            for u in range(8):
                idx = sdst_ref[tile * tb + r0 + u]

                @pl.when(idx >= 0)
                def _():
                    pltpu.make_async_copy(obuf.at[sl, pl.ds(r0 + u, 1), :], y_hbm.at[pl.ds(idx, 1), :],
                                          ssem.at[sl]).start()

                @pl.when(idx < 0)
                def _():
                    pltpu.make_async_copy(obuf.at[sl, pl.ds(r0 + u, 1), :], dump_hbm.at[sl, pl.ds(r0 + u, 1), :],
                                          ssem.at[sl]).start()

            return carry

        lax.fori_loop(0, tb // 8, body, 0)

    def wait_scatter(sl):
        pltpu.make_async_copy(obuf.at[sl], dump_hbm.at[sl], ssem.at[sl]).wait()

    def step(slot):
        @pl.when(q == 0)
        def _():
            start_gather(0, 0)

        @pl.when(q + 1 < n_used)
        def _():
            start_gather(q + 1, 1 - slot)

        wait_gather(slot)

        @pl.when(q >= 2)
        def _():
            wait_scatter(slot)

        rows = gbuf[slot]
        x1, comb = rows[:, 0:d], rows[:, d:d + LANES]
        shift2, scale2, gate2 = mod_ref[:, 3 * d:4 * d], mod_ref[:, 4 * d:5 * d], mod_ref[:, 5 * d:6 * d]
        h2 = (_rmsnorm(x1, g2_ref[...]) * (1.0 + scale2) + shift2).astype(BF16)
        first_lane = N_GROUPS + grp_ref[q] * EXPERTS_PER_GROUP
        lane = _iota(comb.shape, 1)
        acc = jnp.zeros((tb, d), F32)
        for e in range(EXPERTS_PER_GROUP):
            w_e = jnp.sum(jnp.where(lane == first_lane + e, comb, 0.0), axis=1, keepdims=True)
            a = _mm(h2, w1_ref[e])
            b = _mm(h2, w3_ref[e])
            acc = acc + _mm(_silu(a) * b * w_e, w2_ref[e])
        obuf[slot] = _rmsnorm(x1 + gate2 * acc, gf_ref[...])
        start_scatter(q, slot)

        @pl.when(q == n_used - 1)
        def _():
            @pl.when(q >= 1)
            def _():
                wait_scatter(1 - slot)

            wait_scatter(slot)

    for slot in range(2):
        @pl.when((q < n_used) & (lax.rem(q, 2) == slot))
        def _(slot=slot):
            step(slot)


def _moe(xc, mod, p, tb):
    s, l, width = xc.shape
    d = D_MODEL
    gsel = xc[:, :, d + GROUP_LANE].astype(jnp.int32)
    row_src, tile_seg, tile_grp, n_used = _route(gsel, tb)
    max_tiles = tile_seg.shape[0]
    grouped = lambda w: w.reshape((N_GROUPS, EXPERTS_PER_GROUP) + w.shape[1:])
    w1, w3, w2 = grouped(p["moe_w1"]), grouped(p["moe_w3"]), grouped(p["moe_w2"])
    const = lambda shape: pl.BlockSpec(shape, lambda q, *_: (0,) * len(shape))
    by_group = lambda w: pl.BlockSpec((None,) + w.shape[1:], lambda q, gs, sd, seg, grp, used: (grp[q], 0, 0, 0))
    hbm = pl.BlockSpec(memory_space=pl.ANY)
    y, _ = pl.pallas_call(
        functools.partial(_moe_kernel, tb=tb),
        grid_spec=pltpu.PrefetchScalarGridSpec(
            num_scalar_prefetch=5,
            grid=(max_tiles,),
            in_specs=[hbm, pl.BlockSpec((None, 1, 6 * d), lambda q, gs, sd, seg, grp, used: (seg[q], 0, 0)),
                      const((1, d)), by_group(w1), by_group(w3), by_group(w2), const((1, d)), hbm],
            out_specs=[hbm, hbm],
            scratch_shapes=[pltpu.VMEM((2, tb, width), F32), pltpu.VMEM((2, tb, d), F32),
                            pltpu.SemaphoreType.DMA((2,)), pltpu.SemaphoreType.DMA((2,))]),
        out_shape=[jax.ShapeDtypeStruct((s * l, d), F32), jax.ShapeDtypeStruct((2, tb, d), F32)],
        input_output_aliases={12: 1},
        compiler_params=_cparams(("arbitrary",)),
        name="moe",
    )(jnp.maximum(row_src, 0), row_src, tile_seg, tile_grp, n_used, xc.reshape(s * l, width), mod,
      p["norm2_g"], w1, w3, w2, p["final_norm_g"], jnp.zeros((2, tb, d), F32))
    return y.reshape(s, l, d)


def _trunk(x, mod, mod_seg, states, conv2d, p, tm, tb, mixer_seqs):
    b, t, d = x.shape
    xa, ub, o_pre = _in_proj(x, mod, p["norm1_g"], p["w_in"], tm)
    if states is None:
        s_rwkv, mlstm_states = None, None
    else:
        s_rwkv, s_c, s_n, s_m = states
        mlstm_states = (s_c, s_n, jnp.broadcast_to(s_m[..., None], s_m.shape + (LANES,)))
    ya, s_rwkv = _rwkv(xa, s_rwkv, p, mixer_seqs)
    yb, s_c, s_n, s_m = _mlstm(ub, o_pre, mlstm_states, p, conv2d, mixer_seqs)
    xc = _merge(x, ya, yb, mod, p, tm)
    n_seg = mod_seg.shape[0]
    y = _moe(xc.reshape(n_seg, b * t // n_seg, d + LANES), mod_seg, p, tb)
    return y.reshape(b, t, d), s_rwkv, s_c, s_n, s_m[..., 0]


def kernel(x_prompt, x_sample, c, c_ctx, state_rwkv, state_mlstm_C, state_mlstm_n, state_mlstm_m, w_mod, b_mod, norm1_g, norm2_g, w_in, rwkv_mu, rwkv_w_r, rwkv_w_k, rwkv_w_v, rwkv_w0, rwkv_w1, rwkv_w2, rwkv_a0, rwkv_a1, rwkv_a2, rwkv_g1, rwkv_g2, rwkv_k_k, rwkv_k_a, rwkv_r_k, rwkv_ln_g, rwkv_ln_b, rwkv_w_proj, mlstm_conv, mlstm_w_q, mlstm_w_k, mlstm_w_v, mlstm_w_i, mlstm_b_i, mlstm_w_f, mlstm_b_f, mlstm_ln_g, mlstm_w_proj, w_out, moe_w_group, moe_b_group, moe_w_expert, moe_b_expert, moe_w1, moe_w3, moe_w2, final_norm_g):
    assert w_mod.shape[0] == 1, "single trunk layer"
    l = 0
    bp, dec = x_prompt.shape[0], x_sample.shape[0]
    bf = lambda w: w.astype(BF16)
    row = lambda w: w.reshape(1, -1).astype(F32)
    wg = jnp.concatenate([mlstm_w_i[l, 0], mlstm_w_i[l, 1], mlstm_w_f[l, 0], mlstm_w_f[l, 1]], axis=1)
    bg = jnp.concatenate([mlstm_b_i[l, 0], mlstm_b_i[l, 1], mlstm_b_f[l, 0], mlstm_b_f[l, 1]])
    wg = jnp.pad(wg, ((0, 0), (0, LANES - wg.shape[1])))
    bg = jnp.pad(bg, (0, LANES - bg.shape[0]))
    w_router = jnp.pad(jnp.concatenate([moe_w_group[l], moe_w_expert[l]], axis=1),
                       ((0, 0), (0, LANES - N_GROUPS - N_EXPERTS)))
    b_router = jnp.pad(jnp.concatenate([moe_b_group[l], moe_b_expert[l]]), (0, LANES - N_GROUPS - N_EXPERTS))
    p = {
        "norm1_g": row(norm1_g[l]), "norm2_g": row(norm2_g[l]), "w_in": w_in[l],
        "rwkv_mu": rwkv_mu[l], "rwkv_wr": bf(rwkv_w_r[l]), "rwkv_wk": bf(rwkv_w_k[l]), "rwkv_wv": bf(rwkv_w_v[l]),
        "rwkv_w0": rwkv_w0[l], "rwkv_w1": bf(rwkv_w1[l]), "rwkv_w2": bf(rwkv_w2[l]),
        "rwkv_a0": rwkv_a0[l], "rwkv_a1": bf(rwkv_a1[l]), "rwkv_a2": bf(rwkv_a2[l]),
        "rwkv_g1": bf(rwkv_g1[l]), "rwkv_g2": bf(rwkv_g2[l]),
        "rwkv_k_k": row(rwkv_k_k[l]), "rwkv_k_a": row(rwkv_k_a[l]), "rwkv_r_k": row(rwkv_r_k[l]),
        "rwkv_ln_g": row(rwkv_ln_g[l]), "rwkv_ln_b": row(rwkv_ln_b[l]),
        "rwkv_w_proj": rwkv_w_proj[l],
        "mlstm_conv": mlstm_conv[l].reshape(9, B_WIDTH),
        "mlstm_wq": bf(mlstm_w_q[l]), "mlstm_wk": bf(mlstm_w_k[l]), "mlstm_wv": bf(mlstm_w_v[l]),
        "mlstm_wg": bf(wg), "mlstm_bg": row(bg),
        "mlstm_wgt": bf(wg[:, :16].T), "mlstm_bgt": jnp.broadcast_to(bg[:16, None], (16, LANES)),
        "mlstm_ln_g": row(mlstm_ln_g[l]), "mlstm_w_proj": mlstm_w_proj[l], "w_out": w_out[l],
        "moe_w_router": bf(w_router), "moe_b_router": row(b_router),
        "moe_w1": moe_w1[l], "moe_w3": moe_w3[l], "moe_w2": moe_w2[l],
        "final_norm_g": row(final_norm_g),
    }
    cvec = jnp.concatenate([c_ctx[None, :], c, jnp.zeros((8 - 1 - dec, D_MODEL), F32)], axis=0)
    mod = _mod(cvec, w_mod[l], b_mod[l].reshape(1, -1))
    mod_ctx = jnp.broadcast_to(mod[0:1][:, None, :], (bp, 1, 6 * D_MODEL))
    mod_lat = mod[1:1 + dec][:, None, :]

    yp, n_rwkv, n_c, n_n, n_m = _trunk(x_prompt, mod_ctx, mod[0:1][:, None, :], None, False, p, 256, 512, 2)
    lat_states = tuple(s[:, l].astype(F32) for s in (state_rwkv, state_mlstm_C, state_mlstm_n, state_mlstm_m))
    ys, _, _, _, _ = _trunk(x_sample, mod_lat, mod_lat, lat_states, True, p, 512, 256, 1)
    dt = x_prompt.dtype
    return (yp, ys, n_rwkv[:, None].astype(dt), n_c[:, None].astype(dt), n_n[:, None].astype(dt),
            n_m[:, None].astype(dt))
```

```python
import functools

import jax
import jax.numpy as jnp
from jax import lax
from jax.experimental import pallas as pl
from jax.experimental.pallas import tpu as pltpu

F32 = jnp.float32
BF16 = jnp.bfloat16

D_MODEL = 1024
GRID_W = 64
A_HEAD_DIM = 64
A_WIDTH = 512
A_HEADS = 8
B_HEAD_DIM = 128
B_WIDTH = 512
B_HEADS = 4
CHUNK = 64
ROW_TILE = 256
N_GROUPS = 4
EXPERTS_PER_GROUP = 4
N_EXPERTS = 16
D_EXPERT = 256
GROUP_LANE = N_GROUPS + N_EXPERTS
NORM_EPS = 1e-6
RWKV_GN_EPS = 64e-5
MLSTM_GN_EPS = 1e-5
LANES = 128
SUBLANES = 8
CONV_PAD = 72
VMEM_LIMIT = 56 * 1024 * 1024


def _mm(a, b):
    return jnp.dot(a.astype(BF16), b.astype(BF16), preferred_element_type=F32)


_NN, _NT, _TN = ((1,), (0,)), ((1,), (1,)), ((0,), (0,))


def _split(a):
    hi = a.astype(BF16)
    return hi, (a - hi.astype(F32)).astype(BF16)


def _split3(a):
    hi = a.astype(BF16)
    rest = a - hi.astype(F32)
    lo = rest.astype(BF16)
    return hi, lo, (rest - lo.astype(F32)).astype(BF16)


def _sum01(a, b, dims=_NN):
    dot = lambda p, q: lax.dot_general(p, q, (dims, ((), ())), preferred_element_type=F32)
    terms = [dot(x, b) for x in a] if isinstance(a, tuple) else [dot(a, x) for x in b]
    return functools.reduce(lambda p, q: p + q, terms)


def _mm3(a, b, dims=_NN):
    (ah, al), (bh, bl) = a, b
    dot = lambda p, q: lax.dot_general(p, q, (dims, ((), ())), preferred_element_type=F32)
    return dot(ah, bh) + dot(ah, bl) + dot(al, bh)


def _mm_nt(a, b):
    return lax.dot_general(a.astype(BF16), b.astype(BF16), (((1,), (1,)), ((), ())),
                           preferred_element_type=F32)


def _mm_tn(a, b):
    return lax.dot_general(a.astype(BF16), b.astype(BF16), (((0,), (0,)), ((), ())),
                           preferred_element_type=F32)


def _sigmoid(x):
    return 1.0 / (1.0 + jnp.exp(-x))


def _silu(x):
    return x * _sigmoid(x)


def _softplus(x):
    return jnp.maximum(x, 0.0) + jnp.log(1.0 + jnp.exp(-jnp.abs(x)))


def _iota(shape, dim):
    return lax.broadcasted_iota(jnp.int32, shape, dim)


def _tri(reverse, inclusive, n=CHUNK):
    row, col = _iota((n, n), 0), _iota((n, n), 1)
    if reverse:
        return (col >= row) if inclusive else (col > row)
    return (col <= row) if inclusive else (col < row)


def _same_block(n, blk):
    sh = blk.bit_length() - 1
    row, col = _iota((n, n), 0), _iota((n, n), 1)
    return lax.shift_right_logical(row, sh) == lax.shift_right_logical(col, sh)


def _rmsnorm(x, g):
    return x * lax.rsqrt(jnp.mean(x * x, axis=-1, keepdims=True) + NORM_EPS) * g


def _cparams(sem):
    return pltpu.CompilerParams(dimension_semantics=sem, vmem_limit_bytes=VMEM_LIMIT)


def _full(shape):
    nd = len(shape)
    return pl.BlockSpec(shape, lambda *_: (0,) * nd)


def _mod_kernel(c_ref, w_ref, b_ref, o_ref):
    o_ref[...] = _mm(_silu(c_ref[...]), w_ref[...]) + b_ref[...]


def _mod(cvec, w_mod, b_mod):
    rows, d = cvec.shape
    n = w_mod.shape[1]
    tn = 1536
    return pl.pallas_call(
        _mod_kernel,
        grid=(n // tn,),
        in_specs=[_full((rows, d)), pl.BlockSpec((d, tn), lambda j: (0, j)),
                  pl.BlockSpec((1, tn), lambda j: (0, j))],
        out_specs=pl.BlockSpec((rows, tn), lambda j: (0, j)),
        out_shape=jax.ShapeDtypeStruct((rows, n), F32),
        compiler_params=_cparams(("arbitrary",)),
        name="mod",
    )(cvec, w_mod, b_mod)


W_IN_BLOCK = 512


def _modulated_norm1(x_ref, mod_ref, g_ref):
    d = D_MODEL
    shift, scale = mod_ref[:, 0:d], mod_ref[:, d:2 * d]
    return (_rmsnorm(x_ref[...], g_ref[...]) * (1.0 + scale) + shift).astype(BF16)


def _w_in_blocks(w_in, first, count):
    specs = [pl.BlockSpec((w_in.shape[0], W_IN_BLOCK), lambda *_, j=first + k: (0, j)) for k in range(count)]
    return specs, [w_in] * count


def _in_kernel(x_ref, mod_ref, g_ref, wa_ref, wu_ref, wo_ref, xa_ref, ub_ref, op_ref):
    h = _modulated_norm1(x_ref, mod_ref, g_ref)
    for ref, w_ref in ((xa_ref, wa_ref), (ub_ref, wu_ref), (op_ref, wo_ref)):
        ref[...] = _mm(h, w_ref[...])


def _in_proj(x, mod, g1, w_in, tm):
    b, t, d = x.shape
    per = t // tm
    widths = (A_WIDTH, B_WIDTH, B_WIDTH)
    assert all(n == W_IN_BLOCK for n in widths)
    tok = lambda n: pl.BlockSpec((None, tm, n), lambda i: (i // per, i % per, 0))
    w_specs, w_args = _w_in_blocks(w_in, 0, 3)
    return pl.pallas_call(
        _in_kernel,
        grid=(b * per,),
        in_specs=[tok(d), pl.BlockSpec((None, 1, 6 * d), lambda i: (i // per, 0, 0)), _full((1, d))] + w_specs,
        out_specs=[tok(n) for n in widths],
        out_shape=[jax.ShapeDtypeStruct((b, t, n), F32) for n in widths],
        compiler_params=_cparams(("arbitrary",)),
        name="in_proj",
    )(x, mod, g1, *w_args)


def _head_of_lane(shape):
    return lax.shift_right_logical(_iota(shape, 1), A_HEAD_DIM.bit_length() - 1)


def _pair_diag(x):
    head = _head_of_lane(x.shape)
    return jnp.concatenate([x * (head == h).astype(x.dtype) for h in range(x.shape[1] // A_HEAD_DIM)], axis=0)


def _pair_mm3(lhs, rhs, dims=_NN):
    rows = lhs[0].shape[0]
    rh, rl = _split(rhs)
    parts = [_split(l) for l in lhs]
    dot = lambda a, b: lax.dot_general(a, b, (dims, ((), ())), preferred_element_type=F32)
    top = dot(jnp.concatenate([t for part in parts for t in part], axis=0), _pair_diag(rh))
    low = dot(jnp.concatenate([part[0] for part in parts], axis=0), _pair_diag(rl))
    return [top[2 * k * rows:(2 * k + 1) * rows] + top[(2 * k + 1) * rows:(2 * k + 2) * rows]
            + low[k * rows:(k + 1) * rows] for k in range(len(lhs))]


def _pair_index(shape):
    return _iota(shape, 0), _iota(shape, 1) & (A_HEAD_DIM - 1)


def _pair_tri_inverses(mats):
    t_idx, s_idx = _pair_index(mats[0].shape)
    same = lambda blk: (lax.shift_right_logical(t_idx, blk.bit_length() - 1)
                        == lax.shift_right_logical(s_idx, blk.bit_length() - 1))
    eye = (t_idx == s_idx).astype(F32)
    nd = [jnp.where(same(8), m, 0.0) for m in mats]
    n2 = [_pair_mm3([x], x)[0] for x in nd]
    t = [eye + x for x in nd]
    both = [_pair_mm3([a, b], a) for a, b in zip(n2, t)]
    t = [x + r[1] for x, r in zip(t, both)]
    t = [x + _pair_mm3([x], r[0])[0] for x, r in zip(t, both)]
    blk = 8
    while blk < CHUNK:
        sel = same(2 * blk) & jnp.logical_not(same(blk))
        w = [_pair_mm3([jnp.where(sel, m, 0.0)], x)[0] for m, x in zip(mats, t)]
        t = [x + _pair_mm3([x], y)[0] for x, y in zip(t, w)]
        blk *= 2
    return t


def _rwkv_kernel(x_ref, s0_ref, mu_ref, wr_ref, wk_ref, wv_ref, w0_ref, w1_ref, w2_ref,
                 a0_ref, a1_ref, a2_ref, g1_ref, g2_ref, kk_ref, ka_ref, rk_ref, lng_ref, lnb_ref,
                 y_ref, s_ref,
                 r_s, v_s, kk_s, cum_s, cx_s, kd_s, bb_s, g_s, bonus_s, y_s, sp_s, *, seq_len):
    total = x_ref.shape[0]
    n_seq = total // seq_len
    n_chunks = seq_len // CHUNK
    n_tiles = total // ROW_TILE
    hd = A_HEAD_DIM
    n_groups, width = sp_s.shape[2], sp_s.shape[4]
    per_group = width // hd
    x = x_ref[...]
    row = _iota(x.shape, 0) & (seq_len - 1)
    prev = jnp.where(row == 0, 0.0, pltpu.roll(x, 1, 0))
    nxt = jnp.where(row == seq_len - 1, 0.0, pltpu.roll(x, total - 1, 0))
    y_s[...] = 0.5 * (prev + nxt) - x
    head_ones = _same_block(A_WIDTH, hd).astype(BF16)
    chunk_diag = _same_block(ROW_TILE, CHUNK)
    run_sum = [(chunk_diag & _tri(d == 1, True, ROW_TILE)).astype(BF16) for d in range(2)]

    def project_tile(ti, carry):
        rows = pl.ds(pl.multiple_of(ti * ROW_TILE, ROW_TILE), ROW_TILE)
        x, xx = x_ref[rows, :], y_s[rows, :]
        xr, xw, xk, xv, xi, xg = (x + xx * mu_ref[j:j + 1, :] for j in range(6))
        r = _mm(xr, wr_ref[...])
        k = _mm(xk, wk_ref[...])
        v = _mm(xv, wv_ref[...])
        g_s[rows, :] = _mm(_sigmoid(_mm(xg, g1_ref[...])), g2_ref[...])
        kk = k * kk_ref[...]
        kk = kk / jnp.maximum(jnp.sqrt(_sum01(_split(kk * kk), head_ones)), 1e-12)
        r_s[rows, :] = r
        v_s[rows, :] = v
        kk_s[rows, :] = kk
        kd_sum = jnp.zeros_like(x)
        for d in range(2):
            zw = w0_ref[d:d + 1, :] + _mm(jnp.tanh(_mm(xw, w1_ref[d])), w2_ref[d])
            lw = -jnp.exp(-_softplus(-zw) - 0.5)
            cum = _sum01(run_sum[d], _split3(lw))
            cum_s[d, rows, :] = cum
            cx_s[d, rows, :] = cum - lw
            a = _sigmoid(a0_ref[d:d + 1, :] + _mm(_mm(xi, a1_ref[d]), a2_ref[d]))
            kd = k * (1.0 + (a - 1.0) * ka_ref[...])
            kd_s[d, rows, :] = kd
            bb_s[d, rows, :] = kk * a
            kd_sum = kd_sum + kd
        bonus_s[rows, :] = _sum01(_split(r * kd_sum * rk_ref[...]), head_ones) * v
        y_s[rows, :] = jnp.zeros_like(x)
        return carry

    lax.fori_loop(0, n_tiles, project_tile, 0)
    for n in range(n_seq):
        for d in range(2):
            for p in range(n_groups):
                sp_s[n, d, p] = (jnp.zeros((hd, width), F32) if s0_ref is None else jnp.concatenate(
                    [s0_ref[n, d, per_group * p + j] for j in range(per_group)], axis=1))

    t_idx, s_idx = _pair_index((CHUNK, width))
    strict = [s_idx < t_idx, s_idx > t_idx]
    incl = [s_idx <= t_idx, s_idx >= t_idx]
    lane_head = _head_of_lane((CHUNK, width))

    def chunk_body(ci, carry):
        chains = []
        for n, d in [(n, d) for n in range(n_seq) for d in range(2)]:
            cpos = n * n_chunks + ((n_chunks - 1 - ci) if d == 1 else ci)
            last = 0 if d == 1 else CHUNK - 1
            rows = pl.ds(pl.multiple_of(cpos * CHUNK, CHUNK), CHUNK)
            cum, cx = cum_s[d, rows, :], cx_s[d, rows, :]
            tot = cum[last:last + 1, :]
            kdc, bbc = kd_s[d, rows, :], bb_s[d, rows, :]
            e_neg = jnp.exp(-cum)
            e_tail = jnp.exp(tot - cum)
            at = -kk_s[rows, :] * jnp.exp(cx)
            rt = r_s[rows, :] * jnp.exp(cum)
            bt, kt = bbc * e_neg, kdc * e_neg
            bp, kp = bbc * e_tail, kdc * e_tail
            pc = jnp.exp(tot)
            vc = v_s[rows, :]
            for p in range(n_groups):
                ln = slice(p * width, (p + 1) * width)
                chains.append(dict(d=d, st=(n, d, p), rows=rows, ln=ln, v=vc[:, ln], pc=pc[:, ln],
                                   lhs=jnp.concatenate([at[:, ln], rt[:, ln]], axis=0),
                                   bt=bt[:, ln], kt=kt[:, ln], bp=bp[:, ln], kp=kp[:, ln]))
        for c in chains:
            c["sb"] = _pair_mm3([c["lhs"]], c["bt"], _NT)[0]
            c["sk"] = _pair_mm3([c["lhs"]], c["kt"], _NT)[0]
            c["s"] = sp_s[c["st"]]
            c["fs"] = _mm_nt(c["lhs"], _pair_diag(c["s"].astype(BF16)))
            c["v_diag"] = _pair_diag(c["v"].astype(BF16))
        for c in chains:
            a_ak = jnp.where(strict[c["d"]], c["sk"][:CHUNK], 0.0)
            c["x"] = c["fs"][:CHUNK] + _mm(a_ak, c["v_diag"])
        inv = _pair_tri_inverses([jnp.where(strict[c["d"]], c["sb"][:CHUNK], 0.0) for c in chains])
        for c, t in zip(chains, inv):
            c["u"] = _pair_mm3([t], c["x"])[0]
        for c in chains:
            d, rows, ln, u = c["d"], c["rows"], c["ln"], c["u"]
            a_r = jnp.concatenate([jnp.where(incl[d], c["sb"][CHUNK:], 0.0),
                                   jnp.where(incl[d], c["sk"][CHUNK:], 0.0)], axis=1)
            uv_diag = jnp.concatenate([_pair_diag(u.astype(BF16)), c["v_diag"]], axis=0)
            y_s[rows, ln] = y_s[rows, ln] + c["fs"][CHUNK:] + _mm(a_r, uv_diag)
            g = _mm_tn(jnp.concatenate([u, c["v"]], axis=0), jnp.concatenate([c["bp"], c["kp"]], axis=0))
            own = g[0:hd]
            for j in range(1, per_group):
                own = jnp.where(lane_head == j, g[j * hd:(j + 1) * hd], own)
            sp_s[c["st"]] = c["s"] * c["pc"] + own
        return carry

    lax.fori_loop(0, n_chunks, chunk_body, 0)
    for n in range(n_seq):
        for d in range(2):
            for p in range(n_groups):
                for j in range(per_group):
                    s_ref[n, d, per_group * p + j] = sp_s[n, d, p, :, j * hd:(j + 1) * hd]

    def finish_tile(ti, carry):
        rows = pl.ds(pl.multiple_of(ti * ROW_TILE, ROW_TILE), ROW_TILE)
        ys = y_s[rows, :]
        cen = ys - _sum01(_split(ys), head_ones) * (1.0 / hd)
        var = _sum01(_split(cen * cen), head_ones) * (1.0 / hd)
        yn = cen * lax.rsqrt(var + RWKV_GN_EPS) * lng_ref[...] + lnb_ref[...]
        y_ref[rows, :] = (yn + bonus_s[rows, :]) * g_s[rows, :]
        return carry

    lax.fori_loop(0, n_tiles, finish_tile, 0)


def _rwkv(xa, s0, p, n_seq, heads_per_block):
    b, t, w = xa.shape
    assert t & (t - 1) == 0 and b % n_seq == 0
    names = ("mu", "wr", "wk", "wv", "w0", "w1", "w2", "a0", "a1", "a2", "g1", "g2",
             "k_k", "k_a", "r_k", "ln_g", "ln_b")
    weights = [p["rwkv_" + n] for n in names]
    rows = n_seq * t
    seq = pl.BlockSpec((None, rows, w), lambda i: (i, 0, 0))
    st = pl.BlockSpec((n_seq, 2, A_HEADS, A_HEAD_DIM, A_HEAD_DIM), lambda i: (i, 0, 0, 0, 0))
    tw = lambda lead=(): pltpu.VMEM(lead + (rows, w), F32)
    kernel = functools.partial(_rwkv_kernel, seq_len=t)
    if s0 is None:
        kernel, state_specs, states = _zero_state_kernel(kernel, 1, 1), [], ()
    else:
        state_specs, states = [st], (s0,)
    y, s_out = pl.pallas_call(
        kernel,
        grid=(b // n_seq,),
        in_specs=[seq] + state_specs + [_full(x.shape) for x in weights],
        out_specs=[seq, st],
        out_shape=[jax.ShapeDtypeStruct((b // n_seq, rows, w), F32),
                   jax.ShapeDtypeStruct((b, 2, A_HEADS, A_HEAD_DIM, A_HEAD_DIM), F32)],
        scratch_shapes=[tw(), tw(), tw(), tw((2,)), tw((2,)), tw((2,)), tw((2,)), tw(), tw(), tw(),
                        pltpu.VMEM((n_seq, 2, A_HEADS // heads_per_block, A_HEAD_DIM,
                                    heads_per_block * A_HEAD_DIM), F32)],
        compiler_params=_cparams(("arbitrary",)),
        name="rwkv",
    )(xa.reshape(b // n_seq, rows, w), *states, *weights)
    return y.reshape(b, t, w), s_out


def _mlstm_kernel(u_ref, op_ref, c0_ref, n0_ref, m0_ref, conv_ref, wq_ref, wk_ref, wv_ref,
                  wg_ref, bg_ref, wgt_ref, bgt_ref, lng_ref,
                  y_ref, c_ref, n_ref, m_ref,
                  pad_s, q_s, k_s, v_s, dd_s, cr_s, h_s, *, conv2d, seq_len):
    t_len = u_ref.shape[0]
    n_seq = t_len // seq_len
    n_chunks = seq_len // CHUNK
    hd = B_HEAD_DIM
    u = u_ref[...]
    pad_s[0:CONV_PAD, :] = jnp.zeros((CONV_PAD, B_WIDTH), F32)
    pad_s[CONV_PAD + t_len:CONV_PAD + t_len + CONV_PAD, :] = jnp.zeros((CONV_PAD, B_WIDTH), F32)
    pad_s[CONV_PAD:CONV_PAD + t_len, :] = u
    pos = _iota(u.shape, 0) & (seq_len - 1)
    col = pos & (GRID_W - 1)
    conv = jnp.zeros_like(u)
    for kh in range(3):
        if not conv2d and kh != 1:
            continue
        for kw in range(3):
            shift = (kh - 1) * GRID_W + (kw - 1)
            term = pad_s[CONV_PAD + shift:CONV_PAD + shift + t_len, :] * conv_ref[kh * 3 + kw:kh * 3 + kw + 1, :]
            if conv2d and kw != 1:
                src = col + (kw - 1)
                term = jnp.where((src >= 0) & (src < GRID_W), term, 0.0)
            if n_seq > 1 and shift != 0:
                term = jnp.where((pos + shift >= 0) & (pos + shift < seq_len), term, 0.0)
            conv = conv + term
    uc = _silu(conv)
    for h in range(B_HEADS):
        ln = slice(h * hd, (h + 1) * hd)
        q_s[:, ln] = _mm(uc[:, ln], wq_ref[h])
        k_s[:, ln] = _mm(uc[:, ln], wk_ref[h]) * (hd ** -0.5)
        v_s[:, ln] = _mm(u[:, ln], wv_ref[h])
    gcol = _mm(uc, wg_ref[...]) + bg_ref[...]
    gcol = jnp.where(_iota(gcol.shape, 1) >= 8, -_softplus(-gcol), gcol)
    grow = _mm_nt(wgt_ref[...], uc) + bgt_ref[:, 0:1]
    grow = jnp.where(_iota(grow.shape, 0) >= 8, -_softplus(-grow), grow)
    chunk_diag = _same_block(ROW_TILE, CHUNK)
    run_sum = [(chunk_diag & _tri(d == 1, True, ROW_TILE)).astype(BF16) for d in range(2)]
    per_tile = ROW_TILE // CHUNK
    for ti in range(t_len // ROW_TILE):
        sl = slice(ti * ROW_TILE, (ti + 1) * ROW_TILE)
        gt, rt = _split3(gcol[sl, :]), _split3(grow[:, sl])
        backward_lane = (_iota((ROW_TILE, LANES), 1) & 4) != 0
        cum_c = jnp.where(backward_lane, _sum01(run_sum[1], gt, _NN), _sum01(run_sum[0], gt, _NN))
        dd_s[sl, :] = pltpu.roll(cum_c, LANES - 8, 1) - gcol[sl, :]
        backward_row = (_iota((16, ROW_TILE), 0) & 4) != 0
        cum_r = jnp.where(backward_row, _sum01(rt, run_sum[1], _NT), _sum01(rt, run_sum[0], _NT))
        for c in range(per_tile):
            cr_s[ti * per_tile + c] = cum_r[:, c * CHUNK:(c + 1) * CHUNK]
    h_s[...] = jnp.zeros_like(u)
    for ref, init in ((c_ref, c0_ref), (n_ref, n0_ref), (m_ref, m0_ref)):
        ref[...] = jnp.zeros(ref.shape, F32) if init is None else init[...]
    row_i, col_i = _iota((CHUNK, CHUNK), 0), _iota((CHUNK, CHUNK), 1)
    incl_t = [row_i <= col_i, row_i >= col_i]
    eye = row_i == col_i
    pick = [(_iota((LANES, B_HEADS * LANES), 0)
             == d * B_HEADS + lax.shift_right_logical(_iota((LANES, B_HEADS * LANES), 1), LANES.bit_length() - 1)
             ).astype(BF16) for d in range(2)]
    spread = (lax.shift_right_logical(_iota((2 * CHUNK, 2 * LANES), 0), CHUNK.bit_length() - 1)
              == lax.shift_right_logical(_iota((2 * CHUNK, 2 * LANES), 1), LANES.bit_length() - 1)).astype(BF16)
    ones_bf = jnp.ones((CHUNK, LANES), BF16)

    def chunk_body(ci, carry):
        chains = []
        for n, d in [(n, d) for n in range(n_seq) for d in range(2)]:
            cpos = n * n_chunks + ((n_chunks - 1 - ci) if d == 1 else ci)
            last = 0 if d == 1 else CHUNK - 1
            rows = pl.ds(pl.multiple_of(cpos * CHUNK, CHUNK), CHUNK)
            cr = cr_s[cpos]
            d_col = _sum01(_split3(dd_s[rows, :]), pick[d])
            for h in range(B_HEADS):
                ln = slice(h * hd, (h + 1) * hd)
                lf = 8 + d * 4 + h
                chains.append(dict(d=d, st=(n, d, h), vec=(n, d, slice(h, h + 1)), rows=rows, ln=ln, last=last,
                                   b_row=cr[lf:lf + 1, :], d_col=d_col[:, ln],
                                   m=m_ref[n, d, h:h + 1, 0:1], q=q_s[rows, ln], k=k_s[rows, ln], v=v_s[rows, ln],
                                   c=c_ref[n, d, h], n=n_ref[n, d, h:h + 1, :]))
        for c in chains:
            c["kq"] = _mm_nt(c["k"], c["q"])
            c["qc"] = _mm(c["q"], c["c"])
            c["qn"] = _mm_nt(c["q"], jnp.broadcast_to(c["n"], (LANES, hd)))
            logw = jnp.where(incl_t[c["d"]], c["b_row"] - c["d_col"][:, 0:CHUNK], -jnp.inf)
            c["m_row"] = jnp.maximum(c["b_row"] + c["m"], jnp.max(logw, axis=0, keepdims=True))
            c["logw"] = logw
        for c in chains:
            inter_row = jnp.exp(c["b_row"] + c["m"] - c["m_row"])
            diag = jnp.concatenate([jnp.where(eye, c["m_row"], 0.0), jnp.where(eye, inter_row, 0.0)], axis=1)
            cols = _sum01(_split3(diag), spread)
            c["m_col"], c["inter"] = cols[:, 0:LANES], cols[:, LANES:2 * LANES]
            c["s"] = c["kq"] * jnp.exp(c["logw"] - c["m_row"])
        for c in chains:
            s_hi, s_lo = _split(c["s"])
            sv = lax.dot_general(s_hi, jnp.concatenate([c["v"].astype(BF16), ones_bf], axis=1),
                                 (_TN, ((), ())), preferred_element_type=F32)
            c["sv"] = sv[:, 0:hd]
            c["den"] = (c["inter"] * c["qn"] + sv[:, hd:hd + LANES]
                        + lax.dot_general(s_lo, ones_bf, (_TN, ((), ())), preferred_element_type=F32))
            last = c["last"]
            c["m_new"] = c["m_row"][:, last:last + 1]
            b_last = c["b_row"][:, last:last + 1]
            c["decay"] = jnp.exp(b_last + c["m"] - c["m_new"])
            c["kw"] = c["k"] * jnp.exp(b_last - c["m_new"] - c["d_col"])
        for c in chains:
            c["kv"] = _mm_tn(c["kw"], c["v"])
        for c in chains:
            rows, ln = c["rows"], c["ln"]
            num = c["inter"] * c["qc"] + c["sv"]
            h_s[rows, ln] = h_s[rows, ln] + num / jnp.maximum(jnp.abs(c["den"]), jnp.exp(-c["m_col"]))
            c_ref[c["st"]] = c["decay"] * c["c"] + c["kv"]
            n_ref[c["vec"]] = c["decay"] * c["n"] + jnp.sum(c["kw"], axis=0, keepdims=True)
            m_ref[c["vec"]] = jnp.broadcast_to(c["m_new"], (1, LANES))
        return carry

    lax.fori_loop(0, n_chunks, chunk_body, 0)

    for h in range(B_HEADS):
        ln = slice(h * hd, (h + 1) * hd)
        hh = h_s[:, ln]
        cen = hh - jnp.mean(hh, axis=1, keepdims=True)
        var = jnp.mean(cen * cen, axis=1, keepdims=True)
        y_ref[:, ln] = _sigmoid(op_ref[:, ln]) * (cen * lax.rsqrt(var + MLSTM_GN_EPS) * lng_ref[:, ln])


def _zero_state_kernel(kernel, n_data, n_state):
    def wrapped(*refs, **kw):
        return kernel(*refs[:n_data], *([None] * n_state), *refs[n_data:], **kw)
    return wrapped


def _mlstm(ub, o_pre, states, p, conv2d, n_seq):
    b, t, w = ub.shape
    assert t & (t - 1) == 0 and b % n_seq == 0
    names = ("conv", "wq", "wk", "wv", "wg", "bg", "wgt", "bgt", "ln_g")
    weights = [p["mlstm_" + n] for n in names]
    rows, steps = n_seq * t, b // n_seq
    seq = pl.BlockSpec((None, rows, w), lambda i: (i, 0, 0))
    cst = pl.BlockSpec((n_seq, 2, B_HEADS, B_HEAD_DIM, B_HEAD_DIM), lambda i: (i, 0, 0, 0, 0))
    vst = pl.BlockSpec((n_seq, 2, B_HEADS, LANES), lambda i: (i, 0, 0, 0))
    kernel = functools.partial(_mlstm_kernel, conv2d=conv2d, seq_len=t)
    if states is None:
        kernel, state_specs, states = _zero_state_kernel(kernel, 2, 3), [], ()
    else:
        state_specs = [cst, vst, vst]
    y, c_out, n_out, m_out = pl.pallas_call(
        kernel,
        grid=(steps,),
        in_specs=[seq, seq] + state_specs + [_full(x.shape) for x in weights],
        out_specs=[seq, cst, vst, vst],
        out_shape=[jax.ShapeDtypeStruct((steps, rows, w), F32),
                   jax.ShapeDtypeStruct((b, 2, B_HEADS, B_HEAD_DIM, B_HEAD_DIM), F32),
                   jax.ShapeDtypeStruct((b, 2, B_HEADS, LANES), F32),
                   jax.ShapeDtypeStruct((b, 2, B_HEADS, LANES), F32)],
        scratch_shapes=[pltpu.VMEM((rows + 2 * CONV_PAD, w), F32), pltpu.VMEM((rows, w), F32),
                        pltpu.VMEM((rows, w), F32), pltpu.VMEM((rows, w), F32),
                        pltpu.VMEM((rows, LANES), F32), pltpu.VMEM((rows // CHUNK, 16, CHUNK), F32),
                        pltpu.VMEM((rows, w), F32)],
        compiler_params=_cparams(("arbitrary",)),
        name="mlstm",
    )(ub.reshape(steps, rows, w), o_pre.reshape(steps, rows, w), *states, *weights)
    return y.reshape(b, t, w), c_out, n_out, m_out


def _first_lane_of_max(val, lane, valid):
    masked = jnp.where(valid, val, -jnp.inf)
    best = jnp.max(masked, axis=1, keepdims=True)
    idx = jnp.min(jnp.where(valid & (masked == best), lane, LANES), axis=1, keepdims=True)
    return best, idx


def _merge_kernel(x_ref, ya_ref, yb_ref, mod_ref, g1_ref, wga0_ref, wga1_ref, wgb0_ref, wgb1_ref,
                  wpa_ref, wpb_ref, wo_ref, g2_ref, wr_ref, br_ref, xc_ref):
    d = D_MODEL
    gate1 = mod_ref[:, 2 * d:3 * d]
    shift2, scale2 = mod_ref[:, 3 * d:4 * d], mod_ref[:, 4 * d:5 * d]
    h1 = _modulated_norm1(x_ref, mod_ref, g1_ref)
    pa, pb = _mm(ya_ref[...], wpa_ref[...]), _mm(yb_ref[...], wpb_ref[...])
    halves = []
    for j, (wga_ref, wgb_ref) in enumerate(((wga0_ref, wgb0_ref), (wga1_ref, wgb1_ref))):
        cols = slice(j * W_IN_BLOCK, (j + 1) * W_IN_BLOCK)
        halves.append(_sigmoid(_mm(h1, wga_ref[...])) * pa[:, cols] + _sigmoid(_mm(h1, wgb_ref[...])) * pb[:, cols])
    merged = jnp.concatenate(halves, axis=1)
    x1 = x_ref[...] + gate1 * _mm(merged, wo_ref[...])
    xc_ref[:, 0:d] = x1
    h2 = _rmsnorm(x1, g2_ref[...]) * (1.0 + scale2) + shift2
    logits = _mm(h2, wr_ref[...]) + br_ref[...]
    lane = _iota(logits.shape, 1)
    is_group = lane < N_GROUPS
    g_max, g_sel = _first_lane_of_max(logits, lane, is_group)
    g_w = 1.0 / jnp.sum(jnp.where(is_group, jnp.exp(logits - g_max), 0.0), axis=1, keepdims=True)
    expert = lane - N_GROUPS
    in_group = (expert >= 0) & (expert < N_EXPERTS) & (lax.shift_right_arithmetic(expert, EXPERTS_PER_GROUP.bit_length() - 1) == g_sel)
    e_max = jnp.max(jnp.where(in_group, logits, -jnp.inf), axis=1, keepdims=True)
    e_exp = jnp.where(in_group, jnp.exp(logits - e_max), 0.0)
    prob = e_exp / jnp.sum(e_exp, axis=1, keepdims=True)
    p1, i1 = _first_lane_of_max(prob, lane, in_group)
    p2, i2 = _first_lane_of_max(prob, lane, in_group & (lane != i1))
    denom = p1 + p2
    comb = jnp.where(lane == i1, g_w * p1 / denom, 0.0) + jnp.where(lane == i2, g_w * p2 / denom, 0.0)
    xc_ref[:, d:d + LANES] = jnp.where(lane == GROUP_LANE, g_sel.astype(F32), comb)


def _merge(x, ya, yb, mod, p, tm):
    b, t, d = x.shape
    per = t // tm
    assert d == 2 * W_IN_BLOCK
    tok = lambda n: pl.BlockSpec((None, tm, n), lambda i: (i // per, i % per, 0))
    gate_specs, gate_args = _w_in_blocks(p["w_in"], 3, 4)
    weights = [p["rwkv_w_proj"], p["mlstm_w_proj"], p["w_out"], p["norm2_g"], p["moe_w_router"], p["moe_b_router"]]
    return pl.pallas_call(
        _merge_kernel,
        grid=(b * per,),
        in_specs=[tok(d), tok(A_WIDTH), tok(B_WIDTH), pl.BlockSpec((None, 1, 6 * d), lambda i: (i // per, 0, 0)),
                  _full((1, d))] + gate_specs + [_full(w.shape) for w in weights],
        out_specs=tok(d + LANES),
        out_shape=jax.ShapeDtypeStruct((b, t, d + LANES), F32),
        compiler_params=_cparams(("arbitrary",)),
        name="merge",
    )(x, ya, yb, mod, p["norm1_g"], *gate_args, *weights)


def _route(gsel, tb):
    s, l = gsel.shape
    n_buckets = s * N_GROUPS
    max_tiles = s * (l // tb + N_GROUPS - 1)
    onehot = (gsel[..., None] == jnp.arange(N_GROUPS, dtype=jnp.int32)).astype(jnp.int32)
    rank = jnp.sum((jnp.cumsum(onehot, axis=1) - onehot) * onehot, axis=-1)
    n_tiles = ((jnp.sum(onehot, axis=1) + tb - 1) // tb).reshape(n_buckets)
    ends = jnp.cumsum(n_tiles)
    first_tile = jnp.sum(onehot * (ends - n_tiles).reshape(s, 1, N_GROUPS), axis=-1)
    pos = first_tile * tb + rank
    tok = jnp.arange(s * l, dtype=jnp.int32)
    row_src = jnp.full((max_tiles * tb,), -1, jnp.int32).at[pos.reshape(-1)].set(tok)
    tile_bucket = jnp.sum(jnp.arange(max_tiles, dtype=jnp.int32)[:, None] >= ends[None, :], axis=1)
    tile_bucket = jnp.minimum(tile_bucket, n_buckets - 1).astype(jnp.int32)
    return row_src, tile_bucket // N_GROUPS, tile_bucket % N_GROUPS, ends[-1:].astype(jnp.int32)


def _moe_kernel(gsrc_ref, sdst_ref, seg_ref, grp_ref, used_ref,
                xc_hbm, mod_ref, g2_ref, w1_ref, w3_ref, w2_ref, gf_ref, dump_in_hbm,
                y_hbm, dump_hbm, gbuf, obuf, gsem, ssem, *, tb):
    d = D_MODEL
    q = pl.program_id(0)
    n_used = used_ref[0]

    def start_gather(tile, sl):
        def body(i, carry):
            for u in range(SUBLANES):
                idx = gsrc_ref[tile * tb + i * SUBLANES + u]
                pltpu.make_async_copy(xc_hbm.at[lax.shift_right_logical(idx, SUBLANES.bit_length() - 1), pl.ds(idx & (SUBLANES - 1), 1), :],
                                      gbuf.at[sl, i, pl.ds(u, 1), :], gsem.at[sl]).start()
            return carry

        lax.fori_loop(0, tb // SUBLANES, body, 0)

    def wait_gather(sl):
        pltpu.make_async_copy(xc_hbm.at[pl.ds(0, tb // SUBLANES)], gbuf.at[sl], gsem.at[sl]).wait()

    def start_scatter(tile, sl):
        def body(i, carry):
            for u in range(SUBLANES):
                idx = sdst_ref[tile * tb + i * SUBLANES + u]

                @pl.when(idx >= 0)
                def _():
                    pltpu.make_async_copy(
                        obuf.at[sl, i, pl.ds(u, 1), :],
                        y_hbm.at[lax.shift_right_logical(idx, SUBLANES.bit_length() - 1), pl.ds(idx & (SUBLANES - 1), 1), :],
                        ssem.at[sl]).start()

                @pl.when(idx < 0)
                def _():
                    pltpu.make_async_copy(obuf.at[sl, i, pl.ds(u, 1), :], dump_hbm.at[sl, i, pl.ds(u, 1), :],
                                          ssem.at[sl]).start()

            return carry

        lax.fori_loop(0, tb // SUBLANES, body, 0)

    def wait_scatter(sl):
        pltpu.make_async_copy(obuf.at[sl], dump_hbm.at[sl], ssem.at[sl]).wait()

    def step(slot):
        @pl.when(q == 0)
        def _():
            start_gather(0, 0)

        @pl.when(q + 1 < n_used)
        def _():
            start_gather(q + 1, 1 - slot)

        wait_gather(slot)

        @pl.when(q >= 2)
        def _():
            wait_scatter(slot)

        rows = gbuf[slot].reshape(tb, gbuf.shape[-1])
        x1, comb = rows[:, 0:d], rows[:, d:d + LANES]
        shift2, scale2, gate2 = mod_ref[:, 3 * d:4 * d], mod_ref[:, 4 * d:5 * d], mod_ref[:, 5 * d:6 * d]
        h2 = (_rmsnorm(x1, g2_ref[...]) * (1.0 + scale2) + shift2).astype(BF16)
        first_lane = N_GROUPS + grp_ref[q] * EXPERTS_PER_GROUP
        lane = _iota(comb.shape, 1)
        acc = jnp.zeros((tb, d), F32)
        for e in range(EXPERTS_PER_GROUP):
            w_e = jnp.sum(jnp.where(lane == first_lane + e, comb, 0.0), axis=1, keepdims=True)
            a = _mm(h2, w1_ref[e])
            b = _mm(h2, w3_ref[e])
            acc = acc + _mm(_silu(a) * b * w_e, w2_ref[e])
        obuf[slot] = _rmsnorm(x1 + gate2 * acc, gf_ref[...]).reshape(tb // SUBLANES, SUBLANES, d)
        start_scatter(q, slot)

        @pl.when(q == n_used - 1)
        def _():
            @pl.when(q >= 1)
            def _():
                wait_scatter(1 - slot)

            wait_scatter(slot)

    for slot in range(2):
        @pl.when((q < n_used) & (lax.rem(q, 2) == slot))
        def _(slot=slot):
            step(slot)


def _moe(xc, mod, p, tb):
    s, l, width = xc.shape
    d = D_MODEL
    gsel = xc[:, :, d + GROUP_LANE].astype(jnp.int32)
    row_src, tile_seg, tile_grp, n_used = _route(gsel, tb)
    max_tiles = tile_seg.shape[0]
    grouped = lambda w: w.reshape((N_GROUPS, EXPERTS_PER_GROUP) + w.shape[1:])
    w1, w3, w2 = grouped(p["moe_w1"]), grouped(p["moe_w3"]), grouped(p["moe_w2"])
    const = lambda shape: pl.BlockSpec(shape, lambda q, *_: (0,) * len(shape))
    by_group = lambda w: pl.BlockSpec((None,) + w.shape[1:], lambda q, gs, sd, seg, grp, used: (grp[q], 0, 0, 0))
    hbm = pl.BlockSpec(memory_space=pl.ANY)
    y, _ = pl.pallas_call(
        functools.partial(_moe_kernel, tb=tb),
        grid_spec=pltpu.PrefetchScalarGridSpec(
            num_scalar_prefetch=5,
            grid=(max_tiles,),
            in_specs=[hbm, pl.BlockSpec((None, 1, 6 * d), lambda q, gs, sd, seg, grp, used: (seg[q], 0, 0)),
                      const((1, d)), by_group(w1), by_group(w3), by_group(w2), const((1, d)), hbm],
            out_specs=[hbm, hbm],
            scratch_shapes=[pltpu.VMEM((2, tb // SUBLANES, SUBLANES, width), F32),
                            pltpu.VMEM((2, tb // SUBLANES, SUBLANES, d), F32),
                            pltpu.SemaphoreType.DMA((2,)), pltpu.SemaphoreType.DMA((2,))]),
        out_shape=[jax.ShapeDtypeStruct((s * l // SUBLANES, SUBLANES, d), F32),
                   jax.ShapeDtypeStruct((2, tb // SUBLANES, SUBLANES, d), F32)],
        input_output_aliases={12: 1},
        compiler_params=_cparams(("arbitrary",)),
        name="moe",
    )(jnp.maximum(row_src, 0), row_src, tile_seg, tile_grp, n_used,
      xc.reshape(s * l // SUBLANES, SUBLANES, width), mod, p["norm2_g"], w1, w3, w2, p["final_norm_g"],
      jnp.zeros((2, tb // SUBLANES, SUBLANES, d), F32))
    return y.reshape(s, l, d)


def _trunk(x, mod_seg, states, conv2d, p, tm, tb, mixer_seqs, rwkv_packing):
    b, t, d = x.shape
    n_seg = mod_seg.shape[0]
    seg = lambda a: a.reshape(n_seg, b * t // n_seg, a.shape[-1])
    per_seq = lambda a: a.reshape(b, t, a.shape[-1])
    xa, ub, o_pre = (per_seq(a) for a in _in_proj(seg(x), mod_seg, p["norm1_g"], p["w_in"], tm))
    if states is None:
        s_rwkv, mlstm_states = None, None
    else:
        s_rwkv, s_c, s_n, s_m = states
        mlstm_states = (s_c, s_n, jnp.broadcast_to(s_m[..., None], s_m.shape + (LANES,)))
    ya, s_rwkv = _rwkv(xa, s_rwkv, p, *rwkv_packing)
    yb, s_c, s_n, s_m = _mlstm(ub, o_pre, mlstm_states, p, conv2d, mixer_seqs)
    xc = _merge(seg(x), seg(ya), seg(yb), mod_seg, p, tm)
    y = _moe(xc, mod_seg, p, tb)
    return y.reshape(b, t, d), s_rwkv, s_c, s_n, s_m[..., 0]


def kernel(x_prompt, x_sample, c, c_ctx, state_rwkv, state_mlstm_C, state_mlstm_n, state_mlstm_m, w_mod, b_mod, norm1_g, norm2_g, w_in, rwkv_mu, rwkv_w_r, rwkv_w_k, rwkv_w_v, rwkv_w0, rwkv_w1, rwkv_w2, rwkv_a0, rwkv_a1, rwkv_a2, rwkv_g1, rwkv_g2, rwkv_k_k, rwkv_k_a, rwkv_r_k, rwkv_ln_g, rwkv_ln_b, rwkv_w_proj, mlstm_conv, mlstm_w_q, mlstm_w_k, mlstm_w_v, mlstm_w_i, mlstm_b_i, mlstm_w_f, mlstm_b_f, mlstm_ln_g, mlstm_w_proj, w_out, moe_w_group, moe_b_group, moe_w_expert, moe_b_expert, moe_w1, moe_w3, moe_w2, final_norm_g):
    assert w_mod.shape[0] == 1, "single trunk layer"
    l = 0
    bp, dec = x_prompt.shape[0], x_sample.shape[0]
    bf = lambda w: w.astype(BF16)
    row = lambda w: w.reshape(1, -1).astype(F32)
    wg = jnp.concatenate([mlstm_w_i[l, 0], mlstm_w_i[l, 1], mlstm_w_f[l, 0], mlstm_w_f[l, 1]], axis=1)
    bg = jnp.concatenate([mlstm_b_i[l, 0], mlstm_b_i[l, 1], mlstm_b_f[l, 0], mlstm_b_f[l, 1]])
    wg = jnp.pad(wg, ((0, 0), (0, LANES - wg.shape[1])))
    bg = jnp.pad(bg, (0, LANES - bg.shape[0]))
    w_router = jnp.pad(jnp.concatenate([moe_w_group[l], moe_w_expert[l]], axis=1),
                       ((0, 0), (0, LANES - N_GROUPS - N_EXPERTS)))
    b_router = jnp.pad(jnp.concatenate([moe_b_group[l], moe_b_expert[l]]), (0, LANES - N_GROUPS - N_EXPERTS))
    p = {
        "norm1_g": row(norm1_g[l]), "norm2_g": row(norm2_g[l]), "w_in": w_in[l],
        "rwkv_mu": rwkv_mu[l], "rwkv_wr": bf(rwkv_w_r[l]), "rwkv_wk": bf(rwkv_w_k[l]), "rwkv_wv": bf(rwkv_w_v[l]),
        "rwkv_w0": rwkv_w0[l], "rwkv_w1": bf(rwkv_w1[l]), "rwkv_w2": bf(rwkv_w2[l]),
        "rwkv_a0": rwkv_a0[l], "rwkv_a1": bf(rwkv_a1[l]), "rwkv_a2": bf(rwkv_a2[l]),
        "rwkv_g1": bf(rwkv_g1[l]), "rwkv_g2": bf(rwkv_g2[l]),
        "rwkv_k_k": row(rwkv_k_k[l]), "rwkv_k_a": row(rwkv_k_a[l]), "rwkv_r_k": row(rwkv_r_k[l]),
        "rwkv_ln_g": row(rwkv_ln_g[l]), "rwkv_ln_b": row(rwkv_ln_b[l]),
        "rwkv_w_proj": rwkv_w_proj[l],
        "mlstm_conv": mlstm_conv[l].reshape(9, B_WIDTH),
        "mlstm_wq": bf(mlstm_w_q[l]), "mlstm_wk": bf(mlstm_w_k[l]), "mlstm_wv": bf(mlstm_w_v[l]),
        "mlstm_wg": bf(wg), "mlstm_bg": row(bg),
        "mlstm_wgt": bf(wg[:, :16].T), "mlstm_bgt": jnp.broadcast_to(bg[:16, None], (16, LANES)),
        "mlstm_ln_g": row(mlstm_ln_g[l]), "mlstm_w_proj": mlstm_w_proj[l], "w_out": w_out[l],
        "moe_w_router": bf(w_router), "moe_b_router": row(b_router),
        "moe_w1": moe_w1[l], "moe_w3": moe_w3[l], "moe_w2": moe_w2[l],
        "final_norm_g": row(final_norm_g),
    }
    cvec = jnp.concatenate([c_ctx[None, :], c, jnp.zeros((8 - 1 - dec, D_MODEL), F32)], axis=0)
    mod = _mod(cvec, w_mod[l], b_mod[l].reshape(1, -1))
    mod_ctx = mod[0:1][:, None, :]
    mod_lat = mod[1:1 + dec][:, None, :]

    yp, n_rwkv, n_c, n_n, n_m = _trunk(x_prompt, mod_ctx, None, False, p, 512, 512, 2, (2, 2))
    lat_states = tuple(s[:, l].astype(F32) for s in (state_rwkv, state_mlstm_C, state_mlstm_n, state_mlstm_m))
    ys, _, _, _, _ = _trunk(x_sample, mod_lat, lat_states, True, p, 512, 256, 1, (1, 2))
    dt = x_prompt.dtype
    return (yp, ys, n_rwkv[:, None].astype(dt), n_c[:, None].astype(dt), n_n[:, None].astype(dt),
            n_m[:, None].astype(dt))
```

```python
import functools

import jax
import jax.numpy as jnp
from jax import lax
from jax.experimental import pallas as pl
from jax.experimental.pallas import tpu as pltpu

F32 = jnp.float32
BF16 = jnp.bfloat16

D_MODEL = 1024
GRID_W = 64
A_HEAD_DIM = 64
A_WIDTH = 512
A_HEADS = 8
B_HEAD_DIM = 128
B_WIDTH = 512
B_HEADS = 4
CHUNK = 64
ROW_TILE = 512
SUM_TILE = 256
N_GROUPS = 4
EXPERTS_PER_GROUP = 4
N_EXPERTS = 16
D_EXPERT = 256
GROUP_LANE = N_GROUPS + N_EXPERTS
NORM_EPS = 1e-6
RWKV_GN_EPS = 64e-5
MLSTM_GN_EPS = 1e-5
LANES = 128
SUBLANES = 8
CONV_PAD = 72
VMEM_LIMIT = 56 * 1024 * 1024


def _mm(a, b):
    return jnp.dot(a.astype(BF16), b.astype(BF16), preferred_element_type=F32)


_NN, _NT, _TN = ((1,), (0,)), ((1,), (1,)), ((0,), (0,))


def _split(a):
    hi = a.astype(BF16)
    return hi, (a - hi.astype(F32)).astype(BF16)


def _split3(a):
    hi = a.astype(BF16)
    rest = a - hi.astype(F32)
    lo = rest.astype(BF16)
    return hi, lo, (rest - lo.astype(F32)).astype(BF16)


def _sum01(a, b, dims=_NN):
    dot = lambda p, q: lax.dot_general(p, q, (dims, ((), ())), preferred_element_type=F32)
    terms = [dot(x, b) for x in a] if isinstance(a, tuple) else [dot(a, x) for x in b]
    return functools.reduce(lambda p, q: p + q, terms)


def _mm3(a, b, dims=_NN):
    (ah, al), (bh, bl) = a, b
    dot = lambda p, q: lax.dot_general(p, q, (dims, ((), ())), preferred_element_type=F32)
    return dot(ah, bh) + dot(ah, bl) + dot(al, bh)


def _mm_nt(a, b):
    return lax.dot_general(a.astype(BF16), b.astype(BF16), (((1,), (1,)), ((), ())),
                           preferred_element_type=F32)


def _mm_tn(a, b):
    return lax.dot_general(a.astype(BF16), b.astype(BF16), (((0,), (0,)), ((), ())),
                           preferred_element_type=F32)


def _sigmoid(x):
    return 1.0 / (1.0 + jnp.exp(-x))


def _silu(x):
    return x * _sigmoid(x)


def _softplus(x):
    return jnp.maximum(x, 0.0) + jnp.log(1.0 + jnp.exp(-jnp.abs(x)))


def _iota(shape, dim):
    return lax.broadcasted_iota(jnp.int32, shape, dim)


def _tri(reverse, inclusive, n=CHUNK):
    row, col = _iota((n, n), 0), _iota((n, n), 1)
    if reverse:
        return (col >= row) if inclusive else (col > row)
    return (col <= row) if inclusive else (col < row)


def _same_block(n, blk):
    sh = blk.bit_length() - 1
    row, col = _iota((n, n), 0), _iota((n, n), 1)
    return lax.shift_right_logical(row, sh) == lax.shift_right_logical(col, sh)


def _rmsnorm(x, g):
    return x * lax.rsqrt(jnp.mean(x * x, axis=-1, keepdims=True) + NORM_EPS) * g


def _cparams(sem):
    return pltpu.CompilerParams(dimension_semantics=sem, vmem_limit_bytes=VMEM_LIMIT)


def _full(shape):
    nd = len(shape)
    return pl.BlockSpec(shape, lambda *_: (0,) * nd)


def _mod_kernel(c_ref, w_ref, b_ref, o_ref):
    o_ref[...] = _mm(_silu(c_ref[...]), w_ref[...]) + b_ref[...]


def _mod(cvec, w_mod, b_mod):
    rows, d = cvec.shape
    n = w_mod.shape[1]
    tn = 1536
    return pl.pallas_call(
        _mod_kernel,
        grid=(n // tn,),
        in_specs=[_full((rows, d)), pl.BlockSpec((d, tn), lambda j: (0, j)),
                  pl.BlockSpec((1, tn), lambda j: (0, j))],
        out_specs=pl.BlockSpec((rows, tn), lambda j: (0, j)),
        out_shape=jax.ShapeDtypeStruct((rows, n), F32),
        compiler_params=_cparams(("arbitrary",)),
        name="mod",
    )(cvec, w_mod, b_mod)


W_IN_BLOCK = 512


def _modulated_norm1(x_ref, mod_ref, g_ref):
    d = D_MODEL
    shift, scale = mod_ref[:, 0:d], mod_ref[:, d:2 * d]
    return (_rmsnorm(x_ref[...], g_ref[...]) * (1.0 + scale) + shift).astype(BF16)


def _w_in_blocks(w_in, first, count):
    specs = [pl.BlockSpec((w_in.shape[0], W_IN_BLOCK), lambda *_, j=first + k: (0, j)) for k in range(count)]
    return specs, [w_in] * count


def _in_kernel(x_ref, mod_ref, g_ref, wa_ref, wu_ref, wo_ref, xa_ref, ub_ref, op_ref):
    h = _modulated_norm1(x_ref, mod_ref, g_ref)
    for ref, w_ref in ((xa_ref, wa_ref), (ub_ref, wu_ref), (op_ref, wo_ref)):
        ref[...] = _mm(h, w_ref[...])


def _in_proj(x, mod, g1, w_in, tm):
    b, t, d = x.shape
    per = t // tm
    widths = (A_WIDTH, B_WIDTH, B_WIDTH)
    assert all(n == W_IN_BLOCK for n in widths)
    tok = lambda n: pl.BlockSpec((None, tm, n), lambda i: (i // per, i % per, 0))
    w_specs, w_args = _w_in_blocks(w_in, 0, 3)
    return pl.pallas_call(
        _in_kernel,
        grid=(b * per,),
        in_specs=[tok(d), pl.BlockSpec((None, 1, 6 * d), lambda i: (i // per, 0, 0)), _full((1, d))] + w_specs,
        out_specs=[tok(n) for n in widths],
        out_shape=[jax.ShapeDtypeStruct((b, t, n), F32) for n in widths],
        compiler_params=_cparams(("arbitrary",)),
        name="in_proj",
    )(x, mod, g1, *w_args)


def _head_of_lane(shape):
    return lax.shift_right_logical(_iota(shape, 1), A_HEAD_DIM.bit_length() - 1)


def _pair_diag(x):
    head = _head_of_lane(x.shape)
    return jnp.concatenate([x * (head == h).astype(x.dtype) for h in range(x.shape[1] // A_HEAD_DIM)], axis=0)


def _pair_mm3(lhs, rhs, dims=_NN):
    rows = lhs[0].shape[0]
    if dims == _NT:
        wh, wl = _split(_pair_diag(rhs).T)
    else:
        rh, rl = _split(rhs)
        wh, wl = _pair_diag(rh), _pair_diag(rl)
    parts = [_split(l) for l in lhs]
    top = jnp.dot(jnp.concatenate([t for part in parts for t in part], axis=0), wh, preferred_element_type=F32)
    low = jnp.dot(jnp.concatenate([part[0] for part in parts], axis=0), wl, preferred_element_type=F32)
    return [top[2 * k * rows:(2 * k + 1) * rows] + top[(2 * k + 1) * rows:(2 * k + 2) * rows]
            + low[k * rows:(k + 1) * rows] for k in range(len(lhs))]


def _pair_index(shape):
    return _iota(shape, 0), _iota(shape, 1) & (A_HEAD_DIM - 1)


def _pair_tri_inverses(mats):
    t_idx, s_idx = _pair_index(mats[0].shape)
    same = lambda blk: (lax.shift_right_logical(t_idx, blk.bit_length() - 1)
                        == lax.shift_right_logical(s_idx, blk.bit_length() - 1))
    eye = (t_idx == s_idx).astype(F32)
    nd = [jnp.where(same(8), m, 0.0) for m in mats]
    n2 = [_pair_mm3([x], x)[0] for x in nd]
    t = [eye + x for x in nd]
    both = [_pair_mm3([a, b], a) for a, b in zip(n2, t)]
    t = [x + r[1] for x, r in zip(t, both)]
    t = [x + _pair_mm3([x], r[0])[0] for x, r in zip(t, both)]
    blk = 8
    while blk < CHUNK:
        sel = same(2 * blk) & jnp.logical_not(same(blk))
        w = [_pair_mm3([jnp.where(sel, m, 0.0)], x)[0] for m, x in zip(mats, t)]
        t = [x + _pair_mm3([x], y)[0] for x, y in zip(t, w)]
        blk *= 2
    return t


def _rwkv_kernel(x_ref, s0_ref, mu_ref, wr_ref, wk_ref, wv_ref, w0_ref, w1_ref, w2_ref,
                 a0_ref, a1_ref, a2_ref, g1_ref, g2_ref, kk_ref, ka_ref, rk_ref, lng_ref, lnb_ref,
                 y_ref, s_ref,
                 r_s, v_s, kk_s, cum_s, cx_s, kd_s, bb_s, g_s, bonus_s, y_s, sp_s, *, seq_len):
    total = x_ref.shape[0]
    n_seq = total // seq_len
    n_chunks = seq_len // CHUNK
    n_tiles = total // ROW_TILE
    hd = A_HEAD_DIM
    n_groups, width = sp_s.shape[2], sp_s.shape[4]
    per_group = width // hd
    x = x_ref[...]
    row = _iota(x.shape, 0) & (seq_len - 1)
    prev = jnp.where(row == 0, 0.0, pltpu.roll(x, 1, 0))
    nxt = jnp.where(row == seq_len - 1, 0.0, pltpu.roll(x, total - 1, 0))
    y_s[...] = 0.5 * (prev + nxt) - x
    head_ones = _same_block(A_WIDTH, hd).astype(BF16)
    chunk_diag = _same_block(SUM_TILE, CHUNK)
    run_sum = [(chunk_diag & _tri(d == 1, True, SUM_TILE)).astype(BF16) for d in range(2)]

    def project_tile(ti, carry):
        rows = pl.ds(pl.multiple_of(ti * ROW_TILE, ROW_TILE), ROW_TILE)
        x, xx = x_ref[rows, :], y_s[rows, :]
        xr, xw, xk, xv, xi, xg = (x + xx * mu_ref[j:j + 1, :] for j in range(6))
        r = _mm(xr, wr_ref[...])
        k = _mm(xk, wk_ref[...])
        v = _mm(xv, wv_ref[...])
        g_s[rows, :] = _mm(_sigmoid(_mm(xg, g1_ref[...])), g2_ref[...])
        kk = k * kk_ref[...]
        kk = kk / jnp.maximum(jnp.sqrt(_sum01(_split(kk * kk), head_ones)), 1e-12)
        r_s[rows, :] = r
        v_s[rows, :] = v
        kk_s[rows, :] = kk
        kd_sum = jnp.zeros_like(x)
        for d in range(2):
            zw = w0_ref[d:d + 1, :] + _mm(jnp.tanh(_mm(xw, w1_ref[d])), w2_ref[d])
            lw = -jnp.exp(-_softplus(-zw) - 0.5)
            cum = jnp.concatenate([_sum01(run_sum[d], _split3(lw[j:j + SUM_TILE]))
                                   for j in range(0, ROW_TILE, SUM_TILE)], axis=0)
            cum_s[d, rows, :] = cum
            cx_s[d, rows, :] = cum - lw
            a = _sigmoid(a0_ref[d:d + 1, :] + _mm(_mm(xi, a1_ref[d]), a2_ref[d]))
            kd = k * (1.0 + (a - 1.0) * ka_ref[...])
            kd_s[d, rows, :] = kd
            bb_s[d, rows, :] = kk * a
            kd_sum = kd_sum + kd
        bonus_s[rows, :] = _sum01(_split(r * kd_sum * rk_ref[...]), head_ones) * v
        y_s[rows, :] = jnp.zeros_like(x)
        return carry

    lax.fori_loop(0, n_tiles, project_tile, 0)
    for n in range(n_seq):
        for d in range(2):
            for p in range(n_groups):
                sp_s[n, d, p] = (jnp.zeros((hd, width), F32) if s0_ref is None else jnp.concatenate(
                    [s0_ref[n, d, per_group * p + j] for j in range(per_group)], axis=1))

    t_idx, s_idx = _pair_index((CHUNK, width))
    strict = [s_idx < t_idx, s_idx > t_idx]
    incl = [s_idx <= t_idx, s_idx >= t_idx]
    lane_head = _head_of_lane((CHUNK, width))

    def chunk_body(ci, carry):
        chains = []
        for n, d in [(n, d) for n in range(n_seq) for d in range(2)]:
            cpos = n * n_chunks + ((n_chunks - 1 - ci) if d == 1 else ci)
            last = 0 if d == 1 else CHUNK - 1
            rows = pl.ds(pl.multiple_of(cpos * CHUNK, CHUNK), CHUNK)
            cum, cx = cum_s[d, rows, :], cx_s[d, rows, :]
            tot = cum[last:last + 1, :]
            kdc, bbc = kd_s[d, rows, :], bb_s[d, rows, :]
            e_neg = jnp.exp(-cum)
            e_tail = jnp.exp(tot - cum)
            at = -kk_s[rows, :] * jnp.exp(cx)
            rt = r_s[rows, :] * jnp.exp(cum)
            bt, kt = bbc * e_neg, kdc * e_neg
            bp, kp = bbc * e_tail, kdc * e_tail
            pc = jnp.exp(tot)
            vc = v_s[rows, :]
            for p in range(n_groups):
                ln = slice(p * width, (p + 1) * width)
                chains.append(dict(d=d, st=(n, d, p), rows=rows, ln=ln, v=vc[:, ln], pc=pc[:, ln],
                                   lhs=jnp.concatenate([at[:, ln], rt[:, ln]], axis=0),
                                   bt=bt[:, ln], kt=kt[:, ln], bp=bp[:, ln], kp=kp[:, ln]))
        for c in chains:
            c["sb"] = _pair_mm3([c["lhs"]], c["bt"], _NT)[0]
            c["sk"] = _pair_mm3([c["lhs"]], c["kt"], _NT)[0]
            c["s"] = sp_s[c["st"]]
            c["fs"] = _mm(c["lhs"], _pair_diag(c["s"]).T)
            c["v_diag"] = _pair_diag(c["v"].astype(BF16))
        for c in chains:
            a_ak = jnp.where(strict[c["d"]], c["sk"][:CHUNK], 0.0)
            c["x"] = c["fs"][:CHUNK] + _mm(a_ak, c["v_diag"])
        inv = _pair_tri_inverses([jnp.where(strict[c["d"]], c["sb"][:CHUNK], 0.0) for c in chains])
        for c, t in zip(chains, inv):
            c["u"] = _pair_mm3([t], c["x"])[0]
        for c in chains:
            d, rows, ln, u = c["d"], c["rows"], c["ln"], c["u"]
            a_r = jnp.concatenate([jnp.where(incl[d], c["sb"][CHUNK:], 0.0),
                                   jnp.where(incl[d], c["sk"][CHUNK:], 0.0)], axis=1)
            uv_diag = jnp.concatenate([_pair_diag(u.astype(BF16)), c["v_diag"]], axis=0)
            y_s[rows, ln] = y_s[rows, ln] + c["fs"][CHUNK:] + _mm(a_r, uv_diag)
            g = _mm_tn(jnp.concatenate([u, c["v"]], axis=0), jnp.concatenate([c["bp"], c["kp"]], axis=0))
            own = g[0:hd]
            for j in range(1, per_group):
                own = jnp.where(lane_head == j, g[j * hd:(j + 1) * hd], own)
            sp_s[c["st"]] = c["s"] * c["pc"] + own
        return carry

    lax.fori_loop(0, n_chunks, chunk_body, 0)
    for n in range(n_seq):
        for d in range(2):
            for p in range(n_groups):
                for j in range(per_group):
                    s_ref[n, d, per_group * p + j] = sp_s[n, d, p, :, j * hd:(j + 1) * hd]

    def finish_tile(ti, carry):
        rows = pl.ds(pl.multiple_of(ti * ROW_TILE, ROW_TILE), ROW_TILE)
        ys = y_s[rows, :]
        cen = ys - _sum01(_split(ys), head_ones) * (1.0 / hd)
        var = _sum01(_split(cen * cen), head_ones) * (1.0 / hd)
        yn = cen * lax.rsqrt(var + RWKV_GN_EPS) * lng_ref[...] + lnb_ref[...]
        y_ref[rows, :] = (yn + bonus_s[rows, :]) * g_s[rows, :]
        return carry

    lax.fori_loop(0, n_tiles, finish_tile, 0)


def _rwkv(xa, s0, p, n_seq, heads_per_block):
    b, t, w = xa.shape
    assert t & (t - 1) == 0 and b % n_seq == 0
    names = ("mu", "wr", "wk", "wv", "w0", "w1", "w2", "a0", "a1", "a2", "g1", "g2",
             "k_k", "k_a", "r_k", "ln_g", "ln_b")
    weights = [p["rwkv_" + n] for n in names]
    rows = n_seq * t
    seq = pl.BlockSpec((None, rows, w), lambda i: (i, 0, 0))
    st = pl.BlockSpec((n_seq, 2, A_HEADS, A_HEAD_DIM, A_HEAD_DIM), lambda i: (i, 0, 0, 0, 0))
    tw = lambda lead=(): pltpu.VMEM(lead + (rows, w), F32)
    kernel = functools.partial(_rwkv_kernel, seq_len=t)
    if s0 is None:
        kernel, state_specs, states = _zero_state_kernel(kernel, 1, 1), [], ()
    else:
        state_specs, states = [st], (s0,)
    y, s_out = pl.pallas_call(
        kernel,
        grid=(b // n_seq,),
        in_specs=[seq] + state_specs + [_full(x.shape) for x in weights],
        out_specs=[seq, st],
        out_shape=[jax.ShapeDtypeStruct((b // n_seq, rows, w), F32),
                   jax.ShapeDtypeStruct((b, 2, A_HEADS, A_HEAD_DIM, A_HEAD_DIM), F32)],
        scratch_shapes=[tw(), tw(), tw(), tw((2,)), tw((2,)), tw((2,)), tw((2,)), tw(), tw(), tw(),
                        pltpu.VMEM((n_seq, 2, A_HEADS // heads_per_block, A_HEAD_DIM,
                                    heads_per_block * A_HEAD_DIM), F32)],
        compiler_params=_cparams(("arbitrary",)),
        name="rwkv",
    )(xa.reshape(b // n_seq, rows, w), *states, *weights)
    return y.reshape(b, t, w), s_out


def _mlstm_kernel(u_ref, op_ref, c0_ref, n0_ref, m0_ref, conv_ref, wq_ref, wk_ref, wv_ref,
                  wg_ref, bg_ref, wgt_ref, bgt_ref, lng_ref,
                  y_ref, c_ref, n_ref, m_ref,
                  pad_s, q_s, k_s, v_s, dd_s, cr_s, h_s, *, conv2d, seq_len):
    t_len = u_ref.shape[0]
    n_seq = t_len // seq_len
    n_chunks = seq_len // CHUNK
    hd = B_HEAD_DIM
    u = u_ref[...]
    pad_s[0:CONV_PAD, :] = jnp.zeros((CONV_PAD, B_WIDTH), F32)
    pad_s[CONV_PAD + t_len:CONV_PAD + t_len + CONV_PAD, :] = jnp.zeros((CONV_PAD, B_WIDTH), F32)
    pad_s[CONV_PAD:CONV_PAD + t_len, :] = u
    pos = _iota(u.shape, 0) & (seq_len - 1)
    col = pos & (GRID_W - 1)
    conv = jnp.zeros_like(u)
    for kh in range(3):
        if not conv2d and kh != 1:
            continue
        for kw in range(3):
            shift = (kh - 1) * GRID_W + (kw - 1)
            term = pad_s[CONV_PAD + shift:CONV_PAD + shift + t_len, :] * conv_ref[kh * 3 + kw:kh * 3 + kw + 1, :]
            if conv2d and kw != 1:
                src = col + (kw - 1)
                term = jnp.where((src >= 0) & (src < GRID_W), term, 0.0)
            if n_seq > 1 and shift != 0:
                term = jnp.where((pos + shift >= 0) & (pos + shift < seq_len), term, 0.0)
            conv = conv + term
    uc = _silu(conv)
    for h in range(B_HEADS):
        ln = slice(h * hd, (h + 1) * hd)
        q_s[:, ln] = _mm(uc[:, ln], wq_ref[h])
        k_s[:, ln] = _mm(uc[:, ln], wk_ref[h]) * (hd ** -0.5)
        v_s[:, ln] = _mm(u[:, ln], wv_ref[h])
    gcol = _mm(uc, wg_ref[...]) + bg_ref[...]
    gcol = jnp.where(_iota(gcol.shape, 1) >= 8, -_softplus(-gcol), gcol)
    grow = _mm_nt(wgt_ref[...], uc) + bgt_ref[:, 0:1]
    grow = jnp.where(_iota(grow.shape, 0) >= 8, -_softplus(-grow), grow)
    chunk_diag = _same_block(SUM_TILE, CHUNK)
    run_sum = [(chunk_diag & _tri(d == 1, True, SUM_TILE)).astype(BF16) for d in range(2)]
    per_tile = SUM_TILE // CHUNK
    for ti in range(t_len // SUM_TILE):
        sl = slice(ti * SUM_TILE, (ti + 1) * SUM_TILE)
        gt, rt = _split3(gcol[sl, :]), _split3(grow[:, sl])
        backward_lane = (_iota((SUM_TILE, LANES), 1) & 4) != 0
        cum_c = jnp.where(backward_lane, _sum01(run_sum[1], gt, _NN), _sum01(run_sum[0], gt, _NN))
        dd_s[sl, :] = pltpu.roll(cum_c, LANES - 8, 1) - gcol[sl, :]
        backward_row = (_iota((16, SUM_TILE), 0) & 4) != 0
        cum_r = jnp.where(backward_row, _sum01(rt, run_sum[1], _NT), _sum01(rt, run_sum[0], _NT))
        for c in range(per_tile):
            cr_s[ti * per_tile + c] = cum_r[:, c * CHUNK:(c + 1) * CHUNK]
    h_s[...] = jnp.zeros_like(u)
    for ref, init in ((c_ref, c0_ref), (n_ref, n0_ref), (m_ref, m0_ref)):
        ref[...] = jnp.zeros(ref.shape, F32) if init is None else init[...]
    row_i, col_i = _iota((CHUNK, CHUNK), 0), _iota((CHUNK, CHUNK), 1)
    incl_t = [row_i <= col_i, row_i >= col_i]
    eye = row_i == col_i
    pick = [(_iota((LANES, B_HEADS * LANES), 0)
             == d * B_HEADS + lax.shift_right_logical(_iota((LANES, B_HEADS * LANES), 1), LANES.bit_length() - 1)
             ).astype(BF16) for d in range(2)]
    spread = (lax.shift_right_logical(_iota((2 * CHUNK, 2 * LANES), 0), CHUNK.bit_length() - 1)
              == lax.shift_right_logical(_iota((2 * CHUNK, 2 * LANES), 1), LANES.bit_length() - 1)).astype(BF16)
    ones_bf = jnp.ones((CHUNK, LANES), BF16)

    def chunk_body(ci, carry):
        chains = []
        for n, d in [(n, d) for n in range(n_seq) for d in range(2)]:
            cpos = n * n_chunks + ((n_chunks - 1 - ci) if d == 1 else ci)
            last = 0 if d == 1 else CHUNK - 1
            rows = pl.ds(pl.multiple_of(cpos * CHUNK, CHUNK), CHUNK)
            cr = cr_s[cpos]
            d_col = _sum01(_split3(dd_s[rows, :]), pick[d])
            for h in range(B_HEADS):
                ln = slice(h * hd, (h + 1) * hd)
                lf = 8 + d * 4 + h
                chains.append(dict(d=d, st=(n, d, h), vec=(n, d, slice(h, h + 1)), rows=rows, ln=ln, last=last,
                                   b_row=cr[lf:lf + 1, :], d_col=d_col[:, ln],
                                   m=m_ref[n, d, h:h + 1, 0:1], q=q_s[rows, ln], k=k_s[rows, ln], v=v_s[rows, ln],
                                   c=c_ref[n, d, h], n=n_ref[n, d, h:h + 1, :]))
        for c in chains:
            c["kq"] = _mm_nt(c["k"], c["q"])
            c["qc"] = _mm(c["q"], c["c"])
            c["qn"] = _mm_nt(c["q"], jnp.broadcast_to(c["n"], (LANES, hd)))
            logw = jnp.where(incl_t[c["d"]], c["b_row"] - c["d_col"][:, 0:CHUNK], -jnp.inf)
            c["m_row"] = jnp.maximum(c["b_row"] + c["m"], jnp.max(logw, axis=0, keepdims=True))
            c["logw"] = logw
        for c in chains:
            inter_row = jnp.exp(c["b_row"] + c["m"] - c["m_row"])
            diag = jnp.concatenate([jnp.where(eye, c["m_row"], 0.0), jnp.where(eye, inter_row, 0.0)], axis=1)
            cols = _sum01(_split3(diag), spread)
            c["m_col"], c["inter"] = cols[:, 0:LANES], cols[:, LANES:2 * LANES]
            c["s"] = c["kq"] * jnp.exp(c["logw"] - c["m_row"])
        for c in chains:
            s_hi, s_lo = _split(c["s"])
            sv = lax.dot_general(s_hi, jnp.concatenate([c["v"].astype(BF16), ones_bf], axis=1),
                                 (_TN, ((), ())), preferred_element_type=F32)
            c["sv"] = sv[:, 0:hd]
            c["den"] = (c["inter"] * c["qn"] + sv[:, hd:hd + LANES]
                        + lax.dot_general(s_lo, ones_bf, (_TN, ((), ())), preferred_element_type=F32))
            last = c["last"]
            c["m_new"] = c["m_row"][:, last:last + 1]
            b_last = c["b_row"][:, last:last + 1]
            c["decay"] = jnp.exp(b_last + c["m"] - c["m_new"])
            c["kw"] = c["k"] * jnp.exp(b_last - c["m_new"] - c["d_col"])
        for c in chains:
            c["kv"] = _mm_tn(c["kw"], c["v"])
        for c in chains:
            rows, ln = c["rows"], c["ln"]
            num = c["inter"] * c["qc"] + c["sv"]
            h_s[rows, ln] = h_s[rows, ln] + num / jnp.maximum(jnp.abs(c["den"]), jnp.exp(-c["m_col"]))
            c_ref[c["st"]] = c["decay"] * c["c"] + c["kv"]
            n_ref[c["vec"]] = c["decay"] * c["n"] + jnp.sum(c["kw"], axis=0, keepdims=True)
            m_ref[c["vec"]] = jnp.broadcast_to(c["m_new"], (1, LANES))
        return carry

    lax.fori_loop(0, n_chunks, chunk_body, 0)

    for h in range(B_HEADS):
        ln = slice(h * hd, (h + 1) * hd)
        hh = h_s[:, ln]
        cen = hh - jnp.mean(hh, axis=1, keepdims=True)
        var = jnp.mean(cen * cen, axis=1, keepdims=True)
        y_ref[:, ln] = _sigmoid(op_ref[:, ln]) * (cen * lax.rsqrt(var + MLSTM_GN_EPS) * lng_ref[:, ln])


def _zero_state_kernel(kernel, n_data, n_state):
    def wrapped(*refs, **kw):
        return kernel(*refs[:n_data], *([None] * n_state), *refs[n_data:], **kw)
    return wrapped


def _mlstm(ub, o_pre, states, p, conv2d, n_seq):
    b, t, w = ub.shape
    assert t & (t - 1) == 0 and b % n_seq == 0
    names = ("conv", "wq", "wk", "wv", "wg", "bg", "wgt", "bgt", "ln_g")
    weights = [p["mlstm_" + n] for n in names]
    rows, steps = n_seq * t, b // n_seq
    seq = pl.BlockSpec((None, rows, w), lambda i: (i, 0, 0))
    cst = pl.BlockSpec((n_seq, 2, B_HEADS, B_HEAD_DIM, B_HEAD_DIM), lambda i: (i, 0, 0, 0, 0))
    vst = pl.BlockSpec((n_seq, 2, B_HEADS, LANES), lambda i: (i, 0, 0, 0))
    kernel = functools.partial(_mlstm_kernel, conv2d=conv2d, seq_len=t)
    if states is None:
        kernel, state_specs, states = _zero_state_kernel(kernel, 2, 3), [], ()
    else:
        state_specs = [cst, vst, vst]
    y, c_out, n_out, m_out = pl.pallas_call(
        kernel,
        grid=(steps,),
        in_specs=[seq, seq] + state_specs + [_full(x.shape) for x in weights],
        out_specs=[seq, cst, vst, vst],
        out_shape=[jax.ShapeDtypeStruct((steps, rows, w), F32),
                   jax.ShapeDtypeStruct((b, 2, B_HEADS, B_HEAD_DIM, B_HEAD_DIM), F32),
                   jax.ShapeDtypeStruct((b, 2, B_HEADS, LANES), F32),
                   jax.ShapeDtypeStruct((b, 2, B_HEADS, LANES), F32)],
        scratch_shapes=[pltpu.VMEM((rows + 2 * CONV_PAD, w), F32), pltpu.VMEM((rows, w), F32),
                        pltpu.VMEM((rows, w), F32), pltpu.VMEM((rows, w), F32),
                        pltpu.VMEM((rows, LANES), F32), pltpu.VMEM((rows // CHUNK, 16, CHUNK), F32),
                        pltpu.VMEM((rows, w), F32)],
        compiler_params=_cparams(("arbitrary",)),
        name="mlstm",
    )(ub.reshape(steps, rows, w), o_pre.reshape(steps, rows, w), *states, *weights)
    return y.reshape(b, t, w), c_out, n_out, m_out


def _first_lane_of_max(val, lane, valid):
    masked = jnp.where(valid, val, -jnp.inf)
    best = jnp.max(masked, axis=1, keepdims=True)
    idx = jnp.min(jnp.where(valid & (masked == best), lane, LANES), axis=1, keepdims=True)
    return best, idx


def _merge_kernel(x_ref, ya_ref, yb_ref, mod_ref, g1_ref, wga0_ref, wga1_ref, wgb0_ref, wgb1_ref,
                  wpa_ref, wpb_ref, wo_ref, g2_ref, wr_ref, br_ref, xc_ref):
    d = D_MODEL
    gate1 = mod_ref[:, 2 * d:3 * d]
    shift2, scale2 = mod_ref[:, 3 * d:4 * d], mod_ref[:, 4 * d:5 * d]
    h1 = _modulated_norm1(x_ref, mod_ref, g1_ref)
    pa, pb = _mm(ya_ref[...], wpa_ref[...]), _mm(yb_ref[...], wpb_ref[...])
    halves = []
    for j, (wga_ref, wgb_ref) in enumerate(((wga0_ref, wgb0_ref), (wga1_ref, wgb1_ref))):
        cols = slice(j * W_IN_BLOCK, (j + 1) * W_IN_BLOCK)
        halves.append(_sigmoid(_mm(h1, wga_ref[...])) * pa[:, cols] + _sigmoid(_mm(h1, wgb_ref[...])) * pb[:, cols])
    merged = jnp.concatenate(halves, axis=1)
    x1 = x_ref[...] + gate1 * _mm(merged, wo_ref[...])
    xc_ref[:, 0:d] = x1
    h2 = _rmsnorm(x1, g2_ref[...]) * (1.0 + scale2) + shift2
    logits = _mm(h2, wr_ref[...]) + br_ref[...]
    lane = _iota(logits.shape, 1)
    is_group = lane < N_GROUPS
    g_max, g_sel = _first_lane_of_max(logits, lane, is_group)
    g_w = 1.0 / jnp.sum(jnp.where(is_group, jnp.exp(logits - g_max), 0.0), axis=1, keepdims=True)
    expert = lane - N_GROUPS
    in_group = (expert >= 0) & (expert < N_EXPERTS) & (lax.shift_right_arithmetic(expert, EXPERTS_PER_GROUP.bit_length() - 1) == g_sel)
    e_max = jnp.max(jnp.where(in_group, logits, -jnp.inf), axis=1, keepdims=True)
    e_exp = jnp.where(in_group, jnp.exp(logits - e_max), 0.0)
    prob = e_exp / jnp.sum(e_exp, axis=1, keepdims=True)
    p1, i1 = _first_lane_of_max(prob, lane, in_group)
    p2, i2 = _first_lane_of_max(prob, lane, in_group & (lane != i1))
    denom = p1 + p2
    comb = jnp.where(lane == i1, g_w * p1 / denom, 0.0) + jnp.where(lane == i2, g_w * p2 / denom, 0.0)
    xc_ref[:, d:d + LANES] = jnp.where(lane == GROUP_LANE, g_sel.astype(F32), comb)


def _merge(x, ya, yb, mod, p, tm):
    b, t, d = x.shape
    per = t // tm
    assert d == 2 * W_IN_BLOCK
    tok = lambda n: pl.BlockSpec((None, tm, n), lambda i: (i // per, i % per, 0))
    gate_specs, gate_args = _w_in_blocks(p["w_in"], 3, 4)
    weights = [p["rwkv_w_proj"], p["mlstm_w_proj"], p["w_out"], p["norm2_g"], p["moe_w_router"], p["moe_b_router"]]
    return pl.pallas_call(
        _merge_kernel,
        grid=(b * per,),
        in_specs=[tok(d), tok(A_WIDTH), tok(B_WIDTH), pl.BlockSpec((None, 1, 6 * d), lambda i: (i // per, 0, 0)),
                  _full((1, d))] + gate_specs + [_full(w.shape) for w in weights],
        out_specs=tok(d + LANES),
        out_shape=jax.ShapeDtypeStruct((b, t, d + LANES), F32),
        compiler_params=_cparams(("arbitrary",)),
        name="merge",
    )(x, ya, yb, mod, p["norm1_g"], *gate_args, *weights)


def _route(gsel, tb):
    s, l = gsel.shape
    n_buckets = s * N_GROUPS
    max_tiles = s * (l // tb + N_GROUPS - 1)
    onehot = (gsel[..., None] == jnp.arange(N_GROUPS, dtype=jnp.int32)).astype(jnp.int32)
    rank = jnp.sum((jnp.cumsum(onehot, axis=1) - onehot) * onehot, axis=-1)
    n_tiles = ((jnp.sum(onehot, axis=1) + tb - 1) // tb).reshape(n_buckets)
    ends = jnp.cumsum(n_tiles)
    first_tile = jnp.sum(onehot * (ends - n_tiles).reshape(s, 1, N_GROUPS), axis=-1)
    pos = first_tile * tb + rank
    tok = jnp.arange(s * l, dtype=jnp.int32)
    row_src = jnp.full((max_tiles * tb,), -1, jnp.int32).at[pos.reshape(-1)].set(tok)
    tile_bucket = jnp.sum(jnp.arange(max_tiles, dtype=jnp.int32)[:, None] >= ends[None, :], axis=1)
    tile_bucket = jnp.minimum(tile_bucket, n_buckets - 1).astype(jnp.int32)
    return row_src, tile_bucket // N_GROUPS, tile_bucket % N_GROUPS, ends[-1:].astype(jnp.int32)


def _moe_kernel(gsrc_ref, sdst_ref, seg_ref, grp_ref, used_ref,
                xc_hbm, mod_ref, g2_ref, w1_ref, w3_ref, w2_ref, gf_ref, dump_in_hbm,
                y_hbm, dump_hbm, gbuf, obuf, gsem, ssem, *, tb):
    d = D_MODEL
    q = pl.program_id(0)
    n_used = used_ref[0]

    def start_gather(tile, sl):
        def body(i, carry):
            for u in range(SUBLANES):
                idx = gsrc_ref[tile * tb + i * SUBLANES + u]
                pltpu.make_async_copy(xc_hbm.at[lax.shift_right_logical(idx, SUBLANES.bit_length() - 1), pl.ds(idx & (SUBLANES - 1), 1), :],
                                      gbuf.at[sl, i, pl.ds(u, 1), :], gsem.at[sl]).start()
            return carry

        lax.fori_loop(0, tb // SUBLANES, body, 0)

    def wait_gather(sl):
        pltpu.make_async_copy(xc_hbm.at[pl.ds(0, tb // SUBLANES)], gbuf.at[sl], gsem.at[sl]).wait()

    def start_scatter(tile, sl):
        def body(i, carry):
            for u in range(SUBLANES):
                idx = sdst_ref[tile * tb + i * SUBLANES + u]

                @pl.when(idx >= 0)
                def _():
                    pltpu.make_async_copy(
                        obuf.at[sl, i, pl.ds(u, 1), :],
                        y_hbm.at[lax.shift_right_logical(idx, SUBLANES.bit_length() - 1), pl.ds(idx & (SUBLANES - 1), 1), :],
                        ssem.at[sl]).start()

                @pl.when(idx < 0)
                def _():
                    pltpu.make_async_copy(obuf.at[sl, i, pl.ds(u, 1), :], dump_hbm.at[sl, i, pl.ds(u, 1), :],
                                          ssem.at[sl]).start()

            return carry

        lax.fori_loop(0, tb // SUBLANES, body, 0)

    def wait_scatter(sl):
        pltpu.make_async_copy(obuf.at[sl], dump_hbm.at[sl], ssem.at[sl]).wait()

    def step(slot):
        @pl.when(q == 0)
        def _():
            start_gather(0, 0)

        @pl.when(q + 1 < n_used)
        def _():
            start_gather(q + 1, 1 - slot)

        wait_gather(slot)

        @pl.when(q >= 2)
        def _():
            wait_scatter(slot)

        rows = gbuf[slot].reshape(tb, gbuf.shape[-1])
        x1, comb = rows[:, 0:d], rows[:, d:d + LANES]
        shift2, scale2, gate2 = mod_ref[:, 3 * d:4 * d], mod_ref[:, 4 * d:5 * d], mod_ref[:, 5 * d:6 * d]
        h2 = (_rmsnorm(x1, g2_ref[...]) * (1.0 + scale2) + shift2).astype(BF16)
        first_lane = N_GROUPS + grp_ref[q] * EXPERTS_PER_GROUP
        lane = _iota(comb.shape, 1)
        acc = jnp.zeros((tb, d), F32)
        for e in range(EXPERTS_PER_GROUP):
            w_e = jnp.sum(jnp.where(lane == first_lane + e, comb, 0.0), axis=1, keepdims=True)
            a = _mm(h2, w1_ref[e])
            b = _mm(h2, w3_ref[e])
            acc = acc + _mm(_silu(a) * b * w_e, w2_ref[e])
        obuf[slot] = _rmsnorm(x1 + gate2 * acc, gf_ref[...]).reshape(tb // SUBLANES, SUBLANES, d)
        start_scatter(q, slot)

        @pl.when(q == n_used - 1)
        def _():
            @pl.when(q >= 1)
            def _():
                wait_scatter(1 - slot)

            wait_scatter(slot)

    for slot in range(2):
        @pl.when((q < n_used) & (lax.rem(q, 2) == slot))
        def _(slot=slot):
            step(slot)


def _moe(xc, mod, p, tb):
    s, l, width = xc.shape
    d = D_MODEL
    gsel = xc[:, :, d + GROUP_LANE].astype(jnp.int32)
    row_src, tile_seg, tile_grp, n_used = _route(gsel, tb)
    max_tiles = tile_seg.shape[0]
    grouped = lambda w: w.reshape((N_GROUPS, EXPERTS_PER_GROUP) + w.shape[1:])
    w1, w3, w2 = grouped(p["moe_w1"]), grouped(p["moe_w3"]), grouped(p["moe_w2"])
    const = lambda shape: pl.BlockSpec(shape, lambda q, *_: (0,) * len(shape))
    by_group = lambda w: pl.BlockSpec((None,) + w.shape[1:], lambda q, gs, sd, seg, grp, used: (grp[q], 0, 0, 0))
    hbm = pl.BlockSpec(memory_space=pl.ANY)
    y, _ = pl.pallas_call(
        functools.partial(_moe_kernel, tb=tb),
        grid_spec=pltpu.PrefetchScalarGridSpec(
            num_scalar_prefetch=5,
            grid=(max_tiles,),
            in_specs=[hbm, pl.BlockSpec((None, 1, 6 * d), lambda q, gs, sd, seg, grp, used: (seg[q], 0, 0)),
                      const((1, d)), by_group(w1), by_group(w3), by_group(w2), const((1, d)), hbm],
            out_specs=[hbm, hbm],
            scratch_shapes=[pltpu.VMEM((2, tb // SUBLANES, SUBLANES, width), F32),
                            pltpu.VMEM((2, tb // SUBLANES, SUBLANES, d), F32),
                            pltpu.SemaphoreType.DMA((2,)), pltpu.SemaphoreType.DMA((2,))]),
        out_shape=[jax.ShapeDtypeStruct((s * l // SUBLANES, SUBLANES, d), F32),
                   jax.ShapeDtypeStruct((2, tb // SUBLANES, SUBLANES, d), F32)],
        input_output_aliases={12: 1},
        compiler_params=_cparams(("arbitrary",)),
        name="moe",
    )(jnp.maximum(row_src, 0), row_src, tile_seg, tile_grp, n_used,
      xc.reshape(s * l // SUBLANES, SUBLANES, width), mod, p["norm2_g"], w1, w3, w2, p["final_norm_g"],
      jnp.zeros((2, tb // SUBLANES, SUBLANES, d), F32))
    return y.reshape(s, l, d)


def _trunk(x, mod_seg, states, conv2d, p, tm, tb, mixer_seqs, rwkv_packing):
    b, t, d = x.shape
    n_seg = mod_seg.shape[0]
    seg = lambda a: a.reshape(n_seg, b * t // n_seg, a.shape[-1])
    per_seq = lambda a: a.reshape(b, t, a.shape[-1])
    xa, ub, o_pre = (per_seq(a) for a in _in_proj(seg(x), mod_seg, p["norm1_g"], p["w_in"], tm))
    if states is None:
        s_rwkv, mlstm_states = None, None
    else:
        s_rwkv, s_c, s_n, s_m = states
        mlstm_states = (s_c, s_n, jnp.broadcast_to(s_m[..., None], s_m.shape + (LANES,)))
    ya, s_rwkv = _rwkv(xa, s_rwkv, p, *rwkv_packing)
    yb, s_c, s_n, s_m = _mlstm(ub, o_pre, mlstm_states, p, conv2d, mixer_seqs)
    xc = _merge(seg(x), seg(ya), seg(yb), mod_seg, p, tm)
    y = _moe(xc, mod_seg, p, tb)
    return y.reshape(b, t, d), s_rwkv, s_c, s_n, s_m[..., 0]


def kernel(x_prompt, x_sample, c, c_ctx, state_rwkv, state_mlstm_C, state_mlstm_n, state_mlstm_m, w_mod, b_mod, norm1_g, norm2_g, w_in, rwkv_mu, rwkv_w_r, rwkv_w_k, rwkv_w_v, rwkv_w0, rwkv_w1, rwkv_w2, rwkv_a0, rwkv_a1, rwkv_a2, rwkv_g1, rwkv_g2, rwkv_k_k, rwkv_k_a, rwkv_r_k, rwkv_ln_g, rwkv_ln_b, rwkv_w_proj, mlstm_conv, mlstm_w_q, mlstm_w_k, mlstm_w_v, mlstm_w_i, mlstm_b_i, mlstm_w_f, mlstm_b_f, mlstm_ln_g, mlstm_w_proj, w_out, moe_w_group, moe_b_group, moe_w_expert, moe_b_expert, moe_w1, moe_w3, moe_w2, final_norm_g):
    assert w_mod.shape[0] == 1, "single trunk layer"
    l = 0
    bp, dec = x_prompt.shape[0], x_sample.shape[0]
    bf = lambda w: w.astype(BF16)
    row = lambda w: w.reshape(1, -1).astype(F32)
    wg = jnp.concatenate([mlstm_w_i[l, 0], mlstm_w_i[l, 1], mlstm_w_f[l, 0], mlstm_w_f[l, 1]], axis=1)
    bg = jnp.concatenate([mlstm_b_i[l, 0], mlstm_b_i[l, 1], mlstm_b_f[l, 0], mlstm_b_f[l, 1]])
    wg = jnp.pad(wg, ((0, 0), (0, LANES - wg.shape[1])))
    bg = jnp.pad(bg, (0, LANES - bg.shape[0]))
    w_router = jnp.pad(jnp.concatenate([moe_w_group[l], moe_w_expert[l]], axis=1),
                       ((0, 0), (0, LANES - N_GROUPS - N_EXPERTS)))
    b_router = jnp.pad(jnp.concatenate([moe_b_group[l], moe_b_expert[l]]), (0, LANES - N_GROUPS - N_EXPERTS))
    p = {
        "norm1_g": row(norm1_g[l]), "norm2_g": row(norm2_g[l]), "w_in": w_in[l],
        "rwkv_mu": rwkv_mu[l], "rwkv_wr": bf(rwkv_w_r[l]), "rwkv_wk": bf(rwkv_w_k[l]), "rwkv_wv": bf(rwkv_w_v[l]),
        "rwkv_w0": rwkv_w0[l], "rwkv_w1": bf(rwkv_w1[l]), "rwkv_w2": bf(rwkv_w2[l]),
        "rwkv_a0": rwkv_a0[l], "rwkv_a1": bf(rwkv_a1[l]), "rwkv_a2": bf(rwkv_a2[l]),
        "rwkv_g1": bf(rwkv_g1[l]), "rwkv_g2": bf(rwkv_g2[l]),
        "rwkv_k_k": row(rwkv_k_k[l]), "rwkv_k_a": row(rwkv_k_a[l]), "rwkv_r_k": row(rwkv_r_k[l]),
        "rwkv_ln_g": row(rwkv_ln_g[l]), "rwkv_ln_b": row(rwkv_ln_b[l]),
        "rwkv_w_proj": rwkv_w_proj[l],
        "mlstm_conv": mlstm_conv[l].reshape(9, B_WIDTH),
        "mlstm_wq": bf(mlstm_w_q[l]), "mlstm_wk": bf(mlstm_w_k[l]), "mlstm_wv": bf(mlstm_w_v[l]),
        "mlstm_wg": bf(wg), "mlstm_bg": row(bg),
        "mlstm_wgt": bf(wg[:, :16].T), "mlstm_bgt": jnp.broadcast_to(bg[:16, None], (16, LANES)),
        "mlstm_ln_g": row(mlstm_ln_g[l]), "mlstm_w_proj": mlstm_w_proj[l], "w_out": w_out[l],
        "moe_w_router": bf(w_router), "moe_b_router": row(b_router),
        "moe_w1": moe_w1[l], "moe_w3": moe_w3[l], "moe_w2": moe_w2[l],
        "final_norm_g": row(final_norm_g),
    }
    cvec = jnp.concatenate([c_ctx[None, :], c, jnp.zeros((8 - 1 - dec, D_MODEL), F32)], axis=0)
    mod = _mod(cvec, w_mod[l], b_mod[l].reshape(1, -1))
    mod_ctx = mod[0:1][:, None, :]
    mod_lat = mod[1:1 + dec][:, None, :]

    yp, n_rwkv, n_c, n_n, n_m = _trunk(x_prompt, mod_ctx, None, False, p, 512, 512, 2, (2, 2))
    lat_states = tuple(s[:, l].astype(F32) for s in (state_rwkv, state_mlstm_C, state_mlstm_n, state_mlstm_m))
    ys, _, _, _, _ = _trunk(x_sample, mod_lat, lat_states, True, p, 512, 256, 1, (1, 2))
    dt = x_prompt.dtype
    return (yp, ys, n_rwkv[:, None].astype(dt), n_c[:, None].astype(dt), n_n[:, None].astype(dt),
            n_m[:, None].astype(dt))
```

```python
import functools

import jax
import jax.numpy as jnp
from jax import lax
from jax.experimental import pallas as pl
from jax.experimental.pallas import tpu as pltpu

F32 = jnp.float32
BF16 = jnp.bfloat16

D_MODEL = 1024
GRID_W = 64
A_HEAD_DIM = 64
A_WIDTH = 512
A_HEADS = 8
B_HEAD_DIM = 128
B_WIDTH = 512
B_HEADS = 4
CHUNK = 64
ROW_TILE = 512
SUM_TILE = 256
N_GROUPS = 4
EXPERTS_PER_GROUP = 4
N_EXPERTS = 16
D_EXPERT = 256
GROUP_LANE = N_GROUPS + N_EXPERTS
NORM_EPS = 1e-6
RWKV_GN_EPS = 64e-5
MLSTM_GN_EPS = 1e-5
LANES = 128
SUBLANES = 8
CONV_PAD = 72
VMEM_LIMIT = 56 * 1024 * 1024


def _mm(a, b):
    return jnp.dot(a.astype(BF16), b.astype(BF16), preferred_element_type=F32)


_NN, _NT, _TN = ((1,), (0,)), ((1,), (1,)), ((0,), (0,))


def _split(a):
    hi = a.astype(BF16)
    return hi, (a - hi.astype(F32)).astype(BF16)


def _split3(a):
    hi = a.astype(BF16)
    rest = a - hi.astype(F32)
    lo = rest.astype(BF16)
    return hi, lo, (rest - lo.astype(F32)).astype(BF16)


def _sum01(a, b, dims=_NN):
    dot = lambda p, q: lax.dot_general(p, q, (dims, ((), ())), preferred_element_type=F32)
    terms = [dot(x, b) for x in a] if isinstance(a, tuple) else [dot(a, x) for x in b]
    return functools.reduce(lambda p, q: p + q, terms)


def _mm3(a, b, dims=_NN):
    (ah, al), (bh, bl) = a, b
    dot = lambda p, q: lax.dot_general(p, q, (dims, ((), ())), preferred_element_type=F32)
    return dot(ah, bh) + dot(ah, bl) + dot(al, bh)


def _mm_nt(a, b):
    return lax.dot_general(a.astype(BF16), b.astype(BF16), (((1,), (1,)), ((), ())),
                           preferred_element_type=F32)


def _mm_tn(a, b):
    return lax.dot_general(a.astype(BF16), b.astype(BF16), (((0,), (0,)), ((), ())),
                           preferred_element_type=F32)


def _sigmoid(x):
    return 1.0 / (1.0 + jnp.exp(-x))


def _silu(x):
    return x * _sigmoid(x)


def _softplus(x):
    return jnp.maximum(x, 0.0) + jnp.log(1.0 + jnp.exp(-jnp.abs(x)))


def _iota(shape, dim):
    return lax.broadcasted_iota(jnp.int32, shape, dim)


def _tri(reverse, inclusive, n=CHUNK):
    row, col = _iota((n, n), 0), _iota((n, n), 1)
    if reverse:
        return (col >= row) if inclusive else (col > row)
    return (col <= row) if inclusive else (col < row)


def _same_block(n, blk):
    sh = blk.bit_length() - 1
    row, col = _iota((n, n), 0), _iota((n, n), 1)
    return lax.shift_right_logical(row, sh) == lax.shift_right_logical(col, sh)


def _rmsnorm(x, g):
    return x * lax.rsqrt(jnp.mean(x * x, axis=-1, keepdims=True) + NORM_EPS) * g


def _cparams(sem):
    return pltpu.CompilerParams(dimension_semantics=sem, vmem_limit_bytes=VMEM_LIMIT)


def _full(shape):
    nd = len(shape)
    return pl.BlockSpec(shape, lambda *_: (0,) * nd)


def _mod_kernel(c_ref, w_ref, b_ref, o_ref):
    o_ref[...] = _mm(_silu(c_ref[...]), w_ref[...]) + b_ref[...]


def _mod(cvec, w_mod, b_mod):
    rows, d = cvec.shape
    n = w_mod.shape[1]
    tn = 1536
    return pl.pallas_call(
        _mod_kernel,
        grid=(n // tn,),
        in_specs=[_full((rows, d)), pl.BlockSpec((d, tn), lambda j: (0, j)),
                  pl.BlockSpec((1, tn), lambda j: (0, j))],
        out_specs=pl.BlockSpec((rows, tn), lambda j: (0, j)),
        out_shape=jax.ShapeDtypeStruct((rows, n), F32),
        compiler_params=_cparams(("arbitrary",)),
        name="mod",
    )(cvec, w_mod, b_mod)


W_IN_BLOCK = 512


def _modulated_norm1(x_ref, mod_ref, g_ref):
    d = D_MODEL
    shift, scale = mod_ref[:, 0:d], mod_ref[:, d:2 * d]
    return (_rmsnorm(x_ref[...], g_ref[...]) * (1.0 + scale) + shift).astype(BF16)


def _w_in_blocks(w_in, first, count):
    specs = [pl.BlockSpec((w_in.shape[0], W_IN_BLOCK), lambda *_, j=first + k: (0, j)) for k in range(count)]
    return specs, [w_in] * count


def _in_kernel(x_ref, mod_ref, g_ref, wa_ref, wu_ref, wo_ref, xa_ref, ub_ref, op_ref):
    h = _modulated_norm1(x_ref, mod_ref, g_ref)
    for ref, w_ref in ((xa_ref, wa_ref), (ub_ref, wu_ref), (op_ref, wo_ref)):
        ref[...] = _mm(h, w_ref[...])


def _in_proj(x, mod, g1, w_in, tm):
    b, t, d = x.shape
    per = t // tm
    widths = (A_WIDTH, B_WIDTH, B_WIDTH)
    assert all(n == W_IN_BLOCK for n in widths)
    tok = lambda n: pl.BlockSpec((None, tm, n), lambda i: (i // per, i % per, 0))
    w_specs, w_args = _w_in_blocks(w_in, 0, 3)
    return pl.pallas_call(
        _in_kernel,
        grid=(b * per,),
        in_specs=[tok(d), pl.BlockSpec((None, 1, 6 * d), lambda i: (i // per, 0, 0)), _full((1, d))] + w_specs,
        out_specs=[tok(n) for n in widths],
        out_shape=[jax.ShapeDtypeStruct((b, t, n), F32) for n in widths],
        compiler_params=_cparams(("arbitrary",)),
        name="in_proj",
    )(x, mod, g1, *w_args)


def _head_of_lane(shape):
    return lax.shift_right_logical(_iota(shape, 1), A_HEAD_DIM.bit_length() - 1)


def _pair_diag(x):
    head = _head_of_lane(x.shape)
    return jnp.concatenate([x * (head == h).astype(x.dtype) for h in range(x.shape[1] // A_HEAD_DIM)], axis=0)


def _pair_mm3(lhs, rhs, dims=_NN):
    rows = lhs[0].shape[0]
    if dims == _NT:
        wh, wl = _split(_pair_diag(rhs).T)
    else:
        rh, rl = _split(rhs)
        wh, wl = _pair_diag(rh), _pair_diag(rl)
    parts = [_split(l) for l in lhs]
    top = jnp.dot(jnp.concatenate([t for part in parts for t in part], axis=0), wh, preferred_element_type=F32)
    low = jnp.dot(jnp.concatenate([part[0] for part in parts], axis=0), wl, preferred_element_type=F32)
    return [top[2 * k * rows:(2 * k + 1) * rows] + top[(2 * k + 1) * rows:(2 * k + 2) * rows]
            + low[k * rows:(k + 1) * rows] for k in range(len(lhs))]


def _pair_index(shape):
    return _iota(shape, 0), _iota(shape, 1) & (A_HEAD_DIM - 1)


def _pair_tri_inverses(mats):
    t_idx, s_idx = _pair_index(mats[0].shape)
    same = lambda blk: (lax.shift_right_logical(t_idx, blk.bit_length() - 1)
                        == lax.shift_right_logical(s_idx, blk.bit_length() - 1))
    eye = (t_idx == s_idx).astype(F32)
    nd = [jnp.where(same(8), m, 0.0) for m in mats]
    n2 = [_pair_mm3([x], x)[0] for x in nd]
    t = [eye + x for x in nd]
    both = [_pair_mm3([a, b], a) for a, b in zip(n2, t)]
    t = [x + r[1] for x, r in zip(t, both)]
    t = [x + _pair_mm3([x], r[0])[0] for x, r in zip(t, both)]
    blk = 8
    while blk < CHUNK:
        sel = same(2 * blk) & jnp.logical_not(same(blk))
        w = [_pair_mm3([jnp.where(sel, m, 0.0)], x)[0] for m, x in zip(mats, t)]
        t = [x + _pair_mm3([x], y)[0] for x, y in zip(t, w)]
        blk *= 2
    return t


def _rwkv_kernel(x_ref, s0_ref, mu_ref, wr_ref, wk_ref, wv_ref, w0_ref, w1_ref, w2_ref,
                 a0_ref, a1_ref, a2_ref, g1_ref, g2_ref, kk_ref, ka_ref, rk_ref, lng_ref, lnb_ref,
                 y_ref, s_ref,
                 r_s, v_s, kk_s, cum_s, cx_s, kd_s, bb_s, g_s, bonus_s, y_s, sp_s, *, seq_len):
    total = x_ref.shape[0]
    n_seq = total // seq_len
    n_chunks = seq_len // CHUNK
    n_tiles = total // ROW_TILE
    hd = A_HEAD_DIM
    n_groups, width = sp_s.shape[2], sp_s.shape[4]
    per_group = width // hd
    x = x_ref[...]
    row = _iota(x.shape, 0) & (seq_len - 1)
    prev = jnp.where(row == 0, 0.0, pltpu.roll(x, 1, 0))
    nxt = jnp.where(row == seq_len - 1, 0.0, pltpu.roll(x, total - 1, 0))
    y_s[...] = 0.5 * (prev + nxt) - x
    head_ones = _same_block(A_WIDTH, hd).astype(BF16)
    chunk_diag = _same_block(SUM_TILE, CHUNK)
    run_sum = [(chunk_diag & _tri(d == 1, True, SUM_TILE)).astype(BF16) for d in range(2)]

    def project_tile(ti, carry):
        rows = pl.ds(pl.multiple_of(ti * ROW_TILE, ROW_TILE), ROW_TILE)
        x, xx = x_ref[rows, :], y_s[rows, :]
        xr, xw, xk, xv, xi, xg = (x + xx * mu_ref[j:j + 1, :] for j in range(6))
        r = _mm(xr, wr_ref[...])
        k = _mm(xk, wk_ref[...])
        v = _mm(xv, wv_ref[...])
        g_s[rows, :] = _mm(_sigmoid(_mm(xg, g1_ref[...])), g2_ref[...])
        kk = k * kk_ref[...]
        kk = kk / jnp.maximum(jnp.sqrt(_sum01(_split(kk * kk), head_ones)), 1e-12)
        r_s[rows, :] = r
        v_s[rows, :] = v
        kk_s[rows, :] = kk
        kd_sum = jnp.zeros_like(x)
        for d in range(2):
            zw = w0_ref[d:d + 1, :] + _mm(jnp.tanh(_mm(xw, w1_ref[d])), w2_ref[d])
            lw = -jnp.exp(-_softplus(-zw) - 0.5)
            cum = jnp.concatenate([_sum01(run_sum[d], _split3(lw[j:j + SUM_TILE]))
                                   for j in range(0, ROW_TILE, SUM_TILE)], axis=0)
            cum_s[d, rows, :] = cum
            cx_s[d, rows, :] = cum - lw
            a = _sigmoid(a0_ref[d:d + 1, :] + _mm(_mm(xi, a1_ref[d]), a2_ref[d]))
            kd = k * (1.0 + (a - 1.0) * ka_ref[...])
            kd_s[d, rows, :] = kd
            bb_s[d, rows, :] = kk * a
            kd_sum = kd_sum + kd
        bonus_s[rows, :] = _sum01(_split(r * kd_sum * rk_ref[...]), head_ones) * v
        y_s[rows, :] = jnp.zeros_like(x)
        return carry

    lax.fori_loop(0, n_tiles, project_tile, 0)
    for n in range(n_seq):
        for d in range(2):
            for p in range(n_groups):
                sp_s[n, d, p] = (jnp.zeros((hd, width), F32) if s0_ref is None else jnp.concatenate(
                    [s0_ref[n, d, per_group * p + j] for j in range(per_group)], axis=1))

    t_idx, s_idx = _pair_index((CHUNK, width))
    strict = [s_idx < t_idx, s_idx > t_idx]
    incl = [s_idx <= t_idx, s_idx >= t_idx]
    lane_head = _head_of_lane((CHUNK, width))

    def chunk_body(ci, carry):
        chains = []
        for n, d in [(n, d) for n in range(n_seq) for d in range(2)]:
            cpos = n * n_chunks + ((n_chunks - 1 - ci) if d == 1 else ci)
            last = 0 if d == 1 else CHUNK - 1
            rows = pl.ds(pl.multiple_of(cpos * CHUNK, CHUNK), CHUNK)
            cum, cx = cum_s[d, rows, :], cx_s[d, rows, :]
            tot = cum[last:last + 1, :]
            kdc, bbc = kd_s[d, rows, :], bb_s[d, rows, :]
            e_neg = jnp.exp(-cum)
            e_tail = jnp.exp(tot - cum)
            at = -kk_s[rows, :] * jnp.exp(cx)
            rt = r_s[rows, :] * jnp.exp(cum)
            bt, kt = bbc * e_neg, kdc * e_neg
            bp, kp = bbc * e_tail, kdc * e_tail
            pc = jnp.exp(tot)
            vc = v_s[rows, :]
            for p in range(n_groups):
                ln = slice(p * width, (p + 1) * width)
                chains.append(dict(d=d, st=(n, d, p), rows=rows, ln=ln, v=vc[:, ln], pc=pc[:, ln],
                                   lhs=jnp.concatenate([at[:, ln], rt[:, ln]], axis=0),
                                   bt=bt[:, ln], kt=kt[:, ln], bp=bp[:, ln], kp=kp[:, ln]))
        for c in chains:
            c["sb"] = _pair_mm3([c["lhs"]], c["bt"], _NT)[0]
            c["sk"] = _pair_mm3([c["lhs"]], c["kt"], _NT)[0]
            c["s"] = sp_s[c["st"]]
            c["fs"] = _mm(c["lhs"], _pair_diag(c["s"]).T)
            c["v_diag"] = _pair_diag(c["v"].astype(BF16))
        for c in chains:
            a_ak = jnp.where(strict[c["d"]], c["sk"][:CHUNK], 0.0)
            c["x"] = c["fs"][:CHUNK] + _mm(a_ak, c["v_diag"])
        inv = _pair_tri_inverses([jnp.where(strict[c["d"]], c["sb"][:CHUNK], 0.0) for c in chains])
        for c, t in zip(chains, inv):
            c["u"] = _pair_mm3([t], c["x"])[0]
        for c in chains:
            d, rows, ln, u = c["d"], c["rows"], c["ln"], c["u"]
            a_r = jnp.concatenate([jnp.where(incl[d], c["sb"][CHUNK:], 0.0),
                                   jnp.where(incl[d], c["sk"][CHUNK:], 0.0)], axis=1)
            uv_diag = jnp.concatenate([_pair_diag(u.astype(BF16)), c["v_diag"]], axis=0)
            y_s[rows, ln] = y_s[rows, ln] + c["fs"][CHUNK:] + _mm(a_r, uv_diag)
            g = _mm_tn(jnp.concatenate([u, c["v"]], axis=0), jnp.concatenate([c["bp"], c["kp"]], axis=0))
            own = g[0:hd]
            for j in range(1, per_group):
                own = jnp.where(lane_head == j, g[j * hd:(j + 1) * hd], own)
            sp_s[c["st"]] = c["s"] * c["pc"] + own
        return carry

    lax.fori_loop(0, n_chunks, chunk_body, 0)
    for n in range(n_seq):
        for d in range(2):
            for p in range(n_groups):
                for j in range(per_group):
                    s_ref[n, d, per_group * p + j] = sp_s[n, d, p, :, j * hd:(j + 1) * hd]

    def finish_tile(ti, carry):
        rows = pl.ds(pl.multiple_of(ti * ROW_TILE, ROW_TILE), ROW_TILE)
        ys = y_s[rows, :]
        cen = ys - _sum01(_split(ys), head_ones) * (1.0 / hd)
        var = _sum01(_split(cen * cen), head_ones) * (1.0 / hd)
        yn = cen * lax.rsqrt(var + RWKV_GN_EPS) * lng_ref[...] + lnb_ref[...]
        y_ref[rows, :] = (yn + bonus_s[rows, :]) * g_s[rows, :]
        return carry

    lax.fori_loop(0, n_tiles, finish_tile, 0)


def _rwkv(xa, s0, p, n_seq, heads_per_block):
    b, t, w = xa.shape
    assert t & (t - 1) == 0 and b % n_seq == 0
    names = ("mu", "wr", "wk", "wv", "w0", "w1", "w2", "a0", "a1", "a2", "g1", "g2",
             "k_k", "k_a", "r_k", "ln_g", "ln_b")
    weights = [p["rwkv_" + n] for n in names]
    rows = n_seq * t
    seq = pl.BlockSpec((None, rows, w), lambda i: (i, 0, 0))
    st = pl.BlockSpec((n_seq, 2, A_HEADS, A_HEAD_DIM, A_HEAD_DIM), lambda i: (i, 0, 0, 0, 0))
    tw = lambda lead=(): pltpu.VMEM(lead + (rows, w), F32)
    kernel = functools.partial(_rwkv_kernel, seq_len=t)
    if s0 is None:
        kernel, state_specs, states = _zero_state_kernel(kernel, 1, 1), [], ()
    else:
        state_specs, states = [st], (s0,)
    y, s_out = pl.pallas_call(
        kernel,
        grid=(b // n_seq,),
        in_specs=[seq] + state_specs + [_full(x.shape) for x in weights],
        out_specs=[seq, st],
        out_shape=[jax.ShapeDtypeStruct((b // n_seq, rows, w), F32),
                   jax.ShapeDtypeStruct((b, 2, A_HEADS, A_HEAD_DIM, A_HEAD_DIM), F32)],
        scratch_shapes=[tw(), tw(), tw(), tw((2,)), tw((2,)), tw((2,)), tw((2,)), tw(), tw(), tw(),
                        pltpu.VMEM((n_seq, 2, A_HEADS // heads_per_block, A_HEAD_DIM,
                                    heads_per_block * A_HEAD_DIM), F32)],
        compiler_params=_cparams(("arbitrary",)),
        name="rwkv",
    )(xa.reshape(b // n_seq, rows, w), *states, *weights)
    return y.reshape(b, t, w), s_out


def _mlstm_kernel(u_ref, op_ref, c0_ref, n0_ref, m0_ref, conv_ref, wq_ref, wk_ref, wv_ref,
                  wg_ref, bg_ref, wgt_ref, bgt_ref, lng_ref,
                  y_ref, c_ref, n_ref, m_ref,
                  pad_s, q_s, k_s, v_s, dd_s, cr_s, h_s, *, conv2d, seq_len):
    t_len = u_ref.shape[0]
    n_seq = t_len // seq_len
    n_chunks = seq_len // CHUNK
    hd = B_HEAD_DIM
    u = u_ref[...]
    pad_s[0:CONV_PAD, :] = jnp.zeros((CONV_PAD, B_WIDTH), F32)
    pad_s[CONV_PAD + t_len:CONV_PAD + t_len + CONV_PAD, :] = jnp.zeros((CONV_PAD, B_WIDTH), F32)
    pad_s[CONV_PAD:CONV_PAD + t_len, :] = u
    pos = _iota(u.shape, 0) & (seq_len - 1)
    col = pos & (GRID_W - 1)
    conv = jnp.zeros_like(u)
    for kh in range(3):
        if not conv2d and kh != 1:
            continue
        for kw in range(3):
            shift = (kh - 1) * GRID_W + (kw - 1)
            term = pad_s[CONV_PAD + shift:CONV_PAD + shift + t_len, :] * conv_ref[kh * 3 + kw:kh * 3 + kw + 1, :]
            if conv2d and kw != 1:
                src = col + (kw - 1)
                term = jnp.where((src >= 0) & (src < GRID_W), term, 0.0)
            if n_seq > 1 and shift != 0:
                term = jnp.where((pos + shift >= 0) & (pos + shift < seq_len), term, 0.0)
            conv = conv + term
    uc = _silu(conv)
    for h in range(B_HEADS):
        ln = slice(h * hd, (h + 1) * hd)
        q_s[:, ln] = _mm(uc[:, ln], wq_ref[h])
        k_s[:, ln] = _mm(uc[:, ln], wk_ref[h]) * (hd ** -0.5)
        v_s[:, ln] = _mm(u[:, ln], wv_ref[h])
    gcol = _mm(uc, wg_ref[...]) + bg_ref[...]
    gcol = jnp.where(_iota(gcol.shape, 1) >= 8, -_softplus(-gcol), gcol)
    grow = _mm_nt(wgt_ref[...], uc) + bgt_ref[:, 0:1]
    grow = jnp.where(_iota(grow.shape, 0) >= 8, -_softplus(-grow), grow)
    chunk_diag = _same_block(SUM_TILE, CHUNK)
    run_sum = [(chunk_diag & _tri(d == 1, True, SUM_TILE)).astype(BF16) for d in range(2)]
    per_tile = SUM_TILE // CHUNK
    for ti in range(t_len // SUM_TILE):
        sl = slice(ti * SUM_TILE, (ti + 1) * SUM_TILE)
        gt, rt = _split3(gcol[sl, :]), _split3(grow[:, sl])
        backward_lane = (_iota((SUM_TILE, LANES), 1) & 4) != 0
        cum_c = jnp.where(backward_lane, _sum01(run_sum[1], gt, _NN), _sum01(run_sum[0], gt, _NN))
        dd_s[sl, :] = pltpu.roll(cum_c, LANES - 8, 1) - gcol[sl, :]
        backward_row = (_iota((16, SUM_TILE), 0) & 4) != 0
        cum_r = jnp.where(backward_row, _sum01(rt, run_sum[1], _NT), _sum01(rt, run_sum[0], _NT))
        for c in range(per_tile):
            cr_s[ti * per_tile + c] = cum_r[:, c * CHUNK:(c + 1) * CHUNK]
    h_s[...] = jnp.zeros_like(u)
    for ref, init in ((c_ref, c0_ref), (n_ref, n0_ref), (m_ref, m0_ref)):
        ref[...] = jnp.zeros(ref.shape, F32) if init is None else init[...]
    row_i, col_i = _iota((CHUNK, CHUNK), 0), _iota((CHUNK, CHUNK), 1)
    incl_t = [row_i <= col_i, row_i >= col_i]
    eye = row_i == col_i
    pick = [(_iota((LANES, B_HEADS * LANES), 0)
             == d * B_HEADS + lax.shift_right_logical(_iota((LANES, B_HEADS * LANES), 1), LANES.bit_length() - 1)
             ).astype(BF16) for d in range(2)]
    spread = (lax.shift_right_logical(_iota((2 * CHUNK, 2 * LANES), 0), CHUNK.bit_length() - 1)
              == lax.shift_right_logical(_iota((2 * CHUNK, 2 * LANES), 1), LANES.bit_length() - 1)).astype(BF16)
    ones_bf = jnp.ones((CHUNK, LANES), BF16)

    def chunk_body(ci, carry):
        chains = []
        for n, d in [(n, d) for n in range(n_seq) for d in range(2)]:
            cpos = n * n_chunks + ((n_chunks - 1 - ci) if d == 1 else ci)
            last = 0 if d == 1 else CHUNK - 1
            rows = pl.ds(pl.multiple_of(cpos * CHUNK, CHUNK), CHUNK)
            cr = cr_s[cpos]
            d_col = _sum01(_split3(dd_s[rows, :]), pick[d])
            for h in range(B_HEADS):
                ln = slice(h * hd, (h + 1) * hd)
                lf = 8 + d * 4 + h
                chains.append(dict(d=d, st=(n, d, h), vec=(n, d, slice(h, h + 1)), rows=rows, ln=ln, last=last,
                                   b_row=cr[lf:lf + 1, :], d_col=d_col[:, ln],
                                   m=m_ref[n, d, h:h + 1, 0:1], q=q_s[rows, ln], k=k_s[rows, ln], v=v_s[rows, ln],
                                   c=c_ref[n, d, h], n=n_ref[n, d, h:h + 1, :]))
        for c in chains:
            c["kq"] = _mm(c["k"], c["q"].T)
            c["qc"] = _mm(c["q"], c["c"])
            c["qn"] = _mm(c["q"], jnp.broadcast_to(c["n"], (LANES, hd)).T)
            logw = jnp.where(incl_t[c["d"]], c["b_row"] - c["d_col"][:, 0:CHUNK], -jnp.inf)
            c["m_row"] = jnp.maximum(c["b_row"] + c["m"], jnp.max(logw, axis=0, keepdims=True))
            c["logw"] = logw
        for c in chains:
            inter_row = jnp.exp(c["b_row"] + c["m"] - c["m_row"])
            diag = jnp.concatenate([jnp.where(eye, c["m_row"], 0.0), jnp.where(eye, inter_row, 0.0)], axis=1)
            cols = _sum01(_split3(diag), spread)
            c["m_col"], c["inter"] = cols[:, 0:LANES], cols[:, LANES:2 * LANES]
            c["s"] = c["kq"] * jnp.exp(c["logw"] - c["m_row"])
        for c in chains:
            s_hi, s_lo = _split(c["s"])
            sv = lax.dot_general(s_hi, jnp.concatenate([c["v"].astype(BF16), ones_bf], axis=1),
                                 (_TN, ((), ())), preferred_element_type=F32)
            c["sv"] = sv[:, 0:hd]
            c["den"] = (c["inter"] * c["qn"] + sv[:, hd:hd + LANES]
                        + lax.dot_general(s_lo, ones_bf, (_TN, ((), ())), preferred_element_type=F32))
            last = c["last"]
            c["m_new"] = c["m_row"][:, last:last + 1]
            b_last = c["b_row"][:, last:last + 1]
            c["decay"] = jnp.exp(b_last + c["m"] - c["m_new"])
            c["kw"] = c["k"] * jnp.exp(b_last - c["m_new"] - c["d_col"])
        for c in chains:
            c["kv"] = _mm_tn(c["kw"], c["v"])
        for c in chains:
            rows, ln = c["rows"], c["ln"]
            num = c["inter"] * c["qc"] + c["sv"]
            h_s[rows, ln] = h_s[rows, ln] + num / jnp.maximum(jnp.abs(c["den"]), jnp.exp(-c["m_col"]))
            c_ref[c["st"]] = c["decay"] * c["c"] + c["kv"]
            n_ref[c["vec"]] = c["decay"] * c["n"] + jnp.sum(c["kw"], axis=0, keepdims=True)
            m_ref[c["vec"]] = jnp.broadcast_to(c["m_new"], (1, LANES))
        return carry

    lax.fori_loop(0, n_chunks, chunk_body, 0)

    for h in range(B_HEADS):
        ln = slice(h * hd, (h + 1) * hd)
        hh = h_s[:, ln]
        cen = hh - jnp.mean(hh, axis=1, keepdims=True)
        var = jnp.mean(cen * cen, axis=1, keepdims=True)
        y_ref[:, ln] = _sigmoid(op_ref[:, ln]) * (cen * lax.rsqrt(var + MLSTM_GN_EPS) * lng_ref[:, ln])


def _zero_state_kernel(kernel, n_data, n_state):
    def wrapped(*refs, **kw):
        return kernel(*refs[:n_data], *([None] * n_state), *refs[n_data:], **kw)
    return wrapped


def _mlstm(ub, o_pre, states, p, conv2d, n_seq):
    b, t, w = ub.shape
    assert t & (t - 1) == 0 and b % n_seq == 0
    names = ("conv", "wq", "wk", "wv", "wg", "bg", "wgt", "bgt", "ln_g")
    weights = [p["mlstm_" + n] for n in names]
    rows, steps = n_seq * t, b // n_seq
    seq = pl.BlockSpec((None, rows, w), lambda i: (i, 0, 0))
    cst = pl.BlockSpec((n_seq, 2, B_HEADS, B_HEAD_DIM, B_HEAD_DIM), lambda i: (i, 0, 0, 0, 0))
    vst = pl.BlockSpec((n_seq, 2, B_HEADS, LANES), lambda i: (i, 0, 0, 0))
    kernel = functools.partial(_mlstm_kernel, conv2d=conv2d, seq_len=t)
    if states is None:
        kernel, state_specs, states = _zero_state_kernel(kernel, 2, 3), [], ()
    else:
        state_specs = [cst, vst, vst]
    y, c_out, n_out, m_out = pl.pallas_call(
        kernel,
        grid=(steps,),
        in_specs=[seq, seq] + state_specs + [_full(x.shape) for x in weights],
        out_specs=[seq, cst, vst, vst],
        out_shape=[jax.ShapeDtypeStruct((steps, rows, w), F32),
                   jax.ShapeDtypeStruct((b, 2, B_HEADS, B_HEAD_DIM, B_HEAD_DIM), F32),
                   jax.ShapeDtypeStruct((b, 2, B_HEADS, LANES), F32),
                   jax.ShapeDtypeStruct((b, 2, B_HEADS, LANES), F32)],
        scratch_shapes=[pltpu.VMEM((rows + 2 * CONV_PAD, w), F32), pltpu.VMEM((rows, w), F32),
                        pltpu.VMEM((rows, w), F32), pltpu.VMEM((rows, w), F32),
                        pltpu.VMEM((rows, LANES), F32), pltpu.VMEM((rows // CHUNK, 16, CHUNK), F32),
                        pltpu.VMEM((rows, w), F32)],
        compiler_params=_cparams(("arbitrary",)),
        name="mlstm",
    )(ub.reshape(steps, rows, w), o_pre.reshape(steps, rows, w), *states, *weights)
    return y.reshape(b, t, w), c_out, n_out, m_out


def _first_lane_of_max(val, lane, valid):
    masked = jnp.where(valid, val, -jnp.inf)
    best = jnp.max(masked, axis=1, keepdims=True)
    idx = jnp.min(jnp.where(valid & (masked == best), lane, LANES), axis=1, keepdims=True)
    return best, idx


def _merge_kernel(x_ref, ya_ref, yb_ref, mod_ref, g1_ref, wga0_ref, wga1_ref, wgb0_ref, wgb1_ref,
                  wpa_ref, wpb_ref, wo_ref, g2_ref, wr_ref, br_ref, xc_ref):
    d = D_MODEL
    gate1 = mod_ref[:, 2 * d:3 * d]
    shift2, scale2 = mod_ref[:, 3 * d:4 * d], mod_ref[:, 4 * d:5 * d]
    h1 = _modulated_norm1(x_ref, mod_ref, g1_ref)
    pa, pb = _mm(ya_ref[...], wpa_ref[...]), _mm(yb_ref[...], wpb_ref[...])
    halves = []
    for j, (wga_ref, wgb_ref) in enumerate(((wga0_ref, wgb0_ref), (wga1_ref, wgb1_ref))):
        cols = slice(j * W_IN_BLOCK, (j + 1) * W_IN_BLOCK)
        halves.append(_sigmoid(_mm(h1, wga_ref[...])) * pa[:, cols] + _sigmoid(_mm(h1, wgb_ref[...])) * pb[:, cols])
    merged = jnp.concatenate(halves, axis=1)
    x1 = x_ref[...] + gate1 * _mm(merged, wo_ref[...])
    xc_ref[:, 0:d] = x1
    h2 = _rmsnorm(x1, g2_ref[...]) * (1.0 + scale2) + shift2
    logits = _mm(h2, wr_ref[...]) + br_ref[...]
    lane = _iota(logits.shape, 1)
    is_group = lane < N_GROUPS
    g_max, g_sel = _first_lane_of_max(logits, lane, is_group)
    g_w = 1.0 / jnp.sum(jnp.where(is_group, jnp.exp(logits - g_max), 0.0), axis=1, keepdims=True)
    expert = lane - N_GROUPS
    in_group = (expert >= 0) & (expert < N_EXPERTS) & (lax.shift_right_arithmetic(expert, EXPERTS_PER_GROUP.bit_length() - 1) == g_sel)
    e_max = jnp.max(jnp.where(in_group, logits, -jnp.inf), axis=1, keepdims=True)
    e_exp = jnp.where(in_group, jnp.exp(logits - e_max), 0.0)
    prob = e_exp / jnp.sum(e_exp, axis=1, keepdims=True)
    p1, i1 = _first_lane_of_max(prob, lane, in_group)
    p2, i2 = _first_lane_of_max(prob, lane, in_group & (lane != i1))
    denom = p1 + p2
    comb = jnp.where(lane == i1, g_w * p1 / denom, 0.0) + jnp.where(lane == i2, g_w * p2 / denom, 0.0)
    xc_ref[:, d:d + LANES] = jnp.where(lane == GROUP_LANE, g_sel.astype(F32), comb)


def _merge(x, ya, yb, mod, p, tm):
    b, t, d = x.shape
    per = t // tm
    assert d == 2 * W_IN_BLOCK
    tok = lambda n: pl.BlockSpec((None, tm, n), lambda i: (i // per, i % per, 0))
    gate_specs, gate_args = _w_in_blocks(p["w_in"], 3, 4)
    weights = [p["rwkv_w_proj"], p["mlstm_w_proj"], p["w_out"], p["norm2_g"], p["moe_w_router"], p["moe_b_router"]]
    return pl.pallas_call(
        _merge_kernel,
        grid=(b * per,),
        in_specs=[tok(d), tok(A_WIDTH), tok(B_WIDTH), pl.BlockSpec((None, 1, 6 * d), lambda i: (i // per, 0, 0)),
                  _full((1, d))] + gate_specs + [_full(w.shape) for w in weights],
        out_specs=tok(d + LANES),
        out_shape=jax.ShapeDtypeStruct((b, t, d + LANES), F32),
        compiler_params=_cparams(("arbitrary",)),
        name="merge",
    )(x, ya, yb, mod, p["norm1_g"], *gate_args, *weights)


def _route(gsel, tb):
    s, l = gsel.shape
    n_buckets = s * N_GROUPS
    max_tiles = s * (l // tb + N_GROUPS - 1)
    onehot = (gsel[..., None] == jnp.arange(N_GROUPS, dtype=jnp.int32)).astype(jnp.int32)
    rank = jnp.sum((jnp.cumsum(onehot, axis=1) - onehot) * onehot, axis=-1)
    n_tiles = ((jnp.sum(onehot, axis=1) + tb - 1) // tb).reshape(n_buckets)
    ends = jnp.cumsum(n_tiles)
    first_tile = jnp.sum(onehot * (ends - n_tiles).reshape(s, 1, N_GROUPS), axis=-1)
    pos = first_tile * tb + rank
    tok = jnp.arange(s * l, dtype=jnp.int32)
    row_src = jnp.full((max_tiles * tb,), -1, jnp.int32).at[pos.reshape(-1)].set(tok)
    tile_bucket = jnp.sum(jnp.arange(max_tiles, dtype=jnp.int32)[:, None] >= ends[None, :], axis=1)
    tile_bucket = jnp.minimum(tile_bucket, n_buckets - 1).astype(jnp.int32)
    return row_src, tile_bucket // N_GROUPS, tile_bucket % N_GROUPS, ends[-1:].astype(jnp.int32)


def _moe_kernel(gsrc_ref, sdst_ref, seg_ref, grp_ref, used_ref,
                xc_hbm, mod_ref, g2_ref, w1_ref, w3_ref, w2_ref, gf_ref, dump_in_hbm,
                y_hbm, dump_hbm, gbuf, obuf, gsem, ssem, *, tb):
    d = D_MODEL
    q = pl.program_id(0)
    n_used = used_ref[0]

    def start_gather(tile, sl):
        def body(i, carry):
            for u in range(SUBLANES):
                idx = gsrc_ref[tile * tb + i * SUBLANES + u]
                pltpu.make_async_copy(xc_hbm.at[lax.shift_right_logical(idx, SUBLANES.bit_length() - 1), pl.ds(idx & (SUBLANES - 1), 1), :],
                                      gbuf.at[sl, i, pl.ds(u, 1), :], gsem.at[sl]).start()
            return carry

        lax.fori_loop(0, tb // SUBLANES, body, 0)

    def wait_gather(sl):
        pltpu.make_async_copy(xc_hbm.at[pl.ds(0, tb // SUBLANES)], gbuf.at[sl], gsem.at[sl]).wait()

    def start_scatter(tile, sl):
        def body(i, carry):
            for u in range(SUBLANES):
                idx = sdst_ref[tile * tb + i * SUBLANES + u]

                @pl.when(idx >= 0)
                def _():
                    pltpu.make_async_copy(
                        obuf.at[sl, i, pl.ds(u, 1), :],
                        y_hbm.at[lax.shift_right_logical(idx, SUBLANES.bit_length() - 1), pl.ds(idx & (SUBLANES - 1), 1), :],
                        ssem.at[sl]).start()

                @pl.when(idx < 0)
                def _():
                    pltpu.make_async_copy(obuf.at[sl, i, pl.ds(u, 1), :], dump_hbm.at[sl, i, pl.ds(u, 1), :],
                                          ssem.at[sl]).start()

            return carry

        lax.fori_loop(0, tb // SUBLANES, body, 0)

    def wait_scatter(sl):
        pltpu.make_async_copy(obuf.at[sl], dump_hbm.at[sl], ssem.at[sl]).wait()

    def step(slot):
        @pl.when(q == 0)
        def _():
            start_gather(0, 0)

        @pl.when(q + 1 < n_used)
        def _():
            start_gather(q + 1, 1 - slot)

        wait_gather(slot)

        @pl.when(q >= 2)
        def _():
            wait_scatter(slot)

        rows = gbuf[slot].reshape(tb, gbuf.shape[-1])
        x1, comb = rows[:, 0:d], rows[:, d:d + LANES]
        shift2, scale2, gate2 = mod_ref[:, 3 * d:4 * d], mod_ref[:, 4 * d:5 * d], mod_ref[:, 5 * d:6 * d]
        h2 = (_rmsnorm(x1, g2_ref[...]) * (1.0 + scale2) + shift2).astype(BF16)
        first_lane = N_GROUPS + grp_ref[q] * EXPERTS_PER_GROUP
        lane = _iota(comb.shape, 1)
        acc = jnp.zeros((tb, d), F32)
        for e in range(EXPERTS_PER_GROUP):
            w_e = jnp.sum(jnp.where(lane == first_lane + e, comb, 0.0), axis=1, keepdims=True)
            a = _mm(h2, w1_ref[e])
            b = _mm(h2, w3_ref[e])
            acc = acc + _mm(_silu(a) * b * w_e, w2_ref[e])
        obuf[slot] = _rmsnorm(x1 + gate2 * acc, gf_ref[...]).reshape(tb // SUBLANES, SUBLANES, d)
        start_scatter(q, slot)

        @pl.when(q == n_used - 1)
        def _():
            @pl.when(q >= 1)
            def _():
                wait_scatter(1 - slot)

            wait_scatter(slot)

    for slot in range(2):
        @pl.when((q < n_used) & (lax.rem(q, 2) == slot))
        def _(slot=slot):
            step(slot)


def _moe(xc, mod, p, tb):
    s, l, width = xc.shape
    d = D_MODEL
    gsel = xc[:, :, d + GROUP_LANE].astype(jnp.int32)
    row_src, tile_seg, tile_grp, n_used = _route(gsel, tb)
    max_tiles = tile_seg.shape[0]
    grouped = lambda w: w.reshape((N_GROUPS, EXPERTS_PER_GROUP) + w.shape[1:])
    w1, w3, w2 = grouped(p["moe_w1"]), grouped(p["moe_w3"]), grouped(p["moe_w2"])
    const = lambda shape: pl.BlockSpec(shape, lambda q, *_: (0,) * len(shape))
    by_group = lambda w: pl.BlockSpec((None,) + w.shape[1:], lambda q, gs, sd, seg, grp, used: (grp[q], 0, 0, 0))
    hbm = pl.BlockSpec(memory_space=pl.ANY)
    y, _ = pl.pallas_call(
        functools.partial(_moe_kernel, tb=tb),
        grid_spec=pltpu.PrefetchScalarGridSpec(
            num_scalar_prefetch=5,
            grid=(max_tiles,),
            in_specs=[hbm, pl.BlockSpec((None, 1, 6 * d), lambda q, gs, sd, seg, grp, used: (seg[q], 0, 0)),
                      const((1, d)), by_group(w1), by_group(w3), by_group(w2), const((1, d)), hbm],
            out_specs=[hbm, hbm],
            scratch_shapes=[pltpu.VMEM((2, tb // SUBLANES, SUBLANES, width), F32),
                            pltpu.VMEM((2, tb // SUBLANES, SUBLANES, d), F32),
                            pltpu.SemaphoreType.DMA((2,)), pltpu.SemaphoreType.DMA((2,))]),
        out_shape=[jax.ShapeDtypeStruct((s * l // SUBLANES, SUBLANES, d), F32),
                   jax.ShapeDtypeStruct((2, tb // SUBLANES, SUBLANES, d), F32)],
        input_output_aliases={12: 1},
        compiler_params=_cparams(("arbitrary",)),
        name="moe",
    )(jnp.maximum(row_src, 0), row_src, tile_seg, tile_grp, n_used,
      xc.reshape(s * l // SUBLANES, SUBLANES, width), mod, p["norm2_g"], w1, w3, w2, p["final_norm_g"],
      jnp.zeros((2, tb // SUBLANES, SUBLANES, d), F32))
    return y.reshape(s, l, d)


def _trunk(x, mod_seg, states, conv2d, p, tm, tb, mixer_seqs, rwkv_packing):
    b, t, d = x.shape
    n_seg = mod_seg.shape[0]
    seg = lambda a: a.reshape(n_seg, b * t // n_seg, a.shape[-1])
    per_seq = lambda a: a.reshape(b, t, a.shape[-1])
    xa, ub, o_pre = (per_seq(a) for a in _in_proj(seg(x), mod_seg, p["norm1_g"], p["w_in"], tm))
    if states is None:
        s_rwkv, mlstm_states = None, None
    else:
        s_rwkv, s_c, s_n, s_m = states
        mlstm_states = (s_c, s_n, jnp.broadcast_to(s_m[..., None], s_m.shape + (LANES,)))
    ya, s_rwkv = _rwkv(xa, s_rwkv, p, *rwkv_packing)
    yb, s_c, s_n, s_m = _mlstm(ub, o_pre, mlstm_states, p, conv2d, mixer_seqs)
    xc = _merge(seg(x), seg(ya), seg(yb), mod_seg, p, tm)
    y = _moe(xc, mod_seg, p, tb)
    return y.reshape(b, t, d), s_rwkv, s_c, s_n, s_m[..., 0]


def kernel(x_prompt, x_sample, c, c_ctx, state_rwkv, state_mlstm_C, state_mlstm_n, state_mlstm_m, w_mod, b_mod, norm1_g, norm2_g, w_in, rwkv_mu, rwkv_w_r, rwkv_w_k, rwkv_w_v, rwkv_w0, rwkv_w1, rwkv_w2, rwkv_a0, rwkv_a1, rwkv_a2, rwkv_g1, rwkv_g2, rwkv_k_k, rwkv_k_a, rwkv_r_k, rwkv_ln_g, rwkv_ln_b, rwkv_w_proj, mlstm_conv, mlstm_w_q, mlstm_w_k, mlstm_w_v, mlstm_w_i, mlstm_b_i, mlstm_w_f, mlstm_b_f, mlstm_ln_g, mlstm_w_proj, w_out, moe_w_group, moe_b_group, moe_w_expert, moe_b_expert, moe_w1, moe_w3, moe_w2, final_norm_g):
    assert w_mod.shape[0] == 1, "single trunk layer"
    l = 0
    bp, dec = x_prompt.shape[0], x_sample.shape[0]
    bf = lambda w: w.astype(BF16)
    row = lambda w: w.reshape(1, -1).astype(F32)
    wg = jnp.concatenate([mlstm_w_i[l, 0], mlstm_w_i[l, 1], mlstm_w_f[l, 0], mlstm_w_f[l, 1]], axis=1)
    bg = jnp.concatenate([mlstm_b_i[l, 0], mlstm_b_i[l, 1], mlstm_b_f[l, 0], mlstm_b_f[l, 1]])
    wg = jnp.pad(wg, ((0, 0), (0, LANES - wg.shape[1])))
    bg = jnp.pad(bg, (0, LANES - bg.shape[0]))
    w_router = jnp.pad(jnp.concatenate([moe_w_group[l], moe_w_expert[l]], axis=1),
                       ((0, 0), (0, LANES - N_GROUPS - N_EXPERTS)))
    b_router = jnp.pad(jnp.concatenate([moe_b_group[l], moe_b_expert[l]]), (0, LANES - N_GROUPS - N_EXPERTS))
    p = {
        "norm1_g": row(norm1_g[l]), "norm2_g": row(norm2_g[l]), "w_in": w_in[l],
        "rwkv_mu": rwkv_mu[l], "rwkv_wr": bf(rwkv_w_r[l]), "rwkv_wk": bf(rwkv_w_k[l]), "rwkv_wv": bf(rwkv_w_v[l]),
        "rwkv_w0": rwkv_w0[l], "rwkv_w1": bf(rwkv_w1[l]), "rwkv_w2": bf(rwkv_w2[l]),
        "rwkv_a0": rwkv_a0[l], "rwkv_a1": bf(rwkv_a1[l]), "rwkv_a2": bf(rwkv_a2[l]),
        "rwkv_g1": bf(rwkv_g1[l]), "rwkv_g2": bf(rwkv_g2[l]),
        "rwkv_k_k": row(rwkv_k_k[l]), "rwkv_k_a": row(rwkv_k_a[l]), "rwkv_r_k": row(rwkv_r_k[l]),
        "rwkv_ln_g": row(rwkv_ln_g[l]), "rwkv_ln_b": row(rwkv_ln_b[l]),
        "rwkv_w_proj": rwkv_w_proj[l],
        "mlstm_conv": mlstm_conv[l].reshape(9, B_WIDTH),
        "mlstm_wq": bf(mlstm_w_q[l]), "mlstm_wk": bf(mlstm_w_k[l]), "mlstm_wv": bf(mlstm_w_v[l]),
        "mlstm_wg": bf(wg), "mlstm_bg": row(bg),
        "mlstm_wgt": bf(wg[:, :16].T), "mlstm_bgt": jnp.broadcast_to(bg[:16, None], (16, LANES)),
        "mlstm_ln_g": row(mlstm_ln_g[l]), "mlstm_w_proj": mlstm_w_proj[l], "w_out": w_out[l],
        "moe_w_router": bf(w_router), "moe_b_router": row(b_router),
        "moe_w1": moe_w1[l], "moe_w3": moe_w3[l], "moe_w2": moe_w2[l],
        "final_norm_g": row(final_norm_g),
    }
    cvec = jnp.concatenate([c_ctx[None, :], c, jnp.zeros((8 - 1 - dec, D_MODEL), F32)], axis=0)
    mod = _mod(cvec, w_mod[l], b_mod[l].reshape(1, -1))
    mod_ctx = mod[0:1][:, None, :]
    mod_lat = mod[1:1 + dec][:, None, :]

    yp, n_rwkv, n_c, n_n, n_m = _trunk(x_prompt, mod_ctx, None, False, p, 512, 512, 2, (2, 2))
    lat_states = tuple(s[:, l].astype(F32) for s in (state_rwkv, state_mlstm_C, state_mlstm_n, state_mlstm_m))
    ys, _, _, _, _ = _trunk(x_sample, mod_lat, lat_states, True, p, 512, 256, 1, (1, 2))
    dt = x_prompt.dtype
    return (yp, ys, n_rwkv[:, None].astype(dt), n_c[:, None].astype(dt), n_n[:, None].astype(dt),
            n_m[:, None].astype(dt))
```

```python
import functools

import jax
import jax.numpy as jnp
from jax import lax
from jax.experimental import pallas as pl
from jax.experimental.pallas import tpu as pltpu

F32 = jnp.float32
BF16 = jnp.bfloat16

D_MODEL = 1024
GRID_W = 64
A_HEAD_DIM = 64
A_WIDTH = 512
A_HEADS = 8
B_HEAD_DIM = 128
B_WIDTH = 512
B_HEADS = 4
CHUNK = 64
ROW_TILE = 512
SUM_TILE = 256
N_GROUPS = 4
EXPERTS_PER_GROUP = 4
N_EXPERTS = 16
D_EXPERT = 256
GROUP_LANE = N_GROUPS + N_EXPERTS
NORM_EPS = 1e-6
RWKV_GN_EPS = 64e-5
MLSTM_GN_EPS = 1e-5
LANES = 128
SUBLANES = 8
CONV_PAD = 72
VMEM_LIMIT = 56 * 1024 * 1024


def _mm(a, b):
    return jnp.dot(a.astype(BF16), b.astype(BF16), preferred_element_type=F32)


_NN, _NT, _TN = ((1,), (0,)), ((1,), (1,)), ((0,), (0,))


def _split(a):
    hi = a.astype(BF16)
    return hi, (a - hi.astype(F32)).astype(BF16)


def _split3(a):
    hi = a.astype(BF16)
    rest = a - hi.astype(F32)
    lo = rest.astype(BF16)
    return hi, lo, (rest - lo.astype(F32)).astype(BF16)


def _sum01(a, b, dims=_NN):
    dot = lambda p, q: lax.dot_general(p, q, (dims, ((), ())), preferred_element_type=F32)
    terms = [dot(x, b) for x in a] if isinstance(a, tuple) else [dot(a, x) for x in b]
    return functools.reduce(lambda p, q: p + q, terms)


def _mm3(a, b, dims=_NN):
    (ah, al), (bh, bl) = a, b
    dot = lambda p, q: lax.dot_general(p, q, (dims, ((), ())), preferred_element_type=F32)
    return dot(ah, bh) + dot(ah, bl) + dot(al, bh)


def _mm_nt(a, b):
    return lax.dot_general(a.astype(BF16), b.astype(BF16), (((1,), (1,)), ((), ())),
                           preferred_element_type=F32)


def _mm_tn(a, b):
    return lax.dot_general(a.astype(BF16), b.astype(BF16), (((0,), (0,)), ((), ())),
                           preferred_element_type=F32)


def _sigmoid(x):
    return 1.0 / (1.0 + jnp.exp(-x))


def _silu(x):
    return x * _sigmoid(x)


def _softplus(x):
    return jnp.maximum(x, 0.0) + jnp.log(1.0 + jnp.exp(-jnp.abs(x)))


def _iota(shape, dim):
    return lax.broadcasted_iota(jnp.int32, shape, dim)


def _tri(reverse, inclusive, n=CHUNK):
    row, col = _iota((n, n), 0), _iota((n, n), 1)
    if reverse:
        return (col >= row) if inclusive else (col > row)
    return (col <= row) if inclusive else (col < row)


def _same_block(n, blk):
    sh = blk.bit_length() - 1
    row, col = _iota((n, n), 0), _iota((n, n), 1)
    return lax.shift_right_logical(row, sh) == lax.shift_right_logical(col, sh)


def _rmsnorm(x, g):
    return x * lax.rsqrt(jnp.mean(x * x, axis=-1, keepdims=True) + NORM_EPS) * g


def _cparams(sem):
    return pltpu.CompilerParams(dimension_semantics=sem, vmem_limit_bytes=VMEM_LIMIT)


def _full(shape):
    nd = len(shape)
    return pl.BlockSpec(shape, lambda *_: (0,) * nd)


def _mod_kernel(c_ref, w_ref, b_ref, o_ref):
    o_ref[...] = _mm(_silu(c_ref[...]), w_ref[...]) + b_ref[...]


def _mod(cvec, w_mod, b_mod):
    rows, d = cvec.shape
    n = w_mod.shape[1]
    tn = 1536
    return pl.pallas_call(
        _mod_kernel,
        grid=(n // tn,),
        in_specs=[_full((rows, d)), pl.BlockSpec((d, tn), lambda j: (0, j)),
                  pl.BlockSpec((1, tn), lambda j: (0, j))],
        out_specs=pl.BlockSpec((rows, tn), lambda j: (0, j)),
        out_shape=jax.ShapeDtypeStruct((rows, n), F32),
        compiler_params=_cparams(("arbitrary",)),
        name="mod",
    )(cvec, w_mod, b_mod)


W_IN_BLOCK = 512


def _modulated_norm1(x_ref, mod_ref, g_ref):
    d = D_MODEL
    shift, scale = mod_ref[:, 0:d], mod_ref[:, d:2 * d]
    return (_rmsnorm(x_ref[...], g_ref[...]) * (1.0 + scale) + shift).astype(BF16)


def _w_in_blocks(w_in, first, count):
    specs = [pl.BlockSpec((w_in.shape[0], W_IN_BLOCK), lambda *_, j=first + k: (0, j)) for k in range(count)]
    return specs, [w_in] * count


def _in_kernel(x_ref, mod_ref, g_ref, wa_ref, wu_ref, wo_ref, xa_ref, ub_ref, op_ref):
    h = _modulated_norm1(x_ref, mod_ref, g_ref)
    for ref, w_ref in ((xa_ref, wa_ref), (ub_ref, wu_ref), (op_ref, wo_ref)):
        ref[...] = _mm(h, w_ref[...])


def _in_proj(x, mod, g1, w_in, tm):
    b, t, d = x.shape
    per = t // tm
    widths = (A_WIDTH, B_WIDTH, B_WIDTH)
    assert all(n == W_IN_BLOCK for n in widths)
    tok = lambda n: pl.BlockSpec((None, tm, n), lambda i: (i // per, i % per, 0))
    w_specs, w_args = _w_in_blocks(w_in, 0, 3)
    return pl.pallas_call(
        _in_kernel,
        grid=(b * per,),
        in_specs=[tok(d), pl.BlockSpec((None, 1, 6 * d), lambda i: (i // per, 0, 0)), _full((1, d))] + w_specs,
        out_specs=[tok(n) for n in widths],
        out_shape=[jax.ShapeDtypeStruct((b, t, n), F32) for n in widths],
        compiler_params=_cparams(("arbitrary",)),
        name="in_proj",
    )(x, mod, g1, *w_args)


def _head_of_lane(shape):
    return lax.shift_right_logical(_iota(shape, 1), A_HEAD_DIM.bit_length() - 1)


def _pair_diag(x):
    head = _head_of_lane(x.shape)
    return jnp.concatenate([x * (head == h).astype(x.dtype) for h in range(x.shape[1] // A_HEAD_DIM)], axis=0)


def _pair_mm3(lhs, rhs, dims=_NN):
    rows = lhs[0].shape[0]
    if dims == _NT:
        wh, wl = _split(_pair_diag(rhs).T)
    else:
        rh, rl = _split(rhs)
        wh, wl = _pair_diag(rh), _pair_diag(rl)
    parts = [_split(l) for l in lhs]
    top = jnp.dot(jnp.concatenate([t for part in parts for t in part], axis=0), wh, preferred_element_type=F32)
    low = jnp.dot(jnp.concatenate([part[0] for part in parts], axis=0), wl, preferred_element_type=F32)
    return [top[2 * k * rows:(2 * k + 1) * rows] + top[(2 * k + 1) * rows:(2 * k + 2) * rows]
            + low[k * rows:(k + 1) * rows] for k in range(len(lhs))]


def _pair_index(shape):
    return _iota(shape, 0), _iota(shape, 1) & (A_HEAD_DIM - 1)


def _pair_tri_inverses(mats):
    t_idx, s_idx = _pair_index(mats[0].shape)
    same = lambda blk: (lax.shift_right_logical(t_idx, blk.bit_length() - 1)
                        == lax.shift_right_logical(s_idx, blk.bit_length() - 1))
    eye = (t_idx == s_idx).astype(F32)
    nd = [jnp.where(same(8), m, 0.0) for m in mats]
    n2 = [_pair_mm3([x], x)[0] for x in nd]
    t = [eye + x for x in nd]
    both = [_pair_mm3([a, b], a) for a, b in zip(n2, t)]
    t = [x + r[1] for x, r in zip(t, both)]
    t = [x + _pair_mm3([x], r[0])[0] for x, r in zip(t, both)]
    blk = 8
    while blk < CHUNK:
        sel = same(2 * blk) & jnp.logical_not(same(blk))
        w = [_pair_mm3([jnp.where(sel, m, 0.0)], x)[0] for m, x in zip(mats, t)]
        t = [x + _pair_mm3([x], y)[0] for x, y in zip(t, w)]
        blk *= 2
    return t


def _rwkv_kernel(x_ref, s0_ref, mu_ref, wr_ref, wk_ref, wv_ref, w0_ref, w1_ref, w2_ref,
                 a0_ref, a1_ref, a2_ref, g1_ref, g2_ref, kk_ref, ka_ref, rk_ref, lng_ref, lnb_ref,
                 y_ref, s_ref,
                 r_s, v_s, kk_s, cum_s, cx_s, kd_s, bb_s, g_s, bonus_s, y_s, sp_s, *, seq_len):
    total = x_ref.shape[0]
    n_seq = total // seq_len
    n_chunks = seq_len // CHUNK
    n_tiles = total // ROW_TILE
    hd = A_HEAD_DIM
    n_groups, width = sp_s.shape[2], sp_s.shape[4]
    per_group = width // hd
    x = x_ref[...]
    row = _iota(x.shape, 0) & (seq_len - 1)
    prev = jnp.where(row == 0, 0.0, pltpu.roll(x, 1, 0))
    nxt = jnp.where(row == seq_len - 1, 0.0, pltpu.roll(x, total - 1, 0))
    y_s[...] = 0.5 * (prev + nxt) - x
    head_ones = _same_block(A_WIDTH, hd).astype(BF16)
    chunk_diag = _same_block(SUM_TILE, CHUNK)
    run_sum = [(chunk_diag & _tri(d == 1, True, SUM_TILE)).astype(BF16) for d in range(2)]

    def project_tile(ti, carry):
        rows = pl.ds(pl.multiple_of(ti * ROW_TILE, ROW_TILE), ROW_TILE)
        x, xx = x_ref[rows, :], y_s[rows, :]
        xr, xw, xk, xv, xi, xg = (x + xx * mu_ref[j:j + 1, :] for j in range(6))
        r = _mm(xr, wr_ref[...])
        k = _mm(xk, wk_ref[...])
        v = _mm(xv, wv_ref[...])
        g_s[rows, :] = _mm(_sigmoid(_mm(xg, g1_ref[...])), g2_ref[...])
        kk = k * kk_ref[...]
        kk = kk / jnp.maximum(jnp.sqrt(_sum01(_split(kk * kk), head_ones)), 1e-12)
        r_s[rows, :] = r
        v_s[rows, :] = v
        kk_s[rows, :] = kk
        kd_sum = jnp.zeros_like(x)
        for d in range(2):
            zw = w0_ref[d:d + 1, :] + _mm(jnp.tanh(_mm(xw, w1_ref[d])), w2_ref[d])
            lw = -jnp.exp(-_softplus(-zw) - 0.5)
            cum = jnp.concatenate([_sum01(run_sum[d], _split3(lw[j:j + SUM_TILE]))
                                   for j in range(0, ROW_TILE, SUM_TILE)], axis=0)
            cum_s[d, rows, :] = cum
            cx_s[d, rows, :] = cum - lw
            a = _sigmoid(a0_ref[d:d + 1, :] + _mm(_mm(xi, a1_ref[d]), a2_ref[d]))
            kd = k * (1.0 + (a - 1.0) * ka_ref[...])
            kd_s[d, rows, :] = kd
            bb_s[d, rows, :] = kk * a
            kd_sum = kd_sum + kd
        bonus_s[rows, :] = _sum01(_split(r * kd_sum * rk_ref[...]), head_ones) * v
        y_s[rows, :] = jnp.zeros_like(x)
        return carry

    lax.fori_loop(0, n_tiles, project_tile, 0)
    for n in range(n_seq):
        for d in range(2):
            for p in range(n_groups):
                sp_s[n, d, p] = (jnp.zeros((hd, width), F32) if s0_ref is None else jnp.concatenate(
                    [s0_ref[n, d, per_group * p + j] for j in range(per_group)], axis=1))

    t_idx, s_idx = _pair_index((CHUNK, width))
    strict = [s_idx < t_idx, s_idx > t_idx]
    incl = [s_idx <= t_idx, s_idx >= t_idx]
    lane_head = _head_of_lane((CHUNK, width))

    def chunk_body(ci, carry):
        chains = []
        for n, d in [(n, d) for n in range(n_seq) for d in range(2)]:
            cpos = n * n_chunks + ((n_chunks - 1 - ci) if d == 1 else ci)
            last = 0 if d == 1 else CHUNK - 1
            rows = pl.ds(pl.multiple_of(cpos * CHUNK, CHUNK), CHUNK)
            cum, cx = cum_s[d, rows, :], cx_s[d, rows, :]
            tot = cum[last:last + 1, :]
            kdc, bbc = kd_s[d, rows, :], bb_s[d, rows, :]
            e_neg = jnp.exp(-cum)
            e_tail = jnp.exp(tot - cum)
            at = -kk_s[rows, :] * jnp.exp(cx)
            rt = r_s[rows, :] * jnp.exp(cum)
            bt, kt = bbc * e_neg, kdc * e_neg
            bp, kp = bbc * e_tail, kdc * e_tail
            pc = jnp.exp(tot)
            vc = v_s[rows, :]
            for p in range(n_groups):
                ln = slice(p * width, (p + 1) * width)
                chains.append(dict(d=d, st=(n, d, p), rows=rows, ln=ln, v=vc[:, ln], pc=pc[:, ln],
                                   lhs=jnp.concatenate([at[:, ln], rt[:, ln]], axis=0),
                                   bt=bt[:, ln], kt=kt[:, ln], bp=bp[:, ln], kp=kp[:, ln]))
        for c in chains:
            c["sb"] = _pair_mm3([c["lhs"]], c["bt"], _NT)[0]
            c["sk"] = _pair_mm3([c["lhs"]], c["kt"], _NT)[0]
            c["s"] = sp_s[c["st"]]
            c["fs"] = _mm(c["lhs"], _pair_diag(c["s"]).T)
            c["v_diag"] = _pair_diag(c["v"].astype(BF16))
        for c in chains:
            a_ak = jnp.where(strict[c["d"]], c["sk"][:CHUNK], 0.0)
            c["x"] = c["fs"][:CHUNK] + _mm(a_ak, c["v_diag"])
        inv = _pair_tri_inverses([jnp.where(strict[c["d"]], c["sb"][:CHUNK], 0.0) for c in chains])
        for c, t in zip(chains, inv):
            c["u"] = _pair_mm3([t], c["x"])[0]
        for c in chains:
            d, rows, ln, u = c["d"], c["rows"], c["ln"], c["u"]
            a_r = jnp.concatenate([jnp.where(incl[d], c["sb"][CHUNK:], 0.0),
                                   jnp.where(incl[d], c["sk"][CHUNK:], 0.0)], axis=1)
            uv_diag = jnp.concatenate([_pair_diag(u.astype(BF16)), c["v_diag"]], axis=0)
            y_s[rows, ln] = y_s[rows, ln] + c["fs"][CHUNK:] + _mm(a_r, uv_diag)
            g = _mm_tn(jnp.concatenate([u, c["v"]], axis=0), jnp.concatenate([c["bp"], c["kp"]], axis=0))
            own = g[0:hd]
            for j in range(1, per_group):
                own = jnp.where(lane_head == j, g[j * hd:(j + 1) * hd], own)
            sp_s[c["st"]] = c["s"] * c["pc"] + own
        return carry

    lax.fori_loop(0, n_chunks, chunk_body, 0)
    for n in range(n_seq):
        for d in range(2):
            for p in range(n_groups):
                for j in range(per_group):
                    s_ref[n, d, per_group * p + j] = sp_s[n, d, p, :, j * hd:(j + 1) * hd]

    def finish_tile(ti, carry):
        rows = pl.ds(pl.multiple_of(ti * ROW_TILE, ROW_TILE), ROW_TILE)
        ys = y_s[rows, :]
        cen = ys - _sum01(_split(ys), head_ones) * (1.0 / hd)
        var = _sum01(_split(cen * cen), head_ones) * (1.0 / hd)
        yn = cen * lax.rsqrt(var + RWKV_GN_EPS) * lng_ref[...] + lnb_ref[...]
        y_ref[rows, :] = (yn + bonus_s[rows, :]) * g_s[rows, :]
        return carry

    lax.fori_loop(0, n_tiles, finish_tile, 0)


def _rwkv(xa, s0, p, n_seq, heads_per_block):
    b, t, w = xa.shape
    assert t & (t - 1) == 0 and b % n_seq == 0
    names = ("mu", "wr", "wk", "wv", "w0", "w1", "w2", "a0", "a1", "a2", "g1", "g2",
             "k_k", "k_a", "r_k", "ln_g", "ln_b")
    weights = [p["rwkv_" + n] for n in names]
    rows = n_seq * t
    seq = pl.BlockSpec((None, rows, w), lambda i: (i, 0, 0))
    st = pl.BlockSpec((n_seq, 2, A_HEADS, A_HEAD_DIM, A_HEAD_DIM), lambda i: (i, 0, 0, 0, 0))
    tw = lambda lead=(): pltpu.VMEM(lead + (rows, w), F32)
    kernel = functools.partial(_rwkv_kernel, seq_len=t)
    if s0 is None:
        kernel, state_specs, states = _zero_state_kernel(kernel, 1, 1), [], ()
    else:
        state_specs, states = [st], (s0,)
    y, s_out = pl.pallas_call(
        kernel,
        grid=(b // n_seq,),
        in_specs=[seq] + state_specs + [_full(x.shape) for x in weights],
        out_specs=[seq, st],
        out_shape=[jax.ShapeDtypeStruct((b // n_seq, rows, w), F32),
                   jax.ShapeDtypeStruct((b, 2, A_HEADS, A_HEAD_DIM, A_HEAD_DIM), F32)],
        scratch_shapes=[tw(), tw(), tw(), tw((2,)), tw((2,)), tw((2,)), tw((2,)), tw(), tw(), tw(),
                        pltpu.VMEM((n_seq, 2, A_HEADS // heads_per_block, A_HEAD_DIM,
                                    heads_per_block * A_HEAD_DIM), F32)],
        compiler_params=_cparams(("arbitrary",)),
        name="rwkv",
    )(xa.reshape(b // n_seq, rows, w), *states, *weights)
    return y.reshape(b, t, w), s_out


def _mlstm_kernel(u_ref, op_ref, c0_ref, n0_ref, m0_ref, conv_ref, wq_ref, wk_ref, wv_ref,
                  wg_ref, bg_ref, wgt_ref, bgt_ref, lng_ref,
                  y_ref, c_ref, n_ref, m_ref,
                  pad_s, q_s, k_s, v_s, dd_s, cr_s, h_s, *, conv2d, seq_len):
    t_len = u_ref.shape[0]
    n_seq = t_len // seq_len
    n_chunks = seq_len // CHUNK
    hd = B_HEAD_DIM
    u = u_ref[...]
    pad_s[0:CONV_PAD, :] = jnp.zeros((CONV_PAD, B_WIDTH), F32)
    pad_s[CONV_PAD + t_len:CONV_PAD + t_len + CONV_PAD, :] = jnp.zeros((CONV_PAD, B_WIDTH), F32)
    pad_s[CONV_PAD:CONV_PAD + t_len, :] = u
    pos = _iota(u.shape, 0) & (seq_len - 1)
    col = pos & (GRID_W - 1)
    conv = jnp.zeros_like(u)
    for kh in range(3):
        if not conv2d and kh != 1:
            continue
        for kw in range(3):
            shift = (kh - 1) * GRID_W + (kw - 1)
            term = pad_s[CONV_PAD + shift:CONV_PAD + shift + t_len, :] * conv_ref[kh * 3 + kw:kh * 3 + kw + 1, :]
            if conv2d and kw != 1:
                src = col + (kw - 1)
                term = jnp.where((src >= 0) & (src < GRID_W), term, 0.0)
            if n_seq > 1 and shift != 0:
                term = jnp.where((pos + shift >= 0) & (pos + shift < seq_len), term, 0.0)
            conv = conv + term
    uc = _silu(conv)
    for h in range(B_HEADS):
        ln = slice(h * hd, (h + 1) * hd)
        q_s[:, ln] = _mm(uc[:, ln], wq_ref[h])
        k_s[:, ln] = _mm(uc[:, ln], wk_ref[h]) * (hd ** -0.5)
        v_s[:, ln] = _mm(u[:, ln], wv_ref[h])
    gcol = _mm(uc, wg_ref[...]) + bg_ref[...]
    gcol = jnp.where(_iota(gcol.shape, 1) >= 8, -_softplus(-gcol), gcol)
    grow = _mm_nt(wgt_ref[...], uc) + bgt_ref[:, 0:1]
    grow = jnp.where(_iota(grow.shape, 0) >= 8, -_softplus(-grow), grow)
    chunk_diag = _same_block(SUM_TILE, CHUNK)
    run_sum = [(chunk_diag & _tri(d == 1, True, SUM_TILE)).astype(BF16) for d in range(2)]
    per_tile = SUM_TILE // CHUNK
    for ti in range(t_len // SUM_TILE):
        sl = slice(ti * SUM_TILE, (ti + 1) * SUM_TILE)
        gt, rt = _split3(gcol[sl, :]), _split3(grow[:, sl])
        backward_lane = (_iota((SUM_TILE, LANES), 1) & 4) != 0
        cum_c = jnp.where(backward_lane, _sum01(run_sum[1], gt, _NN), _sum01(run_sum[0], gt, _NN))
        dd_s[sl, :] = pltpu.roll(cum_c, LANES - 8, 1) - gcol[sl, :]
        backward_row = (_iota((16, SUM_TILE), 0) & 4) != 0
        cum_r = jnp.where(backward_row, _sum01(rt, run_sum[1], _NT), _sum01(rt, run_sum[0], _NT))
        for c in range(per_tile):
            cr_s[ti * per_tile + c] = cum_r[:, c * CHUNK:(c + 1) * CHUNK]
    h_s[...] = jnp.zeros_like(u)
    for ref, init in ((c_ref, c0_ref), (n_ref, n0_ref), (m_ref, m0_ref)):
        ref[...] = jnp.zeros(ref.shape, F32) if init is None else init[...]
    row_i, col_i = _iota((CHUNK, CHUNK), 0), _iota((CHUNK, CHUNK), 1)
    incl_t = [row_i <= col_i, row_i >= col_i]
    eye = row_i == col_i
    pick = [(_iota((LANES, B_HEADS * LANES), 0)
             == d * B_HEADS + lax.shift_right_logical(_iota((LANES, B_HEADS * LANES), 1), LANES.bit_length() - 1)
             ).astype(BF16) for d in range(2)]
    spread = (lax.shift_right_logical(_iota((2 * CHUNK, 2 * LANES), 0), CHUNK.bit_length() - 1)
              == lax.shift_right_logical(_iota((2 * CHUNK, 2 * LANES), 1), LANES.bit_length() - 1)).astype(BF16)
    ones_bf = jnp.ones((CHUNK, LANES), BF16)

    def chunk_body(ci, carry):
        chains = []
        for n, d in [(n, d) for n in range(n_seq) for d in range(2)]:
            cpos = n * n_chunks + ((n_chunks - 1 - ci) if d == 1 else ci)
            last = 0 if d == 1 else CHUNK - 1
            rows = pl.ds(pl.multiple_of(cpos * CHUNK, CHUNK), CHUNK)
            cr = cr_s[cpos]
            d_col = _sum01(_split3(dd_s[rows, :]), pick[d])
            for h in range(B_HEADS):
                ln = slice(h * hd, (h + 1) * hd)
                lf = 8 + d * 4 + h
                chains.append(dict(d=d, st=(n, d, h), vec=(n, d, slice(h, h + 1)), rows=rows, ln=ln, last=last,
                                   b_row=cr[lf:lf + 1, :], d_col=d_col[:, ln],
                                   m=m_ref[n, d, h:h + 1, 0:1], q=q_s[rows, ln], k=k_s[rows, ln], v=v_s[rows, ln],
                                   c=c_ref[n, d, h], n=n_ref[n, d, h:h + 1, :]))
        for c in chains:
            c["kq"] = _mm(c["k"], c["q"].T)
            c["qc"] = _mm(c["q"], c["c"])
            c["qn"] = _mm(c["q"], jnp.broadcast_to(c["n"], (LANES, hd)).T)
            logw = jnp.where(incl_t[c["d"]], c["b_row"] - c["d_col"][:, 0:CHUNK], -jnp.inf)
            c["m_row"] = jnp.maximum(c["b_row"] + c["m"], jnp.max(logw, axis=0, keepdims=True))
            c["logw"] = logw
        for c in chains:
            inter_row = jnp.exp(c["b_row"] + c["m"] - c["m_row"])
            diag = jnp.concatenate([jnp.where(eye, c["m_row"], 0.0), jnp.where(eye, inter_row, 0.0)], axis=1)
            cols = _sum01(_split3(diag), spread)
            c["m_col"], c["inter"] = cols[:, 0:LANES], cols[:, LANES:2 * LANES]
            c["s"] = c["kq"] * jnp.exp(c["logw"] - c["m_row"])
        for c in chains:
            s_hi, s_lo = _split(c["s"])
            sv = lax.dot_general(s_hi, jnp.concatenate([c["v"].astype(BF16), ones_bf], axis=1),
                                 (_TN, ((), ())), preferred_element_type=F32)
            c["sv"] = sv[:, 0:hd]
            c["den"] = (c["inter"] * c["qn"] + sv[:, hd:hd + LANES]
                        + lax.dot_general(s_lo, ones_bf, (_TN, ((), ())), preferred_element_type=F32))
            last = c["last"]
            c["m_new"] = c["m_row"][:, last:last + 1]
            b_last = c["b_row"][:, last:last + 1]
            c["decay"] = jnp.exp(b_last + c["m"] - c["m_new"])
            c["kw"] = c["k"] * jnp.exp(b_last - c["m_new"] - c["d_col"])
        for c in chains:
            c["kv"] = _mm_tn(c["kw"], c["v"])
        for c in chains:
            rows, ln = c["rows"], c["ln"]
            num = c["inter"] * c["qc"] + c["sv"]
            h_s[rows, ln] = h_s[rows, ln] + num / jnp.maximum(jnp.abs(c["den"]), jnp.exp(-c["m_col"]))
            c_ref[c["st"]] = c["decay"] * c["c"] + c["kv"]
            n_ref[c["vec"]] = c["decay"] * c["n"] + jnp.sum(c["kw"], axis=0, keepdims=True)
            m_ref[c["vec"]] = jnp.broadcast_to(c["m_new"], (1, LANES))
        return carry

    lax.fori_loop(0, n_chunks, chunk_body, 0)

    for h in range(B_HEADS):
        ln = slice(h * hd, (h + 1) * hd)
        hh = h_s[:, ln]
        cen = hh - jnp.mean(hh, axis=1, keepdims=True)
        var = jnp.mean(cen * cen, axis=1, keepdims=True)
        y_ref[:, ln] = _sigmoid(op_ref[:, ln]) * (cen * lax.rsqrt(var + MLSTM_GN_EPS) * lng_ref[:, ln])


def _zero_state_kernel(kernel, n_data, n_state):
    def wrapped(*refs, **kw):
        return kernel(*refs[:n_data], *([None] * n_state), *refs[n_data:], **kw)
    return wrapped


def _mlstm(ub, o_pre, states, p, conv2d, n_seq):
    b, t, w = ub.shape
    assert t & (t - 1) == 0 and b % n_seq == 0
    names = ("conv", "wq", "wk", "wv", "wg", "bg", "wgt", "bgt", "ln_g")
    weights = [p["mlstm_" + n] for n in names]
    rows, steps = n_seq * t, b // n_seq
    seq = pl.BlockSpec((None, rows, w), lambda i: (i, 0, 0))
    cst = pl.BlockSpec((n_seq, 2, B_HEADS, B_HEAD_DIM, B_HEAD_DIM), lambda i: (i, 0, 0, 0, 0))
    vst = pl.BlockSpec((n_seq, 2, B_HEADS, LANES), lambda i: (i, 0, 0, 0))
    kernel = functools.partial(_mlstm_kernel, conv2d=conv2d, seq_len=t)
    if states is None:
        kernel, state_specs, states = _zero_state_kernel(kernel, 2, 3), [], ()
    else:
        state_specs = [cst, vst, vst]
    y, c_out, n_out, m_out = pl.pallas_call(
        kernel,
        grid=(steps,),
        in_specs=[seq, seq] + state_specs + [_full(x.shape) for x in weights],
        out_specs=[seq, cst, vst, vst],
        out_shape=[jax.ShapeDtypeStruct((steps, rows, w), F32),
                   jax.ShapeDtypeStruct((b, 2, B_HEADS, B_HEAD_DIM, B_HEAD_DIM), F32),
                   jax.ShapeDtypeStruct((b, 2, B_HEADS, LANES), F32),
                   jax.ShapeDtypeStruct((b, 2, B_HEADS, LANES), F32)],
        scratch_shapes=[pltpu.VMEM((rows + 2 * CONV_PAD, w), F32), pltpu.VMEM((rows, w), F32),
                        pltpu.VMEM((rows, w), F32), pltpu.VMEM((rows, w), F32),
                        pltpu.VMEM((rows, LANES), F32), pltpu.VMEM((rows // CHUNK, 16, CHUNK), F32),
                        pltpu.VMEM((rows, w), F32)],
        compiler_params=_cparams(("arbitrary",)),
        name="mlstm",
    )(ub.reshape(steps, rows, w), o_pre.reshape(steps, rows, w), *states, *weights)
    return y.reshape(b, t, w), c_out, n_out, m_out


def _first_lane_of_max(val, lane, valid):
    masked = jnp.where(valid, val, -jnp.inf)
    best = jnp.max(masked, axis=1, keepdims=True)
    idx = jnp.min(jnp.where(valid & (masked == best), lane, LANES), axis=1, keepdims=True)
    return best, idx


def _merge_kernel(x_ref, ya_ref, yb_ref, mod_ref, g1_ref, wga0_ref, wga1_ref, wgb0_ref, wgb1_ref,
                  wpa_ref, wpb_ref, wo_ref, g2_ref, wr_ref, br_ref, xc_ref):
    d = D_MODEL
    gate1 = mod_ref[:, 2 * d:3 * d]
    shift2, scale2 = mod_ref[:, 3 * d:4 * d], mod_ref[:, 4 * d:5 * d]
    h1 = _modulated_norm1(x_ref, mod_ref, g1_ref)
    pa, pb = _mm(ya_ref[...], wpa_ref[...]), _mm(yb_ref[...], wpb_ref[...])
    halves = []
    for j, (wga_ref, wgb_ref) in enumerate(((wga0_ref, wgb0_ref), (wga1_ref, wgb1_ref))):
        cols = slice(j * W_IN_BLOCK, (j + 1) * W_IN_BLOCK)
        halves.append(_sigmoid(_mm(h1, wga_ref[...])) * pa[:, cols] + _sigmoid(_mm(h1, wgb_ref[...])) * pb[:, cols])
    merged = jnp.concatenate(halves, axis=1)
    x1 = x_ref[...] + gate1 * _mm(merged, wo_ref[...])
    xc_ref[:, 0:d] = x1
    h2 = _rmsnorm(x1, g2_ref[...]) * (1.0 + scale2) + shift2
    logits = _mm(h2, wr_ref[...]) + br_ref[...]
    lane = _iota(logits.shape, 1)
    is_group = lane < N_GROUPS
    g_max, g_sel = _first_lane_of_max(logits, lane, is_group)
    g_w = 1.0 / jnp.sum(jnp.where(is_group, jnp.exp(logits - g_max), 0.0), axis=1, keepdims=True)
    expert = lane - N_GROUPS
    in_group = (expert >= 0) & (expert < N_EXPERTS) & (lax.shift_right_arithmetic(expert, EXPERTS_PER_GROUP.bit_length() - 1) == g_sel)
    e_max = jnp.max(jnp.where(in_group, logits, -jnp.inf), axis=1, keepdims=True)
    e_exp = jnp.where(in_group, jnp.exp(logits - e_max), 0.0)
    prob = e_exp / jnp.sum(e_exp, axis=1, keepdims=True)
    p1, i1 = _first_lane_of_max(prob, lane, in_group)
    p2, i2 = _first_lane_of_max(prob, lane, in_group & (lane != i1))
    denom = p1 + p2
    comb = jnp.where(lane == i1, g_w * p1 / denom, 0.0) + jnp.where(lane == i2, g_w * p2 / denom, 0.0)
    xc_ref[:, d:d + LANES] = jnp.where(lane == GROUP_LANE, g_sel.astype(F32), comb)


def _merge(x, ya, yb, mod, p, tm):
    b, t, d = x.shape
    per = t // tm
    assert d == 2 * W_IN_BLOCK
    tok = lambda n: pl.BlockSpec((None, tm, n), lambda i: (i // per, i % per, 0))
    gate_specs, gate_args = _w_in_blocks(p["w_in"], 3, 4)
    weights = [p["rwkv_w_proj"], p["mlstm_w_proj"], p["w_out"], p["norm2_g"], p["moe_w_router"], p["moe_b_router"]]
    return pl.pallas_call(
        _merge_kernel,
        grid=(b * per,),
        in_specs=[tok(d), tok(A_WIDTH), tok(B_WIDTH), pl.BlockSpec((None, 1, 6 * d), lambda i: (i // per, 0, 0)),
                  _full((1, d))] + gate_specs + [_full(w.shape) for w in weights],
        out_specs=tok(d + LANES),
        out_shape=jax.ShapeDtypeStruct((b, t, d + LANES), F32),
        compiler_params=_cparams(("arbitrary",)),
        name="merge",
    )(x, ya, yb, mod, p["norm1_g"], *gate_args, *weights)


def _route(gsel, tb):
    s, l = gsel.shape
    n_buckets = s * N_GROUPS
    max_tiles = s * (l // tb + N_GROUPS - 1)
    onehot = (gsel[..., None] == jnp.arange(N_GROUPS, dtype=jnp.int32)).astype(jnp.int32)
    rank = jnp.sum((jnp.cumsum(onehot, axis=1) - onehot) * onehot, axis=-1)
    n_tiles = ((jnp.sum(onehot, axis=1) + tb - 1) // tb).reshape(n_buckets)
    ends = jnp.cumsum(n_tiles)
    first_tile = jnp.sum(onehot * (ends - n_tiles).reshape(s, 1, N_GROUPS), axis=-1)
    pos = first_tile * tb + rank
    tok = jnp.arange(s * l, dtype=jnp.int32)
    row_src = jnp.full((max_tiles * tb,), -1, jnp.int32).at[pos.reshape(-1)].set(tok)
    tile_bucket = jnp.sum(jnp.arange(max_tiles, dtype=jnp.int32)[:, None] >= ends[None, :], axis=1)
    tile_bucket = jnp.minimum(tile_bucket, n_buckets - 1).astype(jnp.int32)
    return row_src, tile_bucket // N_GROUPS, tile_bucket % N_GROUPS, ends[-1:].astype(jnp.int32)


def _moe_kernel(gsrc_ref, sdst_ref, seg_ref, grp_ref, used_ref,
                xc_hbm, mod_ref, g2_ref, w1_ref, w3_ref, w2_ref, gf_ref, dump_in_hbm,
                y_hbm, dump_hbm, gbuf, obuf, gsem, ssem, *, tb):
    d = D_MODEL
    q = pl.program_id(0)
    n_used = used_ref[0]

    def start_gather(tile, sl):
        def body(i, carry):
            for u in range(SUBLANES):
                idx = gsrc_ref[tile * tb + i * SUBLANES + u]
                pltpu.make_async_copy(xc_hbm.at[lax.shift_right_logical(idx, SUBLANES.bit_length() - 1), pl.ds(idx & (SUBLANES - 1), 1), :],
                                      gbuf.at[sl, i, pl.ds(u, 1), :], gsem.at[sl]).start()
            return carry

        lax.fori_loop(0, tb // SUBLANES, body, 0)

    def wait_gather(sl):
        pltpu.make_async_copy(xc_hbm.at[pl.ds(0, tb // SUBLANES)], gbuf.at[sl], gsem.at[sl]).wait()

    def start_scatter(tile, sl):
        def body(i, carry):
            for u in range(SUBLANES):
                idx = sdst_ref[tile * tb + i * SUBLANES + u]

                @pl.when(idx >= 0)
                def _():
                    pltpu.make_async_copy(
                        obuf.at[sl, i, pl.ds(u, 1), :],
                        y_hbm.at[lax.shift_right_logical(idx, SUBLANES.bit_length() - 1), pl.ds(idx & (SUBLANES - 1), 1), :],
                        ssem.at[sl]).start()

                @pl.when(idx < 0)
                def _():
                    pltpu.make_async_copy(obuf.at[sl, i, pl.ds(u, 1), :], dump_hbm.at[sl, i, pl.ds(u, 1), :],
                                          ssem.at[sl]).start()

            return carry

        lax.fori_loop(0, tb // SUBLANES, body, 0)

    def wait_scatter(sl):
        pltpu.make_async_copy(obuf.at[sl], dump_hbm.at[sl], ssem.at[sl]).wait()

    def gather_rows(tile, sl, lo, hi):
        for r in range(lo, hi):
            idx = gsrc_ref[tile * tb + r]
            pltpu.make_async_copy(xc_hbm.at[lax.shift_right_logical(idx, SUBLANES.bit_length() - 1),
                                            pl.ds(idx & (SUBLANES - 1), 1), :],
                                  gbuf.at[sl, r // SUBLANES, pl.ds(r % SUBLANES, 1), :], gsem.at[sl]).start()

    def scatter_rows(tile, sl, lo, hi, enabled):
        for r in range(lo, hi):
            idx = sdst_ref[tile * tb + r]
            src = obuf.at[sl, r // SUBLANES, pl.ds(r % SUBLANES, 1), :]

            @pl.when(enabled & (idx >= 0))
            def _():
                pltpu.make_async_copy(src, y_hbm.at[lax.shift_right_logical(idx, SUBLANES.bit_length() - 1),
                                                    pl.ds(idx & (SUBLANES - 1), 1), :], ssem.at[sl]).start()

            @pl.when(enabled & (idx < 0))
            def _():
                pltpu.make_async_copy(src, dump_hbm.at[sl, r // SUBLANES, pl.ds(r % SUBLANES, 1), :],
                                      ssem.at[sl]).start()

    def step(slot):
        @pl.when(q == 0)
        def _():
            start_gather(0, 0)

        wait_gather(slot)

        @pl.when(q >= 2)
        def _():
            wait_scatter(slot)

        nxt = jnp.minimum(q + 1, n_used - 1)
        prev, has_prev = jnp.maximum(q - 1, 0), q >= 1
        part = tb // EXPERTS_PER_GROUP
        rows = gbuf[slot].reshape(tb, gbuf.shape[-1])
        x1, comb = rows[:, 0:d], rows[:, d:d + LANES]
        shift2, scale2, gate2 = mod_ref[:, 3 * d:4 * d], mod_ref[:, 4 * d:5 * d], mod_ref[:, 5 * d:6 * d]
        h2 = (_rmsnorm(x1, g2_ref[...]) * (1.0 + scale2) + shift2).astype(BF16)
        first_lane = N_GROUPS + grp_ref[q] * EXPERTS_PER_GROUP
        lane = _iota(comb.shape, 1)
        acc = jnp.zeros((tb, d), F32)
        for e in range(EXPERTS_PER_GROUP):
            w_e = jnp.sum(jnp.where(lane == first_lane + e, comb, 0.0), axis=1, keepdims=True)
            a = _mm(h2, w1_ref[e])
            b = _mm(h2, w3_ref[e])
            acc = acc + _mm(_silu(a) * b * w_e, w2_ref[e])
            gather_rows(nxt, 1 - slot, e * part, (e + 1) * part)
            scatter_rows(prev, 1 - slot, e * part, (e + 1) * part, has_prev)
        obuf[slot] = _rmsnorm(x1 + gate2 * acc, gf_ref[...]).reshape(tb // SUBLANES, SUBLANES, d)

        @pl.when(q == n_used - 1)
        def _():
            start_scatter(q, slot)

            @pl.when(q >= 1)
            def _():
                wait_scatter(1 - slot)

            wait_scatter(slot)
            wait_gather(1 - slot)

    for slot in range(2):
        @pl.when((q < n_used) & (lax.rem(q, 2) == slot))
        def _(slot=slot):
            step(slot)


def _moe(xc, mod, p, tb):
    s, l, width = xc.shape
    d = D_MODEL
    gsel = xc[:, :, d + GROUP_LANE].astype(jnp.int32)
    row_src, tile_seg, tile_grp, n_used = _route(gsel, tb)
    max_tiles = tile_seg.shape[0]
    grouped = lambda w: w.reshape((N_GROUPS, EXPERTS_PER_GROUP) + w.shape[1:])
    w1, w3, w2 = grouped(p["moe_w1"]), grouped(p["moe_w3"]), grouped(p["moe_w2"])
    const = lambda shape: pl.BlockSpec(shape, lambda q, *_: (0,) * len(shape))
    by_group = lambda w: pl.BlockSpec((None,) + w.shape[1:], lambda q, gs, sd, seg, grp, used: (grp[q], 0, 0, 0))
    hbm = pl.BlockSpec(memory_space=pl.ANY)
    y, _ = pl.pallas_call(
        functools.partial(_moe_kernel, tb=tb),
        grid_spec=pltpu.PrefetchScalarGridSpec(
            num_scalar_prefetch=5,
            grid=(max_tiles,),
            in_specs=[hbm, pl.BlockSpec((None, 1, 6 * d), lambda q, gs, sd, seg, grp, used: (seg[q], 0, 0)),
                      const((1, d)), by_group(w1), by_group(w3), by_group(w2), const((1, d)), hbm],
            out_specs=[hbm, hbm],
            scratch_shapes=[pltpu.VMEM((2, tb // SUBLANES, SUBLANES, width), F32),
                            pltpu.VMEM((2, tb // SUBLANES, SUBLANES, d), F32),
                            pltpu.SemaphoreType.DMA((2,)), pltpu.SemaphoreType.DMA((2,))]),
        out_shape=[jax.ShapeDtypeStruct((s * l // SUBLANES, SUBLANES, d), F32),
                   jax.ShapeDtypeStruct((2, tb // SUBLANES, SUBLANES, d), F32)],
        input_output_aliases={12: 1},
        compiler_params=_cparams(("arbitrary",)),
        name="moe",
    )(jnp.maximum(row_src, 0), row_src, tile_seg, tile_grp, n_used,
      xc.reshape(s * l // SUBLANES, SUBLANES, width), mod, p["norm2_g"], w1, w3, w2, p["final_norm_g"],
      jnp.zeros((2, tb // SUBLANES, SUBLANES, d), F32))
    return y.reshape(s, l, d)


def _trunk(x, mod_seg, states, conv2d, p, tm, tb, mixer_seqs, rwkv_packing):
    b, t, d = x.shape
    n_seg = mod_seg.shape[0]
    seg = lambda a: a.reshape(n_seg, b * t // n_seg, a.shape[-1])
    per_seq = lambda a: a.reshape(b, t, a.shape[-1])
    xa, ub, o_pre = (per_seq(a) for a in _in_proj(seg(x), mod_seg, p["norm1_g"], p["w_in"], tm))
    if states is None:
        s_rwkv, mlstm_states = None, None
    else:
        s_rwkv, s_c, s_n, s_m = states
        mlstm_states = (s_c, s_n, jnp.broadcast_to(s_m[..., None], s_m.shape + (LANES,)))
    ya, s_rwkv = _rwkv(xa, s_rwkv, p, *rwkv_packing)
    yb, s_c, s_n, s_m = _mlstm(ub, o_pre, mlstm_states, p, conv2d, mixer_seqs)
    xc = _merge(seg(x), seg(ya), seg(yb), mod_seg, p, tm)
    y = _moe(xc, mod_seg, p, tb)
    return y.reshape(b, t, d), s_rwkv, s_c, s_n, s_m[..., 0]


def kernel(x_prompt, x_sample, c, c_ctx, state_rwkv, state_mlstm_C, state_mlstm_n, state_mlstm_m, w_mod, b_mod, norm1_g, norm2_g, w_in, rwkv_mu, rwkv_w_r, rwkv_w_k, rwkv_w_v, rwkv_w0, rwkv_w1, rwkv_w2, rwkv_a0, rwkv_a1, rwkv_a2, rwkv_g1, rwkv_g2, rwkv_k_k, rwkv_k_a, rwkv_r_k, rwkv_ln_g, rwkv_ln_b, rwkv_w_proj, mlstm_conv, mlstm_w_q, mlstm_w_k, mlstm_w_v, mlstm_w_i, mlstm_b_i, mlstm_w_f, mlstm_b_f, mlstm_ln_g, mlstm_w_proj, w_out, moe_w_group, moe_b_group, moe_w_expert, moe_b_expert, moe_w1, moe_w3, moe_w2, final_norm_g):
    assert w_mod.shape[0] == 1, "single trunk layer"
    l = 0
    bp, dec = x_prompt.shape[0], x_sample.shape[0]
    bf = lambda w: w.astype(BF16)
    row = lambda w: w.reshape(1, -1).astype(F32)
    wg = jnp.concatenate([mlstm_w_i[l, 0], mlstm_w_i[l, 1], mlstm_w_f[l, 0], mlstm_w_f[l, 1]], axis=1)
    bg = jnp.concatenate([mlstm_b_i[l, 0], mlstm_b_i[l, 1], mlstm_b_f[l, 0], mlstm_b_f[l, 1]])
    wg = jnp.pad(wg, ((0, 0), (0, LANES - wg.shape[1])))
    bg = jnp.pad(bg, (0, LANES - bg.shape[0]))
    w_router = jnp.pad(jnp.concatenate([moe_w_group[l], moe_w_expert[l]], axis=1),
                       ((0, 0), (0, LANES - N_GROUPS - N_EXPERTS)))
    b_router = jnp.pad(jnp.concatenate([moe_b_group[l], moe_b_expert[l]]), (0, LANES - N_GROUPS - N_EXPERTS))
    p = {
        "norm1_g": row(norm1_g[l]), "norm2_g": row(norm2_g[l]), "w_in": w_in[l],
        "rwkv_mu": rwkv_mu[l], "rwkv_wr": bf(rwkv_w_r[l]), "rwkv_wk": bf(rwkv_w_k[l]), "rwkv_wv": bf(rwkv_w_v[l]),
        "rwkv_w0": rwkv_w0[l], "rwkv_w1": bf(rwkv_w1[l]), "rwkv_w2": bf(rwkv_w2[l]),
        "rwkv_a0": rwkv_a0[l], "rwkv_a1": bf(rwkv_a1[l]), "rwkv_a2": bf(rwkv_a2[l]),
        "rwkv_g1": bf(rwkv_g1[l]), "rwkv_g2": bf(rwkv_g2[l]),
        "rwkv_k_k": row(rwkv_k_k[l]), "rwkv_k_a": row(rwkv_k_a[l]), "rwkv_r_k": row(rwkv_r_k[l]),
        "rwkv_ln_g": row(rwkv_ln_g[l]), "rwkv_ln_b": row(rwkv_ln_b[l]),
        "rwkv_w_proj": rwkv_w_proj[l],
        "mlstm_conv": mlstm_conv[l].reshape(9, B_WIDTH),
        "mlstm_wq": bf(mlstm_w_q[l]), "mlstm_wk": bf(mlstm_w_k[l]), "mlstm_wv": bf(mlstm_w_v[l]),
        "mlstm_wg": bf(wg), "mlstm_bg": row(bg),
        "mlstm_wgt": bf(wg[:, :16].T), "mlstm_bgt": jnp.broadcast_to(bg[:16, None], (16, LANES)),
        "mlstm_ln_g": row(mlstm_ln_g[l]), "mlstm_w_proj": mlstm_w_proj[l], "w_out": w_out[l],
        "moe_w_router": bf(w_router), "moe_b_router": row(b_router),
        "moe_w1": moe_w1[l], "moe_w3": moe_w3[l], "moe_w2": moe_w2[l],
        "final_norm_g": row(final_norm_g),
    }
    cvec = jnp.concatenate([c_ctx[None, :], c, jnp.zeros((8 - 1 - dec, D_MODEL), F32)], axis=0)
    mod = _mod(cvec, w_mod[l], b_mod[l].reshape(1, -1))
    mod_ctx = mod[0:1][:, None, :]
    mod_lat = mod[1:1 + dec][:, None, :]

    yp, n_rwkv, n_c, n_n, n_m = _trunk(x_prompt, mod_ctx, None, False, p, 512, 512, 2, (2, 2))
    lat_states = tuple(s[:, l].astype(F32) for s in (state_rwkv, state_mlstm_C, state_mlstm_n, state_mlstm_m))
    ys, _, _, _, _ = _trunk(x_sample, mod_lat, lat_states, True, p, 512, 256, 1, (1, 2))
    dt = x_prompt.dtype
    return (yp, ys, n_rwkv[:, None].astype(dt), n_c[:, None].astype(dt), n_n[:, None].astype(dt),
            n_m[:, None].astype(dt))
```

```python
import functools

import jax
import jax.numpy as jnp
from jax import lax
from jax.experimental import pallas as pl
from jax.experimental.pallas import tpu as pltpu

F32 = jnp.float32
BF16 = jnp.bfloat16

D_MODEL = 1024
GRID_W = 64
A_HEAD_DIM = 64
A_WIDTH = D_MODEL // 2
A_HEADS = A_WIDTH // A_HEAD_DIM
B_HEAD_DIM = 128
B_WIDTH = D_MODEL // 2
B_HEADS = B_WIDTH // B_HEAD_DIM
CHUNK = 64
SCAN_HEADS = 2
ROW_TILE = 512
SUM_TILE = 256
N_GROUPS = 4
EXPERTS_PER_GROUP = 4
N_EXPERTS = N_GROUPS * EXPERTS_PER_GROUP
GROUP_LANE = N_GROUPS + N_EXPERTS
NORM_EPS = 1e-6
RWKV_GN_EPS = 64e-5
MLSTM_GN_EPS = 1e-5
LANES = 128
SUBLANES = 8
CONV_PAD = 72
VMEM_LIMIT = 56 * 1024 * 1024


def _mm(a, b):
    return jnp.dot(a.astype(BF16), b.astype(BF16), preferred_element_type=F32)


_NN, _NT, _TN = ((1,), (0,)), ((1,), (1,)), ((0,), (0,))


def _split(a):
    hi = a.astype(BF16)
    return hi, (a - hi.astype(F32)).astype(BF16)


def _split3(a):
    hi = a.astype(BF16)
    rest = a - hi.astype(F32)
    lo = rest.astype(BF16)
    return hi, lo, (rest - lo.astype(F32)).astype(BF16)


def _sum01(a, b, dims=_NN):
    dot = lambda p, q: lax.dot_general(p, q, (dims, ((), ())), preferred_element_type=F32)
    terms = [dot(x, b) for x in a] if isinstance(a, tuple) else [dot(a, x) for x in b]
    return functools.reduce(lambda p, q: p + q, terms)


def _mm_nt(a, b):
    return lax.dot_general(a.astype(BF16), b.astype(BF16), (((1,), (1,)), ((), ())),
                           preferred_element_type=F32)


def _mm_tn(a, b):
    return lax.dot_general(a.astype(BF16), b.astype(BF16), (((0,), (0,)), ((), ())),
                           preferred_element_type=F32)


def _sigmoid(x):
    return 1.0 / (1.0 + jnp.exp(-x))


def _silu(x):
    return x * _sigmoid(x)


def _softplus(x):
    return jnp.maximum(x, 0.0) + jnp.log(1.0 + jnp.exp(-jnp.abs(x)))


def _iota(shape, dim):
    return lax.broadcasted_iota(jnp.int32, shape, dim)


def _tri(reverse, inclusive, n=CHUNK):
    row, col = _iota((n, n), 0), _iota((n, n), 1)
    if reverse:
        return (col >= row) if inclusive else (col > row)
    return (col <= row) if inclusive else (col < row)


def _same_block(n, blk):
    sh = blk.bit_length() - 1
    row, col = _iota((n, n), 0), _iota((n, n), 1)
    return lax.shift_right_logical(row, sh) == lax.shift_right_logical(col, sh)


def _rmsnorm(x, g):
    return x * lax.rsqrt(jnp.mean(x * x, axis=-1, keepdims=True) + NORM_EPS) * g


def _cparams(sem):
    return pltpu.CompilerParams(dimension_semantics=sem, vmem_limit_bytes=VMEM_LIMIT)


def _full(shape):
    nd = len(shape)
    return pl.BlockSpec(shape, lambda *_: (0,) * nd)


def _mod_kernel(c_ref, w_ref, b_ref, o_ref):
    o_ref[...] = _mm(_silu(c_ref[...]), w_ref[...]) + b_ref[...]


def _mod(cvec, w_mod, b_mod):
    rows, d = cvec.shape
    n = w_mod.shape[1]
    tn = 1536
    return pl.pallas_call(
        _mod_kernel,
        grid=(n // tn,),
        in_specs=[_full((rows, d)), pl.BlockSpec((d, tn), lambda j: (0, j)),
                  pl.BlockSpec((1, tn), lambda j: (0, j))],
        out_specs=pl.BlockSpec((rows, tn), lambda j: (0, j)),
        out_shape=jax.ShapeDtypeStruct((rows, n), F32),
        compiler_params=_cparams(("arbitrary",)),
        name="mod",
    )(cvec, w_mod, b_mod)


W_IN_BLOCK = 512


def _modulated_norm1(x_ref, mod_ref, g_ref):
    d = D_MODEL
    shift, scale = mod_ref[:, 0:d], mod_ref[:, d:2 * d]
    return (_rmsnorm(x_ref[...], g_ref[...]) * (1.0 + scale) + shift).astype(BF16)


def _w_in_blocks(w_in, first, count):
    specs = [pl.BlockSpec((w_in.shape[0], W_IN_BLOCK), lambda *_, j=first + k: (0, j)) for k in range(count)]
    return specs, [w_in] * count


def _in_kernel(x_ref, mod_ref, g_ref, wa_ref, wu_ref, wo_ref, xa_ref, ub_ref, op_ref):
    h = _modulated_norm1(x_ref, mod_ref, g_ref)
    for ref, w_ref in ((xa_ref, wa_ref), (ub_ref, wu_ref), (op_ref, wo_ref)):
        ref[...] = _mm(h, w_ref[...])


def _in_proj(x, mod, g1, w_in, tm):
    b, t, d = x.shape
    per = t // tm
    widths = (A_WIDTH, B_WIDTH, B_WIDTH)
    assert all(n == W_IN_BLOCK for n in widths)
    tok = lambda n: pl.BlockSpec((None, tm, n), lambda i: (i // per, i % per, 0))
    w_specs, w_args = _w_in_blocks(w_in, 0, 3)
    return pl.pallas_call(
        _in_kernel,
        grid=(b * per,),
        in_specs=[tok(d), pl.BlockSpec((None, 1, 6 * d), lambda i: (i // per, 0, 0)), _full((1, d))] + w_specs,
        out_specs=[tok(n) for n in widths],
        out_shape=[jax.ShapeDtypeStruct((b, t, n), F32) for n in widths],
        compiler_params=_cparams(("arbitrary",)),
        name="in_proj",
    )(x, mod, g1, *w_args)


def _head_of_lane(shape):
    return lax.shift_right_logical(_iota(shape, 1), A_HEAD_DIM.bit_length() - 1)


def _pair_diag(x):
    head = _head_of_lane(x.shape)
    return jnp.concatenate([x * (head == h).astype(x.dtype) for h in range(x.shape[1] // A_HEAD_DIM)], axis=0)


def _pair_mm3(lhs, rhs, dims=_NN):
    rows = lhs[0].shape[0]
    if dims == _NT:
        wh, wl = _split(_pair_diag(rhs).T)
    else:
        rh, rl = _split(rhs)
        wh, wl = _pair_diag(rh), _pair_diag(rl)
    parts = [_split(l) for l in lhs]
    top = jnp.dot(jnp.concatenate([t for part in parts for t in part], axis=0), wh, preferred_element_type=F32)
    low = jnp.dot(jnp.concatenate([part[0] for part in parts], axis=0), wl, preferred_element_type=F32)
    return [top[2 * k * rows:(2 * k + 1) * rows] + top[(2 * k + 1) * rows:(2 * k + 2) * rows]
            + low[k * rows:(k + 1) * rows] for k in range(len(lhs))]


def _pair_index(shape):
    return _iota(shape, 0), _iota(shape, 1) & (A_HEAD_DIM - 1)


def _pair_tri_inverses(mats):
    t_idx, s_idx = _pair_index(mats[0].shape)
    same = lambda blk: (lax.shift_right_logical(t_idx, blk.bit_length() - 1)
                        == lax.shift_right_logical(s_idx, blk.bit_length() - 1))
    eye = (t_idx == s_idx).astype(F32)
    nd = [jnp.where(same(8), m, 0.0) for m in mats]
    n2 = [_pair_mm3([x], x)[0] for x in nd]
    t = [eye + x for x in nd]
    both = [_pair_mm3([a, b], a) for a, b in zip(n2, t)]
    t = [x + r[1] for x, r in zip(t, both)]
    t = [x + _pair_mm3([x], r[0])[0] for x, r in zip(t, both)]
    blk = 8
    while blk < CHUNK:
        sel = same(2 * blk) & jnp.logical_not(same(blk))
        w = [_pair_mm3([jnp.where(sel, m, 0.0)], x)[0] for m, x in zip(mats, t)]
        t = [x + _pair_mm3([x], y)[0] for x, y in zip(t, w)]
        blk *= 2
    return t


def _rwkv_kernel(x_ref, s0_ref, mu_ref, wr_ref, wk_ref, wv_ref, w0_ref, w1_ref, w2_ref,
                 a0_ref, a1_ref, a2_ref, g1_ref, g2_ref, kk_ref, ka_ref, rk_ref, lng_ref, lnb_ref,
                 y_ref, s_ref,
                 r_s, v_s, kk_s, cum_s, cx_s, kd_s, bb_s, g_s, bonus_s, y_s, sp_s, *, seq_len):
    total = x_ref.shape[0]
    n_seq = total // seq_len
    n_chunks = seq_len // CHUNK
    n_tiles = total // ROW_TILE
    hd = A_HEAD_DIM
    n_groups, width = sp_s.shape[2], sp_s.shape[4]
    per_group = width // hd
    x = x_ref[...]
    row = _iota(x.shape, 0) & (seq_len - 1)
    prev = jnp.where(row == 0, 0.0, pltpu.roll(x, 1, 0))
    nxt = jnp.where(row == seq_len - 1, 0.0, pltpu.roll(x, total - 1, 0))
    y_s[...] = 0.5 * (prev + nxt) - x
    head_ones = _same_block(A_WIDTH, hd).astype(BF16)
    chunk_diag = _same_block(SUM_TILE, CHUNK)
    run_sum = [(chunk_diag & _tri(d == 1, True, SUM_TILE)).astype(BF16) for d in range(2)]

    def project_tile(ti, carry):
        rows = pl.ds(pl.multiple_of(ti * ROW_TILE, ROW_TILE), ROW_TILE)
        x, xx = x_ref[rows, :], y_s[rows, :]
        xr, xw, xk, xv, xi, xg = (x + xx * mu_ref[j:j + 1, :] for j in range(6))
        r = _mm(xr, wr_ref[...])
        k = _mm(xk, wk_ref[...])
        v = _mm(xv, wv_ref[...])
        g_s[rows, :] = _mm(_sigmoid(_mm(xg, g1_ref[...])), g2_ref[...])
        kk = k * kk_ref[...]
        kk = kk / jnp.maximum(jnp.sqrt(_sum01(_split(kk * kk), head_ones)), 1e-12)
        r_s[rows, :] = r
        v_s[rows, :] = v
        kk_s[rows, :] = kk
        kd_sum = jnp.zeros_like(x)
        for d in range(2):
            zw = w0_ref[d:d + 1, :] + _mm(jnp.tanh(_mm(xw, w1_ref[d])), w2_ref[d])
            lw = -jnp.exp(-_softplus(-zw) - 0.5)
            cum = jnp.concatenate([_sum01(run_sum[d], _split3(lw[j:j + SUM_TILE]))
                                   for j in range(0, ROW_TILE, SUM_TILE)], axis=0)
            cum_s[d, rows, :] = cum
            cx_s[d, rows, :] = cum - lw
            a = _sigmoid(a0_ref[d:d + 1, :] + _mm(_mm(xi, a1_ref[d]), a2_ref[d]))
            kd = k * (1.0 + (a - 1.0) * ka_ref[...])
            kd_s[d, rows, :] = kd
            bb_s[d, rows, :] = kk * a
            kd_sum = kd_sum + kd
        bonus_s[rows, :] = _sum01(_split(r * kd_sum * rk_ref[...]), head_ones) * v
        y_s[rows, :] = jnp.zeros_like(x)
        return carry

    lax.fori_loop(0, n_tiles, project_tile, 0)
    for n in range(n_seq):
        for d in range(2):
            for p in range(n_groups):
                sp_s[n, d, p] = (jnp.zeros((hd, width), F32) if s0_ref is None else jnp.concatenate(
                    [s0_ref[n, d, per_group * p + j] for j in range(per_group)], axis=1))

    t_idx, s_idx = _pair_index((CHUNK, width))
    strict = [s_idx < t_idx, s_idx > t_idx]
    incl = [s_idx <= t_idx, s_idx >= t_idx]
    lane_head = _head_of_lane((CHUNK, width))

    def chunk_body(ci, carry):
        chains = []
        for n, d in [(n, d) for n in range(n_seq) for d in range(2)]:
            cpos = n * n_chunks + ((n_chunks - 1 - ci) if d == 1 else ci)
            last = 0 if d == 1 else CHUNK - 1
            rows = pl.ds(pl.multiple_of(cpos * CHUNK, CHUNK), CHUNK)
            cum, cx = cum_s[d, rows, :], cx_s[d, rows, :]
            tot = cum[last:last + 1, :]
            kdc, bbc = kd_s[d, rows, :], bb_s[d, rows, :]
            e_neg = jnp.exp(-cum)
            e_tail = jnp.exp(tot - cum)
            at = -kk_s[rows, :] * jnp.exp(cx)
            rt = r_s[rows, :] * jnp.exp(cum)
            bt, kt = bbc * e_neg, kdc * e_neg
            bp, kp = bbc * e_tail, kdc * e_tail
            pc = jnp.exp(tot)
            vc = v_s[rows, :]
            for p in range(n_groups):
                ln = slice(p * width, (p + 1) * width)
                chains.append(dict(d=d, st=(n, d, p), rows=rows, ln=ln, v=vc[:, ln], pc=pc[:, ln],
                                   lhs=jnp.concatenate([at[:, ln], rt[:, ln]], axis=0),
                                   bt=bt[:, ln], kt=kt[:, ln], bp=bp[:, ln], kp=kp[:, ln]))
        for c in chains:
            c["sb"] = _pair_mm3([c["lhs"]], c["bt"], _NT)[0]
            c["sk"] = _pair_mm3([c["lhs"]], c["kt"], _NT)[0]
            c["s"] = sp_s[c["st"]]
            c["fs"] = _mm(c["lhs"], _pair_diag(c["s"]).T)
            c["v_diag"] = _pair_diag(c["v"].astype(BF16))
        for c in chains:
            a_ak = jnp.where(strict[c["d"]], c["sk"][:CHUNK], 0.0)
            c["x"] = c["fs"][:CHUNK] + _mm(a_ak, c["v_diag"])
        inv = _pair_tri_inverses([jnp.where(strict[c["d"]], c["sb"][:CHUNK], 0.0) for c in chains])
        for c, t in zip(chains, inv):
            c["u"] = _pair_mm3([t], c["x"])[0]
        for c in chains:
            d, rows, ln, u = c["d"], c["rows"], c["ln"], c["u"]
            a_r = jnp.concatenate([jnp.where(incl[d], c["sb"][CHUNK:], 0.0),
                                   jnp.where(incl[d], c["sk"][CHUNK:], 0.0)], axis=1)
            uv_diag = jnp.concatenate([_pair_diag(u.astype(BF16)), c["v_diag"]], axis=0)
            y_s[rows, ln] = y_s[rows, ln] + c["fs"][CHUNK:] + _mm(a_r, uv_diag)
            g = _mm_tn(jnp.concatenate([u, c["v"]], axis=0), jnp.concatenate([c["bp"], c["kp"]], axis=0))
            own = g[0:hd]
            for j in range(1, per_group):
                own = jnp.where(lane_head == j, g[j * hd:(j + 1) * hd], own)
            sp_s[c["st"]] = c["s"] * c["pc"] + own
        return carry

    lax.fori_loop(0, n_chunks, chunk_body, 0)
    for n in range(n_seq):
        for d in range(2):
            for p in range(n_groups):
                for j in range(per_group):
                    s_ref[n, d, per_group * p + j] = sp_s[n, d, p, :, j * hd:(j + 1) * hd]

    def finish_tile(ti, carry):
        rows = pl.ds(pl.multiple_of(ti * ROW_TILE, ROW_TILE), ROW_TILE)
        ys = y_s[rows, :]
        cen = ys - _sum01(_split(ys), head_ones) * (1.0 / hd)
        var = _sum01(_split(cen * cen), head_ones) * (1.0 / hd)
        yn = cen * lax.rsqrt(var + RWKV_GN_EPS) * lng_ref[...] + lnb_ref[...]
        y_ref[rows, :] = (yn + bonus_s[rows, :]) * g_s[rows, :]
        return carry

    lax.fori_loop(0, n_tiles, finish_tile, 0)


def _rwkv(xa, s0, p, n_seq):
    b, t, w = xa.shape
    assert t & (t - 1) == 0 and b % n_seq == 0
    names = ("mu", "wr", "wk", "wv", "w0", "w1", "w2", "a0", "a1", "a2", "g1", "g2",
             "k_k", "k_a", "r_k", "ln_g", "ln_b")
    weights = [p["rwkv_" + n] for n in names]
    rows = n_seq * t
    seq = pl.BlockSpec((None, rows, w), lambda i: (i, 0, 0))
    st = pl.BlockSpec((n_seq, 2, A_HEADS, A_HEAD_DIM, A_HEAD_DIM), lambda i: (i, 0, 0, 0, 0))
    tw = lambda lead=(): pltpu.VMEM(lead + (rows, w), F32)
    kernel = functools.partial(_rwkv_kernel, seq_len=t)
    if s0 is None:
        kernel, state_specs, states = _zero_state_kernel(kernel, 1, 1), [], ()
    else:
        state_specs, states = [st], (s0,)
    y, s_out = pl.pallas_call(
        kernel,
        grid=(b // n_seq,),
        in_specs=[seq] + state_specs + [_full(x.shape) for x in weights],
        out_specs=[seq, st],
        out_shape=[jax.ShapeDtypeStruct((b // n_seq, rows, w), F32),
                   jax.ShapeDtypeStruct((b, 2, A_HEADS, A_HEAD_DIM, A_HEAD_DIM), F32)],
        scratch_shapes=[tw(), tw(), tw(), tw((2,)), tw((2,)), tw((2,)), tw((2,)), tw(), tw(), tw(),
                        pltpu.VMEM((n_seq, 2, A_HEADS // SCAN_HEADS, A_HEAD_DIM, SCAN_HEADS * A_HEAD_DIM), F32)],
        compiler_params=_cparams(("arbitrary",)),
        name="rwkv",
    )(xa.reshape(b // n_seq, rows, w), *states, *weights)
    return y.reshape(b, t, w), s_out


def _mlstm_kernel(u_ref, op_ref, c0_ref, n0_ref, m0_ref, conv_ref, wq_ref, wk_ref, wv_ref,
                  wg_ref, bg_ref, wgt_ref, bgt_ref, lng_ref,
                  y_ref, c_ref, n_ref, m_ref,
                  pad_s, q_s, k_s, v_s, dd_s, cr_s, h_s, *, conv2d, seq_len):
    t_len = u_ref.shape[0]
    n_seq = t_len // seq_len
    n_chunks = seq_len // CHUNK
    hd = B_HEAD_DIM
    u = u_ref[...]
    pad_s[0:CONV_PAD, :] = jnp.zeros((CONV_PAD, B_WIDTH), F32)
    pad_s[CONV_PAD + t_len:CONV_PAD + t_len + CONV_PAD, :] = jnp.zeros((CONV_PAD, B_WIDTH), F32)
    pad_s[CONV_PAD:CONV_PAD + t_len, :] = u
    pos = _iota(u.shape, 0) & (seq_len - 1)
    col = pos & (GRID_W - 1)
    conv = jnp.zeros_like(u)
    for kh in range(3):
        if not conv2d and kh != 1:
            continue
        for kw in range(3):
            shift = (kh - 1) * GRID_W + (kw - 1)
            term = pad_s[CONV_PAD + shift:CONV_PAD + shift + t_len, :] * conv_ref[kh * 3 + kw:kh * 3 + kw + 1, :]
            if conv2d and kw != 1:
                src = col + (kw - 1)
                term = jnp.where((src >= 0) & (src < GRID_W), term, 0.0)
            if n_seq > 1 and shift != 0:
                term = jnp.where((pos + shift >= 0) & (pos + shift < seq_len), term, 0.0)
            conv = conv + term
    uc = _silu(conv)
    for h in range(B_HEADS):
        ln = slice(h * hd, (h + 1) * hd)
        q_s[:, ln] = _mm(uc[:, ln], wq_ref[h])
        k_s[:, ln] = _mm(uc[:, ln], wk_ref[h]) * (hd ** -0.5)
        v_s[:, ln] = _mm(u[:, ln], wv_ref[h])
    gcol = _mm(uc, wg_ref[...]) + bg_ref[...]
    gcol = jnp.where(_iota(gcol.shape, 1) >= 8, -_softplus(-gcol), gcol)
    grow = _mm_nt(wgt_ref[...], uc) + bgt_ref[:, 0:1]
    grow = jnp.where(_iota(grow.shape, 0) >= 8, -_softplus(-grow), grow)
    chunk_diag = _same_block(SUM_TILE, CHUNK)
    run_sum = [(chunk_diag & _tri(d == 1, True, SUM_TILE)).astype(BF16) for d in range(2)]
    per_tile = SUM_TILE // CHUNK
    for ti in range(t_len // SUM_TILE):
        sl = slice(ti * SUM_TILE, (ti + 1) * SUM_TILE)
        gt, rt = _split3(gcol[sl, :]), _split3(grow[:, sl])
        backward_lane = (_iota((SUM_TILE, LANES), 1) & 4) != 0
        cum_c = jnp.where(backward_lane, _sum01(run_sum[1], gt, _NN), _sum01(run_sum[0], gt, _NN))
        dd_s[sl, :] = pltpu.roll(cum_c, LANES - 8, 1) - gcol[sl, :]
        backward_row = (_iota((16, SUM_TILE), 0) & 4) != 0
        cum_r = jnp.where(backward_row, _sum01(rt, run_sum[1], _NT), _sum01(rt, run_sum[0], _NT))
        for c in range(per_tile):
            cr_s[ti * per_tile + c] = cum_r[:, c * CHUNK:(c + 1) * CHUNK]
    h_s[...] = jnp.zeros_like(u)
    for ref, init in ((c_ref, c0_ref), (n_ref, n0_ref), (m_ref, m0_ref)):
        ref[...] = jnp.zeros(ref.shape, F32) if init is None else init[...]
    row_i, col_i = _iota((CHUNK, CHUNK), 0), _iota((CHUNK, CHUNK), 1)
    incl_t = [row_i <= col_i, row_i >= col_i]
    eye = row_i == col_i
    pick = [(_iota((LANES, B_HEADS * LANES), 0)
             == d * B_HEADS + lax.shift_right_logical(_iota((LANES, B_HEADS * LANES), 1), LANES.bit_length() - 1)
             ).astype(BF16) for d in range(2)]
    spread = (lax.shift_right_logical(_iota((2 * CHUNK, 2 * LANES), 0), CHUNK.bit_length() - 1)
              == lax.shift_right_logical(_iota((2 * CHUNK, 2 * LANES), 1), LANES.bit_length() - 1)).astype(BF16)
    ones_bf = jnp.ones((CHUNK, LANES), BF16)

    def chunk_body(ci, carry):
        chains = []
        for n, d in [(n, d) for n in range(n_seq) for d in range(2)]:
            cpos = n * n_chunks + ((n_chunks - 1 - ci) if d == 1 else ci)
            last = 0 if d == 1 else CHUNK - 1
            rows = pl.ds(pl.multiple_of(cpos * CHUNK, CHUNK), CHUNK)
            cr = cr_s[cpos]
            d_col = _sum01(_split3(dd_s[rows, :]), pick[d])
            for h in range(B_HEADS):
                ln = slice(h * hd, (h + 1) * hd)
                lf = 8 + d * 4 + h
                chains.append(dict(d=d, st=(n, d, h), vec=(n, d, slice(h, h + 1)), rows=rows, ln=ln, last=last,
                                   b_row=cr[lf:lf + 1, :], d_col=d_col[:, ln],
                                   m=m_ref[n, d, h:h + 1, 0:1], q=q_s[rows, ln], k=k_s[rows, ln], v=v_s[rows, ln],
                                   c=c_ref[n, d, h], n=n_ref[n, d, h:h + 1, :]))
        for c in chains:
            c["kq"] = _mm(c["k"], c["q"].T)
            c["qc"] = _mm(c["q"], c["c"])
            c["qn"] = _mm(c["q"], jnp.broadcast_to(c["n"], (LANES, hd)).T)
            logw = jnp.where(incl_t[c["d"]], c["b_row"] - c["d_col"][:, 0:CHUNK], -jnp.inf)
            c["m_row"] = jnp.maximum(c["b_row"] + c["m"], jnp.max(logw, axis=0, keepdims=True))
            c["logw"] = logw
        for c in chains:
            inter_row = jnp.exp(c["b_row"] + c["m"] - c["m_row"])
            diag = jnp.concatenate([jnp.where(eye, c["m_row"], 0.0), jnp.where(eye, inter_row, 0.0)], axis=1)
            cols = _sum01(_split3(diag), spread)
            c["m_col"], c["inter"] = cols[:, 0:LANES], cols[:, LANES:2 * LANES]
            c["s"] = c["kq"] * jnp.exp(c["logw"] - c["m_row"])
        for c in chains:
            s_hi, s_lo = _split(c["s"])
            sv = lax.dot_general(s_hi, jnp.concatenate([c["v"].astype(BF16), ones_bf], axis=1),
                                 (_TN, ((), ())), preferred_element_type=F32)
            c["sv"] = sv[:, 0:hd]
            c["den"] = (c["inter"] * c["qn"] + sv[:, hd:hd + LANES]
                        + lax.dot_general(s_lo, ones_bf, (_TN, ((), ())), preferred_element_type=F32))
            last = c["last"]
            c["m_new"] = c["m_row"][:, last:last + 1]
            b_last = c["b_row"][:, last:last + 1]
            c["decay"] = jnp.exp(b_last + c["m"] - c["m_new"])
            c["kw"] = c["k"] * jnp.exp(b_last - c["m_new"] - c["d_col"])
        for c in chains:
            c["kv"] = _mm_tn(c["kw"], c["v"])
        for c in chains:
            rows, ln = c["rows"], c["ln"]
            num = c["inter"] * c["qc"] + c["sv"]
            h_s[rows, ln] = h_s[rows, ln] + num / jnp.maximum(jnp.abs(c["den"]), jnp.exp(-c["m_col"]))
            c_ref[c["st"]] = c["decay"] * c["c"] + c["kv"]
            n_ref[c["vec"]] = c["decay"] * c["n"] + jnp.sum(c["kw"], axis=0, keepdims=True)
            m_ref[c["vec"]] = jnp.broadcast_to(c["m_new"], (1, LANES))
        return carry

    lax.fori_loop(0, n_chunks, chunk_body, 0)

    for h in range(B_HEADS):
        ln = slice(h * hd, (h + 1) * hd)
        hh = h_s[:, ln]
        cen = hh - jnp.mean(hh, axis=1, keepdims=True)
        var = jnp.mean(cen * cen, axis=1, keepdims=True)
        y_ref[:, ln] = _sigmoid(op_ref[:, ln]) * (cen * lax.rsqrt(var + MLSTM_GN_EPS) * lng_ref[:, ln])


def _zero_state_kernel(kernel, n_data, n_state):
    def wrapped(*refs, **kw):
        return kernel(*refs[:n_data], *([None] * n_state), *refs[n_data:], **kw)
    return wrapped


def _mlstm(ub, o_pre, states, p, conv2d, n_seq):
    b, t, w = ub.shape
    assert t & (t - 1) == 0 and b % n_seq == 0
    names = ("conv", "wq", "wk", "wv", "wg", "bg", "wgt", "bgt", "ln_g")
    weights = [p["mlstm_" + n] for n in names]
    rows, steps = n_seq * t, b // n_seq
    seq = pl.BlockSpec((None, rows, w), lambda i: (i, 0, 0))
    cst = pl.BlockSpec((n_seq, 2, B_HEADS, B_HEAD_DIM, B_HEAD_DIM), lambda i: (i, 0, 0, 0, 0))
    vst = pl.BlockSpec((n_seq, 2, B_HEADS, LANES), lambda i: (i, 0, 0, 0))
    kernel = functools.partial(_mlstm_kernel, conv2d=conv2d, seq_len=t)
    if states is None:
        kernel, state_specs, states = _zero_state_kernel(kernel, 2, 3), [], ()
    else:
        state_specs = [cst, vst, vst]
    y, c_out, n_out, m_out = pl.pallas_call(
        kernel,
        grid=(steps,),
        in_specs=[seq, seq] + state_specs + [_full(x.shape) for x in weights],
        out_specs=[seq, cst, vst, vst],
        out_shape=[jax.ShapeDtypeStruct((steps, rows, w), F32),
                   jax.ShapeDtypeStruct((b, 2, B_HEADS, B_HEAD_DIM, B_HEAD_DIM), F32),
                   jax.ShapeDtypeStruct((b, 2, B_HEADS, LANES), F32),
                   jax.ShapeDtypeStruct((b, 2, B_HEADS, LANES), F32)],
        scratch_shapes=[pltpu.VMEM((rows + 2 * CONV_PAD, w), F32), pltpu.VMEM((rows, w), F32),
                        pltpu.VMEM((rows, w), F32), pltpu.VMEM((rows, w), F32),
                        pltpu.VMEM((rows, LANES), F32), pltpu.VMEM((rows // CHUNK, 16, CHUNK), F32),
                        pltpu.VMEM((rows, w), F32)],
        compiler_params=_cparams(("arbitrary",)),
        name="mlstm",
    )(ub.reshape(steps, rows, w), o_pre.reshape(steps, rows, w), *states, *weights)
    return y.reshape(b, t, w), c_out, n_out, m_out


def _first_lane_of_max(val, lane, valid):
    masked = jnp.where(valid, val, -jnp.inf)
    best = jnp.max(masked, axis=1, keepdims=True)
    idx = jnp.min(jnp.where(valid & (masked == best), lane, LANES), axis=1, keepdims=True)
    return best, idx


def _merge_kernel(x_ref, ya_ref, yb_ref, mod_ref, g1_ref, wga0_ref, wga1_ref, wgb0_ref, wgb1_ref,
                  wpa_ref, wpb_ref, wo_ref, g2_ref, wr_ref, br_ref, xc_ref):
    d = D_MODEL
    gate1 = mod_ref[:, 2 * d:3 * d]
    shift2, scale2 = mod_ref[:, 3 * d:4 * d], mod_ref[:, 4 * d:5 * d]
    h1 = _modulated_norm1(x_ref, mod_ref, g1_ref)
    pa, pb = _mm(ya_ref[...], wpa_ref[...]), _mm(yb_ref[...], wpb_ref[...])
    halves = []
    for j, (wga_ref, wgb_ref) in enumerate(((wga0_ref, wgb0_ref), (wga1_ref, wgb1_ref))):
        cols = slice(j * W_IN_BLOCK, (j + 1) * W_IN_BLOCK)
        halves.append(_sigmoid(_mm(h1, wga_ref[...])) * pa[:, cols] + _sigmoid(_mm(h1, wgb_ref[...])) * pb[:, cols])
    merged = jnp.concatenate(halves, axis=1)
    x1 = x_ref[...] + gate1 * _mm(merged, wo_ref[...])
    xc_ref[:, 0:d] = x1
    h2 = _rmsnorm(x1, g2_ref[...]) * (1.0 + scale2) + shift2
    logits = _mm(h2, wr_ref[...]) + br_ref[...]
    lane = _iota(logits.shape, 1)
    is_group = lane < N_GROUPS
    g_max, g_sel = _first_lane_of_max(logits, lane, is_group)
    g_w = 1.0 / jnp.sum(jnp.where(is_group, jnp.exp(logits - g_max), 0.0), axis=1, keepdims=True)
    expert = lane - N_GROUPS
    in_group = (expert >= 0) & (expert < N_EXPERTS) & (lax.shift_right_arithmetic(expert, EXPERTS_PER_GROUP.bit_length() - 1) == g_sel)
    e_max = jnp.max(jnp.where(in_group, logits, -jnp.inf), axis=1, keepdims=True)
    e_exp = jnp.where(in_group, jnp.exp(logits - e_max), 0.0)
    prob = e_exp / jnp.sum(e_exp, axis=1, keepdims=True)
    p1, i1 = _first_lane_of_max(prob, lane, in_group)
    p2, i2 = _first_lane_of_max(prob, lane, in_group & (lane != i1))
    denom = p1 + p2
    comb = jnp.where(lane == i1, g_w * p1 / denom, 0.0) + jnp.where(lane == i2, g_w * p2 / denom, 0.0)
    xc_ref[:, d:d + LANES] = jnp.where(lane == GROUP_LANE, g_sel.astype(F32), comb)


def _merge(x, ya, yb, mod, p, tm):
    b, t, d = x.shape
    per = t // tm
    assert d == 2 * W_IN_BLOCK
    tok = lambda n: pl.BlockSpec((None, tm, n), lambda i: (i // per, i % per, 0))
    gate_specs, gate_args = _w_in_blocks(p["w_in"], 3, 4)
    weights = [p["rwkv_w_proj"], p["mlstm_w_proj"], p["w_out"], p["norm2_g"], p["moe_w_router"], p["moe_b_router"]]
    return pl.pallas_call(
        _merge_kernel,
        grid=(b * per,),
        in_specs=[tok(d), tok(A_WIDTH), tok(B_WIDTH), pl.BlockSpec((None, 1, 6 * d), lambda i: (i // per, 0, 0)),
                  _full((1, d))] + gate_specs + [_full(w.shape) for w in weights],
        out_specs=tok(d + LANES),
        out_shape=jax.ShapeDtypeStruct((b, t, d + LANES), F32),
        compiler_params=_cparams(("arbitrary",)),
        name="merge",
    )(x, ya, yb, mod, p["norm1_g"], *gate_args, *weights)


def _route(gsel, tb):
    s, l = gsel.shape
    n_buckets = s * N_GROUPS
    max_tiles = s * (l // tb + N_GROUPS - 1)
    onehot = (gsel[..., None] == jnp.arange(N_GROUPS, dtype=jnp.int32)).astype(jnp.int32)
    rank = jnp.sum((jnp.cumsum(onehot, axis=1) - onehot) * onehot, axis=-1)
    n_tiles = ((jnp.sum(onehot, axis=1) + tb - 1) // tb).T.reshape(n_buckets)
    ends = jnp.cumsum(n_tiles)
    first_tile = jnp.sum(onehot * (ends - n_tiles).reshape(N_GROUPS, s).T[:, None, :], axis=-1)
    pos = first_tile * tb + rank
    tok = jnp.arange(s * l, dtype=jnp.int32)
    row_src = jnp.full((max_tiles * tb,), -1, jnp.int32).at[pos.reshape(-1)].set(tok)
    tile_bucket = jnp.sum(jnp.arange(max_tiles, dtype=jnp.int32)[:, None] >= ends[None, :], axis=1)
    tile_bucket = jnp.minimum(tile_bucket, n_buckets - 1).astype(jnp.int32)
    return row_src, tile_bucket % s, tile_bucket // s, ends[-1:].astype(jnp.int32)


def _moe_kernel(gsrc_ref, sdst_ref, seg_ref, grp_ref, used_ref,
                xc_hbm, mod_ref, g2_ref, w1_ref, w3_ref, w2_ref, gf_ref, dump_in_hbm,
                y_hbm, dump_hbm, gbuf, obuf, gsem, ssem, *, tb):
    d = D_MODEL
    q = pl.program_id(0)
    n_used = used_ref[0]

    def start_gather(tile, sl):
        def body(i, carry):
            for u in range(SUBLANES):
                idx = gsrc_ref[tile * tb + i * SUBLANES + u]
                pltpu.make_async_copy(xc_hbm.at[lax.shift_right_logical(idx, SUBLANES.bit_length() - 1), pl.ds(idx & (SUBLANES - 1), 1), :],
                                      gbuf.at[sl, i, pl.ds(u, 1), :], gsem.at[sl]).start()
            return carry

        lax.fori_loop(0, tb // SUBLANES, body, 0)

    def wait_gather(sl):
        pltpu.make_async_copy(xc_hbm.at[pl.ds(0, tb // SUBLANES)], gbuf.at[sl], gsem.at[sl]).wait()

    def start_scatter(tile, sl):
        def body(i, carry):
            for u in range(SUBLANES):
                idx = sdst_ref[tile * tb + i * SUBLANES + u]

                @pl.when(idx >= 0)
                def _():
                    pltpu.make_async_copy(
                        obuf.at[sl, i, pl.ds(u, 1), :],
                        y_hbm.at[lax.shift_right_logical(idx, SUBLANES.bit_length() - 1), pl.ds(idx & (SUBLANES - 1), 1), :],
                        ssem.at[sl]).start()

                @pl.when(idx < 0)
                def _():
                    pltpu.make_async_copy(obuf.at[sl, i, pl.ds(u, 1), :], dump_hbm.at[sl, i, pl.ds(u, 1), :],
                                          ssem.at[sl]).start()

            return carry

        lax.fori_loop(0, tb // SUBLANES, body, 0)

    def wait_scatter(sl):
        pltpu.make_async_copy(obuf.at[sl], dump_hbm.at[sl], ssem.at[sl]).wait()

    def step(slot):
        @pl.when(q == 0)
        def _():
            start_gather(0, 0)

        @pl.when(q + 1 < n_used)
        def _():
            start_gather(q + 1, 1 - slot)

        wait_gather(slot)

        @pl.when(q >= 2)
        def _():
            wait_scatter(slot)

        rows = gbuf[slot].reshape(tb, gbuf.shape[-1])
        x1, comb = rows[:, 0:d], rows[:, d:d + LANES]
        shift2, scale2, gate2 = mod_ref[:, 3 * d:4 * d], mod_ref[:, 4 * d:5 * d], mod_ref[:, 5 * d:6 * d]
        h2 = (_rmsnorm(x1, g2_ref[...]) * (1.0 + scale2) + shift2).astype(BF16)
        first_lane = N_GROUPS + grp_ref[q] * EXPERTS_PER_GROUP
        lane = _iota(comb.shape, 1)
        acc = jnp.zeros((tb, d), F32)
        for e in range(EXPERTS_PER_GROUP):
            w_e = jnp.sum(jnp.where(lane == first_lane + e, comb, 0.0), axis=1, keepdims=True)
            a = _mm(h2, w1_ref[e])
            b = _mm(h2, w3_ref[e])
            acc = acc + _mm(_silu(a) * b * w_e, w2_ref[e])
        obuf[slot] = _rmsnorm(x1 + gate2 * acc, gf_ref[...]).reshape(tb // SUBLANES, SUBLANES, d)
        start_scatter(q, slot)

        @pl.when(q == n_used - 1)
        def _():
            @pl.when(q >= 1)
            def _():
                wait_scatter(1 - slot)

            wait_scatter(slot)

    for slot in range(2):
        @pl.when((q < n_used) & (lax.rem(q, 2) == slot))
        def _(slot=slot):
            step(slot)


def _moe(xc, mod, p, tb):
    s, l, width = xc.shape
    d = D_MODEL
    gsel = xc[:, :, d + GROUP_LANE].astype(jnp.int32)
    row_src, tile_seg, tile_grp, n_used = _route(gsel, tb)
    max_tiles = tile_seg.shape[0]
    grouped = lambda w: w.reshape((N_GROUPS, EXPERTS_PER_GROUP) + w.shape[1:])
    w1, w3, w2 = grouped(p["moe_w1"]), grouped(p["moe_w3"]), grouped(p["moe_w2"])
    const = lambda shape: pl.BlockSpec(shape, lambda q, *_: (0,) * len(shape))
    by_group = lambda w: pl.BlockSpec((None,) + w.shape[1:], lambda q, gs, sd, seg, grp, used: (grp[q], 0, 0, 0))
    hbm = pl.BlockSpec(memory_space=pl.ANY)
    y, _ = pl.pallas_call(
        functools.partial(_moe_kernel, tb=tb),
        grid_spec=pltpu.PrefetchScalarGridSpec(
            num_scalar_prefetch=5,
            grid=(max_tiles,),
            in_specs=[hbm, pl.BlockSpec((None, 1, 6 * d), lambda q, gs, sd, seg, grp, used: (seg[q], 0, 0)),
                      const((1, d)), by_group(w1), by_group(w3), by_group(w2), const((1, d)), hbm],
            out_specs=[hbm, hbm],
            scratch_shapes=[pltpu.VMEM((2, tb // SUBLANES, SUBLANES, width), F32),
                            pltpu.VMEM((2, tb // SUBLANES, SUBLANES, d), F32),
                            pltpu.SemaphoreType.DMA((2,)), pltpu.SemaphoreType.DMA((2,))]),
        out_shape=[jax.ShapeDtypeStruct((s * l // SUBLANES, SUBLANES, d), F32),
                   jax.ShapeDtypeStruct((2, tb // SUBLANES, SUBLANES, d), F32)],
        input_output_aliases={12: 1},
        compiler_params=_cparams(("arbitrary",)),
        name="moe",
    )(jnp.maximum(row_src, 0), row_src, tile_seg, tile_grp, n_used,
      xc.reshape(s * l // SUBLANES, SUBLANES, width), mod, p["norm2_g"], w1, w3, w2, p["final_norm_g"],
      jnp.zeros((2, tb // SUBLANES, SUBLANES, d), F32))
    return y.reshape(s, l, d)


def _trunk(x, mod_seg, states, conv2d, p, tm, tb, mixer_seqs):
    b, t, d = x.shape
    n_seg = mod_seg.shape[0]
    seg = lambda a: a.reshape(n_seg, b * t // n_seg, a.shape[-1])
    per_seq = lambda a: a.reshape(b, t, a.shape[-1])
    xa, ub, o_pre = (per_seq(a) for a in _in_proj(seg(x), mod_seg, p["norm1_g"], p["w_in"], tm))
    if states is None:
        s_rwkv, mlstm_states = None, None
    else:
        s_rwkv, s_c, s_n, s_m = states
        mlstm_states = (s_c, s_n, jnp.broadcast_to(s_m[..., None], s_m.shape + (LANES,)))
    ya, s_rwkv = _rwkv(xa, s_rwkv, p, mixer_seqs)
    yb, s_c, s_n, s_m = _mlstm(ub, o_pre, mlstm_states, p, conv2d, mixer_seqs)
    xc = _merge(seg(x), seg(ya), seg(yb), mod_seg, p, tm)
    y = _moe(xc, mod_seg, p, tb)
    return y.reshape(b, t, d), s_rwkv, s_c, s_n, s_m[..., 0]


def kernel(x_prompt, x_sample, c, c_ctx, state_rwkv, state_mlstm_C, state_mlstm_n, state_mlstm_m, w_mod, b_mod, norm1_g, norm2_g, w_in, rwkv_mu, rwkv_w_r, rwkv_w_k, rwkv_w_v, rwkv_w0, rwkv_w1, rwkv_w2, rwkv_a0, rwkv_a1, rwkv_a2, rwkv_g1, rwkv_g2, rwkv_k_k, rwkv_k_a, rwkv_r_k, rwkv_ln_g, rwkv_ln_b, rwkv_w_proj, mlstm_conv, mlstm_w_q, mlstm_w_k, mlstm_w_v, mlstm_w_i, mlstm_b_i, mlstm_w_f, mlstm_b_f, mlstm_ln_g, mlstm_w_proj, w_out, moe_w_group, moe_b_group, moe_w_expert, moe_b_expert, moe_w1, moe_w3, moe_w2, final_norm_g):
    assert w_mod.shape[0] == 1, "single trunk layer"
    l = 0
    bp, dec = x_prompt.shape[0], x_sample.shape[0]
    bf = lambda w: w.astype(BF16)
    row = lambda w: w.reshape(1, -1).astype(F32)
    wg = jnp.concatenate([mlstm_w_i[l, 0], mlstm_w_i[l, 1], mlstm_w_f[l, 0], mlstm_w_f[l, 1]], axis=1)
    bg = jnp.concatenate([mlstm_b_i[l, 0], mlstm_b_i[l, 1], mlstm_b_f[l, 0], mlstm_b_f[l, 1]])
    wg = jnp.pad(wg, ((0, 0), (0, LANES - wg.shape[1])))
    bg = jnp.pad(bg, (0, LANES - bg.shape[0]))
    w_router = jnp.pad(jnp.concatenate([moe_w_group[l], moe_w_expert[l]], axis=1),
                       ((0, 0), (0, LANES - N_GROUPS - N_EXPERTS)))
    b_router = jnp.pad(jnp.concatenate([moe_b_group[l], moe_b_expert[l]]), (0, LANES - N_GROUPS - N_EXPERTS))
    p = {
        "norm1_g": row(norm1_g[l]), "norm2_g": row(norm2_g[l]), "w_in": w_in[l],
        "rwkv_mu": rwkv_mu[l], "rwkv_wr": bf(rwkv_w_r[l]), "rwkv_wk": bf(rwkv_w_k[l]), "rwkv_wv": bf(rwkv_w_v[l]),
        "rwkv_w0": rwkv_w0[l], "rwkv_w1": bf(rwkv_w1[l]), "rwkv_w2": bf(rwkv_w2[l]),
        "rwkv_a0": rwkv_a0[l], "rwkv_a1": bf(rwkv_a1[l]), "rwkv_a2": bf(rwkv_a2[l]),
        "rwkv_g1": bf(rwkv_g1[l]), "rwkv_g2": bf(rwkv_g2[l]),
        "rwkv_k_k": row(rwkv_k_k[l]), "rwkv_k_a": row(rwkv_k_a[l]), "rwkv_r_k": row(rwkv_r_k[l]),
        "rwkv_ln_g": row(rwkv_ln_g[l]), "rwkv_ln_b": row(rwkv_ln_b[l]),
        "rwkv_w_proj": rwkv_w_proj[l],
        "mlstm_conv": mlstm_conv[l].reshape(9, B_WIDTH),
        "mlstm_wq": bf(mlstm_w_q[l]), "mlstm_wk": bf(mlstm_w_k[l]), "mlstm_wv": bf(mlstm_w_v[l]),
        "mlstm_wg": bf(wg), "mlstm_bg": row(bg),
        "mlstm_wgt": bf(wg[:, :16].T), "mlstm_bgt": jnp.broadcast_to(bg[:16, None], (16, LANES)),
        "mlstm_ln_g": row(mlstm_ln_g[l]), "mlstm_w_proj": mlstm_w_proj[l], "w_out": w_out[l],
        "moe_w_router": bf(w_router), "moe_b_router": row(b_router),
        "moe_w1": moe_w1[l], "moe_w3": moe_w3[l], "moe_w2": moe_w2[l],
        "final_norm_g": row(final_norm_g),
    }
    cvec = jnp.concatenate([c_ctx[None, :], c, jnp.zeros((8 - 1 - dec, D_MODEL), F32)], axis=0)
    mod = _mod(cvec, w_mod[l], b_mod[l].reshape(1, -1))
    mod_ctx = mod[0:1][:, None, :]
    mod_lat = mod[1:1 + dec][:, None, :]

    yp, n_rwkv, n_c, n_n, n_m = _trunk(x_prompt, mod_ctx, None, False, p, 512, 512, 2)
    lat_states = tuple(s[:, l].astype(F32) for s in (state_rwkv, state_mlstm_C, state_mlstm_n, state_mlstm_m))
    ys, _, _, _, _ = _trunk(x_sample, mod_lat, lat_states, True, p, 512, 256, 1)
    dt = x_prompt.dtype
    return (yp, ys, n_rwkv[:, None].astype(dt), n_c[:, None].astype(dt), n_n[:, None].astype(dt),
            n_m[:, None].astype(dt))
```

```python
import functools

import jax
import jax.numpy as jnp
from jax import lax
from jax.experimental import pallas as pl
from jax.experimental.pallas import tpu as pltpu

F32 = jnp.float32
BF16 = jnp.bfloat16

D_MODEL = 1024
GRID_W = 64
A_HEAD_DIM = 64
A_WIDTH = D_MODEL // 2
A_HEADS = A_WIDTH // A_HEAD_DIM
B_HEAD_DIM = 128
B_WIDTH = D_MODEL // 2
B_HEADS = B_WIDTH // B_HEAD_DIM
CHUNK = 64
SCAN_HEADS = 2
RING = 3
ROW_TILE = 512
SUM_TILE = 256
N_GROUPS = 4
EXPERTS_PER_GROUP = 4
N_EXPERTS = N_GROUPS * EXPERTS_PER_GROUP
GROUP_LANE = N_GROUPS + N_EXPERTS
NORM_EPS = 1e-6
RWKV_GN_EPS = 64e-5
MLSTM_GN_EPS = 1e-5
LANES = 128
SUBLANES = 8
CONV_PAD = 72
VMEM_LIMIT = 56 * 1024 * 1024


def _mm(a, b):
    return jnp.dot(a.astype(BF16), b.astype(BF16), preferred_element_type=F32)


_NN, _NT, _TN = ((1,), (0,)), ((1,), (1,)), ((0,), (0,))


def _split(a):
    hi = a.astype(BF16)
    return hi, (a - hi.astype(F32)).astype(BF16)


def _split3(a):
    hi = a.astype(BF16)
    rest = a - hi.astype(F32)
    lo = rest.astype(BF16)
    return hi, lo, (rest - lo.astype(F32)).astype(BF16)


def _sum01(a, b, dims=_NN):
    dot = lambda p, q: lax.dot_general(p, q, (dims, ((), ())), preferred_element_type=F32)
    terms = [dot(x, b) for x in a] if isinstance(a, tuple) else [dot(a, x) for x in b]
    return functools.reduce(lambda p, q: p + q, terms)


def _mm_nt(a, b):
    return lax.dot_general(a.astype(BF16), b.astype(BF16), (((1,), (1,)), ((), ())),
                           preferred_element_type=F32)


def _mm_tn(a, b):
    return lax.dot_general(a.astype(BF16), b.astype(BF16), (((0,), (0,)), ((), ())),
                           preferred_element_type=F32)


def _sigmoid(x):
    return 1.0 / (1.0 + jnp.exp(-x))


def _silu(x):
    return x * _sigmoid(x)


def _softplus(x):
    return jnp.maximum(x, 0.0) + jnp.log(1.0 + jnp.exp(-jnp.abs(x)))


def _iota(shape, dim):
    return lax.broadcasted_iota(jnp.int32, shape, dim)


def _tri(reverse, inclusive, n=CHUNK):
    row, col = _iota((n, n), 0), _iota((n, n), 1)
    if reverse:
        return (col >= row) if inclusive else (col > row)
    return (col <= row) if inclusive else (col < row)


def _same_block(n, blk):
    sh = blk.bit_length() - 1
    row, col = _iota((n, n), 0), _iota((n, n), 1)
    return lax.shift_right_logical(row, sh) == lax.shift_right_logical(col, sh)


def _rmsnorm(x, g):
    return x * lax.rsqrt(jnp.mean(x * x, axis=-1, keepdims=True) + NORM_EPS) * g


def _cparams(sem):
    return pltpu.CompilerParams(dimension_semantics=sem, vmem_limit_bytes=VMEM_LIMIT)


def _full(shape):
    nd = len(shape)
    return pl.BlockSpec(shape, lambda *_: (0,) * nd)


def _mod_kernel(c_ref, w_ref, b_ref, o_ref):
    o_ref[...] = _mm(_silu(c_ref[...]), w_ref[...]) + b_ref[...]


def _mod(cvec, w_mod, b_mod):
    rows, d = cvec.shape
    n = w_mod.shape[1]
    tn = 1536
    return pl.pallas_call(
        _mod_kernel,
        grid=(n // tn,),
        in_specs=[_full((rows, d)), pl.BlockSpec((d, tn), lambda j: (0, j)),
                  pl.BlockSpec((1, tn), lambda j: (0, j))],
        out_specs=pl.BlockSpec((rows, tn), lambda j: (0, j)),
        out_shape=jax.ShapeDtypeStruct((rows, n), F32),
        compiler_params=_cparams(("arbitrary",)),
        name="mod",
    )(cvec, w_mod, b_mod)


W_IN_BLOCK = 512


def _modulated_norm1(x_ref, mod_ref, g_ref):
    d = D_MODEL
    shift, scale = mod_ref[:, 0:d], mod_ref[:, d:2 * d]
    return (_rmsnorm(x_ref[...], g_ref[...]) * (1.0 + scale) + shift).astype(BF16)


def _w_in_blocks(w_in, first, count):
    specs = [pl.BlockSpec((w_in.shape[0], W_IN_BLOCK), lambda *_, j=first + k: (0, j)) for k in range(count)]
    return specs, [w_in] * count


def _in_kernel(x_ref, mod_ref, g_ref, wa_ref, wu_ref, wo_ref, xa_ref, ub_ref, op_ref):
    h = _modulated_norm1(x_ref, mod_ref, g_ref)
    for ref, w_ref in ((xa_ref, wa_ref), (ub_ref, wu_ref), (op_ref, wo_ref)):
        ref[...] = _mm(h, w_ref[...])


def _in_proj(x, mod, g1, w_in, tm):
    b, t, d = x.shape
    per = t // tm
    widths = (A_WIDTH, B_WIDTH, B_WIDTH)
    assert all(n == W_IN_BLOCK for n in widths)
    tok = lambda n: pl.BlockSpec((None, tm, n), lambda i: (i // per, i % per, 0))
    w_specs, w_args = _w_in_blocks(w_in, 0, 3)
    return pl.pallas_call(
        _in_kernel,
        grid=(b * per,),
        in_specs=[tok(d), pl.BlockSpec((None, 1, 6 * d), lambda i: (i // per, 0, 0)), _full((1, d))] + w_specs,
        out_specs=[tok(n) for n in widths],
        out_shape=[jax.ShapeDtypeStruct((b, t, n), F32) for n in widths],
        compiler_params=_cparams(("arbitrary",)),
        name="in_proj",
    )(x, mod, g1, *w_args)


def _head_of_lane(shape):
    return lax.shift_right_logical(_iota(shape, 1), A_HEAD_DIM.bit_length() - 1)


def _pair_diag(x):
    head = _head_of_lane(x.shape)
    return jnp.concatenate([x * (head == h).astype(x.dtype) for h in range(x.shape[1] // A_HEAD_DIM)], axis=0)


def _pair_mm3(lhs, rhs, dims=_NN):
    rows = lhs[0].shape[0]
    if dims == _NT:
        wh, wl = _split(_pair_diag(rhs).T)
    else:
        rh, rl = _split(rhs)
        wh, wl = _pair_diag(rh), _pair_diag(rl)
    parts = [_split(l) for l in lhs]
    top = jnp.dot(jnp.concatenate([t for part in parts for t in part], axis=0), wh, preferred_element_type=F32)
    low = jnp.dot(jnp.concatenate([part[0] for part in parts], axis=0), wl, preferred_element_type=F32)
    return [top[2 * k * rows:(2 * k + 1) * rows] + top[(2 * k + 1) * rows:(2 * k + 2) * rows]
            + low[k * rows:(k + 1) * rows] for k in range(len(lhs))]


def _pair_index(shape):
    return _iota(shape, 0), _iota(shape, 1) & (A_HEAD_DIM - 1)


def _pair_tri_inverses(mats):
    t_idx, s_idx = _pair_index(mats[0].shape)
    same = lambda blk: (lax.shift_right_logical(t_idx, blk.bit_length() - 1)
                        == lax.shift_right_logical(s_idx, blk.bit_length() - 1))
    eye = (t_idx == s_idx).astype(F32)
    nd = [jnp.where(same(8), m, 0.0) for m in mats]
    n2 = [_pair_mm3([x], x)[0] for x in nd]
    t = [eye + x for x in nd]
    both = [_pair_mm3([a, b], a) for a, b in zip(n2, t)]
    t = [x + r[1] for x, r in zip(t, both)]
    t = [x + _pair_mm3([x], r[0])[0] for x, r in zip(t, both)]
    blk = 8
    while blk < CHUNK:
        sel = same(2 * blk) & jnp.logical_not(same(blk))
        w = [_pair_mm3([jnp.where(sel, m, 0.0)], x)[0] for m, x in zip(mats, t)]
        t = [x + _pair_mm3([x], y)[0] for x, y in zip(t, w)]
        blk *= 2
    return t


def _rwkv_kernel(x_ref, s0_ref, mu_ref, wr_ref, wk_ref, wv_ref, w0_ref, w1_ref, w2_ref,
                 a0_ref, a1_ref, a2_ref, g1_ref, g2_ref, kk_ref, ka_ref, rk_ref, lng_ref, lnb_ref,
                 y_ref, s_ref,
                 r_s, v_s, kk_s, cum_s, cx_s, kd_s, bb_s, g_s, bonus_s, y_s, sp_s, *, seq_len):
    total = x_ref.shape[0]
    n_seq = total // seq_len
    n_chunks = seq_len // CHUNK
    n_tiles = total // ROW_TILE
    hd = A_HEAD_DIM
    n_groups, width = sp_s.shape[2], sp_s.shape[4]
    per_group = width // hd
    x = x_ref[...]
    row = _iota(x.shape, 0) & (seq_len - 1)
    prev = jnp.where(row == 0, 0.0, pltpu.roll(x, 1, 0))
    nxt = jnp.where(row == seq_len - 1, 0.0, pltpu.roll(x, total - 1, 0))
    y_s[...] = 0.5 * (prev + nxt) - x
    head_ones = _same_block(A_WIDTH, hd).astype(BF16)
    chunk_diag = _same_block(SUM_TILE, CHUNK)
    run_sum = [(chunk_diag & _tri(d == 1, True, SUM_TILE)).astype(BF16) for d in range(2)]

    def project_tile(ti, carry):
        rows = pl.ds(pl.multiple_of(ti * ROW_TILE, ROW_TILE), ROW_TILE)
        x, xx = x_ref[rows, :], y_s[rows, :]
        xr, xw, xk, xv, xi, xg = (x + xx * mu_ref[j:j + 1, :] for j in range(6))
        r = _mm(xr, wr_ref[...])
        k = _mm(xk, wk_ref[...])
        v = _mm(xv, wv_ref[...])
        g_s[rows, :] = _mm(_sigmoid(_mm(xg, g1_ref[...])), g2_ref[...])
        kk = k * kk_ref[...]
        kk = kk / jnp.maximum(jnp.sqrt(_sum01(_split(kk * kk), head_ones)), 1e-12)
        r_s[rows, :] = r
        v_s[rows, :] = v
        kk_s[rows, :] = kk
        kd_sum = jnp.zeros_like(x)
        for d in range(2):
            zw = w0_ref[d:d + 1, :] + _mm(jnp.tanh(_mm(xw, w1_ref[d])), w2_ref[d])
            lw = -jnp.exp(-_softplus(-zw) - 0.5)
            cum = jnp.concatenate([_sum01(run_sum[d], _split3(lw[j:j + SUM_TILE]))
                                   for j in range(0, ROW_TILE, SUM_TILE)], axis=0)
            cum_s[d, rows, :] = cum
            cx_s[d, rows, :] = cum - lw
            a = _sigmoid(a0_ref[d:d + 1, :] + _mm(_mm(xi, a1_ref[d]), a2_ref[d]))
            kd = k * (1.0 + (a - 1.0) * ka_ref[...])
            kd_s[d, rows, :] = kd
            bb_s[d, rows, :] = kk * a
            kd_sum = kd_sum + kd
        bonus_s[rows, :] = _sum01(_split(r * kd_sum * rk_ref[...]), head_ones) * v
        y_s[rows, :] = jnp.zeros_like(x)
        return carry

    lax.fori_loop(0, n_tiles, project_tile, 0)
    for n in range(n_seq):
        for d in range(2):
            for p in range(n_groups):
                sp_s[n, d, p] = (jnp.zeros((hd, width), F32) if s0_ref is None else jnp.concatenate(
                    [s0_ref[n, d, per_group * p + j] for j in range(per_group)], axis=1))

    t_idx, s_idx = _pair_index((CHUNK, width))
    strict = [s_idx < t_idx, s_idx > t_idx]
    incl = [s_idx <= t_idx, s_idx >= t_idx]
    lane_head = _head_of_lane((CHUNK, width))

    def chunk_body(ci, carry):
        chains = []
        for n, d in [(n, d) for n in range(n_seq) for d in range(2)]:
            cpos = n * n_chunks + ((n_chunks - 1 - ci) if d == 1 else ci)
            last = 0 if d == 1 else CHUNK - 1
            rows = pl.ds(pl.multiple_of(cpos * CHUNK, CHUNK), CHUNK)
            cum, cx = cum_s[d, rows, :], cx_s[d, rows, :]
            tot = cum[last:last + 1, :]
            kdc, bbc = kd_s[d, rows, :], bb_s[d, rows, :]
            e_neg = jnp.exp(-cum)
            e_tail = jnp.exp(tot - cum)
            at = -kk_s[rows, :] * jnp.exp(cx)
            rt = r_s[rows, :] * jnp.exp(cum)
            bt, kt = bbc * e_neg, kdc * e_neg
            bp, kp = bbc * e_tail, kdc * e_tail
            pc = jnp.exp(tot)
            vc = v_s[rows, :]
            for p in range(n_groups):
                ln = slice(p * width, (p + 1) * width)
                chains.append(dict(d=d, st=(n, d, p), rows=rows, ln=ln, v=vc[:, ln], pc=pc[:, ln],
                                   lhs=jnp.concatenate([at[:, ln], rt[:, ln]], axis=0),
                                   bt=bt[:, ln], kt=kt[:, ln], bp=bp[:, ln], kp=kp[:, ln]))
        for c in chains:
            c["sb"] = _pair_mm3([c["lhs"]], c["bt"], _NT)[0]
            c["sk"] = _pair_mm3([c["lhs"]], c["kt"], _NT)[0]
            c["s"] = sp_s[c["st"]]
            c["fs"] = _mm(c["lhs"], _pair_diag(c["s"]).T)
            c["v_diag"] = _pair_diag(c["v"].astype(BF16))
        for c in chains:
            a_ak = jnp.where(strict[c["d"]], c["sk"][:CHUNK], 0.0)
            c["x"] = c["fs"][:CHUNK] + _mm(a_ak, c["v_diag"])
        inv = _pair_tri_inverses([jnp.where(strict[c["d"]], c["sb"][:CHUNK], 0.0) for c in chains])
        for c, t in zip(chains, inv):
            c["u"] = _pair_mm3([t], c["x"])[0]
        for c in chains:
            d, rows, ln, u = c["d"], c["rows"], c["ln"], c["u"]
            a_r = jnp.concatenate([jnp.where(incl[d], c["sb"][CHUNK:], 0.0),
                                   jnp.where(incl[d], c["sk"][CHUNK:], 0.0)], axis=1)
            uv_diag = jnp.concatenate([_pair_diag(u.astype(BF16)), c["v_diag"]], axis=0)
            y_s[rows, ln] = y_s[rows, ln] + c["fs"][CHUNK:] + _mm(a_r, uv_diag)
            g = _mm_tn(jnp.concatenate([u, c["v"]], axis=0), jnp.concatenate([c["bp"], c["kp"]], axis=0))
            own = g[0:hd]
            for j in range(1, per_group):
                own = jnp.where(lane_head == j, g[j * hd:(j + 1) * hd], own)
            sp_s[c["st"]] = c["s"] * c["pc"] + own
        return carry

    lax.fori_loop(0, n_chunks, chunk_body, 0)
    for n in range(n_seq):
        for d in range(2):
            for p in range(n_groups):
                for j in range(per_group):
                    s_ref[n, d, per_group * p + j] = sp_s[n, d, p, :, j * hd:(j + 1) * hd]

    def finish_tile(ti, carry):
        rows = pl.ds(pl.multiple_of(ti * ROW_TILE, ROW_TILE), ROW_TILE)
        ys = y_s[rows, :]
        cen = ys - _sum01(_split(ys), head_ones) * (1.0 / hd)
        var = _sum01(_split(cen * cen), head_ones) * (1.0 / hd)
        yn = cen * lax.rsqrt(var + RWKV_GN_EPS) * lng_ref[...] + lnb_ref[...]
        y_ref[rows, :] = (yn + bonus_s[rows, :]) * g_s[rows, :]
        return carry

    lax.fori_loop(0, n_tiles, finish_tile, 0)


def _rwkv(xa, s0, p, n_seq):
    b, t, w = xa.shape
    assert t & (t - 1) == 0 and b % n_seq == 0
    names = ("mu", "wr", "wk", "wv", "w0", "w1", "w2", "a0", "a1", "a2", "g1", "g2",
             "k_k", "k_a", "r_k", "ln_g", "ln_b")
    weights = [p["rwkv_" + n] for n in names]
    rows = n_seq * t
    seq = pl.BlockSpec((None, rows, w), lambda i: (i, 0, 0))
    st = pl.BlockSpec((n_seq, 2, A_HEADS, A_HEAD_DIM, A_HEAD_DIM), lambda i: (i, 0, 0, 0, 0))
    tw = lambda lead=(): pltpu.VMEM(lead + (rows, w), F32)
    kernel = functools.partial(_rwkv_kernel, seq_len=t)
    if s0 is None:
        kernel, state_specs, states = _zero_state_kernel(kernel, 1, 1), [], ()
    else:
        state_specs, states = [st], (s0,)
    y, s_out = pl.pallas_call(
        kernel,
        grid=(b // n_seq,),
        in_specs=[seq] + state_specs + [_full(x.shape) for x in weights],
        out_specs=[seq, st],
        out_shape=[jax.ShapeDtypeStruct((b // n_seq, rows, w), F32),
                   jax.ShapeDtypeStruct((b, 2, A_HEADS, A_HEAD_DIM, A_HEAD_DIM), F32)],
        scratch_shapes=[tw(), tw(), tw(), tw((2,)), tw((2,)), tw((2,)), tw((2,)), tw(), tw(), tw(),
                        pltpu.VMEM((n_seq, 2, A_HEADS // SCAN_HEADS, A_HEAD_DIM, SCAN_HEADS * A_HEAD_DIM), F32)],
        compiler_params=_cparams(("arbitrary",)),
        name="rwkv",
    )(xa.reshape(b // n_seq, rows, w), *states, *weights)
    return y.reshape(b, t, w), s_out


def _mlstm_kernel(u_ref, op_ref, c0_ref, n0_ref, m0_ref, conv_ref, wq_ref, wk_ref, wv_ref,
                  wg_ref, bg_ref, wgt_ref, bgt_ref, lng_ref,
                  y_ref, c_ref, n_ref, m_ref,
                  pad_s, q_s, k_s, v_s, dd_s, cr_s, h_s, *, conv2d, seq_len):
    t_len = u_ref.shape[0]
    n_seq = t_len // seq_len
    n_chunks = seq_len // CHUNK
    hd = B_HEAD_DIM
    u = u_ref[...]
    pad_s[0:CONV_PAD, :] = jnp.zeros((CONV_PAD, B_WIDTH), F32)
    pad_s[CONV_PAD + t_len:CONV_PAD + t_len + CONV_PAD, :] = jnp.zeros((CONV_PAD, B_WIDTH), F32)
    pad_s[CONV_PAD:CONV_PAD + t_len, :] = u
    pos = _iota(u.shape, 0) & (seq_len - 1)
    col = pos & (GRID_W - 1)
    conv = jnp.zeros_like(u)
    for kh in range(3):
        if not conv2d and kh != 1:
            continue
        for kw in range(3):
            shift = (kh - 1) * GRID_W + (kw - 1)
            term = pad_s[CONV_PAD + shift:CONV_PAD + shift + t_len, :] * conv_ref[kh * 3 + kw:kh * 3 + kw + 1, :]
            if conv2d and kw != 1:
                src = col + (kw - 1)
                term = jnp.where((src >= 0) & (src < GRID_W), term, 0.0)
            if n_seq > 1 and shift != 0:
                term = jnp.where((pos + shift >= 0) & (pos + shift < seq_len), term, 0.0)
            conv = conv + term
    uc = _silu(conv)
    for h in range(B_HEADS):
        ln = slice(h * hd, (h + 1) * hd)
        q_s[:, ln] = _mm(uc[:, ln], wq_ref[h])
        k_s[:, ln] = _mm(uc[:, ln], wk_ref[h]) * (hd ** -0.5)
        v_s[:, ln] = _mm(u[:, ln], wv_ref[h])
    gcol = _mm(uc, wg_ref[...]) + bg_ref[...]
    gcol = jnp.where(_iota(gcol.shape, 1) >= 8, -_softplus(-gcol), gcol)
    grow = _mm_nt(wgt_ref[...], uc) + bgt_ref[:, 0:1]
    grow = jnp.where(_iota(grow.shape, 0) >= 8, -_softplus(-grow), grow)
    chunk_diag = _same_block(SUM_TILE, CHUNK)
    run_sum = [(chunk_diag & _tri(d == 1, True, SUM_TILE)).astype(BF16) for d in range(2)]
    per_tile = SUM_TILE // CHUNK
    for ti in range(t_len // SUM_TILE):
        sl = slice(ti * SUM_TILE, (ti + 1) * SUM_TILE)
        gt, rt = _split3(gcol[sl, :]), _split3(grow[:, sl])
        backward_lane = (_iota((SUM_TILE, LANES), 1) & 4) != 0
        cum_c = jnp.where(backward_lane, _sum01(run_sum[1], gt, _NN), _sum01(run_sum[0], gt, _NN))
        dd_s[sl, :] = pltpu.roll(cum_c, LANES - 8, 1) - gcol[sl, :]
        backward_row = (_iota((16, SUM_TILE), 0) & 4) != 0
        cum_r = jnp.where(backward_row, _sum01(rt, run_sum[1], _NT), _sum01(rt, run_sum[0], _NT))
        for c in range(per_tile):
            cr_s[ti * per_tile + c] = cum_r[:, c * CHUNK:(c + 1) * CHUNK]
    h_s[...] = jnp.zeros_like(u)
    for ref, init in ((c_ref, c0_ref), (n_ref, n0_ref), (m_ref, m0_ref)):
        ref[...] = jnp.zeros(ref.shape, F32) if init is None else init[...]
    row_i, col_i = _iota((CHUNK, CHUNK), 0), _iota((CHUNK, CHUNK), 1)
    incl_t = [row_i <= col_i, row_i >= col_i]
    eye = row_i == col_i
    pick = [(_iota((LANES, B_HEADS * LANES), 0)
             == d * B_HEADS + lax.shift_right_logical(_iota((LANES, B_HEADS * LANES), 1), LANES.bit_length() - 1)
             ).astype(BF16) for d in range(2)]
    spread = (lax.shift_right_logical(_iota((2 * CHUNK, 2 * LANES), 0), CHUNK.bit_length() - 1)
              == lax.shift_right_logical(_iota((2 * CHUNK, 2 * LANES), 1), LANES.bit_length() - 1)).astype(BF16)
    ones_bf = jnp.ones((CHUNK, LANES), BF16)

    def chunk_body(ci, carry):
        chains = []
        for n, d in [(n, d) for n in range(n_seq) for d in range(2)]:
            cpos = n * n_chunks + ((n_chunks - 1 - ci) if d == 1 else ci)
            last = 0 if d == 1 else CHUNK - 1
            rows = pl.ds(pl.multiple_of(cpos * CHUNK, CHUNK), CHUNK)
            cr = cr_s[cpos]
            d_col = _sum01(_split3(dd_s[rows, :]), pick[d])
            for h in range(B_HEADS):
                ln = slice(h * hd, (h + 1) * hd)
                lf = 8 + d * 4 + h
                chains.append(dict(d=d, st=(n, d, h), vec=(n, d, slice(h, h + 1)), rows=rows, ln=ln, last=last,
                                   b_row=cr[lf:lf + 1, :], d_col=d_col[:, ln],
                                   m=m_ref[n, d, h:h + 1, 0:1], q=q_s[rows, ln], k=k_s[rows, ln], v=v_s[rows, ln],
                                   c=c_ref[n, d, h], n=n_ref[n, d, h:h + 1, :]))
        for c in chains:
            c["kq"] = _mm(c["k"], c["q"].T)
            c["qc"] = _mm(c["q"], c["c"])
            c["qn"] = _mm(c["q"], jnp.broadcast_to(c["n"], (LANES, hd)).T)
            logw = jnp.where(incl_t[c["d"]], c["b_row"] - c["d_col"][:, 0:CHUNK], -jnp.inf)
            c["m_row"] = jnp.maximum(c["b_row"] + c["m"], jnp.max(logw, axis=0, keepdims=True))
            c["logw"] = logw
        for c in chains:
            inter_row = jnp.exp(c["b_row"] + c["m"] - c["m_row"])
            diag = jnp.concatenate([jnp.where(eye, c["m_row"], 0.0), jnp.where(eye, inter_row, 0.0)], axis=1)
            cols = _sum01(_split3(diag), spread)
            c["m_col"], c["inter"] = cols[:, 0:LANES], cols[:, LANES:2 * LANES]
            c["s"] = c["kq"] * jnp.exp(c["logw"] - c["m_row"])
        for c in chains:
            s_hi, s_lo = _split(c["s"])
            sv = lax.dot_general(s_hi, jnp.concatenate([c["v"].astype(BF16), ones_bf], axis=1),
                                 (_TN, ((), ())), preferred_element_type=F32)
            c["sv"] = sv[:, 0:hd]
            c["den"] = (c["inter"] * c["qn"] + sv[:, hd:hd + LANES]
                        + lax.dot_general(s_lo, ones_bf, (_TN, ((), ())), preferred_element_type=F32))
            last = c["last"]
            c["m_new"] = c["m_row"][:, last:last + 1]
            b_last = c["b_row"][:, last:last + 1]
            c["decay"] = jnp.exp(b_last + c["m"] - c["m_new"])
            c["kw"] = c["k"] * jnp.exp(b_last - c["m_new"] - c["d_col"])
        for c in chains:
            c["kv"] = _mm_tn(c["kw"], c["v"])
        for c in chains:
            rows, ln = c["rows"], c["ln"]
            num = c["inter"] * c["qc"] + c["sv"]
            h_s[rows, ln] = h_s[rows, ln] + num / jnp.maximum(jnp.abs(c["den"]), jnp.exp(-c["m_col"]))
            c_ref[c["st"]] = c["decay"] * c["c"] + c["kv"]
            n_ref[c["vec"]] = c["decay"] * c["n"] + jnp.sum(c["kw"], axis=0, keepdims=True)
            m_ref[c["vec"]] = jnp.broadcast_to(c["m_new"], (1, LANES))
        return carry

    lax.fori_loop(0, n_chunks, chunk_body, 0)

    for h in range(B_HEADS):
        ln = slice(h * hd, (h + 1) * hd)
        hh = h_s[:, ln]
        cen = hh - jnp.mean(hh, axis=1, keepdims=True)
        var = jnp.mean(cen * cen, axis=1, keepdims=True)
        y_ref[:, ln] = _sigmoid(op_ref[:, ln]) * (cen * lax.rsqrt(var + MLSTM_GN_EPS) * lng_ref[:, ln])


def _zero_state_kernel(kernel, n_data, n_state):
    def wrapped(*refs, **kw):
        return kernel(*refs[:n_data], *([None] * n_state), *refs[n_data:], **kw)
    return wrapped


def _mlstm(ub, o_pre, states, p, conv2d, n_seq):
    b, t, w = ub.shape
    assert t & (t - 1) == 0 and b % n_seq == 0
    names = ("conv", "wq", "wk", "wv", "wg", "bg", "wgt", "bgt", "ln_g")
    weights = [p["mlstm_" + n] for n in names]
    rows, steps = n_seq * t, b // n_seq
    seq = pl.BlockSpec((None, rows, w), lambda i: (i, 0, 0))
    cst = pl.BlockSpec((n_seq, 2, B_HEADS, B_HEAD_DIM, B_HEAD_DIM), lambda i: (i, 0, 0, 0, 0))
    vst = pl.BlockSpec((n_seq, 2, B_HEADS, LANES), lambda i: (i, 0, 0, 0))
    kernel = functools.partial(_mlstm_kernel, conv2d=conv2d, seq_len=t)
    if states is None:
        kernel, state_specs, states = _zero_state_kernel(kernel, 2, 3), [], ()
    else:
        state_specs = [cst, vst, vst]
    y, c_out, n_out, m_out = pl.pallas_call(
        kernel,
        grid=(steps,),
        in_specs=[seq, seq] + state_specs + [_full(x.shape) for x in weights],
        out_specs=[seq, cst, vst, vst],
        out_shape=[jax.ShapeDtypeStruct((steps, rows, w), F32),
                   jax.ShapeDtypeStruct((b, 2, B_HEADS, B_HEAD_DIM, B_HEAD_DIM), F32),
                   jax.ShapeDtypeStruct((b, 2, B_HEADS, LANES), F32),
                   jax.ShapeDtypeStruct((b, 2, B_HEADS, LANES), F32)],
        scratch_shapes=[pltpu.VMEM((rows + 2 * CONV_PAD, w), F32), pltpu.VMEM((rows, w), F32),
                        pltpu.VMEM((rows, w), F32), pltpu.VMEM((rows, w), F32),
                        pltpu.VMEM((rows, LANES), F32), pltpu.VMEM((rows // CHUNK, 16, CHUNK), F32),
                        pltpu.VMEM((rows, w), F32)],
        compiler_params=_cparams(("arbitrary",)),
        name="mlstm",
    )(ub.reshape(steps, rows, w), o_pre.reshape(steps, rows, w), *states, *weights)
    return y.reshape(b, t, w), c_out, n_out, m_out


def _first_lane_of_max(val, lane, valid):
    masked = jnp.where(valid, val, -jnp.inf)
    best = jnp.max(masked, axis=1, keepdims=True)
    idx = jnp.min(jnp.where(valid & (masked == best), lane, LANES), axis=1, keepdims=True)
    return best, idx


def _merge_kernel(x_ref, ya_ref, yb_ref, mod_ref, g1_ref, wga0_ref, wga1_ref, wgb0_ref, wgb1_ref,
                  wpa_ref, wpb_ref, wo_ref, g2_ref, wr_ref, br_ref, xc_ref):
    d = D_MODEL
    gate1 = mod_ref[:, 2 * d:3 * d]
    shift2, scale2 = mod_ref[:, 3 * d:4 * d], mod_ref[:, 4 * d:5 * d]
    h1 = _modulated_norm1(x_ref, mod_ref, g1_ref)
    pa, pb = _mm(ya_ref[...], wpa_ref[...]), _mm(yb_ref[...], wpb_ref[...])
    halves = []
    for j, (wga_ref, wgb_ref) in enumerate(((wga0_ref, wgb0_ref), (wga1_ref, wgb1_ref))):
        cols = slice(j * W_IN_BLOCK, (j + 1) * W_IN_BLOCK)
        halves.append(_sigmoid(_mm(h1, wga_ref[...])) * pa[:, cols] + _sigmoid(_mm(h1, wgb_ref[...])) * pb[:, cols])
    merged = jnp.concatenate(halves, axis=1)
    x1 = x_ref[...] + gate1 * _mm(merged, wo_ref[...])
    xc_ref[:, 0:d] = x1
    h2 = _rmsnorm(x1, g2_ref[...]) * (1.0 + scale2) + shift2
    logits = _mm(h2, wr_ref[...]) + br_ref[...]
    lane = _iota(logits.shape, 1)
    is_group = lane < N_GROUPS
    g_max, g_sel = _first_lane_of_max(logits, lane, is_group)
    g_w = 1.0 / jnp.sum(jnp.where(is_group, jnp.exp(logits - g_max), 0.0), axis=1, keepdims=True)
    expert = lane - N_GROUPS
    in_group = (expert >= 0) & (expert < N_EXPERTS) & (lax.shift_right_arithmetic(expert, EXPERTS_PER_GROUP.bit_length() - 1) == g_sel)
    e_max = jnp.max(jnp.where(in_group, logits, -jnp.inf), axis=1, keepdims=True)
    e_exp = jnp.where(in_group, jnp.exp(logits - e_max), 0.0)
    prob = e_exp / jnp.sum(e_exp, axis=1, keepdims=True)
    p1, i1 = _first_lane_of_max(prob, lane, in_group)
    p2, i2 = _first_lane_of_max(prob, lane, in_group & (lane != i1))
    denom = p1 + p2
    comb = jnp.where(lane == i1, g_w * p1 / denom, 0.0) + jnp.where(lane == i2, g_w * p2 / denom, 0.0)
    xc_ref[:, d:d + LANES] = jnp.where(lane == GROUP_LANE, g_sel.astype(F32), comb)


def _merge(x, ya, yb, mod, p, tm):
    b, t, d = x.shape
    per = t // tm
    assert d == 2 * W_IN_BLOCK
    tok = lambda n: pl.BlockSpec((None, tm, n), lambda i: (i // per, i % per, 0))
    gate_specs, gate_args = _w_in_blocks(p["w_in"], 3, 4)
    weights = [p["rwkv_w_proj"], p["mlstm_w_proj"], p["w_out"], p["norm2_g"], p["moe_w_router"], p["moe_b_router"]]
    return pl.pallas_call(
        _merge_kernel,
        grid=(b * per,),
        in_specs=[tok(d), tok(A_WIDTH), tok(B_WIDTH), pl.BlockSpec((None, 1, 6 * d), lambda i: (i // per, 0, 0)),
                  _full((1, d))] + gate_specs + [_full(w.shape) for w in weights],
        out_specs=tok(d + LANES),
        out_shape=jax.ShapeDtypeStruct((b, t, d + LANES), F32),
        compiler_params=_cparams(("arbitrary",)),
        name="merge",
    )(x, ya, yb, mod, p["norm1_g"], *gate_args, *weights)


def _route(gsel, tb):
    s, l = gsel.shape
    n_buckets = s * N_GROUPS
    max_tiles = s * (l // tb + N_GROUPS - 1)
    onehot = (gsel[..., None] == jnp.arange(N_GROUPS, dtype=jnp.int32)).astype(jnp.int32)
    rank = jnp.sum((jnp.cumsum(onehot, axis=1) - onehot) * onehot, axis=-1)
    n_tiles = ((jnp.sum(onehot, axis=1) + tb - 1) // tb).T.reshape(n_buckets)
    ends = jnp.cumsum(n_tiles)
    first_tile = jnp.sum(onehot * (ends - n_tiles).reshape(N_GROUPS, s).T[:, None, :], axis=-1)
    pos = first_tile * tb + rank
    tok = jnp.arange(s * l, dtype=jnp.int32)
    row_src = jnp.full((max_tiles * tb,), -1, jnp.int32).at[pos.reshape(-1)].set(tok)
    tile_bucket = jnp.sum(jnp.arange(max_tiles, dtype=jnp.int32)[:, None] >= ends[None, :], axis=1)
    tile_bucket = jnp.minimum(tile_bucket, n_buckets - 1).astype(jnp.int32)
    return row_src, tile_bucket % s, tile_bucket // s, ends[-1:].astype(jnp.int32)


def _moe_kernel(gsrc_ref, sdst_ref, seg_ref, grp_ref, used_ref,
                xc_hbm, mod_ref, g2_ref, w1_ref, w3_ref, w2_ref, gf_ref, dump_in_hbm,
                y_hbm, dump_hbm, gbuf, obuf, gsem, ssem, *, tb):
    d = D_MODEL
    q = pl.program_id(0)
    n_used = used_ref[0]

    def start_gather(tile, sl):
        def body(i, carry):
            for u in range(SUBLANES):
                idx = gsrc_ref[tile * tb + i * SUBLANES + u]
                pltpu.make_async_copy(xc_hbm.at[lax.shift_right_logical(idx, SUBLANES.bit_length() - 1), pl.ds(idx & (SUBLANES - 1), 1), :],
                                      gbuf.at[sl, i, pl.ds(u, 1), :], gsem.at[sl]).start()
            return carry

        lax.fori_loop(0, tb // SUBLANES, body, 0)

    def wait_gather(sl):
        pltpu.make_async_copy(xc_hbm.at[pl.ds(0, tb // SUBLANES)], gbuf.at[sl], gsem.at[sl]).wait()

    def start_scatter(tile, sl):
        def body(i, carry):
            for u in range(SUBLANES):
                idx = sdst_ref[tile * tb + i * SUBLANES + u]

                @pl.when(idx >= 0)
                def _():
                    pltpu.make_async_copy(
                        obuf.at[sl, i, pl.ds(u, 1), :],
                        y_hbm.at[lax.shift_right_logical(idx, SUBLANES.bit_length() - 1), pl.ds(idx & (SUBLANES - 1), 1), :],
                        ssem.at[sl]).start()

                @pl.when(idx < 0)
                def _():
                    pltpu.make_async_copy(obuf.at[sl, i, pl.ds(u, 1), :], dump_hbm.at[sl, i, pl.ds(u, 1), :],
                                          ssem.at[sl]).start()

            return carry

        lax.fori_loop(0, tb // SUBLANES, body, 0)

    def wait_scatter(sl):
        pltpu.make_async_copy(obuf.at[sl], dump_hbm.at[sl], ssem.at[sl]).wait()

    def gather_rows(tile, sl, lo, hi):
        for r in range(lo, hi):
            idx = gsrc_ref[tile * tb + r]
            pltpu.make_async_copy(xc_hbm.at[lax.shift_right_logical(idx, SUBLANES.bit_length() - 1),
                                            pl.ds(idx & (SUBLANES - 1), 1), :],
                                  gbuf.at[sl, r // SUBLANES, pl.ds(r % SUBLANES, 1), :], gsem.at[sl]).start()

    def scatter_rows(tile, sl, lo, hi, enabled):
        for r in range(lo, hi):
            idx = sdst_ref[tile * tb + r]
            src = obuf.at[sl, r // SUBLANES, pl.ds(r % SUBLANES, 1), :]

            @pl.when(enabled & (idx >= 0))
            def _():
                pltpu.make_async_copy(src, y_hbm.at[lax.shift_right_logical(idx, SUBLANES.bit_length() - 1),
                                                    pl.ds(idx & (SUBLANES - 1), 1), :], ssem.at[sl]).start()

            @pl.when(enabled & (idx < 0))
            def _():
                pltpu.make_async_copy(src, dump_hbm.at[sl, r // SUBLANES, pl.ds(r % SUBLANES, 1), :],
                                      ssem.at[sl]).start()

    @pl.when(q < n_used)
    def _():
        slot = lax.rem(q, RING)
        ahead = lax.rem(q + 2, RING)
        behind = lax.rem(q + 1, RING)

        @pl.when(q == 0)
        def _():
            start_gather(0, 0)

            @pl.when(n_used > 1)
            def _():
                start_gather(1, 1)

        wait_gather(slot)

        @pl.when(q >= RING)
        def _():
            wait_scatter(slot)

        nxt = jnp.minimum(q + 2, n_used - 1)
        prev, has_prev = jnp.maximum(q - 1, 0), q >= 1
        part = tb // EXPERTS_PER_GROUP
        rows = gbuf[slot].reshape(tb, gbuf.shape[-1])
        x1, comb = rows[:, 0:d], rows[:, d:d + LANES]
        shift2, scale2, gate2 = mod_ref[:, 3 * d:4 * d], mod_ref[:, 4 * d:5 * d], mod_ref[:, 5 * d:6 * d]
        h2 = (_rmsnorm(x1, g2_ref[...]) * (1.0 + scale2) + shift2).astype(BF16)
        first_lane = N_GROUPS + grp_ref[q] * EXPERTS_PER_GROUP
        lane = _iota(comb.shape, 1)
        acc = jnp.zeros((tb, d), F32)
        for e in range(EXPERTS_PER_GROUP):
            w_e = jnp.sum(jnp.where(lane == first_lane + e, comb, 0.0), axis=1, keepdims=True)
            a = _mm(h2, w1_ref[e])
            b = _mm(h2, w3_ref[e])
            acc = acc + _mm(_silu(a) * b * w_e, w2_ref[e])
            gather_rows(nxt, ahead, e * part, (e + 1) * part)
            scatter_rows(prev, ahead, e * part, (e + 1) * part, has_prev)
        obuf[slot] = _rmsnorm(x1 + gate2 * acc, gf_ref[...]).reshape(tb // SUBLANES, SUBLANES, d)

        @pl.when(q == n_used - 1)
        def _():
            start_scatter(q, slot)
            wait_scatter(slot)
            wait_gather(ahead)

            @pl.when(q >= 1)
            def _():
                wait_scatter(ahead)
                wait_gather(behind)

            @pl.when(q >= 2)
            def _():
                wait_scatter(behind)


def _moe(xc, mod, p, tb):
    s, l, width = xc.shape
    d = D_MODEL
    gsel = xc[:, :, d + GROUP_LANE].astype(jnp.int32)
    row_src, tile_seg, tile_grp, n_used = _route(gsel, tb)
    max_tiles = tile_seg.shape[0]
    grouped = lambda w: w.reshape((N_GROUPS, EXPERTS_PER_GROUP) + w.shape[1:])
    w1, w3, w2 = grouped(p["moe_w1"]), grouped(p["moe_w3"]), grouped(p["moe_w2"])
    const = lambda shape: pl.BlockSpec(shape, lambda q, *_: (0,) * len(shape))
    by_group = lambda w: pl.BlockSpec((None,) + w.shape[1:], lambda q, gs, sd, seg, grp, used: (grp[q], 0, 0, 0))
    hbm = pl.BlockSpec(memory_space=pl.ANY)
    y, _ = pl.pallas_call(
        functools.partial(_moe_kernel, tb=tb),
        grid_spec=pltpu.PrefetchScalarGridSpec(
            num_scalar_prefetch=5,
            grid=(max_tiles,),
            in_specs=[hbm, pl.BlockSpec((None, 1, 6 * d), lambda q, gs, sd, seg, grp, used: (seg[q], 0, 0)),
                      const((1, d)), by_group(w1), by_group(w3), by_group(w2), const((1, d)), hbm],
            out_specs=[hbm, hbm],
            scratch_shapes=[pltpu.VMEM((RING, tb // SUBLANES, SUBLANES, width), F32),
                            pltpu.VMEM((RING, tb // SUBLANES, SUBLANES, d), F32),
                            pltpu.SemaphoreType.DMA((RING,)), pltpu.SemaphoreType.DMA((RING,))]),
        out_shape=[jax.ShapeDtypeStruct((s * l // SUBLANES, SUBLANES, d), F32),
                   jax.ShapeDtypeStruct((RING, tb // SUBLANES, SUBLANES, d), F32)],
        input_output_aliases={12: 1},
        compiler_params=_cparams(("arbitrary",)),
        name="moe",
    )(jnp.maximum(row_src, 0), row_src, tile_seg, tile_grp, n_used,
      xc.reshape(s * l // SUBLANES, SUBLANES, width), mod, p["norm2_g"], w1, w3, w2, p["final_norm_g"],
      jnp.zeros((RING, tb // SUBLANES, SUBLANES, d), F32))
    return y.reshape(s, l, d)


def _trunk(x, mod_seg, states, conv2d, p, tm, tb, mixer_seqs):
    b, t, d = x.shape
    n_seg = mod_seg.shape[0]
    seg = lambda a: a.reshape(n_seg, b * t // n_seg, a.shape[-1])
    per_seq = lambda a: a.reshape(b, t, a.shape[-1])
    xa, ub, o_pre = (per_seq(a) for a in _in_proj(seg(x), mod_seg, p["norm1_g"], p["w_in"], tm))
    if states is None:
        s_rwkv, mlstm_states = None, None
    else:
        s_rwkv, s_c, s_n, s_m = states
        mlstm_states = (s_c, s_n, jnp.broadcast_to(s_m[..., None], s_m.shape + (LANES,)))
    ya, s_rwkv = _rwkv(xa, s_rwkv, p, mixer_seqs)
    yb, s_c, s_n, s_m = _mlstm(ub, o_pre, mlstm_states, p, conv2d, mixer_seqs)
    xc = _merge(seg(x), seg(ya), seg(yb), mod_seg, p, tm)
    y = _moe(xc, mod_seg, p, tb)
    return y.reshape(b, t, d), s_rwkv, s_c, s_n, s_m[..., 0]


def kernel(x_prompt, x_sample, c, c_ctx, state_rwkv, state_mlstm_C, state_mlstm_n, state_mlstm_m, w_mod, b_mod, norm1_g, norm2_g, w_in, rwkv_mu, rwkv_w_r, rwkv_w_k, rwkv_w_v, rwkv_w0, rwkv_w1, rwkv_w2, rwkv_a0, rwkv_a1, rwkv_a2, rwkv_g1, rwkv_g2, rwkv_k_k, rwkv_k_a, rwkv_r_k, rwkv_ln_g, rwkv_ln_b, rwkv_w_proj, mlstm_conv, mlstm_w_q, mlstm_w_k, mlstm_w_v, mlstm_w_i, mlstm_b_i, mlstm_w_f, mlstm_b_f, mlstm_ln_g, mlstm_w_proj, w_out, moe_w_group, moe_b_group, moe_w_expert, moe_b_expert, moe_w1, moe_w3, moe_w2, final_norm_g):
    assert w_mod.shape[0] == 1, "single trunk layer"
    l = 0
    bp, dec = x_prompt.shape[0], x_sample.shape[0]
    bf = lambda w: w.astype(BF16)
    row = lambda w: w.reshape(1, -1).astype(F32)
    wg = jnp.concatenate([mlstm_w_i[l, 0], mlstm_w_i[l, 1], mlstm_w_f[l, 0], mlstm_w_f[l, 1]], axis=1)
    bg = jnp.concatenate([mlstm_b_i[l, 0], mlstm_b_i[l, 1], mlstm_b_f[l, 0], mlstm_b_f[l, 1]])
    wg = jnp.pad(wg, ((0, 0), (0, LANES - wg.shape[1])))
    bg = jnp.pad(bg, (0, LANES - bg.shape[0]))
    w_router = jnp.pad(jnp.concatenate([moe_w_group[l], moe_w_expert[l]], axis=1),
                       ((0, 0), (0, LANES - N_GROUPS - N_EXPERTS)))
    b_router = jnp.pad(jnp.concatenate([moe_b_group[l], moe_b_expert[l]]), (0, LANES - N_GROUPS - N_EXPERTS))
    p = {
        "norm1_g": row(norm1_g[l]), "norm2_g": row(norm2_g[l]), "w_in": w_in[l],
        "rwkv_mu": rwkv_mu[l], "rwkv_wr": bf(rwkv_w_r[l]), "rwkv_wk": bf(rwkv_w_k[l]), "rwkv_wv": bf(rwkv_w_v[l]),
        "rwkv_w0": rwkv_w0[l], "rwkv_w1": bf(rwkv_w1[l]), "rwkv_w2": bf(rwkv_w2[l]),
        "rwkv_a0": rwkv_a0[l], "rwkv_a1": bf(rwkv_a1[l]), "rwkv_a2": bf(rwkv_a2[l]),
        "rwkv_g1": bf(rwkv_g1[l]), "rwkv_g2": bf(rwkv_g2[l]),
        "rwkv_k_k": row(rwkv_k_k[l]), "rwkv_k_a": row(rwkv_k_a[l]), "rwkv_r_k": row(rwkv_r_k[l]),
        "rwkv_ln_g": row(rwkv_ln_g[l]), "rwkv_ln_b": row(rwkv_ln_b[l]),
        "rwkv_w_proj": rwkv_w_proj[l],
        "mlstm_conv": mlstm_conv[l].reshape(9, B_WIDTH),
        "mlstm_wq": bf(mlstm_w_q[l]), "mlstm_wk": bf(mlstm_w_k[l]), "mlstm_wv": bf(mlstm_w_v[l]),
        "mlstm_wg": bf(wg), "mlstm_bg": row(bg),
        "mlstm_wgt": bf(wg[:, :16].T), "mlstm_bgt": jnp.broadcast_to(bg[:16, None], (16, LANES)),
        "mlstm_ln_g": row(mlstm_ln_g[l]), "mlstm_w_proj": mlstm_w_proj[l], "w_out": w_out[l],
        "moe_w_router": bf(w_router), "moe_b_router": row(b_router),
        "moe_w1": moe_w1[l], "moe_w3": moe_w3[l], "moe_w2": moe_w2[l],
        "final_norm_g": row(final_norm_g),
    }
    cvec = jnp.concatenate([c_ctx[None, :], c, jnp.zeros((8 - 1 - dec, D_MODEL), F32)], axis=0)
    mod = _mod(cvec, w_mod[l], b_mod[l].reshape(1, -1))
    mod_ctx = mod[0:1][:, None, :]
    mod_lat = mod[1:1 + dec][:, None, :]

    yp, n_rwkv, n_c, n_n, n_m = _trunk(x_prompt, mod_ctx, None, False, p, 512, 512, 2)
    lat_states = tuple(s[:, l].astype(F32) for s in (state_rwkv, state_mlstm_C, state_mlstm_n, state_mlstm_m))
    ys, _, _, _, _ = _trunk(x_sample, mod_lat, lat_states, True, p, 512, 256, 1)
    dt = x_prompt.dtype
    return (yp, ys, n_rwkv[:, None].astype(dt), n_c[:, None].astype(dt), n_n[:, None].astype(dt),
            n_m[:, None].astype(dt))
```

```python
import functools

import jax
import jax.numpy as jnp
from jax import lax
from jax.experimental import pallas as pl
from jax.experimental.pallas import tpu as pltpu

F32 = jnp.float32
BF16 = jnp.bfloat16

D_MODEL = 1024
GRID_W = 64
A_HEAD_DIM = 64
A_WIDTH = D_MODEL // 2
A_HEADS = A_WIDTH // A_HEAD_DIM
B_HEAD_DIM = 128
B_WIDTH = D_MODEL // 2
B_HEADS = B_WIDTH // B_HEAD_DIM
CHUNK = 64
SCAN_HEADS = 2
RING = 3
GATHER_PRIORITY = 1
ROW_TILE = 512
SUM_TILE = 256
N_GROUPS = 4
EXPERTS_PER_GROUP = 4
N_EXPERTS = N_GROUPS * EXPERTS_PER_GROUP
GROUP_LANE = N_GROUPS + N_EXPERTS
NORM_EPS = 1e-6
RWKV_GN_EPS = 64e-5
MLSTM_GN_EPS = 1e-5
LANES = 128
SUBLANES = 8
CONV_PAD = 72
VMEM_LIMIT = 56 * 1024 * 1024


def _mm(a, b):
    return jnp.dot(a.astype(BF16), b.astype(BF16), preferred_element_type=F32)


_NN, _NT, _TN = ((1,), (0,)), ((1,), (1,)), ((0,), (0,))


def _split(a):
    hi = a.astype(BF16)
    return hi, (a - hi.astype(F32)).astype(BF16)


def _split3(a):
    hi = a.astype(BF16)
    rest = a - hi.astype(F32)
    lo = rest.astype(BF16)
    return hi, lo, (rest - lo.astype(F32)).astype(BF16)


def _sum01(a, b, dims=_NN):
    dot = lambda p, q: lax.dot_general(p, q, (dims, ((), ())), preferred_element_type=F32)
    terms = [dot(x, b) for x in a] if isinstance(a, tuple) else [dot(a, x) for x in b]
    return functools.reduce(lambda p, q: p + q, terms)


def _mm_nt(a, b):
    return lax.dot_general(a.astype(BF16), b.astype(BF16), (((1,), (1,)), ((), ())),
                           preferred_element_type=F32)


def _mm_tn(a, b):
    return lax.dot_general(a.astype(BF16), b.astype(BF16), (((0,), (0,)), ((), ())),
                           preferred_element_type=F32)


def _sigmoid(x):
    return 1.0 / (1.0 + jnp.exp(-x))


def _silu(x):
    return x * _sigmoid(x)


def _softplus(x):
    return jnp.maximum(x, 0.0) + jnp.log(1.0 + jnp.exp(-jnp.abs(x)))


def _iota(shape, dim):
    return lax.broadcasted_iota(jnp.int32, shape, dim)


def _tri(reverse, inclusive, n=CHUNK):
    row, col = _iota((n, n), 0), _iota((n, n), 1)
    if reverse:
        return (col >= row) if inclusive else (col > row)
    return (col <= row) if inclusive else (col < row)


def _same_block(n, blk):
    sh = blk.bit_length() - 1
    row, col = _iota((n, n), 0), _iota((n, n), 1)
    return lax.shift_right_logical(row, sh) == lax.shift_right_logical(col, sh)


def _rmsnorm(x, g):
    return x * lax.rsqrt(jnp.mean(x * x, axis=-1, keepdims=True) + NORM_EPS) * g


def _cparams(sem):
    return pltpu.CompilerParams(dimension_semantics=sem, vmem_limit_bytes=VMEM_LIMIT)


def _full(shape):
    nd = len(shape)
    return pl.BlockSpec(shape, lambda *_: (0,) * nd)


def _mod_kernel(c_ref, w_ref, b_ref, o_ref):
    o_ref[...] = _mm(_silu(c_ref[...]), w_ref[...]) + b_ref[...]


def _mod(cvec, w_mod, b_mod):
    rows, d = cvec.shape
    n = w_mod.shape[1]
    tn = 1536
    return pl.pallas_call(
        _mod_kernel,
        grid=(n // tn,),
        in_specs=[_full((rows, d)), pl.BlockSpec((d, tn), lambda j: (0, j)),
                  pl.BlockSpec((1, tn), lambda j: (0, j))],
        out_specs=pl.BlockSpec((rows, tn), lambda j: (0, j)),
        out_shape=jax.ShapeDtypeStruct((rows, n), F32),
        compiler_params=_cparams(("arbitrary",)),
        name="mod",
    )(cvec, w_mod, b_mod)


W_IN_BLOCK = 512


def _modulated_norm1(x_ref, mod_ref, g_ref):
    d = D_MODEL
    shift, scale = mod_ref[:, 0:d], mod_ref[:, d:2 * d]
    return (_rmsnorm(x_ref[...], g_ref[...]) * (1.0 + scale) + shift).astype(BF16)


def _w_in_blocks(w_in, first, count):
    specs = [pl.BlockSpec((w_in.shape[0], W_IN_BLOCK), lambda *_, j=first + k: (0, j)) for k in range(count)]
    return specs, [w_in] * count


def _in_kernel(x_ref, mod_ref, g_ref, wa_ref, wu_ref, wo_ref, xa_ref, ub_ref, op_ref):
    h = _modulated_norm1(x_ref, mod_ref, g_ref)
    for ref, w_ref in ((xa_ref, wa_ref), (ub_ref, wu_ref), (op_ref, wo_ref)):
        ref[...] = _mm(h, w_ref[...])


def _in_proj(x, mod, g1, w_in, tm):
    b, t, d = x.shape
    per = t // tm
    widths = (A_WIDTH, B_WIDTH, B_WIDTH)
    assert all(n == W_IN_BLOCK for n in widths)
    tok = lambda n: pl.BlockSpec((None, tm, n), lambda i: (i // per, i % per, 0))
    w_specs, w_args = _w_in_blocks(w_in, 0, 3)
    return pl.pallas_call(
        _in_kernel,
        grid=(b * per,),
        in_specs=[tok(d), pl.BlockSpec((None, 1, 6 * d), lambda i: (i // per, 0, 0)), _full((1, d))] + w_specs,
        out_specs=[tok(n) for n in widths],
        out_shape=[jax.ShapeDtypeStruct((b, t, n), F32) for n in widths],
        compiler_params=_cparams(("arbitrary",)),
        name="in_proj",
    )(x, mod, g1, *w_args)


def _head_of_lane(shape):
    return lax.shift_right_logical(_iota(shape, 1), A_HEAD_DIM.bit_length() - 1)


def _pair_diag(x):
    head = _head_of_lane(x.shape)
    return jnp.concatenate([x * (head == h).astype(x.dtype) for h in range(x.shape[1] // A_HEAD_DIM)], axis=0)


def _pair_mm3(lhs, rhs, dims=_NN):
    rows = lhs[0].shape[0]
    if dims == _NT:
        wh, wl = _split(_pair_diag(rhs).T)
    else:
        rh, rl = _split(rhs)
        wh, wl = _pair_diag(rh), _pair_diag(rl)
    parts = [_split(l) for l in lhs]
    top = jnp.dot(jnp.concatenate([t for part in parts for t in part], axis=0), wh, preferred_element_type=F32)
    low = jnp.dot(jnp.concatenate([part[0] for part in parts], axis=0), wl, preferred_element_type=F32)
    return [top[2 * k * rows:(2 * k + 1) * rows] + top[(2 * k + 1) * rows:(2 * k + 2) * rows]
            + low[k * rows:(k + 1) * rows] for k in range(len(lhs))]


def _pair_index(shape):
    return _iota(shape, 0), _iota(shape, 1) & (A_HEAD_DIM - 1)


def _pair_tri_inverses(mats):
    t_idx, s_idx = _pair_index(mats[0].shape)
    same = lambda blk: (lax.shift_right_logical(t_idx, blk.bit_length() - 1)
                        == lax.shift_right_logical(s_idx, blk.bit_length() - 1))
    eye = (t_idx == s_idx).astype(F32)
    nd = [jnp.where(same(8), m, 0.0) for m in mats]
    n2 = [_pair_mm3([x], x)[0] for x in nd]
    t = [eye + x for x in nd]
    both = [_pair_mm3([a, b], a) for a, b in zip(n2, t)]
    t = [x + r[1] for x, r in zip(t, both)]
    t = [x + _pair_mm3([x], r[0])[0] for x, r in zip(t, both)]
    blk = 8
    while blk < CHUNK:
        sel = same(2 * blk) & jnp.logical_not(same(blk))
        w = [_pair_mm3([jnp.where(sel, m, 0.0)], x)[0] for m, x in zip(mats, t)]
        t = [x + _pair_mm3([x], y)[0] for x, y in zip(t, w)]
        blk *= 2
    return t


def _rwkv_kernel(x_ref, s0_ref, mu_ref, wr_ref, wk_ref, wv_ref, w0_ref, w1_ref, w2_ref,
                 a0_ref, a1_ref, a2_ref, g1_ref, g2_ref, kk_ref, ka_ref, rk_ref, lng_ref, lnb_ref,
                 y_ref, s_ref,
                 r_s, v_s, kk_s, cum_s, cx_s, kd_s, bb_s, g_s, bonus_s, y_s, sp_s, *, seq_len):
    total = x_ref.shape[0]
    n_seq = total // seq_len
    n_chunks = seq_len // CHUNK
    n_tiles = total // ROW_TILE
    hd = A_HEAD_DIM
    n_groups, width = sp_s.shape[2], sp_s.shape[4]
    per_group = width // hd
    x = x_ref[...]
    row = _iota(x.shape, 0) & (seq_len - 1)
    prev = jnp.where(row == 0, 0.0, pltpu.roll(x, 1, 0))
    nxt = jnp.where(row == seq_len - 1, 0.0, pltpu.roll(x, total - 1, 0))
    y_s[...] = 0.5 * (prev + nxt) - x
    head_ones = _same_block(A_WIDTH, hd).astype(BF16)
    chunk_diag = _same_block(SUM_TILE, CHUNK)
    run_sum = [(chunk_diag & _tri(d == 1, True, SUM_TILE)).astype(BF16) for d in range(2)]

    def project_tile(ti, carry):
        rows = pl.ds(pl.multiple_of(ti * ROW_TILE, ROW_TILE), ROW_TILE)
        x, xx = x_ref[rows, :], y_s[rows, :]
        xr, xw, xk, xv, xi, xg = (x + xx * mu_ref[j:j + 1, :] for j in range(6))
        r = _mm(xr, wr_ref[...])
        k = _mm(xk, wk_ref[...])
        v = _mm(xv, wv_ref[...])
        g_s[rows, :] = _mm(_sigmoid(_mm(xg, g1_ref[...])), g2_ref[...])
        kk = k * kk_ref[...]
        kk = kk / jnp.maximum(jnp.sqrt(_sum01(_split(kk * kk), head_ones)), 1e-12)
        r_s[rows, :] = r
        v_s[rows, :] = v
        kk_s[rows, :] = kk
        kd_sum = jnp.zeros_like(x)
        for d in range(2):
            zw = w0_ref[d:d + 1, :] + _mm(jnp.tanh(_mm(xw, w1_ref[d])), w2_ref[d])
            lw = -jnp.exp(-_softplus(-zw) - 0.5)
            cum = jnp.concatenate([_sum01(run_sum[d], _split3(lw[j:j + SUM_TILE]))
                                   for j in range(0, ROW_TILE, SUM_TILE)], axis=0)
            cum_s[d, rows, :] = cum
            cx_s[d, rows, :] = cum - lw
            a = _sigmoid(a0_ref[d:d + 1, :] + _mm(_mm(xi, a1_ref[d]), a2_ref[d]))
            kd = k * (1.0 + (a - 1.0) * ka_ref[...])
            kd_s[d, rows, :] = kd
            bb_s[d, rows, :] = kk * a
            kd_sum = kd_sum + kd
        bonus_s[rows, :] = _sum01(_split(r * kd_sum * rk_ref[...]), head_ones) * v
        y_s[rows, :] = jnp.zeros_like(x)
        return carry

    lax.fori_loop(0, n_tiles, project_tile, 0)
    for n in range(n_seq):
        for d in range(2):
            for p in range(n_groups):
                sp_s[n, d, p] = (jnp.zeros((hd, width), F32) if s0_ref is None else jnp.concatenate(
                    [s0_ref[n, d, per_group * p + j] for j in range(per_group)], axis=1))

    t_idx, s_idx = _pair_index((CHUNK, width))
    strict = [s_idx < t_idx, s_idx > t_idx]
    incl = [s_idx <= t_idx, s_idx >= t_idx]
    lane_head = _head_of_lane((CHUNK, width))

    def chunk_body(ci, carry):
        chains = []
        for n, d in [(n, d) for n in range(n_seq) for d in range(2)]:
            cpos = n * n_chunks + ((n_chunks - 1 - ci) if d == 1 else ci)
            last = 0 if d == 1 else CHUNK - 1
            rows = pl.ds(pl.multiple_of(cpos * CHUNK, CHUNK), CHUNK)
            cum, cx = cum_s[d, rows, :], cx_s[d, rows, :]
            tot = cum[last:last + 1, :]
            kdc, bbc = kd_s[d, rows, :], bb_s[d, rows, :]
            e_neg = jnp.exp(-cum)
            e_tail = jnp.exp(tot - cum)
            at = -kk_s[rows, :] * jnp.exp(cx)
            rt = r_s[rows, :] * jnp.exp(cum)
            bt, kt = bbc * e_neg, kdc * e_neg
            bp, kp = bbc * e_tail, kdc * e_tail
            pc = jnp.exp(tot)
            vc = v_s[rows, :]
            for p in range(n_groups):
                ln = slice(p * width, (p + 1) * width)
                chains.append(dict(d=d, st=(n, d, p), rows=rows, ln=ln, v=vc[:, ln], pc=pc[:, ln],
                                   lhs=jnp.concatenate([at[:, ln], rt[:, ln]], axis=0),
                                   bt=bt[:, ln], kt=kt[:, ln], bp=bp[:, ln], kp=kp[:, ln]))
        for c in chains:
            c["sb"] = _pair_mm3([c["lhs"]], c["bt"], _NT)[0]
            c["sk"] = _pair_mm3([c["lhs"]], c["kt"], _NT)[0]
            c["s"] = sp_s[c["st"]]
            c["fs"] = _mm(c["lhs"], _pair_diag(c["s"]).T)
            c["v_diag"] = _pair_diag(c["v"].astype(BF16))
        for c in chains:
            a_ak = jnp.where(strict[c["d"]], c["sk"][:CHUNK], 0.0)
            c["x"] = c["fs"][:CHUNK] + _mm(a_ak, c["v_diag"])
        inv = _pair_tri_inverses([jnp.where(strict[c["d"]], c["sb"][:CHUNK], 0.0) for c in chains])
        for c, t in zip(chains, inv):
            c["u"] = _pair_mm3([t], c["x"])[0]
        for c in chains:
            d, rows, ln, u = c["d"], c["rows"], c["ln"], c["u"]
            a_r = jnp.concatenate([jnp.where(incl[d], c["sb"][CHUNK:], 0.0),
                                   jnp.where(incl[d], c["sk"][CHUNK:], 0.0)], axis=1)
            uv_diag = jnp.concatenate([_pair_diag(u.astype(BF16)), c["v_diag"]], axis=0)
            y_s[rows, ln] = y_s[rows, ln] + c["fs"][CHUNK:] + _mm(a_r, uv_diag)
            g = _mm_tn(jnp.concatenate([u, c["v"]], axis=0), jnp.concatenate([c["bp"], c["kp"]], axis=0))
            own = g[0:hd]
            for j in range(1, per_group):
                own = jnp.where(lane_head == j, g[j * hd:(j + 1) * hd], own)
            sp_s[c["st"]] = c["s"] * c["pc"] + own
        return carry

    lax.fori_loop(0, n_chunks, chunk_body, 0)
    for n in range(n_seq):
        for d in range(2):
            for p in range(n_groups):
                for j in range(per_group):
                    s_ref[n, d, per_group * p + j] = sp_s[n, d, p, :, j * hd:(j + 1) * hd]

    def finish_tile(ti, carry):
        rows = pl.ds(pl.multiple_of(ti * ROW_TILE, ROW_TILE), ROW_TILE)
        ys = y_s[rows, :]
        cen = ys - _sum01(_split(ys), head_ones) * (1.0 / hd)
        var = _sum01(_split(cen * cen), head_ones) * (1.0 / hd)
        yn = cen * lax.rsqrt(var + RWKV_GN_EPS) * lng_ref[...] + lnb_ref[...]
        y_ref[rows, :] = (yn + bonus_s[rows, :]) * g_s[rows, :]
        return carry

    lax.fori_loop(0, n_tiles, finish_tile, 0)


def _rwkv(xa, s0, p, n_seq):
    b, t, w = xa.shape
    assert t & (t - 1) == 0 and b % n_seq == 0
    names = ("mu", "wr", "wk", "wv", "w0", "w1", "w2", "a0", "a1", "a2", "g1", "g2",
             "k_k", "k_a", "r_k", "ln_g", "ln_b")
    weights = [p["rwkv_" + n] for n in names]
    rows = n_seq * t
    seq = pl.BlockSpec((None, rows, w), lambda i: (i, 0, 0))
    st = pl.BlockSpec((n_seq, 2, A_HEADS, A_HEAD_DIM, A_HEAD_DIM), lambda i: (i, 0, 0, 0, 0))
    tw = lambda lead=(): pltpu.VMEM(lead + (rows, w), F32)
    kernel = functools.partial(_rwkv_kernel, seq_len=t)
    if s0 is None:
        kernel, state_specs, states = _zero_state_kernel(kernel, 1, 1), [], ()
    else:
        state_specs, states = [st], (s0,)
    y, s_out = pl.pallas_call(
        kernel,
        grid=(b // n_seq,),
        in_specs=[seq] + state_specs + [_full(x.shape) for x in weights],
        out_specs=[seq, st],
        out_shape=[jax.ShapeDtypeStruct((b // n_seq, rows, w), F32),
                   jax.ShapeDtypeStruct((b, 2, A_HEADS, A_HEAD_DIM, A_HEAD_DIM), F32)],
        scratch_shapes=[tw(), tw(), tw(), tw((2,)), tw((2,)), tw((2,)), tw((2,)), tw(), tw(), tw(),
                        pltpu.VMEM((n_seq, 2, A_HEADS // SCAN_HEADS, A_HEAD_DIM, SCAN_HEADS * A_HEAD_DIM), F32)],
        compiler_params=_cparams(("arbitrary",)),
        name="rwkv",
    )(xa.reshape(b // n_seq, rows, w), *states, *weights)
    return y.reshape(b, t, w), s_out


def _mlstm_kernel(u_ref, op_ref, c0_ref, n0_ref, m0_ref, conv_ref, wq_ref, wk_ref, wv_ref,
                  wg_ref, bg_ref, wgt_ref, bgt_ref, lng_ref,
                  y_ref, c_ref, n_ref, m_ref,
                  pad_s, q_s, k_s, v_s, dd_s, cr_s, h_s, *, conv2d, seq_len):
    t_len = u_ref.shape[0]
    n_seq = t_len // seq_len
    n_chunks = seq_len // CHUNK
    hd = B_HEAD_DIM
    u = u_ref[...]
    pad_s[0:CONV_PAD, :] = jnp.zeros((CONV_PAD, B_WIDTH), F32)
    pad_s[CONV_PAD + t_len:CONV_PAD + t_len + CONV_PAD, :] = jnp.zeros((CONV_PAD, B_WIDTH), F32)
    pad_s[CONV_PAD:CONV_PAD + t_len, :] = u
    pos = _iota(u.shape, 0) & (seq_len - 1)
    col = pos & (GRID_W - 1)
    conv = jnp.zeros_like(u)
    for kh in range(3):
        if not conv2d and kh != 1:
            continue
        for kw in range(3):
            shift = (kh - 1) * GRID_W + (kw - 1)
            term = pad_s[CONV_PAD + shift:CONV_PAD + shift + t_len, :] * conv_ref[kh * 3 + kw:kh * 3 + kw + 1, :]
            if conv2d and kw != 1:
                src = col + (kw - 1)
                term = jnp.where((src >= 0) & (src < GRID_W), term, 0.0)
            if n_seq > 1 and shift != 0:
                term = jnp.where((pos + shift >= 0) & (pos + shift < seq_len), term, 0.0)
            conv = conv + term
    uc = _silu(conv)
    for h in range(B_HEADS):
        ln = slice(h * hd, (h + 1) * hd)
        q_s[:, ln] = _mm(uc[:, ln], wq_ref[h])
        k_s[:, ln] = _mm(uc[:, ln], wk_ref[h]) * (hd ** -0.5)
        v_s[:, ln] = _mm(u[:, ln], wv_ref[h])
    gcol = _mm(uc, wg_ref[...]) + bg_ref[...]
    gcol = jnp.where(_iota(gcol.shape, 1) >= 8, -_softplus(-gcol), gcol)
    grow = _mm_nt(wgt_ref[...], uc) + bgt_ref[:, 0:1]
    grow = jnp.where(_iota(grow.shape, 0) >= 8, -_softplus(-grow), grow)
    chunk_diag = _same_block(SUM_TILE, CHUNK)
    run_sum = [(chunk_diag & _tri(d == 1, True, SUM_TILE)).astype(BF16) for d in range(2)]
    per_tile = SUM_TILE // CHUNK
    for ti in range(t_len // SUM_TILE):
        sl = slice(ti * SUM_TILE, (ti + 1) * SUM_TILE)
        gt, rt = _split3(gcol[sl, :]), _split3(grow[:, sl])
        backward_lane = (_iota((SUM_TILE, LANES), 1) & 4) != 0
        cum_c = jnp.where(backward_lane, _sum01(run_sum[1], gt, _NN), _sum01(run_sum[0], gt, _NN))
        dd_s[sl, :] = pltpu.roll(cum_c, LANES - 8, 1) - gcol[sl, :]
        backward_row = (_iota((16, SUM_TILE), 0) & 4) != 0
        cum_r = jnp.where(backward_row, _sum01(rt, run_sum[1], _NT), _sum01(rt, run_sum[0], _NT))
        for c in range(per_tile):
            cr_s[ti * per_tile + c] = cum_r[:, c * CHUNK:(c + 1) * CHUNK]
    h_s[...] = jnp.zeros_like(u)
    for ref, init in ((c_ref, c0_ref), (n_ref, n0_ref), (m_ref, m0_ref)):
        ref[...] = jnp.zeros(ref.shape, F32) if init is None else init[...]
    row_i, col_i = _iota((CHUNK, CHUNK), 0), _iota((CHUNK, CHUNK), 1)
    incl_t = [row_i <= col_i, row_i >= col_i]
    eye = row_i == col_i
    pick = [(_iota((LANES, B_HEADS * LANES), 0)
             == d * B_HEADS + lax.shift_right_logical(_iota((LANES, B_HEADS * LANES), 1), LANES.bit_length() - 1)
             ).astype(BF16) for d in range(2)]
    spread = (lax.shift_right_logical(_iota((2 * CHUNK, 2 * LANES), 0), CHUNK.bit_length() - 1)
              == lax.shift_right_logical(_iota((2 * CHUNK, 2 * LANES), 1), LANES.bit_length() - 1)).astype(BF16)
    ones_bf = jnp.ones((CHUNK, LANES), BF16)

    def chunk_body(ci, carry):
        chains = []
        for n, d in [(n, d) for n in range(n_seq) for d in range(2)]:
            cpos = n * n_chunks + ((n_chunks - 1 - ci) if d == 1 else ci)
            last = 0 if d == 1 else CHUNK - 1
            rows = pl.ds(pl.multiple_of(cpos * CHUNK, CHUNK), CHUNK)
            cr = cr_s[cpos]
            d_col = _sum01(_split3(dd_s[rows, :]), pick[d])
            for h in range(B_HEADS):
                ln = slice(h * hd, (h + 1) * hd)
                lf = 8 + d * 4 + h
                chains.append(dict(d=d, st=(n, d, h), vec=(n, d, slice(h, h + 1)), rows=rows, ln=ln, last=last,
                                   b_row=cr[lf:lf + 1, :], d_col=d_col[:, ln],
                                   m=m_ref[n, d, h:h + 1, 0:1], q=q_s[rows, ln], k=k_s[rows, ln], v=v_s[rows, ln],
                                   c=c_ref[n, d, h], n=n_ref[n, d, h:h + 1, :]))
        for c in chains:
            c["kq"] = _mm(c["k"], c["q"].T)
            c["qc"] = _mm(c["q"], c["c"])
            c["qn"] = _mm(c["q"], jnp.broadcast_to(c["n"], (LANES, hd)).T)
            logw = jnp.where(incl_t[c["d"]], c["b_row"] - c["d_col"][:, 0:CHUNK], -jnp.inf)
            c["m_row"] = jnp.maximum(c["b_row"] + c["m"], jnp.max(logw, axis=0, keepdims=True))
            c["logw"] = logw
        for c in chains:
            inter_row = jnp.exp(c["b_row"] + c["m"] - c["m_row"])
            diag = jnp.concatenate([jnp.where(eye, c["m_row"], 0.0), jnp.where(eye, inter_row, 0.0)], axis=1)
            cols = _sum01(_split3(diag), spread)
            c["m_col"], c["inter"] = cols[:, 0:LANES], cols[:, LANES:2 * LANES]
            c["s"] = c["kq"] * jnp.exp(c["logw"] - c["m_row"])
        for c in chains:
            s_hi, s_lo = _split(c["s"])
            sv = lax.dot_general(s_hi, jnp.concatenate([c["v"].astype(BF16), ones_bf], axis=1),
                                 (_TN, ((), ())), preferred_element_type=F32)
            c["sv"] = sv[:, 0:hd]
            c["den"] = (c["inter"] * c["qn"] + sv[:, hd:hd + LANES]
                        + lax.dot_general(s_lo, ones_bf, (_TN, ((), ())), preferred_element_type=F32))
            last = c["last"]
            c["m_new"] = c["m_row"][:, last:last + 1]
            b_last = c["b_row"][:, last:last + 1]
            c["decay"] = jnp.exp(b_last + c["m"] - c["m_new"])
            c["kw"] = c["k"] * jnp.exp(b_last - c["m_new"] - c["d_col"])
        for c in chains:
            c["kv"] = _mm_tn(c["kw"], c["v"])
        for c in chains:
            rows, ln = c["rows"], c["ln"]
            num = c["inter"] * c["qc"] + c["sv"]
            h_s[rows, ln] = h_s[rows, ln] + num / jnp.maximum(jnp.abs(c["den"]), jnp.exp(-c["m_col"]))
            c_ref[c["st"]] = c["decay"] * c["c"] + c["kv"]
            n_ref[c["vec"]] = c["decay"] * c["n"] + jnp.sum(c["kw"], axis=0, keepdims=True)
            m_ref[c["vec"]] = jnp.broadcast_to(c["m_new"], (1, LANES))
        return carry

    lax.fori_loop(0, n_chunks, chunk_body, 0)

    for h in range(B_HEADS):
        ln = slice(h * hd, (h + 1) * hd)
        hh = h_s[:, ln]
        cen = hh - jnp.mean(hh, axis=1, keepdims=True)
        var = jnp.mean(cen * cen, axis=1, keepdims=True)
        y_ref[:, ln] = _sigmoid(op_ref[:, ln]) * (cen * lax.rsqrt(var + MLSTM_GN_EPS) * lng_ref[:, ln])


def _zero_state_kernel(kernel, n_data, n_state):
    def wrapped(*refs, **kw):
        return kernel(*refs[:n_data], *([None] * n_state), *refs[n_data:], **kw)
    return wrapped


def _mlstm(ub, o_pre, states, p, conv2d, n_seq):
    b, t, w = ub.shape
    assert t & (t - 1) == 0 and b % n_seq == 0
    names = ("conv", "wq", "wk", "wv", "wg", "bg", "wgt", "bgt", "ln_g")
    weights = [p["mlstm_" + n] for n in names]
    rows, steps = n_seq * t, b // n_seq
    seq = pl.BlockSpec((None, rows, w), lambda i: (i, 0, 0))
    cst = pl.BlockSpec((n_seq, 2, B_HEADS, B_HEAD_DIM, B_HEAD_DIM), lambda i: (i, 0, 0, 0, 0))
    vst = pl.BlockSpec((n_seq, 2, B_HEADS, LANES), lambda i: (i, 0, 0, 0))
    kernel = functools.partial(_mlstm_kernel, conv2d=conv2d, seq_len=t)
    if states is None:
        kernel, state_specs, states = _zero_state_kernel(kernel, 2, 3), [], ()
    else:
        state_specs = [cst, vst, vst]
    y, c_out, n_out, m_out = pl.pallas_call(
        kernel,
        grid=(steps,),
        in_specs=[seq, seq] + state_specs + [_full(x.shape) for x in weights],
        out_specs=[seq, cst, vst, vst],
        out_shape=[jax.ShapeDtypeStruct((steps, rows, w), F32),
                   jax.ShapeDtypeStruct((b, 2, B_HEADS, B_HEAD_DIM, B_HEAD_DIM), F32),
                   jax.ShapeDtypeStruct((b, 2, B_HEADS, LANES), F32),
                   jax.ShapeDtypeStruct((b, 2, B_HEADS, LANES), F32)],
        scratch_shapes=[pltpu.VMEM((rows + 2 * CONV_PAD, w), F32), pltpu.VMEM((rows, w), F32),
                        pltpu.VMEM((rows, w), F32), pltpu.VMEM((rows, w), F32),
                        pltpu.VMEM((rows, LANES), F32), pltpu.VMEM((rows // CHUNK, 16, CHUNK), F32),
                        pltpu.VMEM((rows, w), F32)],
        compiler_params=_cparams(("arbitrary",)),
        name="mlstm",
    )(ub.reshape(steps, rows, w), o_pre.reshape(steps, rows, w), *states, *weights)
    return y.reshape(b, t, w), c_out, n_out, m_out


def _first_lane_of_max(val, lane, valid):
    masked = jnp.where(valid, val, -jnp.inf)
    best = jnp.max(masked, axis=1, keepdims=True)
    idx = jnp.min(jnp.where(valid & (masked == best), lane, LANES), axis=1, keepdims=True)
    return best, idx


def _merge_kernel(x_ref, ya_ref, yb_ref, mod_ref, g1_ref, wga0_ref, wga1_ref, wgb0_ref, wgb1_ref,
                  wpa_ref, wpb_ref, wo_ref, g2_ref, wr_ref, br_ref, xc_ref):
    d = D_MODEL
    gate1 = mod_ref[:, 2 * d:3 * d]
    shift2, scale2 = mod_ref[:, 3 * d:4 * d], mod_ref[:, 4 * d:5 * d]
    h1 = _modulated_norm1(x_ref, mod_ref, g1_ref)
    pa, pb = _mm(ya_ref[...], wpa_ref[...]), _mm(yb_ref[...], wpb_ref[...])
    halves = []
    for j, (wga_ref, wgb_ref) in enumerate(((wga0_ref, wgb0_ref), (wga1_ref, wgb1_ref))):
        cols = slice(j * W_IN_BLOCK, (j + 1) * W_IN_BLOCK)
        halves.append(_sigmoid(_mm(h1, wga_ref[...])) * pa[:, cols] + _sigmoid(_mm(h1, wgb_ref[...])) * pb[:, cols])
    merged = jnp.concatenate(halves, axis=1)
    x1 = x_ref[...] + gate1 * _mm(merged, wo_ref[...])
    xc_ref[:, 0:d] = x1
    h2 = _rmsnorm(x1, g2_ref[...]) * (1.0 + scale2) + shift2
    logits = _mm(h2, wr_ref[...]) + br_ref[...]
    lane = _iota(logits.shape, 1)
    is_group = lane < N_GROUPS
    g_max, g_sel = _first_lane_of_max(logits, lane, is_group)
    g_w = 1.0 / jnp.sum(jnp.where(is_group, jnp.exp(logits - g_max), 0.0), axis=1, keepdims=True)
    expert = lane - N_GROUPS
    in_group = (expert >= 0) & (expert < N_EXPERTS) & (lax.shift_right_arithmetic(expert, EXPERTS_PER_GROUP.bit_length() - 1) == g_sel)
    e_max = jnp.max(jnp.where(in_group, logits, -jnp.inf), axis=1, keepdims=True)
    e_exp = jnp.where(in_group, jnp.exp(logits - e_max), 0.0)
    prob = e_exp / jnp.sum(e_exp, axis=1, keepdims=True)
    p1, i1 = _first_lane_of_max(prob, lane, in_group)
    p2, i2 = _first_lane_of_max(prob, lane, in_group & (lane != i1))
    denom = p1 + p2
    comb = jnp.where(lane == i1, g_w * p1 / denom, 0.0) + jnp.where(lane == i2, g_w * p2 / denom, 0.0)
    xc_ref[:, d:d + LANES] = jnp.where(lane == GROUP_LANE, g_sel.astype(F32), comb)


def _merge(x, ya, yb, mod, p, tm):
    b, t, d = x.shape
    per = t // tm
    assert d == 2 * W_IN_BLOCK
    tok = lambda n: pl.BlockSpec((None, tm, n), lambda i: (i // per, i % per, 0))
    gate_specs, gate_args = _w_in_blocks(p["w_in"], 3, 4)
    weights = [p["rwkv_w_proj"], p["mlstm_w_proj"], p["w_out"], p["norm2_g"], p["moe_w_router"], p["moe_b_router"]]
    return pl.pallas_call(
        _merge_kernel,
        grid=(b * per,),
        in_specs=[tok(d), tok(A_WIDTH), tok(B_WIDTH), pl.BlockSpec((None, 1, 6 * d), lambda i: (i // per, 0, 0)),
                  _full((1, d))] + gate_specs + [_full(w.shape) for w in weights],
        out_specs=tok(d + LANES),
        out_shape=jax.ShapeDtypeStruct((b, t, d + LANES), F32),
        compiler_params=_cparams(("arbitrary",)),
        name="merge",
    )(x, ya, yb, mod, p["norm1_g"], *gate_args, *weights)


def _route(gsel, tb):
    s, l = gsel.shape
    n_buckets = s * N_GROUPS
    max_tiles = s * (l // tb + N_GROUPS - 1)
    onehot = (gsel[..., None] == jnp.arange(N_GROUPS, dtype=jnp.int32)).astype(jnp.int32)
    rank = jnp.sum((jnp.cumsum(onehot, axis=1) - onehot) * onehot, axis=-1)
    n_tiles = ((jnp.sum(onehot, axis=1) + tb - 1) // tb).T.reshape(n_buckets)
    ends = jnp.cumsum(n_tiles)
    first_tile = jnp.sum(onehot * (ends - n_tiles).reshape(N_GROUPS, s).T[:, None, :], axis=-1)
    pos = first_tile * tb + rank
    tok = jnp.arange(s * l, dtype=jnp.int32)
    row_src = jnp.full((max_tiles * tb,), -1, jnp.int32).at[pos.reshape(-1)].set(tok)
    tile_bucket = jnp.sum(jnp.arange(max_tiles, dtype=jnp.int32)[:, None] >= ends[None, :], axis=1)
    tile_bucket = jnp.minimum(tile_bucket, n_buckets - 1).astype(jnp.int32)
    return row_src, tile_bucket % s, tile_bucket // s, ends[-1:].astype(jnp.int32)


def _moe_kernel(gsrc_ref, sdst_ref, seg_ref, grp_ref, used_ref,
                xc_hbm, mod_ref, g2_ref, w1_ref, w3_ref, w2_ref, gf_ref, dump_in_hbm,
                y_hbm, dump_hbm, gbuf, obuf, gsem, ssem, *, tb):
    d = D_MODEL
    q = pl.program_id(0)
    n_used = used_ref[0]

    def start_gather(tile, sl):
        def body(i, carry):
            for u in range(SUBLANES):
                idx = gsrc_ref[tile * tb + i * SUBLANES + u]
                pltpu.make_async_copy(xc_hbm.at[lax.shift_right_logical(idx, SUBLANES.bit_length() - 1), pl.ds(idx & (SUBLANES - 1), 1), :],
                                      gbuf.at[sl, i, pl.ds(u, 1), :], gsem.at[sl]).start()
            return carry

        lax.fori_loop(0, tb // SUBLANES, body, 0)

    def wait_gather(sl):
        pltpu.make_async_copy(xc_hbm.at[pl.ds(0, tb // SUBLANES)], gbuf.at[sl], gsem.at[sl]).wait()

    def start_scatter(tile, sl):
        def body(i, carry):
            for u in range(SUBLANES):
                idx = sdst_ref[tile * tb + i * SUBLANES + u]

                @pl.when(idx >= 0)
                def _():
                    pltpu.make_async_copy(
                        obuf.at[sl, i, pl.ds(u, 1), :],
                        y_hbm.at[lax.shift_right_logical(idx, SUBLANES.bit_length() - 1), pl.ds(idx & (SUBLANES - 1), 1), :],
                        ssem.at[sl]).start()

                @pl.when(idx < 0)
                def _():
                    pltpu.make_async_copy(obuf.at[sl, i, pl.ds(u, 1), :], dump_hbm.at[sl, i, pl.ds(u, 1), :],
                                          ssem.at[sl]).start()

            return carry

        lax.fori_loop(0, tb // SUBLANES, body, 0)

    def wait_scatter(sl):
        pltpu.make_async_copy(obuf.at[sl], dump_hbm.at[sl], ssem.at[sl]).wait()

    def gather_rows(tile, sl, lo, hi):
        for r in range(lo, hi):
            idx = gsrc_ref[tile * tb + r]
            pltpu.make_async_copy(xc_hbm.at[lax.shift_right_logical(idx, SUBLANES.bit_length() - 1),
                                            pl.ds(idx & (SUBLANES - 1), 1), :],
                                  gbuf.at[sl, r // SUBLANES, pl.ds(r % SUBLANES, 1), :],
                                  gsem.at[sl]).start(priority=GATHER_PRIORITY)

    def scatter_rows(tile, sl, lo, hi, enabled):
        for r in range(lo, hi):
            idx = sdst_ref[tile * tb + r]
            src = obuf.at[sl, r // SUBLANES, pl.ds(r % SUBLANES, 1), :]

            @pl.when(enabled & (idx >= 0))
            def _():
                pltpu.make_async_copy(src, y_hbm.at[lax.shift_right_logical(idx, SUBLANES.bit_length() - 1),
                                                    pl.ds(idx & (SUBLANES - 1), 1), :],
                                      ssem.at[sl]).start(priority=r % 2)

            @pl.when(enabled & (idx < 0))
            def _():
                pltpu.make_async_copy(src, dump_hbm.at[sl, r // SUBLANES, pl.ds(r % SUBLANES, 1), :],
                                      ssem.at[sl]).start(priority=r % 2)

    @pl.when(q < n_used)
    def _():
        slot = lax.rem(q, RING)
        ahead = lax.rem(q + 2, RING)
        behind = lax.rem(q + 1, RING)

        @pl.when(q == 0)
        def _():
            start_gather(0, 0)

            @pl.when(n_used > 1)
            def _():
                start_gather(1, 1)

        wait_gather(slot)

        @pl.when(q >= RING)
        def _():
            wait_scatter(slot)

        nxt = jnp.minimum(q + 2, n_used - 1)
        prev, has_prev = jnp.maximum(q - 1, 0), q >= 1
        part = tb // EXPERTS_PER_GROUP
        rows = gbuf[slot].reshape(tb, gbuf.shape[-1])
        x1, comb = rows[:, 0:d], rows[:, d:d + LANES]
        shift2, scale2, gate2 = mod_ref[:, 3 * d:4 * d], mod_ref[:, 4 * d:5 * d], mod_ref[:, 5 * d:6 * d]
        h2 = (_rmsnorm(x1, g2_ref[...]) * (1.0 + scale2) + shift2).astype(BF16)
        first_lane = N_GROUPS + grp_ref[q] * EXPERTS_PER_GROUP
        lane = _iota(comb.shape, 1)
        acc = jnp.zeros((tb, d), F32)
        for e in range(EXPERTS_PER_GROUP):
            w_e = jnp.sum(jnp.where(lane == first_lane + e, comb, 0.0), axis=1, keepdims=True)
            a = _mm(h2, w1_ref[e])
            b = _mm(h2, w3_ref[e])
            acc = acc + _mm(_silu(a) * b * w_e, w2_ref[e])
            gather_rows(nxt, ahead, e * part, (e + 1) * part)
            scatter_rows(prev, ahead, e * part, (e + 1) * part, has_prev)
        obuf[slot] = _rmsnorm(x1 + gate2 * acc, gf_ref[...]).reshape(tb // SUBLANES, SUBLANES, d)

        @pl.when(q == n_used - 1)
        def _():
            start_scatter(q, slot)
            wait_scatter(slot)
            wait_gather(ahead)

            @pl.when(q >= 1)
            def _():
                wait_scatter(ahead)
                wait_gather(behind)

            @pl.when(q >= 2)
            def _():
                wait_scatter(behind)


def _moe(xc, mod, p, tb):
    s, l, width = xc.shape
    d = D_MODEL
    gsel = xc[:, :, d + GROUP_LANE].astype(jnp.int32)
    row_src, tile_seg, tile_grp, n_used = _route(gsel, tb)
    max_tiles = tile_seg.shape[0]
    grouped = lambda w: w.reshape((N_GROUPS, EXPERTS_PER_GROUP) + w.shape[1:])
    w1, w3, w2 = grouped(p["moe_w1"]), grouped(p["moe_w3"]), grouped(p["moe_w2"])
    const = lambda shape: pl.BlockSpec(shape, lambda q, *_: (0,) * len(shape))
    by_group = lambda w: pl.BlockSpec((None,) + w.shape[1:], lambda q, gs, sd, seg, grp, used: (grp[q], 0, 0, 0))
    hbm = pl.BlockSpec(memory_space=pl.ANY)
    y, _ = pl.pallas_call(
        functools.partial(_moe_kernel, tb=tb),
        grid_spec=pltpu.PrefetchScalarGridSpec(
            num_scalar_prefetch=5,
            grid=(max_tiles,),
            in_specs=[hbm, pl.BlockSpec((None, 1, 6 * d), lambda q, gs, sd, seg, grp, used: (seg[q], 0, 0)),
                      const((1, d)), by_group(w1), by_group(w3), by_group(w2), const((1, d)), hbm],
            out_specs=[hbm, hbm],
            scratch_shapes=[pltpu.VMEM((RING, tb // SUBLANES, SUBLANES, width), F32),
                            pltpu.VMEM((RING, tb // SUBLANES, SUBLANES, d), F32),
                            pltpu.SemaphoreType.DMA((RING,)), pltpu.SemaphoreType.DMA((RING,))]),
        out_shape=[jax.ShapeDtypeStruct((s * l // SUBLANES, SUBLANES, d), F32),
                   jax.ShapeDtypeStruct((RING, tb // SUBLANES, SUBLANES, d), F32)],
        input_output_aliases={12: 1},
        compiler_params=_cparams(("arbitrary",)),
        name="moe",
    )(jnp.maximum(row_src, 0), row_src, tile_seg, tile_grp, n_used,
      xc.reshape(s * l // SUBLANES, SUBLANES, width), mod, p["norm2_g"], w1, w3, w2, p["final_norm_g"],
      jnp.zeros((RING, tb // SUBLANES, SUBLANES, d), F32))
    return y.reshape(s, l, d)


def _trunk(x, mod_seg, states, conv2d, p, tm, tb, mixer_seqs):
    b, t, d = x.shape
    n_seg = mod_seg.shape[0]
    seg = lambda a: a.reshape(n_seg, b * t // n_seg, a.shape[-1])
    per_seq = lambda a: a.reshape(b, t, a.shape[-1])
    xa, ub, o_pre = (per_seq(a) for a in _in_proj(seg(x), mod_seg, p["norm1_g"], p["w_in"], tm))
    if states is None:
        s_rwkv, mlstm_states = None, None
    else:
        s_rwkv, s_c, s_n, s_m = states
        mlstm_states = (s_c, s_n, jnp.broadcast_to(s_m[..., None], s_m.shape + (LANES,)))
    ya, s_rwkv = _rwkv(xa, s_rwkv, p, mixer_seqs)
    yb, s_c, s_n, s_m = _mlstm(ub, o_pre, mlstm_states, p, conv2d, mixer_seqs)
    xc = _merge(seg(x), seg(ya), seg(yb), mod_seg, p, tm)
    y = _moe(xc, mod_seg, p, tb)
    return y.reshape(b, t, d), s_rwkv, s_c, s_n, s_m[..., 0]


def kernel(x_prompt, x_sample, c, c_ctx, state_rwkv, state_mlstm_C, state_mlstm_n, state_mlstm_m, w_mod, b_mod, norm1_g, norm2_g, w_in, rwkv_mu, rwkv_w_r, rwkv_w_k, rwkv_w_v, rwkv_w0, rwkv_w1, rwkv_w2, rwkv_a0, rwkv_a1, rwkv_a2, rwkv_g1, rwkv_g2, rwkv_k_k, rwkv_k_a, rwkv_r_k, rwkv_ln_g, rwkv_ln_b, rwkv_w_proj, mlstm_conv, mlstm_w_q, mlstm_w_k, mlstm_w_v, mlstm_w_i, mlstm_b_i, mlstm_w_f, mlstm_b_f, mlstm_ln_g, mlstm_w_proj, w_out, moe_w_group, moe_b_group, moe_w_expert, moe_b_expert, moe_w1, moe_w3, moe_w2, final_norm_g):
    assert w_mod.shape[0] == 1, "single trunk layer"
    l = 0
    bp, dec = x_prompt.shape[0], x_sample.shape[0]
    bf = lambda w: w.astype(BF16)
    row = lambda w: w.reshape(1, -1).astype(F32)
    wg = jnp.concatenate([mlstm_w_i[l, 0], mlstm_w_i[l, 1], mlstm_w_f[l, 0], mlstm_w_f[l, 1]], axis=1)
    bg = jnp.concatenate([mlstm_b_i[l, 0], mlstm_b_i[l, 1], mlstm_b_f[l, 0], mlstm_b_f[l, 1]])
    wg = jnp.pad(wg, ((0, 0), (0, LANES - wg.shape[1])))
    bg = jnp.pad(bg, (0, LANES - bg.shape[0]))
    w_router = jnp.pad(jnp.concatenate([moe_w_group[l], moe_w_expert[l]], axis=1),
                       ((0, 0), (0, LANES - N_GROUPS - N_EXPERTS)))
    b_router = jnp.pad(jnp.concatenate([moe_b_group[l], moe_b_expert[l]]), (0, LANES - N_GROUPS - N_EXPERTS))
    p = {
        "norm1_g": row(norm1_g[l]), "norm2_g": row(norm2_g[l]), "w_in": w_in[l],
        "rwkv_mu": rwkv_mu[l], "rwkv_wr": bf(rwkv_w_r[l]), "rwkv_wk": bf(rwkv_w_k[l]), "rwkv_wv": bf(rwkv_w_v[l]),
        "rwkv_w0": rwkv_w0[l], "rwkv_w1": bf(rwkv_w1[l]), "rwkv_w2": bf(rwkv_w2[l]),
        "rwkv_a0": rwkv_a0[l], "rwkv_a1": bf(rwkv_a1[l]), "rwkv_a2": bf(rwkv_a2[l]),
        "rwkv_g1": bf(rwkv_g1[l]), "rwkv_g2": bf(rwkv_g2[l]),
        "rwkv_k_k": row(rwkv_k_k[l]), "rwkv_k_a": row(rwkv_k_a[l]), "rwkv_r_k": row(rwkv_r_k[l]),
        "rwkv_ln_g": row(rwkv_ln_g[l]), "rwkv_ln_b": row(rwkv_ln_b[l]),
        "rwkv_w_proj": rwkv_w_proj[l],
        "mlstm_conv": mlstm_conv[l].reshape(9, B_WIDTH),
        "mlstm_wq": bf(mlstm_w_q[l]), "mlstm_wk": bf(mlstm_w_k[l]), "mlstm_wv": bf(mlstm_w_v[l]),
        "mlstm_wg": bf(wg), "mlstm_bg": row(bg),
        "mlstm_wgt": bf(wg[:, :16].T), "mlstm_bgt": jnp.broadcast_to(bg[:16, None], (16, LANES)),
        "mlstm_ln_g": row(mlstm_ln_g[l]), "mlstm_w_proj": mlstm_w_proj[l], "w_out": w_out[l],
        "moe_w_router": bf(w_router), "moe_b_router": row(b_router),
        "moe_w1": moe_w1[l], "moe_w3": moe_w3[l], "moe_w2": moe_w2[l],
        "final_norm_g": row(final_norm_g),
    }
    cvec = jnp.concatenate([c_ctx[None, :], c, jnp.zeros((8 - 1 - dec, D_MODEL), F32)], axis=0)
    mod = _mod(cvec, w_mod[l], b_mod[l].reshape(1, -1))
    mod_ctx = mod[0:1][:, None, :]
    mod_lat = mod[1:1 + dec][:, None, :]

    yp, n_rwkv, n_c, n_n, n_m = _trunk(x_prompt, mod_ctx, None, False, p, 512, 512, 2)
    lat_states = tuple(s[:, l].astype(F32) for s in (state_rwkv, state_mlstm_C, state_mlstm_n, state_mlstm_m))
    ys, _, _, _, _ = _trunk(x_sample, mod_lat, lat_states, True, p, 512, 256, 1)
    dt = x_prompt.dtype
    return (yp, ys, n_rwkv[:, None].astype(dt), n_c[:, None].astype(dt), n_n[:, None].astype(dt),
            n_m[:, None].astype(dt))
```

```python
import functools

import jax
import jax.numpy as jnp
from jax import lax
from jax.experimental import pallas as pl
from jax.experimental.pallas import tpu as pltpu

F32 = jnp.float32
BF16 = jnp.bfloat16

D_MODEL = 1024
GRID_W = 64
A_HEAD_DIM = 64
A_WIDTH = D_MODEL // 2
A_HEADS = A_WIDTH // A_HEAD_DIM
B_HEAD_DIM = 128
B_WIDTH = D_MODEL // 2
B_HEADS = B_WIDTH // B_HEAD_DIM
CHUNK = 64
SCAN_HEADS = 2
RING = 3
ROW_TILE = 512
SUM_TILE = 256
N_GROUPS = 4
EXPERTS_PER_GROUP = 4
N_EXPERTS = N_GROUPS * EXPERTS_PER_GROUP
GROUP_LANE = N_GROUPS + N_EXPERTS
NORM_EPS = 1e-6
RWKV_GN_EPS = 64e-5
MLSTM_GN_EPS = 1e-5
LANES = 128
SUBLANES = 8
CONV_PAD = 72
VMEM_LIMIT = 56 * 1024 * 1024


def _mm(a, b):
    return jnp.dot(a.astype(BF16), b.astype(BF16), preferred_element_type=F32)


_NN, _NT, _TN = ((1,), (0,)), ((1,), (1,)), ((0,), (0,))


def _split(a):
    hi = a.astype(BF16)
    return hi, (a - hi.astype(F32)).astype(BF16)


def _split3(a):
    hi = a.astype(BF16)
    rest = a - hi.astype(F32)
    lo = rest.astype(BF16)
    return hi, lo, (rest - lo.astype(F32)).astype(BF16)


def _sum01(a, b, dims=_NN):
    dot = lambda p, q: lax.dot_general(p, q, (dims, ((), ())), preferred_element_type=F32)
    terms = [dot(x, b) for x in a] if isinstance(a, tuple) else [dot(a, x) for x in b]
    return functools.reduce(lambda p, q: p + q, terms)


def _mm_nt(a, b):
    return lax.dot_general(a.astype(BF16), b.astype(BF16), (((1,), (1,)), ((), ())),
                           preferred_element_type=F32)


def _mm_tn(a, b):
    return lax.dot_general(a.astype(BF16), b.astype(BF16), (((0,), (0,)), ((), ())),
                           preferred_element_type=F32)


def _sigmoid(x):
    return 1.0 / (1.0 + jnp.exp(-x))


def _silu(x):
    return x * _sigmoid(x)


def _softplus(x):
    return jnp.maximum(x, 0.0) + jnp.log(1.0 + jnp.exp(-jnp.abs(x)))


def _iota(shape, dim):
    return lax.broadcasted_iota(jnp.int32, shape, dim)


def _tri(reverse, inclusive, n=CHUNK):
    row, col = _iota((n, n), 0), _iota((n, n), 1)
    if reverse:
        return (col >= row) if inclusive else (col > row)
    return (col <= row) if inclusive else (col < row)


def _same_block(n, blk):
    sh = blk.bit_length() - 1
    row, col = _iota((n, n), 0), _iota((n, n), 1)
    return lax.shift_right_logical(row, sh) == lax.shift_right_logical(col, sh)


def _rmsnorm(x, g):
    return x * lax.rsqrt(jnp.mean(x * x, axis=-1, keepdims=True) + NORM_EPS) * g


def _cparams(sem):
    return pltpu.CompilerParams(dimension_semantics=sem, vmem_limit_bytes=VMEM_LIMIT)


def _full(shape):
    nd = len(shape)
    return pl.BlockSpec(shape, lambda *_: (0,) * nd)


def _mod_kernel(c_ref, w_ref, b_ref, o_ref):
    o_ref[...] = _mm(_silu(c_ref[...]), w_ref[...]) + b_ref[...]


def _mod(cvec, w_mod, b_mod):
    rows, d = cvec.shape
    n = w_mod.shape[1]
    tn = 1536
    return pl.pallas_call(
        _mod_kernel,
        grid=(n // tn,),
        in_specs=[_full((rows, d)), pl.BlockSpec((d, tn), lambda j: (0, j)),
                  pl.BlockSpec((1, tn), lambda j: (0, j))],
        out_specs=pl.BlockSpec((rows, tn), lambda j: (0, j)),
        out_shape=jax.ShapeDtypeStruct((rows, n), F32),
        compiler_params=_cparams(("arbitrary",)),
        name="mod",
    )(cvec, w_mod, b_mod)


W_IN_BLOCK = 512


def _modulated_norm1(x_ref, mod_ref, g_ref):
    d = D_MODEL
    shift, scale = mod_ref[:, 0:d], mod_ref[:, d:2 * d]
    return (_rmsnorm(x_ref[...], g_ref[...]) * (1.0 + scale) + shift).astype(BF16)


def _w_in_blocks(w_in, first, count):
    specs = [pl.BlockSpec((w_in.shape[0], W_IN_BLOCK), lambda *_, j=first + k: (0, j)) for k in range(count)]
    return specs, [w_in] * count


def _in_kernel(x_ref, mod_ref, g_ref, wa_ref, wu_ref, wo_ref, xa_ref, ub_ref, op_ref):
    h = _modulated_norm1(x_ref, mod_ref, g_ref)
    for ref, w_ref in ((xa_ref, wa_ref), (ub_ref, wu_ref), (op_ref, wo_ref)):
        ref[...] = _mm(h, w_ref[...])


def _in_proj(x, mod, g1, w_in, tm):
    b, t, d = x.shape
    per = t // tm
    widths = (A_WIDTH, B_WIDTH, B_WIDTH)
    assert all(n == W_IN_BLOCK for n in widths)
    tok = lambda n: pl.BlockSpec((None, tm, n), lambda i: (i // per, i % per, 0))
    w_specs, w_args = _w_in_blocks(w_in, 0, 3)
    return pl.pallas_call(
        _in_kernel,
        grid=(b * per,),
        in_specs=[tok(d), pl.BlockSpec((None, 1, 6 * d), lambda i: (i // per, 0, 0)), _full((1, d))] + w_specs,
        out_specs=[tok(n) for n in widths],
        out_shape=[jax.ShapeDtypeStruct((b, t, n), F32) for n in widths],
        compiler_params=_cparams(("arbitrary",)),
        name="in_proj",
    )(x, mod, g1, *w_args)


def _head_of_lane(shape):
    return lax.shift_right_logical(_iota(shape, 1), A_HEAD_DIM.bit_length() - 1)


def _pair_diag(x):
    head = _head_of_lane(x.shape)
    return jnp.concatenate([x * (head == h).astype(x.dtype) for h in range(x.shape[1] // A_HEAD_DIM)], axis=0)


def _pair_mm3(lhs, rhs, dims=_NN):
    rows = lhs[0].shape[0]
    if dims == _NT:
        wh, wl = _split(_pair_diag(rhs).T)
    else:
        rh, rl = _split(rhs)
        wh, wl = _pair_diag(rh), _pair_diag(rl)
    parts = [_split(l) for l in lhs]
    top = jnp.dot(jnp.concatenate([t for part in parts for t in part], axis=0), wh, preferred_element_type=F32)
    low = jnp.dot(jnp.concatenate([part[0] for part in parts], axis=0), wl, preferred_element_type=F32)
    return [top[2 * k * rows:(2 * k + 1) * rows] + top[(2 * k + 1) * rows:(2 * k + 2) * rows]
            + low[k * rows:(k + 1) * rows] for k in range(len(lhs))]


def _pair_index(shape):
    return _iota(shape, 0), _iota(shape, 1) & (A_HEAD_DIM - 1)


def _pair_tri_inverses(mats):
    t_idx, s_idx = _pair_index(mats[0].shape)
    same = lambda blk: (lax.shift_right_logical(t_idx, blk.bit_length() - 1)
                        == lax.shift_right_logical(s_idx, blk.bit_length() - 1))
    eye = (t_idx == s_idx).astype(F32)
    nd = [jnp.where(same(8), m, 0.0) for m in mats]
    n2 = [_pair_mm3([x], x)[0] for x in nd]
    t = [eye + x for x in nd]
    both = [_pair_mm3([a, b], a) for a, b in zip(n2, t)]
    t = [x + r[1] for x, r in zip(t, both)]
    t = [x + _pair_mm3([x], r[0])[0] for x, r in zip(t, both)]
    blk = 8
    while blk < CHUNK:
        sel = same(2 * blk) & jnp.logical_not(same(blk))
        w = [_pair_mm3([jnp.where(sel, m, 0.0)], x)[0] for m, x in zip(mats, t)]
        t = [x + _pair_mm3([x], y)[0] for x, y in zip(t, w)]
        blk *= 2
    return t


def _rwkv_kernel(x_ref, s0_ref, mu_ref, wr_ref, wk_ref, wv_ref, w0_ref, w1_ref, w2_ref,
                 a0_ref, a1_ref, a2_ref, g1_ref, g2_ref, kk_ref, ka_ref, rk_ref, lng_ref, lnb_ref,
                 y_ref, s_ref,
                 r_s, v_s, kk_s, cum_s, cx_s, kd_s, bb_s, g_s, bonus_s, y_s, sp_s, *, seq_len):
    total = x_ref.shape[0]
    n_seq = total // seq_len
    n_chunks = seq_len // CHUNK
    n_tiles = total // ROW_TILE
    hd = A_HEAD_DIM
    n_groups, width = sp_s.shape[2], sp_s.shape[4]
    per_group = width // hd
    x = x_ref[...]
    row = _iota(x.shape, 0) & (seq_len - 1)
    prev = jnp.where(row == 0, 0.0, pltpu.roll(x, 1, 0))
    nxt = jnp.where(row == seq_len - 1, 0.0, pltpu.roll(x, total - 1, 0))
    y_s[...] = 0.5 * (prev + nxt) - x
    head_ones = _same_block(A_WIDTH, hd).astype(BF16)
    chunk_diag = _same_block(SUM_TILE, CHUNK)
    run_sum = [(chunk_diag & _tri(d == 1, True, SUM_TILE)).astype(BF16) for d in range(2)]

    def project_tile(ti, carry):
        rows = pl.ds(pl.multiple_of(ti * ROW_TILE, ROW_TILE), ROW_TILE)
        x, xx = x_ref[rows, :], y_s[rows, :]
        xr, xw, xk, xv, xi, xg = (x + xx * mu_ref[j:j + 1, :] for j in range(6))
        r = _mm(xr, wr_ref[...])
        k = _mm(xk, wk_ref[...])
        v = _mm(xv, wv_ref[...])
        g_s[rows, :] = _mm(_sigmoid(_mm(xg, g1_ref[...])), g2_ref[...])
        kk = k * kk_ref[...]
        kk = kk / jnp.maximum(jnp.sqrt(_sum01(_split(kk * kk), head_ones)), 1e-12)
        r_s[rows, :] = r
        v_s[rows, :] = v
        kk_s[rows, :] = kk
        kd_sum = jnp.zeros_like(x)
        for d in range(2):
            zw = w0_ref[d:d + 1, :] + _mm(jnp.tanh(_mm(xw, w1_ref[d])), w2_ref[d])
            lw = -jnp.exp(-_softplus(-zw) - 0.5)
            cum = jnp.concatenate([_sum01(run_sum[d], _split3(lw[j:j + SUM_TILE]))
                                   for j in range(0, ROW_TILE, SUM_TILE)], axis=0)
            cum_s[d, rows, :] = cum
            cx_s[d, rows, :] = cum - lw
            a = _sigmoid(a0_ref[d:d + 1, :] + _mm(_mm(xi, a1_ref[d]), a2_ref[d]))
            kd = k * (1.0 + (a - 1.0) * ka_ref[...])
            kd_s[d, rows, :] = kd
            bb_s[d, rows, :] = kk * a
            kd_sum = kd_sum + kd
        bonus_s[rows, :] = _sum01(_split(r * kd_sum * rk_ref[...]), head_ones) * v
        y_s[rows, :] = jnp.zeros_like(x)
        return carry

    lax.fori_loop(0, n_tiles, project_tile, 0)
    for n in range(n_seq):
        for d in range(2):
            for p in range(n_groups):
                sp_s[n, d, p] = (jnp.zeros((hd, width), F32) if s0_ref is None else jnp.concatenate(
                    [s0_ref[n, d, per_group * p + j] for j in range(per_group)], axis=1))

    t_idx, s_idx = _pair_index((CHUNK, width))
    strict = [s_idx < t_idx, s_idx > t_idx]
    incl = [s_idx <= t_idx, s_idx >= t_idx]
    lane_head = _head_of_lane((CHUNK, width))

    def chunk_body(ci, carry):
        chains = []
        for n, d in [(n, d) for n in range(n_seq) for d in range(2)]:
            cpos = n * n_chunks + ((n_chunks - 1 - ci) if d == 1 else ci)
            last = 0 if d == 1 else CHUNK - 1
            rows = pl.ds(pl.multiple_of(cpos * CHUNK, CHUNK), CHUNK)
            cum, cx = cum_s[d, rows, :], cx_s[d, rows, :]
            tot = cum[last:last + 1, :]
            kdc, bbc = kd_s[d, rows, :], bb_s[d, rows, :]
            e_neg = jnp.exp(-cum)
            e_tail = jnp.exp(tot - cum)
            at = -kk_s[rows, :] * jnp.exp(cx)
            rt = r_s[rows, :] * jnp.exp(cum)
            bt, kt = bbc * e_neg, kdc * e_neg
            bp, kp = bbc * e_tail, kdc * e_tail
            pc = jnp.exp(tot)
            vc = v_s[rows, :]
            for p in range(n_groups):
                ln = slice(p * width, (p + 1) * width)
                chains.append(dict(d=d, st=(n, d, p), rows=rows, ln=ln, v=vc[:, ln], pc=pc[:, ln],
                                   lhs=jnp.concatenate([at[:, ln], rt[:, ln]], axis=0),
                                   bt=bt[:, ln], kt=kt[:, ln], bp=bp[:, ln], kp=kp[:, ln]))
        for c in chains:
            c["sb"] = _pair_mm3([c["lhs"]], c["bt"], _NT)[0]
            c["sk"] = _pair_mm3([c["lhs"]], c["kt"], _NT)[0]
            c["s"] = sp_s[c["st"]]
            c["fs"] = _mm(c["lhs"], _pair_diag(c["s"]).T)
            c["v_diag"] = _pair_diag(c["v"].astype(BF16))
        for c in chains:
            a_ak = jnp.where(strict[c["d"]], c["sk"][:CHUNK], 0.0)
            c["x"] = c["fs"][:CHUNK] + _mm(a_ak, c["v_diag"])
        inv = _pair_tri_inverses([jnp.where(strict[c["d"]], c["sb"][:CHUNK], 0.0) for c in chains])
        for c, t in zip(chains, inv):
            c["u"] = _pair_mm3([t], c["x"])[0]
        for c in chains:
            d, rows, ln, u = c["d"], c["rows"], c["ln"], c["u"]
            a_r = jnp.concatenate([jnp.where(incl[d], c["sb"][CHUNK:], 0.0),
                                   jnp.where(incl[d], c["sk"][CHUNK:], 0.0)], axis=1)
            uv_diag = jnp.concatenate([_pair_diag(u.astype(BF16)), c["v_diag"]], axis=0)
            y_s[rows, ln] = y_s[rows, ln] + c["fs"][CHUNK:] + _mm(a_r, uv_diag)
            g = _mm_tn(jnp.concatenate([u, c["v"]], axis=0), jnp.concatenate([c["bp"], c["kp"]], axis=0))
            own = g[0:hd]
            for j in range(1, per_group):
                own = jnp.where(lane_head == j, g[j * hd:(j + 1) * hd], own)
            sp_s[c["st"]] = c["s"] * c["pc"] + own
        return carry

    lax.fori_loop(0, n_chunks, chunk_body, 0)
    for n in range(n_seq):
        for d in range(2):
            for p in range(n_groups):
                for j in range(per_group):
                    s_ref[n, d, per_group * p + j] = sp_s[n, d, p, :, j * hd:(j + 1) * hd]

    def finish_tile(ti, carry):
        rows = pl.ds(pl.multiple_of(ti * ROW_TILE, ROW_TILE), ROW_TILE)
        ys = y_s[rows, :]
        cen = ys - _sum01(_split(ys), head_ones) * (1.0 / hd)
        var = _sum01(_split(cen * cen), head_ones) * (1.0 / hd)
        yn = cen * lax.rsqrt(var + RWKV_GN_EPS) * lng_ref[...] + lnb_ref[...]
        y_ref[rows, :] = (yn + bonus_s[rows, :]) * g_s[rows, :]
        return carry

    lax.fori_loop(0, n_tiles, finish_tile, 0)


def _rwkv(xa, s0, p, n_seq):
    b, t, w = xa.shape
    assert t & (t - 1) == 0 and b % n_seq == 0
    names = ("mu", "wr", "wk", "wv", "w0", "w1", "w2", "a0", "a1", "a2", "g1", "g2",
             "k_k", "k_a", "r_k", "ln_g", "ln_b")
    weights = [p["rwkv_" + n] for n in names]
    rows = n_seq * t
    seq = pl.BlockSpec((None, rows, w), lambda i: (i, 0, 0))
    st = pl.BlockSpec((n_seq, 2, A_HEADS, A_HEAD_DIM, A_HEAD_DIM), lambda i: (i, 0, 0, 0, 0))
    tw = lambda lead=(): pltpu.VMEM(lead + (rows, w), F32)
    kernel = functools.partial(_rwkv_kernel, seq_len=t)
    if s0 is None:
        kernel, state_specs, states = _zero_state_kernel(kernel, 1, 1), [], ()
    else:
        state_specs, states = [st], (s0,)
    y, s_out = pl.pallas_call(
        kernel,
        grid=(b // n_seq,),
        in_specs=[seq] + state_specs + [_full(x.shape) for x in weights],
        out_specs=[seq, st],
        out_shape=[jax.ShapeDtypeStruct((b // n_seq, rows, w), F32),
                   jax.ShapeDtypeStruct((b, 2, A_HEADS, A_HEAD_DIM, A_HEAD_DIM), F32)],
        scratch_shapes=[tw(), tw(), tw(), tw((2,)), tw((2,)), tw((2,)), tw((2,)), tw(), tw(), tw(),
                        pltpu.VMEM((n_seq, 2, A_HEADS // SCAN_HEADS, A_HEAD_DIM, SCAN_HEADS * A_HEAD_DIM), F32)],
        compiler_params=_cparams(("arbitrary",)),
        name="rwkv",
    )(xa.reshape(b // n_seq, rows, w), *states, *weights)
    return y.reshape(b, t, w), s_out


def _mlstm_kernel(u_ref, op_ref, c0_ref, n0_ref, m0_ref, conv_ref, wq_ref, wk_ref, wv_ref,
                  wg_ref, bg_ref, wgt_ref, bgt_ref, lng_ref,
                  y_ref, c_ref, n_ref, m_ref,
                  pad_s, q_s, k_s, v_s, dd_s, cr_s, h_s, *, conv2d, seq_len):
    t_len = u_ref.shape[0]
    n_seq = t_len // seq_len
    n_chunks = seq_len // CHUNK
    hd = B_HEAD_DIM
    u = u_ref[...]
    pad_s[0:CONV_PAD, :] = jnp.zeros((CONV_PAD, B_WIDTH), F32)
    pad_s[CONV_PAD + t_len:CONV_PAD + t_len + CONV_PAD, :] = jnp.zeros((CONV_PAD, B_WIDTH), F32)
    pad_s[CONV_PAD:CONV_PAD + t_len, :] = u
    pos = _iota(u.shape, 0) & (seq_len - 1)
    col = pos & (GRID_W - 1)
    conv = jnp.zeros_like(u)
    for kh in range(3):
        if not conv2d and kh != 1:
            continue
        for kw in range(3):
            shift = (kh - 1) * GRID_W + (kw - 1)
            term = pad_s[CONV_PAD + shift:CONV_PAD + shift + t_len, :] * conv_ref[kh * 3 + kw:kh * 3 + kw + 1, :]
            if conv2d and kw != 1:
                src = col + (kw - 1)
                term = jnp.where((src >= 0) & (src < GRID_W), term, 0.0)
            if n_seq > 1 and shift != 0:
                term = jnp.where((pos + shift >= 0) & (pos + shift < seq_len), term, 0.0)
            conv = conv + term
    uc = _silu(conv)
    for h in range(B_HEADS):
        ln = slice(h * hd, (h + 1) * hd)
        q_s[:, ln] = _mm(uc[:, ln], wq_ref[h])
        k_s[:, ln] = _mm(uc[:, ln], wk_ref[h]) * (hd ** -0.5)
        v_s[:, ln] = _mm(u[:, ln], wv_ref[h])
    gcol = _mm(uc, wg_ref[...]) + bg_ref[...]
    gcol = jnp.where(_iota(gcol.shape, 1) >= 8, -_softplus(-gcol), gcol)
    grow = _mm_nt(wgt_ref[...], uc) + bgt_ref[:, 0:1]
    grow = jnp.where(_iota(grow.shape, 0) >= 8, -_softplus(-grow), grow)
    chunk_diag = _same_block(SUM_TILE, CHUNK)
    run_sum = [(chunk_diag & _tri(d == 1, True, SUM_TILE)).astype(BF16) for d in range(2)]
    per_tile = SUM_TILE // CHUNK
    for ti in range(t_len // SUM_TILE):
        sl = slice(ti * SUM_TILE, (ti + 1) * SUM_TILE)
        gt, rt = _split3(gcol[sl, :]), _split3(grow[:, sl])
        backward_lane = (_iota((SUM_TILE, LANES), 1) & 4) != 0
        cum_c = jnp.where(backward_lane, _sum01(run_sum[1], gt, _NN), _sum01(run_sum[0], gt, _NN))
        dd_s[sl, :] = pltpu.roll(cum_c, LANES - 8, 1) - gcol[sl, :]
        backward_row = (_iota((16, SUM_TILE), 0) & 4) != 0
        cum_r = jnp.where(backward_row, _sum01(rt, run_sum[1], _NT), _sum01(rt, run_sum[0], _NT))
        for c in range(per_tile):
            cr_s[ti * per_tile + c] = cum_r[:, c * CHUNK:(c + 1) * CHUNK]
    h_s[...] = jnp.zeros_like(u)
    for ref, init in ((c_ref, c0_ref), (n_ref, n0_ref), (m_ref, m0_ref)):
        ref[...] = jnp.zeros(ref.shape, F32) if init is None else init[...]
    row_i, col_i = _iota((CHUNK, CHUNK), 0), _iota((CHUNK, CHUNK), 1)
    incl_t = [row_i <= col_i, row_i >= col_i]
    eye = row_i == col_i
    pick = [(_iota((LANES, B_HEADS * LANES), 0)
             == d * B_HEADS + lax.shift_right_logical(_iota((LANES, B_HEADS * LANES), 1), LANES.bit_length() - 1)
             ).astype(BF16) for d in range(2)]
    spread = (lax.shift_right_logical(_iota((2 * CHUNK, 2 * LANES), 0), CHUNK.bit_length() - 1)
              == lax.shift_right_logical(_iota((2 * CHUNK, 2 * LANES), 1), LANES.bit_length() - 1)).astype(BF16)
    ones_bf = jnp.ones((CHUNK, LANES), BF16)

    def chunk_body(ci, carry):
        chains = []
        for n, d in [(n, d) for n in range(n_seq) for d in range(2)]:
            cpos = n * n_chunks + ((n_chunks - 1 - ci) if d == 1 else ci)
            last = 0 if d == 1 else CHUNK - 1
            rows = pl.ds(pl.multiple_of(cpos * CHUNK, CHUNK), CHUNK)
            cr = cr_s[cpos]
            d_col = _sum01(_split3(dd_s[rows, :]), pick[d])
            for h in range(B_HEADS):
                ln = slice(h * hd, (h + 1) * hd)
                lf = 8 + d * 4 + h
                chains.append(dict(d=d, st=(n, d, h), vec=(n, d, slice(h, h + 1)), rows=rows, ln=ln, last=last,
                                   b_row=cr[lf:lf + 1, :], d_col=d_col[:, ln],
                                   m=m_ref[n, d, h:h + 1, 0:1], q=q_s[rows, ln], k=k_s[rows, ln], v=v_s[rows, ln],
                                   c=c_ref[n, d, h], n=n_ref[n, d, h:h + 1, :]))
        for c in chains:
            c["kq"] = _mm(c["k"], c["q"].T)
            c["qc"] = _mm(c["q"], c["c"])
            c["qn"] = _mm(c["q"], jnp.broadcast_to(c["n"], (LANES, hd)).T)
            logw = jnp.where(incl_t[c["d"]], c["b_row"] - c["d_col"][:, 0:CHUNK], -jnp.inf)
            c["m_row"] = jnp.maximum(c["b_row"] + c["m"], jnp.max(logw, axis=0, keepdims=True))
            c["logw"] = logw
        for c in chains:
            inter_row = jnp.exp(c["b_row"] + c["m"] - c["m_row"])
            diag = jnp.concatenate([jnp.where(eye, c["m_row"], 0.0), jnp.where(eye, inter_row, 0.0)], axis=1)
            cols = _sum01(_split3(diag), spread)
            c["m_col"], c["inter"] = cols[:, 0:LANES], cols[:, LANES:2 * LANES]
            c["s"] = c["kq"] * jnp.exp(c["logw"] - c["m_row"])
        for c in chains:
            s_hi, s_lo = _split(c["s"])
            sv = lax.dot_general(s_hi, jnp.concatenate([c["v"].astype(BF16), ones_bf], axis=1),
                                 (_TN, ((), ())), preferred_element_type=F32)
            c["sv"] = sv[:, 0:hd]
            c["den"] = (c["inter"] * c["qn"] + sv[:, hd:hd + LANES]
                        + lax.dot_general(s_lo, ones_bf, (_TN, ((), ())), preferred_element_type=F32))
            last = c["last"]
            c["m_new"] = c["m_row"][:, last:last + 1]
            b_last = c["b_row"][:, last:last + 1]
            c["decay"] = jnp.exp(b_last + c["m"] - c["m_new"])
            c["kw"] = c["k"] * jnp.exp(b_last - c["m_new"] - c["d_col"])
        for c in chains:
            c["kv"] = _mm_tn(c["kw"], c["v"])
        for c in chains:
            rows, ln = c["rows"], c["ln"]
            num = c["inter"] * c["qc"] + c["sv"]
            h_s[rows, ln] = h_s[rows, ln] + num / jnp.maximum(jnp.abs(c["den"]), jnp.exp(-c["m_col"]))
            c_ref[c["st"]] = c["decay"] * c["c"] + c["kv"]
            n_ref[c["vec"]] = c["decay"] * c["n"] + jnp.sum(c["kw"], axis=0, keepdims=True)
            m_ref[c["vec"]] = jnp.broadcast_to(c["m_new"], (1, LANES))
        return carry

    lax.fori_loop(0, n_chunks, chunk_body, 0)

    for h in range(B_HEADS):
        ln = slice(h * hd, (h + 1) * hd)
        hh = h_s[:, ln]
        cen = hh - jnp.mean(hh, axis=1, keepdims=True)
        var = jnp.mean(cen * cen, axis=1, keepdims=True)
        y_ref[:, ln] = _sigmoid(op_ref[:, ln]) * (cen * lax.rsqrt(var + MLSTM_GN_EPS) * lng_ref[:, ln])


def _zero_state_kernel(kernel, n_data, n_state):
    def wrapped(*refs, **kw):
        return kernel(*refs[:n_data], *([None] * n_state), *refs[n_data:], **kw)
    return wrapped


def _mlstm(ub, o_pre, states, p, conv2d, n_seq):
    b, t, w = ub.shape
    assert t & (t - 1) == 0 and b % n_seq == 0
    names = ("conv", "wq", "wk", "wv", "wg", "bg", "wgt", "bgt", "ln_g")
    weights = [p["mlstm_" + n] for n in names]
    rows, steps = n_seq * t, b // n_seq
    seq = pl.BlockSpec((None, rows, w), lambda i: (i, 0, 0))
    cst = pl.BlockSpec((n_seq, 2, B_HEADS, B_HEAD_DIM, B_HEAD_DIM), lambda i: (i, 0, 0, 0, 0))
    vst = pl.BlockSpec((n_seq, 2, B_HEADS, LANES), lambda i: (i, 0, 0, 0))
    kernel = functools.partial(_mlstm_kernel, conv2d=conv2d, seq_len=t)
    if states is None:
        kernel, state_specs, states = _zero_state_kernel(kernel, 2, 3), [], ()
    else:
        state_specs = [cst, vst, vst]
    y, c_out, n_out, m_out = pl.pallas_call(
        kernel,
        grid=(steps,),
        in_specs=[seq, seq] + state_specs + [_full(x.shape) for x in weights],
        out_specs=[seq, cst, vst, vst],
        out_shape=[jax.ShapeDtypeStruct((steps, rows, w), F32),
                   jax.ShapeDtypeStruct((b, 2, B_HEADS, B_HEAD_DIM, B_HEAD_DIM), F32),
                   jax.ShapeDtypeStruct((b, 2, B_HEADS, LANES), F32),
                   jax.ShapeDtypeStruct((b, 2, B_HEADS, LANES), F32)],
        scratch_shapes=[pltpu.VMEM((rows + 2 * CONV_PAD, w), F32), pltpu.VMEM((rows, w), F32),
                        pltpu.VMEM((rows, w), F32), pltpu.VMEM((rows, w), F32),
                        pltpu.VMEM((rows, LANES), F32), pltpu.VMEM((rows // CHUNK, 16, CHUNK), F32),
                        pltpu.VMEM((rows, w), F32)],
        compiler_params=_cparams(("arbitrary",)),
        name="mlstm",
    )(ub.reshape(steps, rows, w), o_pre.reshape(steps, rows, w), *states, *weights)
    return y.reshape(b, t, w), c_out, n_out, m_out


def _first_lane_of_max(val, lane, valid):
    masked = jnp.where(valid, val, -jnp.inf)
    best = jnp.max(masked, axis=1, keepdims=True)
    idx = jnp.min(jnp.where(valid & (masked == best), lane, LANES), axis=1, keepdims=True)
    return best, idx


def _merge_kernel(x_ref, ya_ref, yb_ref, mod_ref, g1_ref, wga0_ref, wga1_ref, wgb0_ref, wgb1_ref,
                  wpa_ref, wpb_ref, wo_ref, g2_ref, wr_ref, br_ref, xc_ref):
    d = D_MODEL
    gate1 = mod_ref[:, 2 * d:3 * d]
    shift2, scale2 = mod_ref[:, 3 * d:4 * d], mod_ref[:, 4 * d:5 * d]
    h1 = _modulated_norm1(x_ref, mod_ref, g1_ref)
    pa, pb = _mm(ya_ref[...], wpa_ref[...]), _mm(yb_ref[...], wpb_ref[...])
    halves = []
    for j, (wga_ref, wgb_ref) in enumerate(((wga0_ref, wgb0_ref), (wga1_ref, wgb1_ref))):
        cols = slice(j * W_IN_BLOCK, (j + 1) * W_IN_BLOCK)
        halves.append(_sigmoid(_mm(h1, wga_ref[...])) * pa[:, cols] + _sigmoid(_mm(h1, wgb_ref[...])) * pb[:, cols])
    merged = jnp.concatenate(halves, axis=1)
    x1 = x_ref[...] + gate1 * _mm(merged, wo_ref[...])
    xc_ref[:, 0:d] = x1
    h2 = _rmsnorm(x1, g2_ref[...]) * (1.0 + scale2) + shift2
    logits = _mm(h2, wr_ref[...]) + br_ref[...]
    lane = _iota(logits.shape, 1)
    is_group = lane < N_GROUPS
    g_max, g_sel = _first_lane_of_max(logits, lane, is_group)
    g_w = 1.0 / jnp.sum(jnp.where(is_group, jnp.exp(logits - g_max), 0.0), axis=1, keepdims=True)
    expert = lane - N_GROUPS
    in_group = (expert >= 0) & (expert < N_EXPERTS) & (lax.shift_right_arithmetic(expert, EXPERTS_PER_GROUP.bit_length() - 1) == g_sel)
    e_max = jnp.max(jnp.where(in_group, logits, -jnp.inf), axis=1, keepdims=True)
    e_exp = jnp.where(in_group, jnp.exp(logits - e_max), 0.0)
    prob = e_exp / jnp.sum(e_exp, axis=1, keepdims=True)
    p1, i1 = _first_lane_of_max(prob, lane, in_group)
    p2, i2 = _first_lane_of_max(prob, lane, in_group & (lane != i1))
    denom = p1 + p2
    comb = jnp.where(lane == i1, g_w * p1 / denom, 0.0) + jnp.where(lane == i2, g_w * p2 / denom, 0.0)
    xc_ref[:, d:d + LANES] = jnp.where(lane == GROUP_LANE, g_sel.astype(F32), comb)


def _merge(x, ya, yb, mod, p, tm):
    b, t, d = x.shape
    per = t // tm
    assert d == 2 * W_IN_BLOCK
    tok = lambda n: pl.BlockSpec((None, tm, n), lambda i: (i // per, i % per, 0))
    gate_specs, gate_args = _w_in_blocks(p["w_in"], 3, 4)
    weights = [p["rwkv_w_proj"], p["mlstm_w_proj"], p["w_out"], p["norm2_g"], p["moe_w_router"], p["moe_b_router"]]
    return pl.pallas_call(
        _merge_kernel,
        grid=(b * per,),
        in_specs=[tok(d), tok(A_WIDTH), tok(B_WIDTH), pl.BlockSpec((None, 1, 6 * d), lambda i: (i // per, 0, 0)),
                  _full((1, d))] + gate_specs + [_full(w.shape) for w in weights],
        out_specs=tok(d + LANES),
        out_shape=jax.ShapeDtypeStruct((b, t, d + LANES), F32),
        compiler_params=_cparams(("arbitrary",)),
        name="merge",
    )(x, ya, yb, mod, p["norm1_g"], *gate_args, *weights)


def _route(gsel, tb):
    s, l = gsel.shape
    n_buckets = s * N_GROUPS
    max_tiles = s * (l // tb + N_GROUPS - 1)
    onehot = (gsel[..., None] == jnp.arange(N_GROUPS, dtype=jnp.int32)).astype(jnp.int32)
    rank = jnp.sum((jnp.cumsum(onehot, axis=1) - onehot) * onehot, axis=-1)
    n_tiles = ((jnp.sum(onehot, axis=1) + tb - 1) // tb).T.reshape(n_buckets)
    ends = jnp.cumsum(n_tiles)
    first_tile = jnp.sum(onehot * (ends - n_tiles).reshape(N_GROUPS, s).T[:, None, :], axis=-1)
    pos = first_tile * tb + rank
    tok = jnp.arange(s * l, dtype=jnp.int32)
    row_src = jnp.full((max_tiles * tb,), -1, jnp.int32).at[pos.reshape(-1)].set(tok)
    tile_bucket = jnp.sum(jnp.arange(max_tiles, dtype=jnp.int32)[:, None] >= ends[None, :], axis=1)
    tile_bucket = jnp.minimum(tile_bucket, n_buckets - 1).astype(jnp.int32)
    return row_src, tile_bucket % s, tile_bucket // s, ends[-1:].astype(jnp.int32)


def _moe_kernel(gsrc_ref, sdst_ref, seg_ref, grp_ref, used_ref,
                xc_hbm, mod_ref, g2_ref, w1_ref, w3_ref, w2_ref, gf_ref, dump_in_hbm,
                y_hbm, dump_hbm, gbuf, obuf, gsem, ssem, *, tb):
    d = D_MODEL
    q = pl.program_id(0)
    n_used = used_ref[0]

    def start_gather(tile, sl):
        def body(i, carry):
            for u in range(SUBLANES):
                idx = gsrc_ref[tile * tb + i * SUBLANES + u]
                pltpu.make_async_copy(xc_hbm.at[lax.shift_right_logical(idx, SUBLANES.bit_length() - 1), pl.ds(idx & (SUBLANES - 1), 1), :],
                                      gbuf.at[sl, i, pl.ds(u, 1), :], gsem.at[sl]).start()
            return carry

        lax.fori_loop(0, tb // SUBLANES, body, 0)

    def wait_gather(sl):
        pltpu.make_async_copy(xc_hbm.at[pl.ds(0, tb // SUBLANES)], gbuf.at[sl], gsem.at[sl]).wait()

    def start_scatter(tile, sl):
        def body(i, carry):
            for u in range(SUBLANES):
                idx = sdst_ref[tile * tb + i * SUBLANES + u]

                @pl.when(idx >= 0)
                def _():
                    pltpu.make_async_copy(
                        obuf.at[sl, i, pl.ds(u, 1), :],
                        y_hbm.at[lax.shift_right_logical(idx, SUBLANES.bit_length() - 1), pl.ds(idx & (SUBLANES - 1), 1), :],
                        ssem.at[sl]).start()

                @pl.when(idx < 0)
                def _():
                    pltpu.make_async_copy(obuf.at[sl, i, pl.ds(u, 1), :], dump_hbm.at[sl, i, pl.ds(u, 1), :],
                                          ssem.at[sl]).start()

            return carry

        lax.fori_loop(0, tb // SUBLANES, body, 0)

    def wait_scatter(sl):
        pltpu.make_async_copy(obuf.at[sl], dump_hbm.at[sl], ssem.at[sl]).wait()

    def gather_rows(tile, sl, lo, hi):
        for r in range(lo, hi):
            idx = gsrc_ref[tile * tb + r]
            pltpu.make_async_copy(xc_hbm.at[lax.shift_right_logical(idx, SUBLANES.bit_length() - 1),
                                            pl.ds(idx & (SUBLANES - 1), 1), :],
                                  gbuf.at[sl, r // SUBLANES, pl.ds(r % SUBLANES, 1), :], gsem.at[sl]).start()

    def scatter_rows(tile, sl, lo, hi, enabled):
        for r in range(lo, hi):
            idx = sdst_ref[tile * tb + r]
            src = obuf.at[sl, r // SUBLANES, pl.ds(r % SUBLANES, 1), :]

            @pl.when(enabled & (idx >= 0))
            def _():
                pltpu.make_async_copy(src, y_hbm.at[lax.shift_right_logical(idx, SUBLANES.bit_length() - 1),
                                                    pl.ds(idx & (SUBLANES - 1), 1), :], ssem.at[sl]).start()

            @pl.when(enabled & (idx < 0))
            def _():
                pltpu.make_async_copy(src, dump_hbm.at[sl, r // SUBLANES, pl.ds(r % SUBLANES, 1), :],
                                      ssem.at[sl]).start()

    @pl.when(q < n_used)
    def _():
        slot = lax.rem(q, RING)
        ahead = lax.rem(q + 2, RING)
        behind = lax.rem(q + 1, RING)

        @pl.when(q == 0)
        def _():
            start_gather(0, 0)

            @pl.when(n_used > 1)
            def _():
                start_gather(1, 1)

        wait_gather(slot)

        @pl.when(q >= RING)
        def _():
            wait_scatter(slot)

        nxt = jnp.minimum(q + 2, n_used - 1)
        prev, has_prev = jnp.maximum(q - 1, 0), q >= 1
        part = tb // EXPERTS_PER_GROUP
        rows = gbuf[slot].reshape(tb, gbuf.shape[-1])
        x1, comb = rows[:, 0:d], rows[:, d:d + LANES]
        shift2, scale2, gate2 = mod_ref[:, 3 * d:4 * d], mod_ref[:, 4 * d:5 * d], mod_ref[:, 5 * d:6 * d]
        h2 = (_rmsnorm(x1, g2_ref[...]) * (1.0 + scale2) + shift2).astype(BF16)
        first_lane = N_GROUPS + grp_ref[q] * EXPERTS_PER_GROUP
        lane = _iota(comb.shape, 1)
        acc = jnp.zeros((tb, d), F32)
        for e in range(EXPERTS_PER_GROUP):
            w_e = jnp.sum(jnp.where(lane == first_lane + e, comb, 0.0), axis=1, keepdims=True)
            a = _mm(h2, w1_ref[e])
            b = _mm(h2, w3_ref[e])
            acc = acc + _mm(_silu(a) * b * w_e, w2_ref[e])
            gather_rows(nxt, ahead, e * part, (e + 1) * part)
            scatter_rows(prev, ahead, e * part, (e + 1) * part, has_prev)
        obuf[slot] = _rmsnorm(x1 + gate2 * acc, gf_ref[...]).reshape(tb // SUBLANES, SUBLANES, d)

        @pl.when(q == n_used - 1)
        def _():
            start_scatter(q, slot)
            wait_scatter(slot)
            wait_gather(ahead)

            @pl.when(q >= 1)
            def _():
                wait_scatter(ahead)
                wait_gather(behind)

            @pl.when(q >= 2)
            def _():
                wait_scatter(behind)


def _moe(xc, mod, p, tb):
    s, l, width = xc.shape
    d = D_MODEL
    gsel = xc[:, :, d + GROUP_LANE].astype(jnp.int32)
    row_src, tile_seg, tile_grp, n_used = _route(gsel, tb)
    max_tiles = tile_seg.shape[0]
    grouped = lambda w: w.reshape((N_GROUPS, EXPERTS_PER_GROUP) + w.shape[1:])
    w1, w3, w2 = grouped(p["moe_w1"]), grouped(p["moe_w3"]), grouped(p["moe_w2"])
    const = lambda shape: pl.BlockSpec(shape, lambda q, *_: (0,) * len(shape))
    by_group = lambda w: pl.BlockSpec((None,) + w.shape[1:], lambda q, gs, sd, seg, grp, used: (grp[q], 0, 0, 0))
    hbm = pl.BlockSpec(memory_space=pl.ANY)
    y, _ = pl.pallas_call(
        functools.partial(_moe_kernel, tb=tb),
        grid_spec=pltpu.PrefetchScalarGridSpec(
            num_scalar_prefetch=5,
            grid=(max_tiles,),
            in_specs=[hbm, pl.BlockSpec((None, 1, 6 * d), lambda q, gs, sd, seg, grp, used: (seg[q], 0, 0)),
                      const((1, d)), by_group(w1), by_group(w3), by_group(w2), const((1, d)), hbm],
            out_specs=[hbm, hbm],
            scratch_shapes=[pltpu.VMEM((RING, tb // SUBLANES, SUBLANES, width), F32),
                            pltpu.VMEM((RING, tb // SUBLANES, SUBLANES, d), F32),
                            pltpu.SemaphoreType.DMA((RING,)), pltpu.SemaphoreType.DMA((RING,))]),
        out_shape=[jax.ShapeDtypeStruct((s * l // SUBLANES, SUBLANES, d), F32),
                   jax.ShapeDtypeStruct((RING, tb // SUBLANES, SUBLANES, d), F32)],
        input_output_aliases={12: 1},
        compiler_params=_cparams(("arbitrary",)),
        name="moe",
    )(jnp.maximum(row_src, 0), row_src, tile_seg, tile_grp, n_used,
      xc.reshape(s * l // SUBLANES, SUBLANES, width), mod, p["norm2_g"], w1, w3, w2, p["final_norm_g"],
      jnp.zeros((RING, tb // SUBLANES, SUBLANES, d), F32))
    return y.reshape(s, l, d)


def _trunk(x, mod_seg, states, conv2d, p, tm, tb, mixer_seqs):
    b, t, d = x.shape
    n_seg = mod_seg.shape[0]
    seg = lambda a: a.reshape(n_seg, b * t // n_seg, a.shape[-1])
    per_seq = lambda a: a.reshape(b, t, a.shape[-1])
    xa, ub, o_pre = (per_seq(a) for a in _in_proj(seg(x), mod_seg, p["norm1_g"], p["w_in"], tm))
    if states is None:
        s_rwkv, mlstm_states = None, None
    else:
        s_rwkv, s_c, s_n, s_m = states
        mlstm_states = (s_c, s_n, jnp.broadcast_to(s_m[..., None], s_m.shape + (LANES,)))
    ya, s_rwkv = _rwkv(xa, s_rwkv, p, mixer_seqs)
    yb, s_c, s_n, s_m = _mlstm(ub, o_pre, mlstm_states, p, conv2d, mixer_seqs)
    xc = _merge(seg(x), seg(ya), seg(yb), mod_seg, p, tm)
    y = _moe(xc, mod_seg, p, tb)
    return y.reshape(b, t, d), s_rwkv, s_c, s_n, s_m[..., 0]


def kernel(x_prompt, x_sample, c, c_ctx, state_rwkv, state_mlstm_C, state_mlstm_n, state_mlstm_m, w_mod, b_mod, norm1_g, norm2_g, w_in, rwkv_mu, rwkv_w_r, rwkv_w_k, rwkv_w_v, rwkv_w0, rwkv_w1, rwkv_w2, rwkv_a0, rwkv_a1, rwkv_a2, rwkv_g1, rwkv_g2, rwkv_k_k, rwkv_k_a, rwkv_r_k, rwkv_ln_g, rwkv_ln_b, rwkv_w_proj, mlstm_conv, mlstm_w_q, mlstm_w_k, mlstm_w_v, mlstm_w_i, mlstm_b_i, mlstm_w_f, mlstm_b_f, mlstm_ln_g, mlstm_w_proj, w_out, moe_w_group, moe_b_group, moe_w_expert, moe_b_expert, moe_w1, moe_w3, moe_w2, final_norm_g):
    assert w_mod.shape[0] == 1, "single trunk layer"
    l = 0
    bp, dec = x_prompt.shape[0], x_sample.shape[0]
    bf = lambda w: w.astype(BF16)
    row = lambda w: w.reshape(1, -1).astype(F32)
    wg = jnp.concatenate([mlstm_w_i[l, 0], mlstm_w_i[l, 1], mlstm_w_f[l, 0], mlstm_w_f[l, 1]], axis=1)
    bg = jnp.concatenate([mlstm_b_i[l, 0], mlstm_b_i[l, 1], mlstm_b_f[l, 0], mlstm_b_f[l, 1]])
    wg = jnp.pad(wg, ((0, 0), (0, LANES - wg.shape[1])))
    bg = jnp.pad(bg, (0, LANES - bg.shape[0]))
    w_router = jnp.pad(jnp.concatenate([moe_w_group[l], moe_w_expert[l]], axis=1),
                       ((0, 0), (0, LANES - N_GROUPS - N_EXPERTS)))
    b_router = jnp.pad(jnp.concatenate([moe_b_group[l], moe_b_expert[l]]), (0, LANES - N_GROUPS - N_EXPERTS))
    p = {
        "norm1_g": row(norm1_g[l]), "norm2_g": row(norm2_g[l]), "w_in": w_in[l],
        "rwkv_mu": rwkv_mu[l], "rwkv_wr": bf(rwkv_w_r[l]), "rwkv_wk": bf(rwkv_w_k[l]), "rwkv_wv": bf(rwkv_w_v[l]),
        "rwkv_w0": rwkv_w0[l], "rwkv_w1": bf(rwkv_w1[l]), "rwkv_w2": bf(rwkv_w2[l]),
        "rwkv_a0": rwkv_a0[l], "rwkv_a1": bf(rwkv_a1[l]), "rwkv_a2": bf(rwkv_a2[l]),
        "rwkv_g1": bf(rwkv_g1[l]), "rwkv_g2": bf(rwkv_g2[l]),
        "rwkv_k_k": row(rwkv_k_k[l]), "rwkv_k_a": row(rwkv_k_a[l]), "rwkv_r_k": row(rwkv_r_k[l]),
        "rwkv_ln_g": row(rwkv_ln_g[l]), "rwkv_ln_b": row(rwkv_ln_b[l]),
        "rwkv_w_proj": rwkv_w_proj[l],
        "mlstm_conv": mlstm_conv[l].reshape(9, B_WIDTH),
        "mlstm_wq": bf(mlstm_w_q[l]), "mlstm_wk": bf(mlstm_w_k[l]), "mlstm_wv": bf(mlstm_w_v[l]),
        "mlstm_wg": bf(wg), "mlstm_bg": row(bg),
        "mlstm_wgt": bf(wg[:, :16].T), "mlstm_bgt": jnp.broadcast_to(bg[:16, None], (16, LANES)),
        "mlstm_ln_g": row(mlstm_ln_g[l]), "mlstm_w_proj": mlstm_w_proj[l], "w_out": w_out[l],
        "moe_w_router": bf(w_router), "moe_b_router": row(b_router),
        "moe_w1": moe_w1[l], "moe_w3": moe_w3[l], "moe_w2": moe_w2[l],
        "final_norm_g": row(final_norm_g),
    }
    cvec = jnp.concatenate([c_ctx[None, :], c, jnp.zeros((8 - 1 - dec, D_MODEL), F32)], axis=0)
    mod = _mod(cvec, w_mod[l], b_mod[l].reshape(1, -1))
    mod_ctx = mod[0:1][:, None, :]
    mod_lat = mod[1:1 + dec][:, None, :]

    yp, n_rwkv, n_c, n_n, n_m = _trunk(x_prompt, mod_ctx, None, False, p, 512, 256, 2)
    lat_states = tuple(s[:, l].astype(F32) for s in (state_rwkv, state_mlstm_C, state_mlstm_n, state_mlstm_m))
    ys, _, _, _, _ = _trunk(x_sample, mod_lat, lat_states, True, p, 512, 128, 1)
    dt = x_prompt.dtype
    return (yp, ys, n_rwkv[:, None].astype(dt), n_c[:, None].astype(dt), n_n[:, None].astype(dt),
            n_m[:, None].astype(dt))
```

```python
import functools

import jax
import jax.numpy as jnp
from jax import lax
from jax.experimental import pallas as pl
from jax.experimental.pallas import tpu as pltpu

F32 = jnp.float32
BF16 = jnp.bfloat16

D_MODEL = 1024
GRID_W = 64
A_HEAD_DIM = 64
A_WIDTH = D_MODEL // 2
A_HEADS = A_WIDTH // A_HEAD_DIM
B_HEAD_DIM = 128
B_WIDTH = D_MODEL // 2
B_HEADS = B_WIDTH // B_HEAD_DIM
CHUNK = 64
SCAN_HEADS = 2
RING = 3
ROW_TILE = 512
SUM_TILE = 256
N_GROUPS = 4
EXPERTS_PER_GROUP = 4
N_EXPERTS = N_GROUPS * EXPERTS_PER_GROUP
GROUP_LANE = N_GROUPS + N_EXPERTS
NORM_EPS = 1e-6
RWKV_GN_EPS = 64e-5
MLSTM_GN_EPS = 1e-5
LANES = 128
SUBLANES = 8
CONV_PAD = 72
VMEM_LIMIT = 56 * 1024 * 1024


def _mm(a, b):
    return jnp.dot(a.astype(BF16), b.astype(BF16), preferred_element_type=F32)


_NN, _NT, _TN = ((1,), (0,)), ((1,), (1,)), ((0,), (0,))


def _split(a):
    hi = a.astype(BF16)
    return hi, (a - hi.astype(F32)).astype(BF16)


def _split3(a):
    hi = a.astype(BF16)
    rest = a - hi.astype(F32)
    lo = rest.astype(BF16)
    return hi, lo, (rest - lo.astype(F32)).astype(BF16)


def _sum01(a, b, dims=_NN):
    dot = lambda p, q: lax.dot_general(p, q, (dims, ((), ())), preferred_element_type=F32)
    terms = [dot(x, b) for x in a] if isinstance(a, tuple) else [dot(a, x) for x in b]
    return functools.reduce(lambda p, q: p + q, terms)


def _mm_nt(a, b):
    return lax.dot_general(a.astype(BF16), b.astype(BF16), (((1,), (1,)), ((), ())),
                           preferred_element_type=F32)


def _mm_tn(a, b):
    return lax.dot_general(a.astype(BF16), b.astype(BF16), (((0,), (0,)), ((), ())),
                           preferred_element_type=F32)


def _sigmoid(x):
    return 1.0 / (1.0 + jnp.exp(-x))


def _silu(x):
    return x * _sigmoid(x)


def _softplus(x):
    return jnp.maximum(x, 0.0) + jnp.log(1.0 + jnp.exp(-jnp.abs(x)))


def _iota(shape, dim):
    return lax.broadcasted_iota(jnp.int32, shape, dim)


def _tri(reverse, inclusive, n=CHUNK):
    row, col = _iota((n, n), 0), _iota((n, n), 1)
    if reverse:
        return (col >= row) if inclusive else (col > row)
    return (col <= row) if inclusive else (col < row)


def _same_block(n, blk):
    sh = blk.bit_length() - 1
    row, col = _iota((n, n), 0), _iota((n, n), 1)
    return lax.shift_right_logical(row, sh) == lax.shift_right_logical(col, sh)


def _rmsnorm(x, g):
    return x * lax.rsqrt(jnp.mean(x * x, axis=-1, keepdims=True) + NORM_EPS) * g


def _cparams(sem):
    return pltpu.CompilerParams(dimension_semantics=sem, vmem_limit_bytes=VMEM_LIMIT)


def _full(shape):
    nd = len(shape)
    return pl.BlockSpec(shape, lambda *_: (0,) * nd)


def _mod_kernel(c_ref, w_ref, b_ref, o_ref):
    o_ref[...] = _mm(_silu(c_ref[...]), w_ref[...]) + b_ref[...]


def _mod(cvec, w_mod, b_mod):
    rows, d = cvec.shape
    n = w_mod.shape[1]
    tn = 1536
    return pl.pallas_call(
        _mod_kernel,
        grid=(n // tn,),
        in_specs=[_full((rows, d)), pl.BlockSpec((d, tn), lambda j: (0, j)),
                  pl.BlockSpec((1, tn), lambda j: (0, j))],
        out_specs=pl.BlockSpec((rows, tn), lambda j: (0, j)),
        out_shape=jax.ShapeDtypeStruct((rows, n), F32),
        compiler_params=_cparams(("arbitrary",)),
        name="mod",
    )(cvec, w_mod, b_mod)


W_IN_BLOCK = 512


def _modulated_norm1(x_ref, mod_ref, g_ref):
    d = D_MODEL
    shift, scale = mod_ref[:, 0:d], mod_ref[:, d:2 * d]
    return (_rmsnorm(x_ref[...], g_ref[...]) * (1.0 + scale) + shift).astype(BF16)


def _w_in_blocks(w_in, first, count):
    specs = [pl.BlockSpec((w_in.shape[0], W_IN_BLOCK), lambda *_, j=first + k: (0, j)) for k in range(count)]
    return specs, [w_in] * count


def _in_kernel(x_ref, mod_ref, g_ref, wa_ref, wu_ref, wo_ref, xa_ref, ub_ref, op_ref):
    h = _modulated_norm1(x_ref, mod_ref, g_ref)
    for ref, w_ref in ((xa_ref, wa_ref), (ub_ref, wu_ref), (op_ref, wo_ref)):
        ref[...] = _mm(h, w_ref[...])


def _in_proj(x, mod, g1, w_in, tm):
    b, t, d = x.shape
    per = t // tm
    widths = (A_WIDTH, B_WIDTH, B_WIDTH)
    assert all(n == W_IN_BLOCK for n in widths)
    tok = lambda n: pl.BlockSpec((None, tm, n), lambda i: (i // per, i % per, 0))
    w_specs, w_args = _w_in_blocks(w_in, 0, 3)
    return pl.pallas_call(
        _in_kernel,
        grid=(b * per,),
        in_specs=[tok(d), pl.BlockSpec((None, 1, 6 * d), lambda i: (i // per, 0, 0)), _full((1, d))] + w_specs,
        out_specs=[tok(n) for n in widths],
        out_shape=[jax.ShapeDtypeStruct((b, t, n), F32) for n in widths],
        compiler_params=_cparams(("arbitrary",)),
        name="in_proj",
    )(x, mod, g1, *w_args)


def _head_of_lane(shape):
    return lax.shift_right_logical(_iota(shape, 1), A_HEAD_DIM.bit_length() - 1)


def _pair_diag(x):
    head = _head_of_lane(x.shape)
    return jnp.concatenate([x * (head == h).astype(x.dtype) for h in range(x.shape[1] // A_HEAD_DIM)], axis=0)


def _pair_mm3(lhs, rhs, dims=_NN):
    rows = lhs[0].shape[0]
    if dims == _NT:
        wh, wl = _split(_pair_diag(rhs).T)
    else:
        rh, rl = _split(rhs)
        wh, wl = _pair_diag(rh), _pair_diag(rl)
    parts = [_split(l) for l in lhs]
    top = jnp.dot(jnp.concatenate([t for part in parts for t in part], axis=0), wh, preferred_element_type=F32)
    low = jnp.dot(jnp.concatenate([part[0] for part in parts], axis=0), wl, preferred_element_type=F32)
    return [top[2 * k * rows:(2 * k + 1) * rows] + top[(2 * k + 1) * rows:(2 * k + 2) * rows]
            + low[k * rows:(k + 1) * rows] for k in range(len(lhs))]


def _pair_index(shape):
    return _iota(shape, 0), _iota(shape, 1) & (A_HEAD_DIM - 1)


def _pair_tri_inverses(mats):
    t_idx, s_idx = _pair_index(mats[0].shape)
    same = lambda blk: (lax.shift_right_logical(t_idx, blk.bit_length() - 1)
                        == lax.shift_right_logical(s_idx, blk.bit_length() - 1))
    eye = (t_idx == s_idx).astype(F32)
    nd = [jnp.where(same(8), m, 0.0) for m in mats]
    n2 = [_pair_mm3([x], x)[0] for x in nd]
    t = [eye + x for x in nd]
    both = [_pair_mm3([a, b], a) for a, b in zip(n2, t)]
    t = [x + r[1] for x, r in zip(t, both)]
    t = [x + _pair_mm3([x], r[0])[0] for x, r in zip(t, both)]
    blk = 8
    while blk < CHUNK:
        sel = same(2 * blk) & jnp.logical_not(same(blk))
        w = [_pair_mm3([jnp.where(sel, m, 0.0)], x)[0] for m, x in zip(mats, t)]
        t = [x + _pair_mm3([x], y)[0] for x, y in zip(t, w)]
        blk *= 2
    return t


def _rwkv_kernel(x_ref, s0_ref, mu_ref, wr_ref, wk_ref, wv_ref, w0_ref, w1_ref, w2_ref,
                 a0_ref, a1_ref, a2_ref, g1_ref, g2_ref, kk_ref, ka_ref, rk_ref, lng_ref, lnb_ref,
                 y_ref, s_ref,
                 r_s, v_s, kk_s, cum_s, cx_s, kd_s, bb_s, g_s, bonus_s, y_s, sp_s, *, seq_len):
    total = x_ref.shape[0]
    n_seq = total // seq_len
    n_chunks = seq_len // CHUNK
    n_tiles = total // ROW_TILE
    hd = A_HEAD_DIM
    n_groups, width = sp_s.shape[2], sp_s.shape[4]
    per_group = width // hd
    x = x_ref[...]
    row = _iota(x.shape, 0) & (seq_len - 1)
    prev = jnp.where(row == 0, 0.0, pltpu.roll(x, 1, 0))
    nxt = jnp.where(row == seq_len - 1, 0.0, pltpu.roll(x, total - 1, 0))
    y_s[...] = 0.5 * (prev + nxt) - x
    head_ones = _same_block(A_WIDTH, hd).astype(BF16)
    chunk_diag = _same_block(SUM_TILE, CHUNK)
    run_sum = [(chunk_diag & _tri(d == 1, True, SUM_TILE)).astype(BF16) for d in range(2)]

    def project_tile(ti, carry):
        rows = pl.ds(pl.multiple_of(ti * ROW_TILE, ROW_TILE), ROW_TILE)
        x, xx = x_ref[rows, :], y_s[rows, :]
        xr, xw, xk, xv, xi, xg = (x + xx * mu_ref[j:j + 1, :] for j in range(6))
        r = _mm(xr, wr_ref[...])
        k = _mm(xk, wk_ref[...])
        v = _mm(xv, wv_ref[...])
        g_s[rows, :] = _mm(_sigmoid(_mm(xg, g1_ref[...])), g2_ref[...])
        kk = k * kk_ref[...]
        kk = kk / jnp.maximum(jnp.sqrt(_sum01(_split(kk * kk), head_ones)), 1e-12)
        r_s[rows, :] = r
        v_s[rows, :] = v
        kk_s[rows, :] = kk
        kd_sum = jnp.zeros_like(x)
        for d in range(2):
            zw = w0_ref[d:d + 1, :] + _mm(jnp.tanh(_mm(xw, w1_ref[d])), w2_ref[d])
            lw = -jnp.exp(-_softplus(-zw) - 0.5)
            cum = jnp.concatenate([_sum01(run_sum[d], _split3(lw[j:j + SUM_TILE]))
                                   for j in range(0, ROW_TILE, SUM_TILE)], axis=0)
            cum_s[d, rows, :] = cum
            cx_s[d, rows, :] = cum - lw
            a = _sigmoid(a0_ref[d:d + 1, :] + _mm(_mm(xi, a1_ref[d]), a2_ref[d]))
            kd = k * (1.0 + (a - 1.0) * ka_ref[...])
            kd_s[d, rows, :] = kd
            bb_s[d, rows, :] = kk * a
            kd_sum = kd_sum + kd
        bonus_s[rows, :] = _sum01(_split(r * kd_sum * rk_ref[...]), head_ones) * v
        y_s[rows, :] = jnp.zeros_like(x)
        return carry

    lax.fori_loop(0, n_tiles, project_tile, 0)
    for n in range(n_seq):
        for d in range(2):
            for p in range(n_groups):
                sp_s[n, d, p] = (jnp.zeros((hd, width), F32) if s0_ref is None else jnp.concatenate(
                    [s0_ref[n, d, per_group * p + j] for j in range(per_group)], axis=1))

    t_idx, s_idx = _pair_index((CHUNK, width))
    strict = [s_idx < t_idx, s_idx > t_idx]
    incl = [s_idx <= t_idx, s_idx >= t_idx]
    lane_head = _head_of_lane((CHUNK, width))

    def chunk_body(ci, carry):
        chains = []
        for n, d in [(n, d) for n in range(n_seq) for d in range(2)]:
            cpos = n * n_chunks + ((n_chunks - 1 - ci) if d == 1 else ci)
            last = 0 if d == 1 else CHUNK - 1
            rows = pl.ds(pl.multiple_of(cpos * CHUNK, CHUNK), CHUNK)
            cum, cx = cum_s[d, rows, :], cx_s[d, rows, :]
            tot = cum[last:last + 1, :]
            kdc, bbc = kd_s[d, rows, :], bb_s[d, rows, :]
            e_neg = jnp.exp(-cum)
            e_tail = jnp.exp(tot - cum)
            at = -kk_s[rows, :] * jnp.exp(cx)
            rt = r_s[rows, :] * jnp.exp(cum)
            bt, kt = bbc * e_neg, kdc * e_neg
            bp, kp = bbc * e_tail, kdc * e_tail
            pc = jnp.exp(tot)
            vc = v_s[rows, :]
            for p in range(n_groups):
                ln = slice(p * width, (p + 1) * width)
                chains.append(dict(d=d, st=(n, d, p), rows=rows, ln=ln, v=vc[:, ln], pc=pc[:, ln],
                                   lhs=jnp.concatenate([at[:, ln], rt[:, ln]], axis=0),
                                   bt=bt[:, ln], kt=kt[:, ln], bp=bp[:, ln], kp=kp[:, ln]))
        for c in chains:
            c["sb"] = _pair_mm3([c["lhs"]], c["bt"], _NT)[0]
            c["sk"] = _pair_mm3([c["lhs"]], c["kt"], _NT)[0]
            c["s"] = sp_s[c["st"]]
            c["fs"] = _mm(c["lhs"], _pair_diag(c["s"]).T)
            c["v_diag"] = _pair_diag(c["v"].astype(BF16))
        for c in chains:
            a_ak = jnp.where(strict[c["d"]], c["sk"][:CHUNK], 0.0)
            c["x"] = c["fs"][:CHUNK] + _mm(a_ak, c["v_diag"])
        inv = _pair_tri_inverses([jnp.where(strict[c["d"]], c["sb"][:CHUNK], 0.0) for c in chains])
        for c, t in zip(chains, inv):
            c["u"] = _pair_mm3([t], c["x"])[0]
        for c in chains:
            d, rows, ln, u = c["d"], c["rows"], c["ln"], c["u"]
            a_r = jnp.concatenate([jnp.where(incl[d], c["sb"][CHUNK:], 0.0),
                                   jnp.where(incl[d], c["sk"][CHUNK:], 0.0)], axis=1)
            uv_diag = jnp.concatenate([_pair_diag(u.astype(BF16)), c["v_diag"]], axis=0)
            y_s[rows, ln] = y_s[rows, ln] + c["fs"][CHUNK:] + _mm(a_r, uv_diag)
            g = _mm_tn(jnp.concatenate([u, c["v"]], axis=0), jnp.concatenate([c["bp"], c["kp"]], axis=0))
            own = g[0:hd]
            for j in range(1, per_group):
                own = jnp.where(lane_head == j, g[j * hd:(j + 1) * hd], own)
            sp_s[c["st"]] = c["s"] * c["pc"] + own
        return carry

    lax.fori_loop(0, n_chunks, chunk_body, 0)
    for n in range(n_seq):
        for d in range(2):
            for p in range(n_groups):
                for j in range(per_group):
                    s_ref[n, d, per_group * p + j] = sp_s[n, d, p, :, j * hd:(j + 1) * hd]

    def finish_tile(ti, carry):
        rows = pl.ds(pl.multiple_of(ti * ROW_TILE, ROW_TILE), ROW_TILE)
        ys = y_s[rows, :]
        cen = ys - _sum01(_split(ys), head_ones) * (1.0 / hd)
        var = _sum01(_split(cen * cen), head_ones) * (1.0 / hd)
        yn = cen * lax.rsqrt(var + RWKV_GN_EPS) * lng_ref[...] + lnb_ref[...]
        y_ref[rows, :] = (yn + bonus_s[rows, :]) * g_s[rows, :]
        return carry

    lax.fori_loop(0, n_tiles, finish_tile, 0)


def _rwkv(xa, s0, p, n_seq):
    b, t, w = xa.shape
    assert t & (t - 1) == 0 and b % n_seq == 0
    names = ("mu", "wr", "wk", "wv", "w0", "w1", "w2", "a0", "a1", "a2", "g1", "g2",
             "k_k", "k_a", "r_k", "ln_g", "ln_b")
    weights = [p["rwkv_" + n] for n in names]
    rows = n_seq * t
    seq = pl.BlockSpec((None, rows, w), lambda i: (i, 0, 0))
    st = pl.BlockSpec((n_seq, 2, A_HEADS, A_HEAD_DIM, A_HEAD_DIM), lambda i: (i, 0, 0, 0, 0))
    tw = lambda lead=(): pltpu.VMEM(lead + (rows, w), F32)
    kernel = functools.partial(_rwkv_kernel, seq_len=t)
    if s0 is None:
        kernel, state_specs, states = _zero_state_kernel(kernel, 1, 1), [], ()
    else:
        state_specs, states = [st], (s0,)
    y, s_out = pl.pallas_call(
        kernel,
        grid=(b // n_seq,),
        in_specs=[seq] + state_specs + [_full(x.shape) for x in weights],
        out_specs=[seq, st],
        out_shape=[jax.ShapeDtypeStruct((b // n_seq, rows, w), F32),
                   jax.ShapeDtypeStruct((b, 2, A_HEADS, A_HEAD_DIM, A_HEAD_DIM), F32)],
        scratch_shapes=[tw(), tw(), tw(), tw((2,)), tw((2,)), tw((2,)), tw((2,)), tw(), tw(), tw(),
                        pltpu.VMEM((n_seq, 2, A_HEADS // SCAN_HEADS, A_HEAD_DIM, SCAN_HEADS * A_HEAD_DIM), F32)],
        compiler_params=_cparams(("arbitrary",)),
        name="rwkv",
    )(xa.reshape(b // n_seq, rows, w), *states, *weights)
    return y.reshape(b, t, w), s_out


def _mlstm_kernel(u_ref, op_ref, c0_ref, n0_ref, m0_ref, conv_ref, wq_ref, wk_ref, wv_ref,
                  wg_ref, bg_ref, wgt_ref, bgt_ref, lng_ref,
                  y_ref, c_ref, n_ref, m_ref,
                  pad_s, q_s, k_s, v_s, dd_s, cr_s, h_s, *, conv2d, seq_len):
    t_len = u_ref.shape[0]
    n_seq = t_len // seq_len
    n_chunks = seq_len // CHUNK
    hd = B_HEAD_DIM
    u = u_ref[...]
    pad_s[0:CONV_PAD, :] = jnp.zeros((CONV_PAD, B_WIDTH), F32)
    pad_s[CONV_PAD + t_len:CONV_PAD + t_len + CONV_PAD, :] = jnp.zeros((CONV_PAD, B_WIDTH), F32)
    pad_s[CONV_PAD:CONV_PAD + t_len, :] = u
    pos = _iota(u.shape, 0) & (seq_len - 1)
    col = pos & (GRID_W - 1)
    conv = jnp.zeros_like(u)
    for kh in range(3):
        if not conv2d and kh != 1:
            continue
        for kw in range(3):
            shift = (kh - 1) * GRID_W + (kw - 1)
            term = pad_s[CONV_PAD + shift:CONV_PAD + shift + t_len, :] * conv_ref[kh * 3 + kw:kh * 3 + kw + 1, :]
            if conv2d and kw != 1:
                src = col + (kw - 1)
                term = jnp.where((src >= 0) & (src < GRID_W), term, 0.0)
            if n_seq > 1 and shift != 0:
                term = jnp.where((pos + shift >= 0) & (pos + shift < seq_len), term, 0.0)
            conv = conv + term
    uc = _silu(conv)
    for h in range(B_HEADS):
        ln = slice(h * hd, (h + 1) * hd)
        q_s[:, ln] = _mm(uc[:, ln], wq_ref[h])
        k_s[:, ln] = _mm(uc[:, ln], wk_ref[h]) * (hd ** -0.5)
        v_s[:, ln] = _mm(u[:, ln], wv_ref[h])
    gcol = _mm(uc, wg_ref[...]) + bg_ref[...]
    gcol = jnp.where(_iota(gcol.shape, 1) >= 8, -_softplus(-gcol), gcol)
    grow = _mm_nt(wgt_ref[...], uc) + bgt_ref[:, 0:1]
    grow = jnp.where(_iota(grow.shape, 0) >= 8, -_softplus(-grow), grow)
    chunk_diag = _same_block(SUM_TILE, CHUNK)
    run_sum = [(chunk_diag & _tri(d == 1, True, SUM_TILE)).astype(BF16) for d in range(2)]
    per_tile = SUM_TILE // CHUNK
    for ti in range(t_len // SUM_TILE):
        sl = slice(ti * SUM_TILE, (ti + 1) * SUM_TILE)
        gt, rt = _split3(gcol[sl, :]), _split3(grow[:, sl])
        backward_lane = (_iota((SUM_TILE, LANES), 1) & 4) != 0
        cum_c = jnp.where(backward_lane, _sum01(run_sum[1], gt, _NN), _sum01(run_sum[0], gt, _NN))
        dd_s[sl, :] = pltpu.roll(cum_c, LANES - 8, 1) - gcol[sl, :]
        backward_row = (_iota((16, SUM_TILE), 0) & 4) != 0
        cum_r = jnp.where(backward_row, _sum01(rt, run_sum[1], _NT), _sum01(rt, run_sum[0], _NT))
        for c in range(per_tile):
            cr_s[ti * per_tile + c] = cum_r[:, c * CHUNK:(c + 1) * CHUNK]
    h_s[...] = jnp.zeros_like(u)
    for ref, init in ((c_ref, c0_ref), (n_ref, n0_ref), (m_ref, m0_ref)):
        ref[...] = jnp.zeros(ref.shape, F32) if init is None else init[...]
    row_i, col_i = _iota((CHUNK, CHUNK), 0), _iota((CHUNK, CHUNK), 1)
    incl_t = [row_i <= col_i, row_i >= col_i]
    eye = row_i == col_i
    pick = [(_iota((LANES, B_HEADS * LANES), 0)
             == d * B_HEADS + lax.shift_right_logical(_iota((LANES, B_HEADS * LANES), 1), LANES.bit_length() - 1)
             ).astype(BF16) for d in range(2)]
    spread = (lax.shift_right_logical(_iota((2 * CHUNK, 2 * LANES), 0), CHUNK.bit_length() - 1)
              == lax.shift_right_logical(_iota((2 * CHUNK, 2 * LANES), 1), LANES.bit_length() - 1)).astype(BF16)
    ones_bf = jnp.ones((CHUNK, LANES), BF16)

    def chunk_body(ci, carry):
        chains = []
        for n, d in [(n, d) for n in range(n_seq) for d in range(2)]:
            cpos = n * n_chunks + ((n_chunks - 1 - ci) if d == 1 else ci)
            last = 0 if d == 1 else CHUNK - 1
            rows = pl.ds(pl.multiple_of(cpos * CHUNK, CHUNK), CHUNK)
            cr = cr_s[cpos]
            d_col = _sum01(_split3(dd_s[rows, :]), pick[d])
            for h in range(B_HEADS):
                ln = slice(h * hd, (h + 1) * hd)
                lf = 8 + d * 4 + h
                chains.append(dict(d=d, st=(n, d, h), vec=(n, d, slice(h, h + 1)), rows=rows, ln=ln, last=last,
                                   b_row=cr[lf:lf + 1, :], d_col=d_col[:, ln],
                                   m=m_ref[n, d, h:h + 1, 0:1], q=q_s[rows, ln], k=k_s[rows, ln], v=v_s[rows, ln],
                                   c=c_ref[n, d, h], n=n_ref[n, d, h:h + 1, :]))
        for c in chains:
            c["kq"] = _mm(c["k"], c["q"].T)
            c["qc"] = _mm(c["q"], c["c"])
            c["qn"] = _mm(c["q"], jnp.broadcast_to(c["n"], (LANES, hd)).T)
            logw = jnp.where(incl_t[c["d"]], c["b_row"] - c["d_col"][:, 0:CHUNK], -jnp.inf)
            c["m_row"] = jnp.maximum(c["b_row"] + c["m"], jnp.max(logw, axis=0, keepdims=True))
            c["logw"] = logw
        for c in chains:
            inter_row = jnp.exp(c["b_row"] + c["m"] - c["m_row"])
            diag = jnp.concatenate([jnp.where(eye, c["m_row"], 0.0), jnp.where(eye, inter_row, 0.0)], axis=1)
            cols = _sum01(_split3(diag), spread)
            c["m_col"], c["inter"] = cols[:, 0:LANES], cols[:, LANES:2 * LANES]
            c["s"] = c["kq"] * jnp.exp(c["logw"] - c["m_row"])
        for c in chains:
            s_hi, s_lo = _split(c["s"])
            sv = lax.dot_general(s_hi, jnp.concatenate([c["v"].astype(BF16), ones_bf], axis=1),
                                 (_TN, ((), ())), preferred_element_type=F32)
            c["sv"] = sv[:, 0:hd]
            c["den"] = (c["inter"] * c["qn"] + sv[:, hd:hd + LANES]
                        + lax.dot_general(s_lo, ones_bf, (_TN, ((), ())), preferred_element_type=F32))
            last = c["last"]
            c["m_new"] = c["m_row"][:, last:last + 1]
            b_last = c["b_row"][:, last:last + 1]
            c["decay"] = jnp.exp(b_last + c["m"] - c["m_new"])
            c["kw"] = c["k"] * jnp.exp(b_last - c["m_new"] - c["d_col"])
        for c in chains:
            c["kv"] = _mm_tn(c["kw"], c["v"])
        for c in chains:
            rows, ln = c["rows"], c["ln"]
            num = c["inter"] * c["qc"] + c["sv"]
            h_s[rows, ln] = h_s[rows, ln] + num / jnp.maximum(jnp.abs(c["den"]), jnp.exp(-c["m_col"]))
            c_ref[c["st"]] = c["decay"] * c["c"] + c["kv"]
            n_ref[c["vec"]] = c["decay"] * c["n"] + jnp.sum(c["kw"], axis=0, keepdims=True)
            m_ref[c["vec"]] = jnp.broadcast_to(c["m_new"], (1, LANES))
        return carry

    lax.fori_loop(0, n_chunks, chunk_body, 0)

    for h in range(B_HEADS):
        ln = slice(h * hd, (h + 1) * hd)
        hh = h_s[:, ln]
        cen = hh - jnp.mean(hh, axis=1, keepdims=True)
        var = jnp.mean(cen * cen, axis=1, keepdims=True)
        y_ref[:, ln] = _sigmoid(op_ref[:, ln]) * (cen * lax.rsqrt(var + MLSTM_GN_EPS) * lng_ref[:, ln])


def _zero_state_kernel(kernel, n_data, n_state):
    def wrapped(*refs, **kw):
        return kernel(*refs[:n_data], *([None] * n_state), *refs[n_data:], **kw)
    return wrapped


def _mlstm(ub, o_pre, states, p, conv2d, n_seq):
    b, t, w = ub.shape
    assert t & (t - 1) == 0 and b % n_seq == 0
    names = ("conv", "wq", "wk", "wv", "wg", "bg", "wgt", "bgt", "ln_g")
    weights = [p["mlstm_" + n] for n in names]
    rows, steps = n_seq * t, b // n_seq
    seq = pl.BlockSpec((None, rows, w), lambda i: (i, 0, 0))
    cst = pl.BlockSpec((n_seq, 2, B_HEADS, B_HEAD_DIM, B_HEAD_DIM), lambda i: (i, 0, 0, 0, 0))
    vst = pl.BlockSpec((n_seq, 2, B_HEADS, LANES), lambda i: (i, 0, 0, 0))
    kernel = functools.partial(_mlstm_kernel, conv2d=conv2d, seq_len=t)
    if states is None:
        kernel, state_specs, states = _zero_state_kernel(kernel, 2, 3), [], ()
    else:
        state_specs = [cst, vst, vst]
    y, c_out, n_out, m_out = pl.pallas_call(
        kernel,
        grid=(steps,),
        in_specs=[seq, seq] + state_specs + [_full(x.shape) for x in weights],
        out_specs=[seq, cst, vst, vst],
        out_shape=[jax.ShapeDtypeStruct((steps, rows, w), F32),
                   jax.ShapeDtypeStruct((b, 2, B_HEADS, B_HEAD_DIM, B_HEAD_DIM), F32),
                   jax.ShapeDtypeStruct((b, 2, B_HEADS, LANES), F32),
                   jax.ShapeDtypeStruct((b, 2, B_HEADS, LANES), F32)],
        scratch_shapes=[pltpu.VMEM((rows + 2 * CONV_PAD, w), F32), pltpu.VMEM((rows, w), F32),
                        pltpu.VMEM((rows, w), F32), pltpu.VMEM((rows, w), F32),
                        pltpu.VMEM((rows, LANES), F32), pltpu.VMEM((rows // CHUNK, 16, CHUNK), F32),
                        pltpu.VMEM((rows, w), F32)],
        compiler_params=_cparams(("arbitrary",)),
        name="mlstm",
    )(ub.reshape(steps, rows, w), o_pre.reshape(steps, rows, w), *states, *weights)
    return y.reshape(b, t, w), c_out, n_out, m_out


def _first_lane_of_max(val, lane, valid):
    masked = jnp.where(valid, val, -jnp.inf)
    best = jnp.max(masked, axis=1, keepdims=True)
    idx = jnp.min(jnp.where(valid & (masked == best), lane, LANES), axis=1, keepdims=True)
    return best, idx


def _merge_kernel(x_ref, ya_ref, yb_ref, mod_ref, g1_ref, wga0_ref, wga1_ref, wgb0_ref, wgb1_ref,
                  wpa_ref, wpb_ref, wo_ref, g2_ref, wr_ref, br_ref, xc_ref):
    d = D_MODEL
    gate1 = mod_ref[:, 2 * d:3 * d]
    shift2, scale2 = mod_ref[:, 3 * d:4 * d], mod_ref[:, 4 * d:5 * d]
    h1 = _modulated_norm1(x_ref, mod_ref, g1_ref)
    pa, pb = _mm(ya_ref[...], wpa_ref[...]), _mm(yb_ref[...], wpb_ref[...])
    halves = []
    for j, (wga_ref, wgb_ref) in enumerate(((wga0_ref, wgb0_ref), (wga1_ref, wgb1_ref))):
        cols = slice(j * W_IN_BLOCK, (j + 1) * W_IN_BLOCK)
        halves.append(_sigmoid(_mm(h1, wga_ref[...])) * pa[:, cols] + _sigmoid(_mm(h1, wgb_ref[...])) * pb[:, cols])
    merged = jnp.concatenate(halves, axis=1)
    x1 = x_ref[...] + gate1 * _mm(merged, wo_ref[...])
    xc_ref[:, 0:d] = x1
    h2 = _rmsnorm(x1, g2_ref[...]) * (1.0 + scale2) + shift2
    logits = _mm(h2, wr_ref[...]) + br_ref[...]
    lane = _iota(logits.shape, 1)
    is_group = lane < N_GROUPS
    g_max, g_sel = _first_lane_of_max(logits, lane, is_group)
    g_w = 1.0 / jnp.sum(jnp.where(is_group, jnp.exp(logits - g_max), 0.0), axis=1, keepdims=True)
    expert = lane - N_GROUPS
    in_group = (expert >= 0) & (expert < N_EXPERTS) & (lax.shift_right_arithmetic(expert, EXPERTS_PER_GROUP.bit_length() - 1) == g_sel)
    e_max = jnp.max(jnp.where(in_group, logits, -jnp.inf), axis=1, keepdims=True)
    e_exp = jnp.where(in_group, jnp.exp(logits - e_max), 0.0)
    prob = e_exp / jnp.sum(e_exp, axis=1, keepdims=True)
    p1, i1 = _first_lane_of_max(prob, lane, in_group)
    p2, i2 = _first_lane_of_max(prob, lane, in_group & (lane != i1))
    denom = p1 + p2
    comb = jnp.where(lane == i1, g_w * p1 / denom, 0.0) + jnp.where(lane == i2, g_w * p2 / denom, 0.0)
    xc_ref[:, d:d + LANES] = jnp.where(lane == GROUP_LANE, g_sel.astype(F32), comb)


def _merge(x, ya, yb, mod, p, tm):
    b, t, d = x.shape
    per = t // tm
    assert d == 2 * W_IN_BLOCK
    tok = lambda n: pl.BlockSpec((None, tm, n), lambda i: (i // per, i % per, 0))
    gate_specs, gate_args = _w_in_blocks(p["w_in"], 3, 4)
    weights = [p["rwkv_w_proj"], p["mlstm_w_proj"], p["w_out"], p["norm2_g"], p["moe_w_router"], p["moe_b_router"]]
    return pl.pallas_call(
        _merge_kernel,
        grid=(b * per,),
        in_specs=[tok(d), tok(A_WIDTH), tok(B_WIDTH), pl.BlockSpec((None, 1, 6 * d), lambda i: (i // per, 0, 0)),
                  _full((1, d))] + gate_specs + [_full(w.shape) for w in weights],
        out_specs=tok(d + LANES),
        out_shape=jax.ShapeDtypeStruct((b, t, d + LANES), F32),
        compiler_params=_cparams(("arbitrary",)),
        name="merge",
    )(x, ya, yb, mod, p["norm1_g"], *gate_args, *weights)


def _route(gsel, tb):
    s, l = gsel.shape
    n_buckets = s * N_GROUPS
    max_tiles = s * (l // tb + N_GROUPS - 1)
    onehot = (gsel[..., None] == jnp.arange(N_GROUPS, dtype=jnp.int32)).astype(jnp.int32)
    rank = jnp.sum((jnp.cumsum(onehot, axis=1) - onehot) * onehot, axis=-1)
    n_tiles = ((jnp.sum(onehot, axis=1) + tb - 1) // tb).T.reshape(n_buckets)
    ends = jnp.cumsum(n_tiles)
    first_tile = jnp.sum(onehot * (ends - n_tiles).reshape(N_GROUPS, s).T[:, None, :], axis=-1)
    pos = first_tile * tb + rank
    tok = jnp.arange(s * l, dtype=jnp.int32)
    row_src = jnp.full((max_tiles * tb,), -1, jnp.int32).at[pos.reshape(-1)].set(tok)
    tile_bucket = jnp.sum(jnp.arange(max_tiles, dtype=jnp.int32)[:, None] >= ends[None, :], axis=1)
    tile_bucket = jnp.minimum(tile_bucket, n_buckets - 1).astype(jnp.int32)
    return row_src, tile_bucket % s, tile_bucket // s, ends[-1:].astype(jnp.int32)


def _moe_kernel(gsrc_ref, sdst_ref, seg_ref, grp_ref, used_ref, cnt_ref,
                xc_hbm, mod_ref, g2_ref, w1_ref, w3_ref, w2_ref, gf_ref, dump_in_hbm,
                y_hbm, dump_hbm, gbuf, obuf, gsem, ssem, *, tb):
    d = D_MODEL
    q = pl.program_id(0)
    n_used = used_ref[0]

    def groups(tile):
        return lax.shift_right_logical(cnt_ref[tile] + (SUBLANES - 1), SUBLANES.bit_length() - 1)

    def start_gather(tile, sl):
        def body(i, carry):
            for u in range(SUBLANES):
                idx = gsrc_ref[tile * tb + i * SUBLANES + u]
                pltpu.make_async_copy(xc_hbm.at[lax.shift_right_logical(idx, SUBLANES.bit_length() - 1), pl.ds(idx & (SUBLANES - 1), 1), :],
                                      gbuf.at[sl, i, pl.ds(u, 1), :], gsem.at[sl]).start()
            return carry

        lax.fori_loop(0, groups(tile), body, 0)

    def wait_gather(tile, sl):
        def body(i, carry):
            pltpu.make_async_copy(xc_hbm.at[pl.ds(0, 1)], gbuf.at[sl, pl.ds(0, 1)], gsem.at[sl]).wait()
            return carry

        lax.fori_loop(0, groups(tile), body, 0)

    def start_scatter(tile, sl):
        def body(i, carry):
            for u in range(SUBLANES):
                idx = sdst_ref[tile * tb + i * SUBLANES + u]

                @pl.when(idx >= 0)
                def _():
                    pltpu.make_async_copy(
                        obuf.at[sl, i, pl.ds(u, 1), :],
                        y_hbm.at[lax.shift_right_logical(idx, SUBLANES.bit_length() - 1), pl.ds(idx & (SUBLANES - 1), 1), :],
                        ssem.at[sl]).start()

                @pl.when(idx < 0)
                def _():
                    pltpu.make_async_copy(obuf.at[sl, i, pl.ds(u, 1), :], dump_hbm.at[sl, i, pl.ds(u, 1), :],
                                          ssem.at[sl]).start()

            return carry

        lax.fori_loop(0, groups(tile), body, 0)

    def wait_scatter(tile, sl):
        def body(i, carry):
            pltpu.make_async_copy(obuf.at[sl, pl.ds(0, 1)], dump_hbm.at[sl, pl.ds(0, 1)], ssem.at[sl]).wait()
            return carry

        lax.fori_loop(0, groups(tile), body, 0)

    def gather_rows(tile, sl, lo, hi):
        n = groups(tile)
        for r in range(lo, hi):
            idx = gsrc_ref[tile * tb + r]

            @pl.when(r // SUBLANES < n)
            def _():
                pltpu.make_async_copy(xc_hbm.at[lax.shift_right_logical(idx, SUBLANES.bit_length() - 1),
                                                pl.ds(idx & (SUBLANES - 1), 1), :],
                                      gbuf.at[sl, r // SUBLANES, pl.ds(r % SUBLANES, 1), :], gsem.at[sl]).start()

    def scatter_rows(tile, sl, lo, hi, enabled):
        n = groups(tile)
        for r in range(lo, hi):
            idx = sdst_ref[tile * tb + r]
            src = obuf.at[sl, r // SUBLANES, pl.ds(r % SUBLANES, 1), :]
            live = enabled & (r // SUBLANES < n)

            @pl.when(live & (idx >= 0))
            def _():
                pltpu.make_async_copy(src, y_hbm.at[lax.shift_right_logical(idx, SUBLANES.bit_length() - 1),
                                                    pl.ds(idx & (SUBLANES - 1), 1), :], ssem.at[sl]).start()

            @pl.when(live & (idx < 0))
            def _():
                pltpu.make_async_copy(src, dump_hbm.at[sl, r // SUBLANES, pl.ds(r % SUBLANES, 1), :],
                                      ssem.at[sl]).start()

    @pl.when(q < n_used)
    def _():
        slot = lax.rem(q, RING)
        ahead = lax.rem(q + 2, RING)
        behind = lax.rem(q + 1, RING)

        @pl.when(q == 0)
        def _():
            gbuf[...] = jnp.zeros(gbuf.shape, F32)
            start_gather(0, 0)

            @pl.when(n_used > 1)
            def _():
                start_gather(1, 1)

        wait_gather(q, slot)

        @pl.when(q >= RING)
        def _():
            wait_scatter(q - RING, slot)

        nxt = jnp.minimum(q + 2, n_used - 1)
        prev, has_prev = jnp.maximum(q - 1, 0), q >= 1
        part = tb // EXPERTS_PER_GROUP
        rows = gbuf[slot].reshape(tb, gbuf.shape[-1])
        x1, comb = rows[:, 0:d], rows[:, d:d + LANES]
        shift2, scale2, gate2 = mod_ref[:, 3 * d:4 * d], mod_ref[:, 4 * d:5 * d], mod_ref[:, 5 * d:6 * d]
        h2 = (_rmsnorm(x1, g2_ref[...]) * (1.0 + scale2) + shift2).astype(BF16)
        first_lane = N_GROUPS + grp_ref[q] * EXPERTS_PER_GROUP
        lane = _iota(comb.shape, 1)
        acc = jnp.zeros((tb, d), F32)
        for e in range(EXPERTS_PER_GROUP):
            w_e = jnp.sum(jnp.where(lane == first_lane + e, comb, 0.0), axis=1, keepdims=True)
            a = _mm(h2, w1_ref[e])
            b = _mm(h2, w3_ref[e])
            acc = acc + _mm(_silu(a) * b * w_e, w2_ref[e])
            gather_rows(nxt, ahead, e * part, (e + 1) * part)
            scatter_rows(prev, ahead, e * part, (e + 1) * part, has_prev)
        obuf[slot] = _rmsnorm(x1 + gate2 * acc, gf_ref[...]).reshape(tb // SUBLANES, SUBLANES, d)

        @pl.when(q == n_used - 1)
        def _():
            start_scatter(q, slot)
            wait_scatter(q, slot)
            wait_gather(q, ahead)

            @pl.when(q >= 1)
            def _():
                wait_scatter(q - 1, ahead)
                wait_gather(q, behind)

            @pl.when(q >= 2)
            def _():
                wait_scatter(q - 2, behind)


def _moe(xc, mod, p, tb):
    s, l, width = xc.shape
    d = D_MODEL
    gsel = xc[:, :, d + GROUP_LANE].astype(jnp.int32)
    row_src, tile_seg, tile_grp, n_used = _route(gsel, tb)
    max_tiles = tile_seg.shape[0]
    tile_rows = jnp.sum((row_src.reshape(max_tiles, tb) >= 0).astype(jnp.int32), axis=1)
    grouped = lambda w: w.reshape((N_GROUPS, EXPERTS_PER_GROUP) + w.shape[1:])
    w1, w3, w2 = grouped(p["moe_w1"]), grouped(p["moe_w3"]), grouped(p["moe_w2"])
    const = lambda shape: pl.BlockSpec(shape, lambda q, *_: (0,) * len(shape))
    by_group = lambda w: pl.BlockSpec((None,) + w.shape[1:],
                                      lambda q, gs, sd, seg, grp, used, cnt: (grp[q], 0, 0, 0))
    hbm = pl.BlockSpec(memory_space=pl.ANY)
    y, _ = pl.pallas_call(
        functools.partial(_moe_kernel, tb=tb),
        grid_spec=pltpu.PrefetchScalarGridSpec(
            num_scalar_prefetch=6,
            grid=(max_tiles,),
            in_specs=[hbm, pl.BlockSpec((None, 1, 6 * d), lambda q, gs, sd, seg, grp, used, cnt: (seg[q], 0, 0)),
                      const((1, d)), by_group(w1), by_group(w3), by_group(w2), const((1, d)), hbm],
            out_specs=[hbm, hbm],
            scratch_shapes=[pltpu.VMEM((RING, tb // SUBLANES, SUBLANES, width), F32),
                            pltpu.VMEM((RING, tb // SUBLANES, SUBLANES, d), F32),
                            pltpu.SemaphoreType.DMA((RING,)), pltpu.SemaphoreType.DMA((RING,))]),
        out_shape=[jax.ShapeDtypeStruct((s * l // SUBLANES, SUBLANES, d), F32),
                   jax.ShapeDtypeStruct((RING, tb // SUBLANES, SUBLANES, d), F32)],
        input_output_aliases={13: 1},
        compiler_params=_cparams(("arbitrary",)),
        name="moe",
    )(jnp.maximum(row_src, 0), row_src, tile_seg, tile_grp, n_used, tile_rows,
      xc.reshape(s * l // SUBLANES, SUBLANES, width), mod, p["norm2_g"], w1, w3, w2, p["final_norm_g"],
      jnp.zeros((RING, tb // SUBLANES, SUBLANES, d), F32))
    return y.reshape(s, l, d)


def _trunk(x, mod_seg, states, conv2d, p, tm, tb, mixer_seqs):
    b, t, d = x.shape
    n_seg = mod_seg.shape[0]
    seg = lambda a: a.reshape(n_seg, b * t // n_seg, a.shape[-1])
    per_seq = lambda a: a.reshape(b, t, a.shape[-1])
    xa, ub, o_pre = (per_seq(a) for a in _in_proj(seg(x), mod_seg, p["norm1_g"], p["w_in"], tm))
    if states is None:
        s_rwkv, mlstm_states = None, None
    else:
        s_rwkv, s_c, s_n, s_m = states
        mlstm_states = (s_c, s_n, jnp.broadcast_to(s_m[..., None], s_m.shape + (LANES,)))
    ya, s_rwkv = _rwkv(xa, s_rwkv, p, mixer_seqs)
    yb, s_c, s_n, s_m = _mlstm(ub, o_pre, mlstm_states, p, conv2d, mixer_seqs)
    xc = _merge(seg(x), seg(ya), seg(yb), mod_seg, p, tm)
    y = _moe(xc, mod_seg, p, tb)
    return y.reshape(b, t, d), s_rwkv, s_c, s_n, s_m[..., 0]


def kernel(x_prompt, x_sample, c, c_ctx, state_rwkv, state_mlstm_C, state_mlstm_n, state_mlstm_m, w_mod, b_mod, norm1_g, norm2_g, w_in, rwkv_mu, rwkv_w_r, rwkv_w_k, rwkv_w_v, rwkv_w0, rwkv_w1, rwkv_w2, rwkv_a0, rwkv_a1, rwkv_a2, rwkv_g1, rwkv_g2, rwkv_k_k, rwkv_k_a, rwkv_r_k, rwkv_ln_g, rwkv_ln_b, rwkv_w_proj, mlstm_conv, mlstm_w_q, mlstm_w_k, mlstm_w_v, mlstm_w_i, mlstm_b_i, mlstm_w_f, mlstm_b_f, mlstm_ln_g, mlstm_w_proj, w_out, moe_w_group, moe_b_group, moe_w_expert, moe_b_expert, moe_w1, moe_w3, moe_w2, final_norm_g):
    assert w_mod.shape[0] == 1, "single trunk layer"
    l = 0
    bp, dec = x_prompt.shape[0], x_sample.shape[0]
    bf = lambda w: w.astype(BF16)
    row = lambda w: w.reshape(1, -1).astype(F32)
    wg = jnp.concatenate([mlstm_w_i[l, 0], mlstm_w_i[l, 1], mlstm_w_f[l, 0], mlstm_w_f[l, 1]], axis=1)
    bg = jnp.concatenate([mlstm_b_i[l, 0], mlstm_b_i[l, 1], mlstm_b_f[l, 0], mlstm_b_f[l, 1]])
    wg = jnp.pad(wg, ((0, 0), (0, LANES - wg.shape[1])))
    bg = jnp.pad(bg, (0, LANES - bg.shape[0]))
    w_router = jnp.pad(jnp.concatenate([moe_w_group[l], moe_w_expert[l]], axis=1),
                       ((0, 0), (0, LANES - N_GROUPS - N_EXPERTS)))
    b_router = jnp.pad(jnp.concatenate([moe_b_group[l], moe_b_expert[l]]), (0, LANES - N_GROUPS - N_EXPERTS))
    p = {
        "norm1_g": row(norm1_g[l]), "norm2_g": row(norm2_g[l]), "w_in": w_in[l],
        "rwkv_mu": rwkv_mu[l], "rwkv_wr": bf(rwkv_w_r[l]), "rwkv_wk": bf(rwkv_w_k[l]), "rwkv_wv": bf(rwkv_w_v[l]),
        "rwkv_w0": rwkv_w0[l], "rwkv_w1": bf(rwkv_w1[l]), "rwkv_w2": bf(rwkv_w2[l]),
        "rwkv_a0": rwkv_a0[l], "rwkv_a1": bf(rwkv_a1[l]), "rwkv_a2": bf(rwkv_a2[l]),
        "rwkv_g1": bf(rwkv_g1[l]), "rwkv_g2": bf(rwkv_g2[l]),
        "rwkv_k_k": row(rwkv_k_k[l]), "rwkv_k_a": row(rwkv_k_a[l]), "rwkv_r_k": row(rwkv_r_k[l]),
        "rwkv_ln_g": row(rwkv_ln_g[l]), "rwkv_ln_b": row(rwkv_ln_b[l]),
        "rwkv_w_proj": rwkv_w_proj[l],
        "mlstm_conv": mlstm_conv[l].reshape(9, B_WIDTH),
        "mlstm_wq": bf(mlstm_w_q[l]), "mlstm_wk": bf(mlstm_w_k[l]), "mlstm_wv": bf(mlstm_w_v[l]),
        "mlstm_wg": bf(wg), "mlstm_bg": row(bg),
        "mlstm_wgt": bf(wg[:, :16].T), "mlstm_bgt": jnp.broadcast_to(bg[:16, None], (16, LANES)),
        "mlstm_ln_g": row(mlstm_ln_g[l]), "mlstm_w_proj": mlstm_w_proj[l], "w_out": w_out[l],
        "moe_w_router": bf(w_router), "moe_b_router": row(b_router),
        "moe_w1": moe_w1[l], "moe_w3": moe_w3[l], "moe_w2": moe_w2[l],
        "final_norm_g": row(final_norm_g),
    }
    cvec = jnp.concatenate([c_ctx[None, :], c, jnp.zeros((8 - 1 - dec, D_MODEL), F32)], axis=0)
    mod = _mod(cvec, w_mod[l], b_mod[l].reshape(1, -1))
    mod_ctx = mod[0:1][:, None, :]
    mod_lat = mod[1:1 + dec][:, None, :]

    yp, n_rwkv, n_c, n_n, n_m = _trunk(x_prompt, mod_ctx, None, False, p, 512, 256, 2)
    lat_states = tuple(s[:, l].astype(F32) for s in (state_rwkv, state_mlstm_C, state_mlstm_n, state_mlstm_m))
    ys, _, _, _, _ = _trunk(x_sample, mod_lat, lat_states, True, p, 512, 128, 1)
    dt = x_prompt.dtype
    return (yp, ys, n_rwkv[:, None].astype(dt), n_c[:, None].astype(dt), n_n[:, None].astype(dt),
            n_m[:, None].astype(dt))
```

```python
import functools

import jax
import jax.numpy as jnp
from jax import lax
from jax.experimental import pallas as pl
from jax.experimental.pallas import tpu as pltpu

F32 = jnp.float32
BF16 = jnp.bfloat16

D_MODEL = 1024
GRID_W = 64
A_HEAD_DIM = 64
A_WIDTH = D_MODEL // 2
A_HEADS = A_WIDTH // A_HEAD_DIM
B_HEAD_DIM = 128
B_WIDTH = D_MODEL // 2
B_HEADS = B_WIDTH // B_HEAD_DIM
CHUNK = 64
SCAN_HEADS = 2
RING = 3
ROW_TILE = 512
SUM_TILE = 256
N_GROUPS = 4
EXPERTS_PER_GROUP = 4
N_EXPERTS = N_GROUPS * EXPERTS_PER_GROUP
GROUP_LANE = N_GROUPS + N_EXPERTS
NORM_EPS = 1e-6
RWKV_GN_EPS = 64e-5
MLSTM_GN_EPS = 1e-5
LANES = 128
SUBLANES = 8
CONV_PAD = 72
VMEM_LIMIT = 56 * 1024 * 1024


def _mm(a, b):
    return jnp.dot(a.astype(BF16), b.astype(BF16), preferred_element_type=F32)


_NN, _NT, _TN = ((1,), (0,)), ((1,), (1,)), ((0,), (0,))


def _split(a):
    hi = a.astype(BF16)
    return hi, (a - hi.astype(F32)).astype(BF16)


def _split3(a):
    hi = a.astype(BF16)
    rest = a - hi.astype(F32)
    lo = rest.astype(BF16)
    return hi, lo, (rest - lo.astype(F32)).astype(BF16)


def _sum01(a, b, dims=_NN):
    dot = lambda p, q: lax.dot_general(p, q, (dims, ((), ())), preferred_element_type=F32)
    terms = [dot(x, b) for x in a] if isinstance(a, tuple) else [dot(a, x) for x in b]
    return functools.reduce(lambda p, q: p + q, terms)


def _mm_nt(a, b):
    return lax.dot_general(a.astype(BF16), b.astype(BF16), (((1,), (1,)), ((), ())),
                           preferred_element_type=F32)


def _mm_tn(a, b):
    return lax.dot_general(a.astype(BF16), b.astype(BF16), (((0,), (0,)), ((), ())),
                           preferred_element_type=F32)


def _sigmoid(x):
    return 1.0 / (1.0 + jnp.exp(-x))


def _silu(x):
    return x * _sigmoid(x)


def _softplus(x):
    return jnp.maximum(x, 0.0) + jnp.log(1.0 + jnp.exp(-jnp.abs(x)))


def _iota(shape, dim):
    return lax.broadcasted_iota(jnp.int32, shape, dim)


def _tri(reverse, inclusive, n=CHUNK):
    row, col = _iota((n, n), 0), _iota((n, n), 1)
    if reverse:
        return (col >= row) if inclusive else (col > row)
    return (col <= row) if inclusive else (col < row)


def _same_block(n, blk):
    sh = blk.bit_length() - 1
    row, col = _iota((n, n), 0), _iota((n, n), 1)
    return lax.shift_right_logical(row, sh) == lax.shift_right_logical(col, sh)


def _rmsnorm(x, g):
    return x * lax.rsqrt(jnp.mean(x * x, axis=-1, keepdims=True) + NORM_EPS) * g


def _cparams(sem):
    return pltpu.CompilerParams(dimension_semantics=sem, vmem_limit_bytes=VMEM_LIMIT)


def _full(shape):
    nd = len(shape)
    return pl.BlockSpec(shape, lambda *_: (0,) * nd)


def _mod_kernel(c_ref, w_ref, b_ref, o_ref):
    o_ref[...] = _mm(_silu(c_ref[...]), w_ref[...]) + b_ref[...]


def _mod(cvec, w_mod, b_mod):
    rows, d = cvec.shape
    n = w_mod.shape[1]
    tn = 1536
    return pl.pallas_call(
        _mod_kernel,
        grid=(n // tn,),
        in_specs=[_full((rows, d)), pl.BlockSpec((d, tn), lambda j: (0, j)),
                  pl.BlockSpec((1, tn), lambda j: (0, j))],
        out_specs=pl.BlockSpec((rows, tn), lambda j: (0, j)),
        out_shape=jax.ShapeDtypeStruct((rows, n), F32),
        compiler_params=_cparams(("arbitrary",)),
        name="mod",
    )(cvec, w_mod, b_mod)


W_IN_BLOCK = 512


def _modulated_norm1(x_ref, mod_ref, g_ref):
    d = D_MODEL
    shift, scale = mod_ref[:, 0:d], mod_ref[:, d:2 * d]
    return (_rmsnorm(x_ref[...], g_ref[...]) * (1.0 + scale) + shift).astype(BF16)


def _w_in_blocks(w_in, first, count):
    specs = [pl.BlockSpec((w_in.shape[0], W_IN_BLOCK), lambda *_, j=first + k: (0, j)) for k in range(count)]
    return specs, [w_in] * count


def _in_kernel(x_ref, mod_ref, g_ref, wa_ref, wu_ref, wo_ref, xa_ref, ub_ref, op_ref):
    h = _modulated_norm1(x_ref, mod_ref, g_ref)
    for ref, w_ref in ((xa_ref, wa_ref), (ub_ref, wu_ref), (op_ref, wo_ref)):
        ref[...] = _mm(h, w_ref[...])


def _in_proj(x, mod, g1, w_in, tm):
    b, t, d = x.shape
    per = t // tm
    widths = (A_WIDTH, B_WIDTH, B_WIDTH)
    assert all(n == W_IN_BLOCK for n in widths)
    tok = lambda n: pl.BlockSpec((None, tm, n), lambda i: (i // per, i % per, 0))
    w_specs, w_args = _w_in_blocks(w_in, 0, 3)
    return pl.pallas_call(
        _in_kernel,
        grid=(b * per,),
        in_specs=[tok(d), pl.BlockSpec((None, 1, 6 * d), lambda i: (i // per, 0, 0)), _full((1, d))] + w_specs,
        out_specs=[tok(n) for n in widths],
        out_shape=[jax.ShapeDtypeStruct((b, t, n), F32) for n in widths],
        compiler_params=_cparams(("arbitrary",)),
        name="in_proj",
    )(x, mod, g1, *w_args)


def _head_of_lane(shape):
    return lax.shift_right_logical(_iota(shape, 1), A_HEAD_DIM.bit_length() - 1)


def _pair_diag(x):
    head = _head_of_lane(x.shape)
    return jnp.concatenate([x * (head == h).astype(x.dtype) for h in range(x.shape[1] // A_HEAD_DIM)], axis=0)


def _pair_mm3(lhs, rhs, dims=_NN):
    rows = lhs[0].shape[0]
    if dims == _NT:
        wh, wl = _split(_pair_diag(rhs).T)
    else:
        rh, rl = _split(rhs)
        wh, wl = _pair_diag(rh), _pair_diag(rl)
    parts = [_split(l) for l in lhs]
    top = jnp.dot(jnp.concatenate([t for part in parts for t in part], axis=0), wh, preferred_element_type=F32)
    low = jnp.dot(jnp.concatenate([part[0] for part in parts], axis=0), wl, preferred_element_type=F32)
    return [top[2 * k * rows:(2 * k + 1) * rows] + top[(2 * k + 1) * rows:(2 * k + 2) * rows]
            + low[k * rows:(k + 1) * rows] for k in range(len(lhs))]


def _pair_index(shape):
    return _iota(shape, 0), _iota(shape, 1) & (A_HEAD_DIM - 1)


def _pair_tri_inverses(mats):
    t_idx, s_idx = _pair_index(mats[0].shape)
    same = lambda blk: (lax.shift_right_logical(t_idx, blk.bit_length() - 1)
                        == lax.shift_right_logical(s_idx, blk.bit_length() - 1))
    eye = (t_idx == s_idx).astype(F32)
    nd = [jnp.where(same(8), m, 0.0) for m in mats]
    n2 = [_pair_mm3([x], x)[0] for x in nd]
    t = [eye + x for x in nd]
    both = [_pair_mm3([a, b], a) for a, b in zip(n2, t)]
    t = [x + r[1] for x, r in zip(t, both)]
    t = [x + _pair_mm3([x], r[0])[0] for x, r in zip(t, both)]
    blk = 8
    while blk < CHUNK:
        sel = same(2 * blk) & jnp.logical_not(same(blk))
        w = [_pair_mm3([jnp.where(sel, m, 0.0)], x)[0] for m, x in zip(mats, t)]
        t = [x + _pair_mm3([x], y)[0] for x, y in zip(t, w)]
        blk *= 2
    return t


def _rwkv_kernel(x_ref, s0_ref, mu_ref, wr_ref, wk_ref, wv_ref, w0_ref, w1_ref, w2_ref,
                 a0_ref, a1_ref, a2_ref, g1_ref, g2_ref, kk_ref, ka_ref, rk_ref, lng_ref, lnb_ref,
                 y_ref, s_ref,
                 r_s, v_s, kk_s, cum_s, cx_s, kd_s, bb_s, g_s, bonus_s, y_s, sp_s, *, seq_len):
    total = x_ref.shape[0]
    n_seq = total // seq_len
    n_chunks = seq_len // CHUNK
    n_tiles = total // ROW_TILE
    hd = A_HEAD_DIM
    n_groups, width = sp_s.shape[2], sp_s.shape[4]
    per_group = width // hd
    x = x_ref[...]
    row = _iota(x.shape, 0) & (seq_len - 1)
    prev = jnp.where(row == 0, 0.0, pltpu.roll(x, 1, 0))
    nxt = jnp.where(row == seq_len - 1, 0.0, pltpu.roll(x, total - 1, 0))
    y_s[...] = 0.5 * (prev + nxt) - x
    head_ones = _same_block(A_WIDTH, hd).astype(BF16)
    chunk_diag = _same_block(SUM_TILE, CHUNK)
    run_sum = [(chunk_diag & _tri(d == 1, True, SUM_TILE)).astype(BF16) for d in range(2)]

    def project_tile(ti, carry):
        rows = pl.ds(pl.multiple_of(ti * ROW_TILE, ROW_TILE), ROW_TILE)
        x, xx = x_ref[rows, :], y_s[rows, :]
        xr, xw, xk, xv, xi, xg = (x + xx * mu_ref[j:j + 1, :] for j in range(6))
        r = _mm(xr, wr_ref[...])
        k = _mm(xk, wk_ref[...])
        v = _mm(xv, wv_ref[...])
        g_s[rows, :] = _mm(_sigmoid(_mm(xg, g1_ref[...])), g2_ref[...])
        kk = k * kk_ref[...]
        kk = kk / jnp.maximum(jnp.sqrt(_sum01(_split(kk * kk), head_ones)), 1e-12)
        r_s[rows, :] = r
        v_s[rows, :] = v
        kk_s[rows, :] = kk
        kd_sum = jnp.zeros_like(x)
        for d in range(2):
            zw = w0_ref[d:d + 1, :] + _mm(jnp.tanh(_mm(xw, w1_ref[d])), w2_ref[d])
            lw = -jnp.exp(-_softplus(-zw) - 0.5)
            cum = jnp.concatenate([_sum01(run_sum[d], _split3(lw[j:j + SUM_TILE]))
                                   for j in range(0, ROW_TILE, SUM_TILE)], axis=0)
            cum_s[d, rows, :] = cum
            cx_s[d, rows, :] = cum - lw
            a = _sigmoid(a0_ref[d:d + 1, :] + _mm(_mm(xi, a1_ref[d]), a2_ref[d]))
            kd = k * (1.0 + (a - 1.0) * ka_ref[...])
            kd_s[d, rows, :] = kd
            bb_s[d, rows, :] = kk * a
            kd_sum = kd_sum + kd
        bonus_s[rows, :] = _sum01(_split(r * kd_sum * rk_ref[...]), head_ones) * v
        y_s[rows, :] = jnp.zeros_like(x)
        return carry

    lax.fori_loop(0, n_tiles, project_tile, 0)
    for n in range(n_seq):
        for d in range(2):
            for p in range(n_groups):
                sp_s[n, d, p] = (jnp.zeros((hd, width), F32) if s0_ref is None else jnp.concatenate(
                    [s0_ref[n, d, per_group * p + j] for j in range(per_group)], axis=1))

    t_idx, s_idx = _pair_index((CHUNK, width))
    strict = [s_idx < t_idx, s_idx > t_idx]
    incl = [s_idx <= t_idx, s_idx >= t_idx]
    lane_head = _head_of_lane((CHUNK, width))

    def chunk_body(ci, carry):
        chains = []
        for n, d in [(n, d) for n in range(n_seq) for d in range(2)]:
            cpos = n * n_chunks + ((n_chunks - 1 - ci) if d == 1 else ci)
            last = 0 if d == 1 else CHUNK - 1
            rows = pl.ds(pl.multiple_of(cpos * CHUNK, CHUNK), CHUNK)
            cum, cx = cum_s[d, rows, :], cx_s[d, rows, :]
            tot = cum[last:last + 1, :]
            kdc, bbc = kd_s[d, rows, :], bb_s[d, rows, :]
            e_neg = jnp.exp(-cum)
            e_tail = jnp.exp(tot - cum)
            at = -kk_s[rows, :] * jnp.exp(cx)
            rt = r_s[rows, :] * jnp.exp(cum)
            bt, kt = bbc * e_neg, kdc * e_neg
            bp, kp = bbc * e_tail, kdc * e_tail
            pc = jnp.exp(tot)
            vc = v_s[rows, :]
            for p in range(n_groups):
                ln = slice(p * width, (p + 1) * width)
                chains.append(dict(d=d, st=(n, d, p), rows=rows, ln=ln, v=vc[:, ln], pc=pc[:, ln],
                                   lhs=jnp.concatenate([at[:, ln], rt[:, ln]], axis=0),
                                   bt=bt[:, ln], kt=kt[:, ln], bp=bp[:, ln], kp=kp[:, ln]))
        for c in chains:
            c["sb"] = _pair_mm3([c["lhs"]], c["bt"], _NT)[0]
            c["sk"] = _pair_mm3([c["lhs"]], c["kt"], _NT)[0]
            c["s"] = sp_s[c["st"]]
            c["fs"] = _mm(c["lhs"], _pair_diag(c["s"]).T)
            c["v_diag"] = _pair_diag(c["v"].astype(BF16))
        for c in chains:
            a_ak = jnp.where(strict[c["d"]], c["sk"][:CHUNK], 0.0)
            c["x"] = c["fs"][:CHUNK] + _mm(a_ak, c["v_diag"])
        inv = _pair_tri_inverses([jnp.where(strict[c["d"]], c["sb"][:CHUNK], 0.0) for c in chains])
        for c, t in zip(chains, inv):
            c["u"] = _pair_mm3([t], c["x"])[0]
        for c in chains:
            d, rows, ln, u = c["d"], c["rows"], c["ln"], c["u"]
            a_r = jnp.concatenate([jnp.where(incl[d], c["sb"][CHUNK:], 0.0),
                                   jnp.where(incl[d], c["sk"][CHUNK:], 0.0)], axis=1)
            uv_diag = jnp.concatenate([_pair_diag(u.astype(BF16)), c["v_diag"]], axis=0)
            y_s[rows, ln] = y_s[rows, ln] + c["fs"][CHUNK:] + _mm(a_r, uv_diag)
            g = _mm_tn(jnp.concatenate([u, c["v"]], axis=0), jnp.concatenate([c["bp"], c["kp"]], axis=0))
            own = g[0:hd]
            for j in range(1, per_group):
                own = jnp.where(lane_head == j, g[j * hd:(j + 1) * hd], own)
            sp_s[c["st"]] = c["s"] * c["pc"] + own
        return carry

    lax.fori_loop(0, n_chunks, chunk_body, 0)
    for n in range(n_seq):
        for d in range(2):
            for p in range(n_groups):
                for j in range(per_group):
                    s_ref[n, d, per_group * p + j] = sp_s[n, d, p, :, j * hd:(j + 1) * hd]

    def finish_tile(ti, carry):
        rows = pl.ds(pl.multiple_of(ti * ROW_TILE, ROW_TILE), ROW_TILE)
        ys = y_s[rows, :]
        cen = ys - _sum01(_split(ys), head_ones) * (1.0 / hd)
        var = _sum01(_split(cen * cen), head_ones) * (1.0 / hd)
        yn = cen * lax.rsqrt(var + RWKV_GN_EPS) * lng_ref[...] + lnb_ref[...]
        y_ref[rows, :] = (yn + bonus_s[rows, :]) * g_s[rows, :]
        return carry

    lax.fori_loop(0, n_tiles, finish_tile, 0)


def _rwkv(xa, s0, p, n_seq):
    b, t, w = xa.shape
    assert t & (t - 1) == 0 and b % n_seq == 0
    names = ("mu", "wr", "wk", "wv", "w0", "w1", "w2", "a0", "a1", "a2", "g1", "g2",
             "k_k", "k_a", "r_k", "ln_g", "ln_b")
    weights = [p["rwkv_" + n] for n in names]
    rows = n_seq * t
    seq = pl.BlockSpec((None, rows, w), lambda i: (i, 0, 0))
    st = pl.BlockSpec((n_seq, 2, A_HEADS, A_HEAD_DIM, A_HEAD_DIM), lambda i: (i, 0, 0, 0, 0))
    tw = lambda lead=(): pltpu.VMEM(lead + (rows, w), F32)
    kernel = functools.partial(_rwkv_kernel, seq_len=t)
    if s0 is None:
        kernel, state_specs, states = _zero_state_kernel(kernel, 1, 1), [], ()
    else:
        state_specs, states = [st], (s0,)
    y, s_out = pl.pallas_call(
        kernel,
        grid=(b // n_seq,),
        in_specs=[seq] + state_specs + [_full(x.shape) for x in weights],
        out_specs=[seq, st],
        out_shape=[jax.ShapeDtypeStruct((b // n_seq, rows, w), F32),
                   jax.ShapeDtypeStruct((b, 2, A_HEADS, A_HEAD_DIM, A_HEAD_DIM), F32)],
        scratch_shapes=[tw(), tw(), tw(), tw((2,)), tw((2,)), tw((2,)), tw((2,)), tw(), tw(), tw(),
                        pltpu.VMEM((n_seq, 2, A_HEADS // SCAN_HEADS, A_HEAD_DIM, SCAN_HEADS * A_HEAD_DIM), F32)],
        compiler_params=_cparams(("arbitrary",)),
        name="rwkv",
    )(xa.reshape(b // n_seq, rows, w), *states, *weights)
    return y.reshape(b, t, w), s_out


def _mlstm_kernel(u_ref, op_ref, c0_ref, n0_ref, m0_ref, conv_ref, wq_ref, wk_ref, wv_ref,
                  wg_ref, bg_ref, wgt_ref, bgt_ref, lng_ref,
                  y_ref, c_ref, n_ref, m_ref,
                  pad_s, q_s, k_s, v_s, dd_s, cr_s, h_s, *, conv2d, seq_len):
    t_len = u_ref.shape[0]
    n_seq = t_len // seq_len
    n_chunks = seq_len // CHUNK
    hd = B_HEAD_DIM
    u = u_ref[...]
    pad_s[0:CONV_PAD, :] = jnp.zeros((CONV_PAD, B_WIDTH), F32)
    pad_s[CONV_PAD + t_len:CONV_PAD + t_len + CONV_PAD, :] = jnp.zeros((CONV_PAD, B_WIDTH), F32)
    pad_s[CONV_PAD:CONV_PAD + t_len, :] = u
    pos = _iota(u.shape, 0) & (seq_len - 1)
    col = pos & (GRID_W - 1)
    conv = jnp.zeros_like(u)
    for kh in range(3):
        if not conv2d and kh != 1:
            continue
        for kw in range(3):
            shift = (kh - 1) * GRID_W + (kw - 1)
            term = pad_s[CONV_PAD + shift:CONV_PAD + shift + t_len, :] * conv_ref[kh * 3 + kw:kh * 3 + kw + 1, :]
            if conv2d and kw != 1:
                src = col + (kw - 1)
                term = jnp.where((src >= 0) & (src < GRID_W), term, 0.0)
            if n_seq > 1 and shift != 0:
                term = jnp.where((pos + shift >= 0) & (pos + shift < seq_len), term, 0.0)
            conv = conv + term
    uc = _silu(conv)
    for h in range(B_HEADS):
        ln = slice(h * hd, (h + 1) * hd)
        q_s[:, ln] = _mm(uc[:, ln], wq_ref[h])
        k_s[:, ln] = _mm(uc[:, ln], wk_ref[h]) * (hd ** -0.5)
        v_s[:, ln] = _mm(u[:, ln], wv_ref[h])
    gcol = _mm(uc, wg_ref[...]) + bg_ref[...]
    gcol = jnp.where(_iota(gcol.shape, 1) >= 8, -_softplus(-gcol), gcol)
    grow = _mm_nt(wgt_ref[...], uc) + bgt_ref[:, 0:1]
    grow = jnp.where(_iota(grow.shape, 0) >= 8, -_softplus(-grow), grow)
    chunk_diag = _same_block(SUM_TILE, CHUNK)
    run_sum = [(chunk_diag & _tri(d == 1, True, SUM_TILE)).astype(BF16) for d in range(2)]
    per_tile = SUM_TILE // CHUNK
    for ti in range(t_len // SUM_TILE):
        sl = slice(ti * SUM_TILE, (ti + 1) * SUM_TILE)
        gt, rt = _split3(gcol[sl, :]), _split3(grow[:, sl])
        backward_lane = (_iota((SUM_TILE, LANES), 1) & 4) != 0
        cum_c = jnp.where(backward_lane, _sum01(run_sum[1], gt, _NN), _sum01(run_sum[0], gt, _NN))
        dd_s[sl, :] = pltpu.roll(cum_c, LANES - 8, 1) - gcol[sl, :]
        backward_row = (_iota((16, SUM_TILE), 0) & 4) != 0
        cum_r = jnp.where(backward_row, _sum01(rt, run_sum[1], _NT), _sum01(rt, run_sum[0], _NT))
        for c in range(per_tile):
            cr_s[ti * per_tile + c] = cum_r[:, c * CHUNK:(c + 1) * CHUNK]
    h_s[...] = jnp.zeros_like(u)
    for ref, init in ((c_ref, c0_ref), (n_ref, n0_ref), (m_ref, m0_ref)):
        ref[...] = jnp.zeros(ref.shape, F32) if init is None else init[...]
    row_i, col_i = _iota((CHUNK, CHUNK), 0), _iota((CHUNK, CHUNK), 1)
    incl_t = [row_i <= col_i, row_i >= col_i]
    eye = row_i == col_i
    pick = [(_iota((LANES, B_HEADS * LANES), 0)
             == d * B_HEADS + lax.shift_right_logical(_iota((LANES, B_HEADS * LANES), 1), LANES.bit_length() - 1)
             ).astype(BF16) for d in range(2)]
    spread = (lax.shift_right_logical(_iota((2 * CHUNK, 2 * LANES), 0), CHUNK.bit_length() - 1)
              == lax.shift_right_logical(_iota((2 * CHUNK, 2 * LANES), 1), LANES.bit_length() - 1)).astype(BF16)
    ones_bf = jnp.ones((CHUNK, LANES), BF16)

    def chunk_body(ci, carry):
        chains = []
        for n, d in [(n, d) for n in range(n_seq) for d in range(2)]:
            cpos = n * n_chunks + ((n_chunks - 1 - ci) if d == 1 else ci)
            last = 0 if d == 1 else CHUNK - 1
            rows = pl.ds(pl.multiple_of(cpos * CHUNK, CHUNK), CHUNK)
            cr = cr_s[cpos]
            d_col = _sum01(_split3(dd_s[rows, :]), pick[d])
            for h in range(B_HEADS):
                ln = slice(h * hd, (h + 1) * hd)
                lf = 8 + d * 4 + h
                chains.append(dict(d=d, st=(n, d, h), vec=(n, d, slice(h, h + 1)), rows=rows, ln=ln, last=last,
                                   b_row=cr[lf:lf + 1, :], d_col=d_col[:, ln],
                                   m=m_ref[n, d, h:h + 1, 0:1], q=q_s[rows, ln], k=k_s[rows, ln], v=v_s[rows, ln],
                                   c=c_ref[n, d, h], n=n_ref[n, d, h:h + 1, :]))
        for c in chains:
            c["kq"] = _mm(c["k"], c["q"].T)
            c["qc"] = _mm(c["q"], c["c"])
            c["qn"] = _mm(c["q"], jnp.broadcast_to(c["n"], (LANES, hd)).T)
            logw = jnp.where(incl_t[c["d"]], c["b_row"] - c["d_col"][:, 0:CHUNK], -jnp.inf)
            c["m_row"] = jnp.maximum(c["b_row"] + c["m"], jnp.max(logw, axis=0, keepdims=True))
            c["logw"] = logw
        for c in chains:
            inter_row = jnp.exp(c["b_row"] + c["m"] - c["m_row"])
            diag = jnp.concatenate([jnp.where(eye, c["m_row"], 0.0), jnp.where(eye, inter_row, 0.0)], axis=1)
            cols = _sum01(_split3(diag), spread)
            c["m_col"], c["inter"] = cols[:, 0:LANES], cols[:, LANES:2 * LANES]
            c["s"] = c["kq"] * jnp.exp(c["logw"] - c["m_row"])
        for c in chains:
            s_hi, s_lo = _split(c["s"])
            sv = lax.dot_general(s_hi, jnp.concatenate([c["v"].astype(BF16), ones_bf], axis=1),
                                 (_TN, ((), ())), preferred_element_type=F32)
            c["sv"] = sv[:, 0:hd]
            c["den"] = (c["inter"] * c["qn"] + sv[:, hd:hd + LANES]
                        + lax.dot_general(s_lo, ones_bf, (_TN, ((), ())), preferred_element_type=F32))
            last = c["last"]
            c["m_new"] = c["m_row"][:, last:last + 1]
            b_last = c["b_row"][:, last:last + 1]
            c["decay"] = jnp.exp(b_last + c["m"] - c["m_new"])
            c["kw"] = c["k"] * jnp.exp(b_last - c["m_new"] - c["d_col"])
        for c in chains:
            c["kv"] = _mm_tn(c["kw"], c["v"])
        for c in chains:
            rows, ln = c["rows"], c["ln"]
            num = c["inter"] * c["qc"] + c["sv"]
            h_s[rows, ln] = h_s[rows, ln] + num / jnp.maximum(jnp.abs(c["den"]), jnp.exp(-c["m_col"]))
            c_ref[c["st"]] = c["decay"] * c["c"] + c["kv"]
            n_ref[c["vec"]] = c["decay"] * c["n"] + jnp.sum(c["kw"], axis=0, keepdims=True)
            m_ref[c["vec"]] = jnp.broadcast_to(c["m_new"], (1, LANES))
        return carry

    lax.fori_loop(0, n_chunks, chunk_body, 0)

    for h in range(B_HEADS):
        ln = slice(h * hd, (h + 1) * hd)
        hh = h_s[:, ln]
        cen = hh - jnp.mean(hh, axis=1, keepdims=True)
        var = jnp.mean(cen * cen, axis=1, keepdims=True)
        y_ref[:, ln] = _sigmoid(op_ref[:, ln]) * (cen * lax.rsqrt(var + MLSTM_GN_EPS) * lng_ref[:, ln])


def _zero_state_kernel(kernel, n_data, n_state):
    def wrapped(*refs, **kw):
        return kernel(*refs[:n_data], *([None] * n_state), *refs[n_data:], **kw)
    return wrapped


def _mlstm(ub, o_pre, states, p, conv2d, n_seq):
    b, t, w = ub.shape
    assert t & (t - 1) == 0 and b % n_seq == 0
    names = ("conv", "wq", "wk", "wv", "wg", "bg", "wgt", "bgt", "ln_g")
    weights = [p["mlstm_" + n] for n in names]
    rows, steps = n_seq * t, b // n_seq
    seq = pl.BlockSpec((None, rows, w), lambda i: (i, 0, 0))
    cst = pl.BlockSpec((n_seq, 2, B_HEADS, B_HEAD_DIM, B_HEAD_DIM), lambda i: (i, 0, 0, 0, 0))
    vst = pl.BlockSpec((n_seq, 2, B_HEADS, LANES), lambda i: (i, 0, 0, 0))
    kernel = functools.partial(_mlstm_kernel, conv2d=conv2d, seq_len=t)
    if states is None:
        kernel, state_specs, states = _zero_state_kernel(kernel, 2, 3), [], ()
    else:
        state_specs = [cst, vst, vst]
    y, c_out, n_out, m_out = pl.pallas_call(
        kernel,
        grid=(steps,),
        in_specs=[seq, seq] + state_specs + [_full(x.shape) for x in weights],
        out_specs=[seq, cst, vst, vst],
        out_shape=[jax.ShapeDtypeStruct((steps, rows, w), F32),
                   jax.ShapeDtypeStruct((b, 2, B_HEADS, B_HEAD_DIM, B_HEAD_DIM), F32),
                   jax.ShapeDtypeStruct((b, 2, B_HEADS, LANES), F32),
                   jax.ShapeDtypeStruct((b, 2, B_HEADS, LANES), F32)],
        scratch_shapes=[pltpu.VMEM((rows + 2 * CONV_PAD, w), F32), pltpu.VMEM((rows, w), F32),
                        pltpu.VMEM((rows, w), F32), pltpu.VMEM((rows, w), F32),
                        pltpu.VMEM((rows, LANES), F32), pltpu.VMEM((rows // CHUNK, 16, CHUNK), F32),
                        pltpu.VMEM((rows, w), F32)],
        compiler_params=_cparams(("arbitrary",)),
        name="mlstm",
    )(ub.reshape(steps, rows, w), o_pre.reshape(steps, rows, w), *states, *weights)
    return y.reshape(b, t, w), c_out, n_out, m_out


def _first_lane_of_max(val, lane, valid):
    masked = jnp.where(valid, val, -jnp.inf)
    best = jnp.max(masked, axis=1, keepdims=True)
    idx = jnp.min(jnp.where(valid & (masked == best), lane, LANES), axis=1, keepdims=True)
    return best, idx


def _merge_kernel(x_ref, ya_ref, yb_ref, mod_ref, g1_ref, wga0_ref, wga1_ref, wgb0_ref, wgb1_ref,
                  wpa_ref, wpb_ref, wo_ref, g2_ref, wr_ref, br_ref, xc_ref):
    d = D_MODEL
    gate1 = mod_ref[:, 2 * d:3 * d]
    shift2, scale2 = mod_ref[:, 3 * d:4 * d], mod_ref[:, 4 * d:5 * d]
    h1 = _modulated_norm1(x_ref, mod_ref, g1_ref)
    pa, pb = _mm(ya_ref[...], wpa_ref[...]), _mm(yb_ref[...], wpb_ref[...])
    halves = []
    for j, (wga_ref, wgb_ref) in enumerate(((wga0_ref, wgb0_ref), (wga1_ref, wgb1_ref))):
        cols = slice(j * W_IN_BLOCK, (j + 1) * W_IN_BLOCK)
        halves.append(_sigmoid(_mm(h1, wga_ref[...])) * pa[:, cols] + _sigmoid(_mm(h1, wgb_ref[...])) * pb[:, cols])
    merged = jnp.concatenate(halves, axis=1)
    x1 = x_ref[...] + gate1 * _mm(merged, wo_ref[...])
    xc_ref[:, 0:d] = x1
    h2 = _rmsnorm(x1, g2_ref[...]) * (1.0 + scale2) + shift2
    logits = _mm(h2, wr_ref[...]) + br_ref[...]
    lane = _iota(logits.shape, 1)
    is_group = lane < N_GROUPS
    g_max, g_sel = _first_lane_of_max(logits, lane, is_group)
    g_w = 1.0 / jnp.sum(jnp.where(is_group, jnp.exp(logits - g_max), 0.0), axis=1, keepdims=True)
    expert = lane - N_GROUPS
    in_group = (expert >= 0) & (expert < N_EXPERTS) & (lax.shift_right_arithmetic(expert, EXPERTS_PER_GROUP.bit_length() - 1) == g_sel)
    e_max = jnp.max(jnp.where(in_group, logits, -jnp.inf), axis=1, keepdims=True)
    e_exp = jnp.where(in_group, jnp.exp(logits - e_max), 0.0)
    prob = e_exp / jnp.sum(e_exp, axis=1, keepdims=True)
    p1, i1 = _first_lane_of_max(prob, lane, in_group)
    p2, i2 = _first_lane_of_max(prob, lane, in_group & (lane != i1))
    denom = p1 + p2
    comb = jnp.where(lane == i1, g_w * p1 / denom, 0.0) + jnp.where(lane == i2, g_w * p2 / denom, 0.0)
    xc_ref[:, d:d + LANES] = jnp.where(lane == GROUP_LANE, g_sel.astype(F32), comb)


def _merge(x, ya, yb, mod, p, tm):
    b, t, d = x.shape
    per = t // tm
    assert d == 2 * W_IN_BLOCK
    tok = lambda n: pl.BlockSpec((None, tm, n), lambda i: (i // per, i % per, 0))
    gate_specs, gate_args = _w_in_blocks(p["w_in"], 3, 4)
    weights = [p["rwkv_w_proj"], p["mlstm_w_proj"], p["w_out"], p["norm2_g"], p["moe_w_router"], p["moe_b_router"]]
    return pl.pallas_call(
        _merge_kernel,
        grid=(b * per,),
        in_specs=[tok(d), tok(A_WIDTH), tok(B_WIDTH), pl.BlockSpec((None, 1, 6 * d), lambda i: (i // per, 0, 0)),
                  _full((1, d))] + gate_specs + [_full(w.shape) for w in weights],
        out_specs=tok(d + LANES),
        out_shape=jax.ShapeDtypeStruct((b, t, d + LANES), F32),
        compiler_params=_cparams(("arbitrary",)),
        name="merge",
    )(x, ya, yb, mod, p["norm1_g"], *gate_args, *weights)


def _route(gsel, tb):
    s, l = gsel.shape
    n_buckets = s * N_GROUPS
    max_tiles = s * (l // tb + N_GROUPS - 1)
    onehot = (gsel[..., None] == jnp.arange(N_GROUPS, dtype=jnp.int32)).astype(jnp.int32)
    rank = jnp.sum((jnp.cumsum(onehot, axis=1) - onehot) * onehot, axis=-1)
    n_tiles = ((jnp.sum(onehot, axis=1) + tb - 1) // tb).T.reshape(n_buckets)
    ends = jnp.cumsum(n_tiles)
    first_tile = jnp.sum(onehot * (ends - n_tiles).reshape(N_GROUPS, s).T[:, None, :], axis=-1)
    pos = first_tile * tb + rank
    tok = jnp.arange(s * l, dtype=jnp.int32)
    row_src = jnp.full((max_tiles * tb,), -1, jnp.int32).at[pos.reshape(-1)].set(tok)
    tile_bucket = jnp.sum(jnp.arange(max_tiles, dtype=jnp.int32)[:, None] >= ends[None, :], axis=1)
    tile_bucket = jnp.minimum(tile_bucket, n_buckets - 1).astype(jnp.int32)
    return row_src, tile_bucket % s, tile_bucket // s, ends[-1:].astype(jnp.int32)


def _moe_kernel(gsrc_ref, sdst_ref, seg_ref, grp_ref, used_ref, cnt_ref,
                xc_hbm, mod_ref, g2_ref, w1_ref, w3_ref, w2_ref, gf_ref, dump_in_hbm,
                y_hbm, dump_hbm, gbuf, obuf, gsem, ssem, *, tb):
    d = D_MODEL
    q = pl.program_id(0)
    n_used = used_ref[0]

    def groups(tile):
        return lax.shift_right_logical(cnt_ref[tile] + (SUBLANES - 1), SUBLANES.bit_length() - 1)

    def start_gather(tile, sl):
        def body(i, carry):
            for u in range(SUBLANES):
                idx = gsrc_ref[tile * tb + i * SUBLANES + u]
                pltpu.make_async_copy(xc_hbm.at[lax.shift_right_logical(idx, SUBLANES.bit_length() - 1), pl.ds(idx & (SUBLANES - 1), 1), :],
                                      gbuf.at[sl, i, pl.ds(u, 1), :], gsem.at[sl]).start()
            return carry

        lax.fori_loop(0, groups(tile), body, 0)

    def wait_gather(tile, sl):
        def body(i, carry):
            pltpu.make_async_copy(xc_hbm.at[pl.ds(0, 1)], gbuf.at[sl, pl.ds(0, 1)], gsem.at[sl]).wait()
            return carry

        lax.fori_loop(0, groups(tile), body, 0)

    def start_scatter(tile, sl):
        def body(i, carry):
            for u in range(SUBLANES):
                idx = sdst_ref[tile * tb + i * SUBLANES + u]

                @pl.when(idx >= 0)
                def _():
                    pltpu.make_async_copy(
                        obuf.at[sl, i, pl.ds(u, 1), :],
                        y_hbm.at[lax.shift_right_logical(idx, SUBLANES.bit_length() - 1), pl.ds(idx & (SUBLANES - 1), 1), :],
                        ssem.at[sl]).start()

                @pl.when(idx < 0)
                def _():
                    pltpu.make_async_copy(obuf.at[sl, i, pl.ds(u, 1), :], dump_hbm.at[sl, i, pl.ds(u, 1), :],
                                          ssem.at[sl]).start()

            return carry

        lax.fori_loop(0, groups(tile), body, 0)

    def wait_scatter(tile, sl):
        def body(i, carry):
            pltpu.make_async_copy(obuf.at[sl, pl.ds(0, 1)], dump_hbm.at[sl, pl.ds(0, 1)], ssem.at[sl]).wait()
            return carry

        lax.fori_loop(0, groups(tile), body, 0)

    def gather_rows(tile, sl, lo, hi):
        n = groups(tile)
        for r in range(lo, hi):
            idx = gsrc_ref[tile * tb + r]

            @pl.when(r // SUBLANES < n)
            def _():
                pltpu.make_async_copy(xc_hbm.at[lax.shift_right_logical(idx, SUBLANES.bit_length() - 1),
                                                pl.ds(idx & (SUBLANES - 1), 1), :],
                                      gbuf.at[sl, r // SUBLANES, pl.ds(r % SUBLANES, 1), :], gsem.at[sl]).start()

    def scatter_rows(tile, sl, lo, hi, enabled):
        n = groups(tile)
        for r in range(lo, hi):
            idx = sdst_ref[tile * tb + r]
            src = obuf.at[sl, r // SUBLANES, pl.ds(r % SUBLANES, 1), :]
            live = enabled & (r // SUBLANES < n)

            @pl.when(live & (idx >= 0))
            def _():
                pltpu.make_async_copy(src, y_hbm.at[lax.shift_right_logical(idx, SUBLANES.bit_length() - 1),
                                                    pl.ds(idx & (SUBLANES - 1), 1), :], ssem.at[sl]).start()

            @pl.when(live & (idx < 0))
            def _():
                pltpu.make_async_copy(src, dump_hbm.at[sl, r // SUBLANES, pl.ds(r % SUBLANES, 1), :],
                                      ssem.at[sl]).start()

    @pl.when(q < n_used)
    def _():
        slot = lax.rem(q, RING)
        ahead = lax.rem(q + 2, RING)
        behind = lax.rem(q + 1, RING)

        @pl.when(q == 0)
        def _():
            gbuf[...] = jnp.zeros(gbuf.shape, F32)
            start_gather(0, 0)

            @pl.when(n_used > 1)
            def _():
                start_gather(1, 1)

        wait_gather(q, slot)

        @pl.when(q >= RING)
        def _():
            wait_scatter(q - RING, slot)

        nxt = jnp.minimum(q + 2, n_used - 1)
        prev, has_prev = jnp.maximum(q - 1, 0), q >= 1
        part = tb // EXPERTS_PER_GROUP
        rows = gbuf[slot].reshape(tb, gbuf.shape[-1])
        x1, comb = rows[:, 0:d], rows[:, d:d + LANES]
        shift2, scale2, gate2 = mod_ref[:, 3 * d:4 * d], mod_ref[:, 4 * d:5 * d], mod_ref[:, 5 * d:6 * d]
        h2 = (_rmsnorm(x1, g2_ref[...]) * (1.0 + scale2) + shift2).astype(BF16)
        first_lane = N_GROUPS + grp_ref[q] * EXPERTS_PER_GROUP
        lane = _iota(comb.shape, 1)
        acc = jnp.zeros((tb, d), F32)
        for e in range(EXPERTS_PER_GROUP):
            w_e = jnp.sum(jnp.where(lane == first_lane + e, comb, 0.0), axis=1, keepdims=True)
            a = _mm(h2, w1_ref[e])
            b = _mm(h2, w3_ref[e])
            acc = acc + _mm(_silu(a) * b * w_e, w2_ref[e])
            gather_rows(nxt, ahead, e * part, (e + 1) * part)
            scatter_rows(prev, ahead, e * part, (e + 1) * part, has_prev)
        obuf[slot] = _rmsnorm(x1 + gate2 * acc, gf_ref[...]).reshape(tb // SUBLANES, SUBLANES, d)

        @pl.when(q == n_used - 1)
        def _():
            start_scatter(q, slot)
            wait_scatter(q, slot)
            wait_gather(q, ahead)

            @pl.when(q >= 1)
            def _():
                wait_scatter(q - 1, ahead)
                wait_gather(q, behind)

            @pl.when(q >= 2)
            def _():
                wait_scatter(q - 2, behind)


def _moe(xc, mod, p, tb):
    s, l, width = xc.shape
    d = D_MODEL
    gsel = xc[:, :, d + GROUP_LANE].astype(jnp.int32)
    row_src, tile_seg, tile_grp, n_used = _route(gsel, tb)
    max_tiles = tile_seg.shape[0]
    tile_rows = jnp.sum((row_src.reshape(max_tiles, tb) >= 0).astype(jnp.int32), axis=1)
    grouped = lambda w: w.reshape((N_GROUPS, EXPERTS_PER_GROUP) + w.shape[1:])
    w1, w3, w2 = grouped(p["moe_w1"]), grouped(p["moe_w3"]), grouped(p["moe_w2"])
    const = lambda shape: pl.BlockSpec(shape, lambda q, *_: (0,) * len(shape))
    by_group = lambda w: pl.BlockSpec((None,) + w.shape[1:],
                                      lambda q, gs, sd, seg, grp, used, cnt: (grp[q], 0, 0, 0))
    hbm = pl.BlockSpec(memory_space=pl.ANY)
    y, _ = pl.pallas_call(
        functools.partial(_moe_kernel, tb=tb),
        grid_spec=pltpu.PrefetchScalarGridSpec(
            num_scalar_prefetch=6,
            grid=(max_tiles,),
            in_specs=[hbm, pl.BlockSpec((None, 1, 6 * d), lambda q, gs, sd, seg, grp, used, cnt: (seg[q], 0, 0)),
                      const((1, d)), by_group(w1), by_group(w3), by_group(w2), const((1, d)), hbm],
            out_specs=[hbm, hbm],
            scratch_shapes=[pltpu.VMEM((RING, tb // SUBLANES, SUBLANES, width), F32),
                            pltpu.VMEM((RING, tb // SUBLANES, SUBLANES, d), F32),
                            pltpu.SemaphoreType.DMA((RING,)), pltpu.SemaphoreType.DMA((RING,))]),
        out_shape=[jax.ShapeDtypeStruct((s * l // SUBLANES, SUBLANES, d), F32),
                   jax.ShapeDtypeStruct((RING, tb // SUBLANES, SUBLANES, d), F32)],
        input_output_aliases={13: 1},
        compiler_params=_cparams(("arbitrary",)),
        name="moe",
    )(jnp.maximum(row_src, 0), row_src, tile_seg, tile_grp, n_used, tile_rows,
      xc.reshape(s * l // SUBLANES, SUBLANES, width), mod, p["norm2_g"], w1, w3, w2, p["final_norm_g"],
      jnp.zeros((RING, tb // SUBLANES, SUBLANES, d), F32))
    return y.reshape(s, l, d)


def _trunk(x, mod_seg, states, conv2d, p, tm, tb, mixer_seqs):
    b, t, d = x.shape
    n_seg = mod_seg.shape[0]
    seg = lambda a: a.reshape(n_seg, b * t // n_seg, a.shape[-1])
    per_seq = lambda a: a.reshape(b, t, a.shape[-1])
    xa, ub, o_pre = (per_seq(a) for a in _in_proj(seg(x), mod_seg, p["norm1_g"], p["w_in"], tm))
    if states is None:
        s_rwkv, mlstm_states = None, None
    else:
        s_rwkv, s_c, s_n, s_m = states
        mlstm_states = (s_c, s_n, jnp.broadcast_to(s_m[..., None], s_m.shape + (LANES,)))
    ya, s_rwkv = _rwkv(xa, s_rwkv, p, mixer_seqs)
    yb, s_c, s_n, s_m = _mlstm(ub, o_pre, mlstm_states, p, conv2d, mixer_seqs)
    xc = _merge(seg(x), seg(ya), seg(yb), mod_seg, p, tm)
    y = _moe(xc, mod_seg, p, tb)
    return y.reshape(b, t, d), s_rwkv, s_c, s_n, s_m[..., 0]


def kernel(x_prompt, x_sample, c, c_ctx, state_rwkv, state_mlstm_C, state_mlstm_n, state_mlstm_m, w_mod, b_mod, norm1_g, norm2_g, w_in, rwkv_mu, rwkv_w_r, rwkv_w_k, rwkv_w_v, rwkv_w0, rwkv_w1, rwkv_w2, rwkv_a0, rwkv_a1, rwkv_a2, rwkv_g1, rwkv_g2, rwkv_k_k, rwkv_k_a, rwkv_r_k, rwkv_ln_g, rwkv_ln_b, rwkv_w_proj, mlstm_conv, mlstm_w_q, mlstm_w_k, mlstm_w_v, mlstm_w_i, mlstm_b_i, mlstm_w_f, mlstm_b_f, mlstm_ln_g, mlstm_w_proj, w_out, moe_w_group, moe_b_group, moe_w_expert, moe_b_expert, moe_w1, moe_w3, moe_w2, final_norm_g):
    assert w_mod.shape[0] == 1, "single trunk layer"
    l = 0
    bp, dec = x_prompt.shape[0], x_sample.shape[0]
    bf = lambda w: w.astype(BF16)
    row = lambda w: w.reshape(1, -1).astype(F32)
    wg = jnp.concatenate([mlstm_w_i[l, 0], mlstm_w_i[l, 1], mlstm_w_f[l, 0], mlstm_w_f[l, 1]], axis=1)
    bg = jnp.concatenate([mlstm_b_i[l, 0], mlstm_b_i[l, 1], mlstm_b_f[l, 0], mlstm_b_f[l, 1]])
    wg = jnp.pad(wg, ((0, 0), (0, LANES - wg.shape[1])))
    bg = jnp.pad(bg, (0, LANES - bg.shape[0]))
    w_router = jnp.pad(jnp.concatenate([moe_w_group[l], moe_w_expert[l]], axis=1),
                       ((0, 0), (0, LANES - N_GROUPS - N_EXPERTS)))
    b_router = jnp.pad(jnp.concatenate([moe_b_group[l], moe_b_expert[l]]), (0, LANES - N_GROUPS - N_EXPERTS))
    p = {
        "norm1_g": row(norm1_g[l]), "norm2_g": row(norm2_g[l]), "w_in": w_in[l],
        "rwkv_mu": rwkv_mu[l], "rwkv_wr": bf(rwkv_w_r[l]), "rwkv_wk": bf(rwkv_w_k[l]), "rwkv_wv": bf(rwkv_w_v[l]),
        "rwkv_w0": rwkv_w0[l], "rwkv_w1": bf(rwkv_w1[l]), "rwkv_w2": bf(rwkv_w2[l]),
        "rwkv_a0": rwkv_a0[l], "rwkv_a1": bf(rwkv_a1[l]), "rwkv_a2": bf(rwkv_a2[l]),
        "rwkv_g1": bf(rwkv_g1[l]), "rwkv_g2": bf(rwkv_g2[l]),
        "rwkv_k_k": row(rwkv_k_k[l]), "rwkv_k_a": row(rwkv_k_a[l]), "rwkv_r_k": row(rwkv_r_k[l]),
        "rwkv_ln_g": row(rwkv_ln_g[l]), "rwkv_ln_b": row(rwkv_ln_b[l]),
        "rwkv_w_proj": rwkv_w_proj[l],
        "mlstm_conv": mlstm_conv[l].reshape(9, B_WIDTH),
        "mlstm_wq": bf(mlstm_w_q[l]), "mlstm_wk": bf(mlstm_w_k[l]), "mlstm_wv": bf(mlstm_w_v[l]),
        "mlstm_wg": bf(wg), "mlstm_bg": row(bg),
        "mlstm_wgt": bf(wg[:, :16].T), "mlstm_bgt": jnp.broadcast_to(bg[:16, None], (16, LANES)),
        "mlstm_ln_g": row(mlstm_ln_g[l]), "mlstm_w_proj": mlstm_w_proj[l], "w_out": w_out[l],
        "moe_w_router": bf(w_router), "moe_b_router": row(b_router),
        "moe_w1": moe_w1[l], "moe_w3": moe_w3[l], "moe_w2": moe_w2[l],
        "final_norm_g": row(final_norm_g),
    }
    cvec = jnp.concatenate([c_ctx[None, :], c, jnp.zeros((8 - 1 - dec, D_MODEL), F32)], axis=0)
    mod = _mod(cvec, w_mod[l], b_mod[l].reshape(1, -1))
    mod_ctx = mod[0:1][:, None, :]
    mod_lat = mod[1:1 + dec][:, None, :]

    yp, n_rwkv, n_c, n_n, n_m = _trunk(x_prompt, mod_ctx, None, False, p, 512, 256, 2)
    lat_states = tuple(s[:, l].astype(F32) for s in (state_rwkv, state_mlstm_C, state_mlstm_n, state_mlstm_m))
    ys, _, _, _, _ = _trunk(x_sample, mod_lat, lat_states, True, p, 512, 256, 1)
    dt = x_prompt.dtype
    return (yp, ys, n_rwkv[:, None].astype(dt), n_c[:, None].astype(dt), n_n[:, None].astype(dt),
            n_m[:, None].astype(dt))
```

```python
import functools

import jax
import jax.numpy as jnp
from jax import lax
from jax.experimental import pallas as pl
from jax.experimental.pallas import tpu as pltpu

F32 = jnp.float32
BF16 = jnp.bfloat16

D_MODEL = 1024
GRID_W = 64
A_HEAD_DIM = 64
A_WIDTH = D_MODEL // 2
A_HEADS = A_WIDTH // A_HEAD_DIM
B_HEAD_DIM = 128
B_WIDTH = D_MODEL // 2
B_HEADS = B_WIDTH // B_HEAD_DIM
CHUNK = 64
SCAN_HEADS = 2
RING = 3
ROW_TILE = 512
SUM_TILE = 256
N_GROUPS = 4
EXPERTS_PER_GROUP = 4
N_EXPERTS = N_GROUPS * EXPERTS_PER_GROUP
GROUP_LANE = N_GROUPS + N_EXPERTS
NORM_EPS = 1e-6
RWKV_GN_EPS = 64e-5
MLSTM_GN_EPS = 1e-5
LANES = 128
SUBLANES = 8
CONV_PAD = 72
VMEM_LIMIT = 56 * 1024 * 1024


def _mm(a, b):
    return jnp.dot(a.astype(BF16), b.astype(BF16), preferred_element_type=F32)


_NN, _NT, _TN = ((1,), (0,)), ((1,), (1,)), ((0,), (0,))


def _split(a):
    hi = a.astype(BF16)
    return hi, (a - hi.astype(F32)).astype(BF16)


def _split3(a):
    hi = a.astype(BF16)
    rest = a - hi.astype(F32)
    lo = rest.astype(BF16)
    return hi, lo, (rest - lo.astype(F32)).astype(BF16)


def _sum01(a, b, dims=_NN):
    dot = lambda p, q: lax.dot_general(p, q, (dims, ((), ())), preferred_element_type=F32)
    terms = [dot(x, b) for x in a] if isinstance(a, tuple) else [dot(a, x) for x in b]
    return functools.reduce(lambda p, q: p + q, terms)


def _mm_nt(a, b):
    return lax.dot_general(a.astype(BF16), b.astype(BF16), (((1,), (1,)), ((), ())),
                           preferred_element_type=F32)


def _mm_tn(a, b):
    return lax.dot_general(a.astype(BF16), b.astype(BF16), (((0,), (0,)), ((), ())),
                           preferred_element_type=F32)


def _sigmoid(x):
    return 1.0 / (1.0 + jnp.exp(-x))


def _silu(x):
    return x * _sigmoid(x)


def _softplus(x):
    return jnp.maximum(x, 0.0) + jnp.log(1.0 + jnp.exp(-jnp.abs(x)))


def _iota(shape, dim):
    return lax.broadcasted_iota(jnp.int32, shape, dim)


def _tri(reverse, inclusive, n=CHUNK):
    row, col = _iota((n, n), 0), _iota((n, n), 1)
    if reverse:
        return (col >= row) if inclusive else (col > row)
    return (col <= row) if inclusive else (col < row)


def _same_block(n, blk):
    sh = blk.bit_length() - 1
    row, col = _iota((n, n), 0), _iota((n, n), 1)
    return lax.shift_right_logical(row, sh) == lax.shift_right_logical(col, sh)


def _rmsnorm(x, g):
    return x * lax.rsqrt(jnp.mean(x * x, axis=-1, keepdims=True) + NORM_EPS) * g


def _cparams(sem):
    return pltpu.CompilerParams(dimension_semantics=sem, vmem_limit_bytes=VMEM_LIMIT)


def _full(shape):
    nd = len(shape)
    return pl.BlockSpec(shape, lambda *_: (0,) * nd)


def _mod_kernel(c_ref, w_ref, b_ref, o_ref):
    o_ref[...] = _mm(_silu(c_ref[...]), w_ref[...]) + b_ref[...]


def _mod(cvec, w_mod, b_mod):
    rows, d = cvec.shape
    n = w_mod.shape[1]
    tn = 1536
    return pl.pallas_call(
        _mod_kernel,
        grid=(n // tn,),
        in_specs=[_full((rows, d)), pl.BlockSpec((d, tn), lambda j: (0, j)),
                  pl.BlockSpec((1, tn), lambda j: (0, j))],
        out_specs=pl.BlockSpec((rows, tn), lambda j: (0, j)),
        out_shape=jax.ShapeDtypeStruct((rows, n), F32),
        compiler_params=_cparams(("arbitrary",)),
        name="mod",
    )(cvec, w_mod, b_mod)


W_IN_BLOCK = 512


def _modulated_norm1(x_ref, mod_ref, g_ref):
    d = D_MODEL
    shift, scale = mod_ref[:, 0:d], mod_ref[:, d:2 * d]
    return (_rmsnorm(x_ref[...], g_ref[...]) * (1.0 + scale) + shift).astype(BF16)


def _w_in_blocks(w_in, first, count):
    specs = [pl.BlockSpec((w_in.shape[0], W_IN_BLOCK), lambda *_, j=first + k: (0, j)) for k in range(count)]
    return specs, [w_in] * count


def _in_kernel(x_ref, mod_ref, g_ref, wa_ref, wu_ref, wo_ref, xa_ref, ub_ref, op_ref):
    h = _modulated_norm1(x_ref, mod_ref, g_ref)
    for ref, w_ref in ((xa_ref, wa_ref), (ub_ref, wu_ref), (op_ref, wo_ref)):
        ref[...] = _mm(h, w_ref[...])


def _in_proj(x, mod, g1, w_in, tm):
    b, t, d = x.shape
    per = t // tm
    widths = (A_WIDTH, B_WIDTH, B_WIDTH)
    assert all(n == W_IN_BLOCK for n in widths)
    tok = lambda n: pl.BlockSpec((None, tm, n), lambda i: (i // per, i % per, 0))
    w_specs, w_args = _w_in_blocks(w_in, 0, 3)
    return pl.pallas_call(
        _in_kernel,
        grid=(b * per,),
        in_specs=[tok(d), pl.BlockSpec((None, 1, 6 * d), lambda i: (i // per, 0, 0)), _full((1, d))] + w_specs,
        out_specs=[tok(n) for n in widths],
        out_shape=[jax.ShapeDtypeStruct((b, t, n), F32) for n in widths],
        compiler_params=_cparams(("arbitrary",)),
        name="in_proj",
    )(x, mod, g1, *w_args)


def _head_of_lane(shape):
    return lax.shift_right_logical(_iota(shape, 1), A_HEAD_DIM.bit_length() - 1)


def _pair_diag(x):
    head = _head_of_lane(x.shape)
    return jnp.concatenate([x * (head == h).astype(x.dtype) for h in range(x.shape[1] // A_HEAD_DIM)], axis=0)


def _pair_mm3(lhs, rhs, dims=_NN):
    rows = lhs[0].shape[0]
    if dims == _NT:
        wh, wl = _split(_pair_diag(rhs).T)
    else:
        rh, rl = _split(rhs)
        wh, wl = _pair_diag(rh), _pair_diag(rl)
    parts = [_split(l) for l in lhs]
    top = jnp.dot(jnp.concatenate([t for part in parts for t in part], axis=0), wh, preferred_element_type=F32)
    low = jnp.dot(jnp.concatenate([part[0] for part in parts], axis=0), wl, preferred_element_type=F32)
    return [top[2 * k * rows:(2 * k + 1) * rows] + top[(2 * k + 1) * rows:(2 * k + 2) * rows]
            + low[k * rows:(k + 1) * rows] for k in range(len(lhs))]


def _pair_index(shape):
    return _iota(shape, 0), _iota(shape, 1) & (A_HEAD_DIM - 1)


def _pair_tri_inverses(mats):
    t_idx, s_idx = _pair_index(mats[0].shape)
    same = lambda blk: (lax.shift_right_logical(t_idx, blk.bit_length() - 1)
                        == lax.shift_right_logical(s_idx, blk.bit_length() - 1))
    eye = (t_idx == s_idx).astype(F32)
    nd = [jnp.where(same(8), m, 0.0) for m in mats]
    n2 = [_pair_mm3([x], x)[0] for x in nd]
    t = [eye + x for x in nd]
    both = [_pair_mm3([a, b], a) for a, b in zip(n2, t)]
    t = [x + r[1] for x, r in zip(t, both)]
    t = [x + _pair_mm3([x], r[0])[0] for x, r in zip(t, both)]
    blk = 8
    while blk < CHUNK:
        sel = same(2 * blk) & jnp.logical_not(same(blk))
        w = [_pair_mm3([jnp.where(sel, m, 0.0)], x)[0] for m, x in zip(mats, t)]
        t = [x + _pair_mm3([x], y)[0] for x, y in zip(t, w)]
        blk *= 2
    return t


def _rwkv_kernel(x_ref, s0_ref, mu_ref, wr_ref, wk_ref, wv_ref, w0_ref, w1_ref, w2_ref,
                 a0_ref, a1_ref, a2_ref, g1_ref, g2_ref, kk_ref, ka_ref, rk_ref, lng_ref, lnb_ref,
                 y_ref, s_ref,
                 r_s, v_s, kk_s, cum_s, cx_s, kd_s, bb_s, g_s, bonus_s, y_s, sp_s, *, seq_len):
    total = x_ref.shape[0]
    n_seq = total // seq_len
    n_chunks = seq_len // CHUNK
    n_tiles = total // ROW_TILE
    hd = A_HEAD_DIM
    n_groups, width = sp_s.shape[2], sp_s.shape[4]
    per_group = width // hd
    x = x_ref[...]
    row = _iota(x.shape, 0) & (seq_len - 1)
    prev = jnp.where(row == 0, 0.0, pltpu.roll(x, 1, 0))
    nxt = jnp.where(row == seq_len - 1, 0.0, pltpu.roll(x, total - 1, 0))
    y_s[...] = 0.5 * (prev + nxt) - x
    head_ones = _same_block(A_WIDTH, hd).astype(BF16)
    chunk_diag = _same_block(SUM_TILE, CHUNK)
    run_sum = [(chunk_diag & _tri(d == 1, True, SUM_TILE)).astype(BF16) for d in range(2)]

    def project_tile(ti, carry):
        rows = pl.ds(pl.multiple_of(ti * ROW_TILE, ROW_TILE), ROW_TILE)
        x, xx = x_ref[rows, :], y_s[rows, :]
        xr, xw, xk, xv, xi, xg = (x + xx * mu_ref[j:j + 1, :] for j in range(6))
        r = _mm(xr, wr_ref[...])
        k = _mm(xk, wk_ref[...])
        v = _mm(xv, wv_ref[...])
        g_s[rows, :] = _mm(_sigmoid(_mm(xg, g1_ref[...])), g2_ref[...])
        kk = k * kk_ref[...]
        kk = kk / jnp.maximum(jnp.sqrt(_sum01(_split(kk * kk), head_ones)), 1e-12)
        r_s[rows, :] = r
        v_s[rows, :] = v
        kk_s[rows, :] = kk
        kd_sum = jnp.zeros_like(x)
        for d in range(2):
            zw = w0_ref[d:d + 1, :] + _mm(jnp.tanh(_mm(xw, w1_ref[d])), w2_ref[d])
            lw = -jnp.exp(-_softplus(-zw) - 0.5)
            cum = jnp.concatenate([_sum01(run_sum[d], _split3(lw[j:j + SUM_TILE]))
                                   for j in range(0, ROW_TILE, SUM_TILE)], axis=0)
            cum_s[d, rows, :] = cum
            cx_s[d, rows, :] = cum - lw
            a = _sigmoid(a0_ref[d:d + 1, :] + _mm(_mm(xi, a1_ref[d]), a2_ref[d]))
            kd = k * (1.0 + (a - 1.0) * ka_ref[...])
            kd_s[d, rows, :] = kd
            bb_s[d, rows, :] = kk * a
            kd_sum = kd_sum + kd
        bonus_s[rows, :] = _sum01(_split(r * kd_sum * rk_ref[...]), head_ones) * v
        y_s[rows, :] = jnp.zeros_like(x)
        return carry

    lax.fori_loop(0, n_tiles, project_tile, 0)
    for n in range(n_seq):
        for d in range(2):
            for p in range(n_groups):
                sp_s[n, d, p] = (jnp.zeros((hd, width), F32) if s0_ref is None else jnp.concatenate(
                    [s0_ref[n, d, per_group * p + j] for j in range(per_group)], axis=1))

    t_idx, s_idx = _pair_index((CHUNK, width))
    strict = [s_idx < t_idx, s_idx > t_idx]
    incl = [s_idx <= t_idx, s_idx >= t_idx]
    lane_head = _head_of_lane((CHUNK, width))

    def chunk_body(ci, carry):
        chains = []
        for n, d in [(n, d) for n in range(n_seq) for d in range(2)]:
            cpos = n * n_chunks + ((n_chunks - 1 - ci) if d == 1 else ci)
            last = 0 if d == 1 else CHUNK - 1
            rows = pl.ds(pl.multiple_of(cpos * CHUNK, CHUNK), CHUNK)
            cum, cx = cum_s[d, rows, :], cx_s[d, rows, :]
            tot = cum[last:last + 1, :]
            kdc, bbc = kd_s[d, rows, :], bb_s[d, rows, :]
            e_neg = jnp.exp(-cum)
            e_tail = jnp.exp(tot - cum)
            at = -kk_s[rows, :] * jnp.exp(cx)
            rt = r_s[rows, :] * jnp.exp(cum)
            bt, kt = bbc * e_neg, kdc * e_neg
            bp, kp = bbc * e_tail, kdc * e_tail
            pc = jnp.exp(tot)
            vc = v_s[rows, :]
            for p in range(n_groups):
                ln = slice(p * width, (p + 1) * width)
                chains.append(dict(d=d, st=(n, d, p), rows=rows, ln=ln, v=vc[:, ln], pc=pc[:, ln],
                                   lhs=jnp.concatenate([at[:, ln], rt[:, ln]], axis=0),
                                   bt=bt[:, ln], kt=kt[:, ln], bp=bp[:, ln], kp=kp[:, ln]))
        for c in chains:
            c["sb"] = _pair_mm3([c["lhs"]], c["bt"], _NT)[0]
            c["sk"] = _pair_mm3([c["lhs"]], c["kt"], _NT)[0]
            c["s"] = sp_s[c["st"]]
            c["fs"] = _mm(c["lhs"], _pair_diag(c["s"]).T)
            c["v_diag"] = _pair_diag(c["v"].astype(BF16))
        for c in chains:
            a_ak = jnp.where(strict[c["d"]], c["sk"][:CHUNK], 0.0)
            c["x"] = c["fs"][:CHUNK] + _mm(a_ak, c["v_diag"])
        inv = _pair_tri_inverses([jnp.where(strict[c["d"]], c["sb"][:CHUNK], 0.0) for c in chains])
        for c, t in zip(chains, inv):
            c["u"] = _pair_mm3([t], c["x"])[0]
        for c in chains:
            d, rows, ln, u = c["d"], c["rows"], c["ln"], c["u"]
            a_r = jnp.concatenate([jnp.where(incl[d], c["sb"][CHUNK:], 0.0),
                                   jnp.where(incl[d], c["sk"][CHUNK:], 0.0)], axis=1)
            uv_diag = jnp.concatenate([_pair_diag(u.astype(BF16)), c["v_diag"]], axis=0)
            y_s[rows, ln] = y_s[rows, ln] + c["fs"][CHUNK:] + _mm(a_r, uv_diag)
            g = _mm_tn(jnp.concatenate([u, c["v"]], axis=0), jnp.concatenate([c["bp"], c["kp"]], axis=0))
            own = g[0:hd]
            for j in range(1, per_group):
                own = jnp.where(lane_head == j, g[j * hd:(j + 1) * hd], own)
            sp_s[c["st"]] = c["s"] * c["pc"] + own
        return carry

    lax.fori_loop(0, n_chunks, chunk_body, 0)
    for n in range(n_seq):
        for d in range(2):
            for p in range(n_groups):
                for j in range(per_group):
                    s_ref[n, d, per_group * p + j] = sp_s[n, d, p, :, j * hd:(j + 1) * hd]

    def finish_tile(ti, carry):
        rows = pl.ds(pl.multiple_of(ti * ROW_TILE, ROW_TILE), ROW_TILE)
        ys = y_s[rows, :]
        cen = ys - _sum01(_split(ys), head_ones) * (1.0 / hd)
        var = _sum01(_split(cen * cen), head_ones) * (1.0 / hd)
        yn = cen * lax.rsqrt(var + RWKV_GN_EPS) * lng_ref[...] + lnb_ref[...]
        y_ref[rows, :] = (yn + bonus_s[rows, :]) * g_s[rows, :]
        return carry

    lax.fori_loop(0, n_tiles, finish_tile, 0)


def _rwkv(xa, s0, p, n_seq):
    b, t, w = xa.shape
    assert t & (t - 1) == 0 and b % n_seq == 0
    names = ("mu", "wr", "wk", "wv", "w0", "w1", "w2", "a0", "a1", "a2", "g1", "g2",
             "k_k", "k_a", "r_k", "ln_g", "ln_b")
    weights = [p["rwkv_" + n] for n in names]
    rows = n_seq * t
    seq = pl.BlockSpec((None, rows, w), lambda i: (i, 0, 0))
    st = pl.BlockSpec((n_seq, 2, A_HEADS, A_HEAD_DIM, A_HEAD_DIM), lambda i: (i, 0, 0, 0, 0))
    tw = lambda lead=(): pltpu.VMEM(lead + (rows, w), F32)
    kernel = functools.partial(_rwkv_kernel, seq_len=t)
    if s0 is None:
        kernel, state_specs, states = _zero_state_kernel(kernel, 1, 1), [], ()
    else:
        state_specs, states = [st], (s0,)
    y, s_out = pl.pallas_call(
        kernel,
        grid=(b // n_seq,),
        in_specs=[seq] + state_specs + [_full(x.shape) for x in weights],
        out_specs=[seq, st],
        out_shape=[jax.ShapeDtypeStruct((b // n_seq, rows, w), F32),
                   jax.ShapeDtypeStruct((b, 2, A_HEADS, A_HEAD_DIM, A_HEAD_DIM), F32)],
        scratch_shapes=[tw(), tw(), tw(), tw((2,)), tw((2,)), tw((2,)), tw((2,)), tw(), tw(), tw(),
                        pltpu.VMEM((n_seq, 2, A_HEADS // SCAN_HEADS, A_HEAD_DIM, SCAN_HEADS * A_HEAD_DIM), F32)],
        compiler_params=_cparams(("arbitrary",)),
        name="rwkv",
    )(xa.reshape(b // n_seq, rows, w), *states, *weights)
    return y.reshape(b, t, w), s_out


def _mlstm_kernel(u_ref, op_ref, c0_ref, n0_ref, m0_ref, conv_ref, wq_ref, wk_ref, wv_ref,
                  wg_ref, bg_ref, wgt_ref, bgt_ref, lng_ref,
                  y_ref, c_ref, n_ref, m_ref,
                  pad_s, q_s, k_s, v_s, dd_s, cr_s, h_s, *, conv2d, seq_len):
    t_len = u_ref.shape[0]
    n_seq = t_len // seq_len
    n_chunks = seq_len // CHUNK
    hd = B_HEAD_DIM
    u = u_ref[...]
    pad_s[0:CONV_PAD, :] = jnp.zeros((CONV_PAD, B_WIDTH), F32)
    pad_s[CONV_PAD + t_len:CONV_PAD + t_len + CONV_PAD, :] = jnp.zeros((CONV_PAD, B_WIDTH), F32)
    pad_s[CONV_PAD:CONV_PAD + t_len, :] = u
    pos = _iota(u.shape, 0) & (seq_len - 1)
    col = pos & (GRID_W - 1)
    conv = jnp.zeros_like(u)
    for kh in range(3):
        if not conv2d and kh != 1:
            continue
        for kw in range(3):
            shift = (kh - 1) * GRID_W + (kw - 1)
            term = pad_s[CONV_PAD + shift:CONV_PAD + shift + t_len, :] * conv_ref[kh * 3 + kw:kh * 3 + kw + 1, :]
            if conv2d and kw != 1:
                src = col + (kw - 1)
                term = jnp.where((src >= 0) & (src < GRID_W), term, 0.0)
            if n_seq > 1 and shift != 0:
                term = jnp.where((pos + shift >= 0) & (pos + shift < seq_len), term, 0.0)
            conv = conv + term
    uc = _silu(conv)
    for h in range(B_HEADS):
        ln = slice(h * hd, (h + 1) * hd)
        q_s[:, ln] = _mm(uc[:, ln], wq_ref[h])
        k_s[:, ln] = _mm(uc[:, ln], wk_ref[h]) * (hd ** -0.5)
        v_s[:, ln] = _mm(u[:, ln], wv_ref[h])
    gcol = _mm(uc, wg_ref[...]) + bg_ref[...]
    gcol = jnp.where(_iota(gcol.shape, 1) >= 8, -_softplus(-gcol), gcol)
    grow = _mm_nt(wgt_ref[...], uc) + bgt_ref[:, 0:1]
    grow = jnp.where(_iota(grow.shape, 0) >= 8, -_softplus(-grow), grow)
    chunk_diag = _same_block(SUM_TILE, CHUNK)
    run_sum = [(chunk_diag & _tri(d == 1, True, SUM_TILE)).astype(BF16) for d in range(2)]
    per_tile = SUM_TILE // CHUNK
    for ti in range(t_len // SUM_TILE):
        sl = slice(ti * SUM_TILE, (ti + 1) * SUM_TILE)
        gt, rt = _split3(gcol[sl, :]), _split3(grow[:, sl])
        backward_lane = (_iota((SUM_TILE, LANES), 1) & 4) != 0
        cum_c = jnp.where(backward_lane, _sum01(run_sum[1], gt, _NN), _sum01(run_sum[0], gt, _NN))
        dd_s[sl, :] = pltpu.roll(cum_c, LANES - 8, 1) - gcol[sl, :]
        backward_row = (_iota((16, SUM_TILE), 0) & 4) != 0
        cum_r = jnp.where(backward_row, _sum01(rt, run_sum[1], _NT), _sum01(rt, run_sum[0], _NT))
        for c in range(per_tile):
            cr_s[ti * per_tile + c] = cum_r[:, c * CHUNK:(c + 1) * CHUNK]
    h_s[...] = jnp.zeros_like(u)
    for ref, init in ((c_ref, c0_ref), (n_ref, n0_ref), (m_ref, m0_ref)):
        ref[...] = jnp.zeros(ref.shape, F32) if init is None else init[...]
    row_i, col_i = _iota((CHUNK, CHUNK), 0), _iota((CHUNK, CHUNK), 1)
    incl_t = [row_i <= col_i, row_i >= col_i]
    eye = row_i == col_i
    pick = [(_iota((LANES, B_HEADS * LANES), 0)
             == d * B_HEADS + lax.shift_right_logical(_iota((LANES, B_HEADS * LANES), 1), LANES.bit_length() - 1)
             ).astype(BF16) for d in range(2)]
    spread = (lax.shift_right_logical(_iota((2 * CHUNK, 2 * LANES), 0), CHUNK.bit_length() - 1)
              == lax.shift_right_logical(_iota((2 * CHUNK, 2 * LANES), 1), LANES.bit_length() - 1)).astype(BF16)
    ones_bf = jnp.ones((CHUNK, LANES), BF16)

    def chunk_body(ci, carry):
        chains = []
        for n, d in [(n, d) for n in range(n_seq) for d in range(2)]:
            cpos = n * n_chunks + ((n_chunks - 1 - ci) if d == 1 else ci)
            last = 0 if d == 1 else CHUNK - 1
            rows = pl.ds(pl.multiple_of(cpos * CHUNK, CHUNK), CHUNK)
            cr = cr_s[cpos]
            d_col = _sum01(_split3(dd_s[rows, :]), pick[d])
            for h in range(B_HEADS):
                ln = slice(h * hd, (h + 1) * hd)
                lf = 8 + d * 4 + h
                chains.append(dict(d=d, st=(n, d, h), vec=(n, d, slice(h, h + 1)), rows=rows, ln=ln, last=last,
                                   b_row=cr[lf:lf + 1, :], d_col=d_col[:, ln],
                                   m=m_ref[n, d, h:h + 1, 0:1], q=q_s[rows, ln], k=k_s[rows, ln], v=v_s[rows, ln],
                                   c=c_ref[n, d, h], n=n_ref[n, d, h:h + 1, :]))
        for c in chains:
            c["kq"] = _mm(c["k"], c["q"].T)
            c["qc"] = _mm(c["q"], c["c"])
            c["qn"] = _mm(c["q"], jnp.broadcast_to(c["n"], (LANES, hd)).T)
            logw = jnp.where(incl_t[c["d"]], c["b_row"] - c["d_col"][:, 0:CHUNK], -jnp.inf)
            c["m_row"] = jnp.maximum(c["b_row"] + c["m"], jnp.max(logw, axis=0, keepdims=True))
            c["logw"] = logw
        for c in chains:
            inter_row = jnp.exp(c["b_row"] + c["m"] - c["m_row"])
            diag = jnp.concatenate([jnp.where(eye, c["m_row"], 0.0), jnp.where(eye, inter_row, 0.0)], axis=1)
            cols = _sum01(_split3(diag), spread)
            c["m_col"], c["inter"] = cols[:, 0:LANES], cols[:, LANES:2 * LANES]
            c["s"] = c["kq"] * jnp.exp(c["logw"] - c["m_row"])
        for c in chains:
            s_hi, s_lo = _split(c["s"])
            sv = lax.dot_general(s_hi, jnp.concatenate([c["v"].astype(BF16), ones_bf], axis=1),
                                 (_TN, ((), ())), preferred_element_type=F32)
            c["sv"] = sv[:, 0:hd]
            c["den"] = (c["inter"] * c["qn"] + sv[:, hd:hd + LANES]
                        + lax.dot_general(s_lo, ones_bf, (_TN, ((), ())), preferred_element_type=F32))
            last = c["last"]
            c["m_new"] = c["m_row"][:, last:last + 1]
            b_last = c["b_row"][:, last:last + 1]
            c["decay"] = jnp.exp(b_last + c["m"] - c["m_new"])
            c["kw"] = c["k"] * jnp.exp(b_last - c["m_new"] - c["d_col"])
        for c in chains:
            c["kv"] = _mm_tn(c["kw"], c["v"])
        for c in chains:
            rows, ln = c["rows"], c["ln"]
            num = c["inter"] * c["qc"] + c["sv"]
            h_s[rows, ln] = h_s[rows, ln] + num / jnp.maximum(jnp.abs(c["den"]), jnp.exp(-c["m_col"]))
            c_ref[c["st"]] = c["decay"] * c["c"] + c["kv"]
            n_ref[c["vec"]] = c["decay"] * c["n"] + jnp.sum(c["kw"], axis=0, keepdims=True)
            m_ref[c["vec"]] = jnp.broadcast_to(c["m_new"], (1, LANES))
        return carry

    lax.fori_loop(0, n_chunks, chunk_body, 0)

    for h in range(B_HEADS):
        ln = slice(h * hd, (h + 1) * hd)
        hh = h_s[:, ln]
        cen = hh - jnp.mean(hh, axis=1, keepdims=True)
        var = jnp.mean(cen * cen, axis=1, keepdims=True)
        y_ref[:, ln] = _sigmoid(op_ref[:, ln]) * (cen * lax.rsqrt(var + MLSTM_GN_EPS) * lng_ref[:, ln])


def _zero_state_kernel(kernel, n_data, n_state):
    def wrapped(*refs, **kw):
        return kernel(*refs[:n_data], *([None] * n_state), *refs[n_data:], **kw)
    return wrapped


def _mlstm(ub, o_pre, states, p, conv2d, n_seq):
    b, t, w = ub.shape
    assert t & (t - 1) == 0 and b % n_seq == 0
    names = ("conv", "wq", "wk", "wv", "wg", "bg", "wgt", "bgt", "ln_g")
    weights = [p["mlstm_" + n] for n in names]
    rows, steps = n_seq * t, b // n_seq
    seq = pl.BlockSpec((None, rows, w), lambda i: (i, 0, 0))
    cst = pl.BlockSpec((n_seq, 2, B_HEADS, B_HEAD_DIM, B_HEAD_DIM), lambda i: (i, 0, 0, 0, 0))
    vst = pl.BlockSpec((n_seq, 2, B_HEADS, LANES), lambda i: (i, 0, 0, 0))
    kernel = functools.partial(_mlstm_kernel, conv2d=conv2d, seq_len=t)
    if states is None:
        kernel, state_specs, states = _zero_state_kernel(kernel, 2, 3), [], ()
    else:
        state_specs = [cst, vst, vst]
    y, c_out, n_out, m_out = pl.pallas_call(
        kernel,
        grid=(steps,),
        in_specs=[seq, seq] + state_specs + [_full(x.shape) for x in weights],
        out_specs=[seq, cst, vst, vst],
        out_shape=[jax.ShapeDtypeStruct((steps, rows, w), F32),
                   jax.ShapeDtypeStruct((b, 2, B_HEADS, B_HEAD_DIM, B_HEAD_DIM), F32),
                   jax.ShapeDtypeStruct((b, 2, B_HEADS, LANES), F32),
                   jax.ShapeDtypeStruct((b, 2, B_HEADS, LANES), F32)],
        scratch_shapes=[pltpu.VMEM((rows + 2 * CONV_PAD, w), F32), pltpu.VMEM((rows, w), F32),
                        pltpu.VMEM((rows, w), F32), pltpu.VMEM((rows, w), F32),
                        pltpu.VMEM((rows, LANES), F32), pltpu.VMEM((rows // CHUNK, 16, CHUNK), F32),
                        pltpu.VMEM((rows, w), F32)],
        compiler_params=_cparams(("arbitrary",)),
        name="mlstm",
    )(ub.reshape(steps, rows, w), o_pre.reshape(steps, rows, w), *states, *weights)
    return y.reshape(b, t, w), c_out, n_out, m_out


def _first_lane_of_max(val, lane, valid):
    masked = jnp.where(valid, val, -jnp.inf)
    best = jnp.max(masked, axis=1, keepdims=True)
    idx = jnp.min(jnp.where(valid & (masked == best), lane, LANES), axis=1, keepdims=True)
    return best, idx


def _merge_kernel(x_ref, ya_ref, yb_ref, mod_ref, g1_ref, wga0_ref, wga1_ref, wgb0_ref, wgb1_ref,
                  wpa_ref, wpb_ref, wo_ref, g2_ref, wr_ref, br_ref, xc_ref):
    d = D_MODEL
    gate1 = mod_ref[:, 2 * d:3 * d]
    shift2, scale2 = mod_ref[:, 3 * d:4 * d], mod_ref[:, 4 * d:5 * d]
    h1 = _modulated_norm1(x_ref, mod_ref, g1_ref)
    pa, pb = _mm(ya_ref[...], wpa_ref[...]), _mm(yb_ref[...], wpb_ref[...])
    halves = []
    for j, (wga_ref, wgb_ref) in enumerate(((wga0_ref, wgb0_ref), (wga1_ref, wgb1_ref))):
        cols = slice(j * W_IN_BLOCK, (j + 1) * W_IN_BLOCK)
        halves.append(_sigmoid(_mm(h1, wga_ref[...])) * pa[:, cols] + _sigmoid(_mm(h1, wgb_ref[...])) * pb[:, cols])
    merged = jnp.concatenate(halves, axis=1)
    x1 = x_ref[...] + gate1 * _mm(merged, wo_ref[...])
    xc_ref[:, 0:d] = x1
    h2 = _rmsnorm(x1, g2_ref[...]) * (1.0 + scale2) + shift2
    logits = _mm(h2, wr_ref[...]) + br_ref[...]
    lane = _iota(logits.shape, 1)
    is_group = lane < N_GROUPS
    g_max, g_sel = _first_lane_of_max(logits, lane, is_group)
    g_w = 1.0 / jnp.sum(jnp.where(is_group, jnp.exp(logits - g_max), 0.0), axis=1, keepdims=True)
    expert = lane - N_GROUPS
    in_group = (expert >= 0) & (expert < N_EXPERTS) & (lax.shift_right_arithmetic(expert, EXPERTS_PER_GROUP.bit_length() - 1) == g_sel)
    e_max = jnp.max(jnp.where(in_group, logits, -jnp.inf), axis=1, keepdims=True)
    e_exp = jnp.where(in_group, jnp.exp(logits - e_max), 0.0)
    prob = e_exp / jnp.sum(e_exp, axis=1, keepdims=True)
    p1, i1 = _first_lane_of_max(prob, lane, in_group)
    p2, i2 = _first_lane_of_max(prob, lane, in_group & (lane != i1))
    denom = p1 + p2
    comb = jnp.where(lane == i1, g_w * p1 / denom, 0.0) + jnp.where(lane == i2, g_w * p2 / denom, 0.0)
    xc_ref[:, d:d + LANES] = jnp.where(lane == GROUP_LANE, g_sel.astype(F32), comb)


def _merge(x, ya, yb, mod, p, tm):
    b, t, d = x.shape
    per = t // tm
    assert d == 2 * W_IN_BLOCK
    tok = lambda n: pl.BlockSpec((None, tm, n), lambda i: (i // per, i % per, 0))
    gate_specs, gate_args = _w_in_blocks(p["w_in"], 3, 4)
    weights = [p["rwkv_w_proj"], p["mlstm_w_proj"], p["w_out"], p["norm2_g"], p["moe_w_router"], p["moe_b_router"]]
    return pl.pallas_call(
        _merge_kernel,
        grid=(b * per,),
        in_specs=[tok(d), tok(A_WIDTH), tok(B_WIDTH), pl.BlockSpec((None, 1, 6 * d), lambda i: (i // per, 0, 0)),
                  _full((1, d))] + gate_specs + [_full(w.shape) for w in weights],
        out_specs=tok(d + LANES),
        out_shape=jax.ShapeDtypeStruct((b, t, d + LANES), F32),
        compiler_params=_cparams(("arbitrary",)),
        name="merge",
    )(x, ya, yb, mod, p["norm1_g"], *gate_args, *weights)


def _route(gsel, tb):
    s, l = gsel.shape
    n_buckets = s * N_GROUPS
    max_tiles = s * (l // tb + N_GROUPS - 1)
    onehot = (gsel[..., None] == jnp.arange(N_GROUPS, dtype=jnp.int32)).astype(jnp.int32)
    rank = jnp.sum((jnp.cumsum(onehot, axis=1) - onehot) * onehot, axis=-1)
    n_tiles = ((jnp.sum(onehot, axis=1) + tb - 1) // tb).T.reshape(n_buckets)
    ends = jnp.cumsum(n_tiles)
    first_tile = jnp.sum(onehot * (ends - n_tiles).reshape(N_GROUPS, s).T[:, None, :], axis=-1)
    pos = first_tile * tb + rank
    tok = jnp.arange(s * l, dtype=jnp.int32)
    row_src = jnp.full((max_tiles * tb,), -1, jnp.int32).at[pos.reshape(-1)].set(tok)
    tile_bucket = jnp.sum(jnp.arange(max_tiles, dtype=jnp.int32)[:, None] >= ends[None, :], axis=1)
    tile_bucket = jnp.minimum(tile_bucket, n_buckets - 1).astype(jnp.int32)
    return row_src, tile_bucket % s, tile_bucket // s, ends[-1:].astype(jnp.int32)


def _moe_kernel(gsrc_ref, sdst_ref, seg_ref, grp_ref, used_ref, cnt_ref,
                xc_hbm, mod_ref, g2_ref, w1_ref, w3_ref, w2_ref, gf_ref, dump_in_hbm,
                y_hbm, dump_hbm, gbuf, obuf, gsem, ssem, *, tb):
    d = D_MODEL
    q = pl.program_id(0)
    n_used = used_ref[0]

    def groups(tile):
        return lax.shift_right_logical(cnt_ref[tile] + (SUBLANES - 1), SUBLANES.bit_length() - 1)

    def start_gather(tile, sl):
        def body(i, carry):
            for u in range(SUBLANES):
                idx = gsrc_ref[tile * tb + i * SUBLANES + u]
                pltpu.make_async_copy(xc_hbm.at[lax.shift_right_logical(idx, SUBLANES.bit_length() - 1), pl.ds(idx & (SUBLANES - 1), 1), :],
                                      gbuf.at[sl, i, pl.ds(u, 1), :], gsem.at[sl]).start()
            return carry

        lax.fori_loop(0, groups(tile), body, 0)

    def wait_gather(tile, sl):
        def body(i, carry):
            pltpu.make_async_copy(xc_hbm.at[pl.ds(0, 1)], gbuf.at[sl, pl.ds(0, 1)], gsem.at[sl]).wait()
            return carry

        lax.fori_loop(0, groups(tile), body, 0)

    def start_scatter(tile, sl):
        def body(i, carry):
            for u in range(SUBLANES):
                idx = sdst_ref[tile * tb + i * SUBLANES + u]

                @pl.when(idx >= 0)
                def _():
                    pltpu.make_async_copy(
                        obuf.at[sl, i, pl.ds(u, 1), :],
                        y_hbm.at[lax.shift_right_logical(idx, SUBLANES.bit_length() - 1), pl.ds(idx & (SUBLANES - 1), 1), :],
                        ssem.at[sl]).start()

                @pl.when(idx < 0)
                def _():
                    pltpu.make_async_copy(obuf.at[sl, i, pl.ds(u, 1), :], dump_hbm.at[sl, i, pl.ds(u, 1), :],
                                          ssem.at[sl]).start()

            return carry

        lax.fori_loop(0, groups(tile), body, 0)

    def wait_scatter(tile, sl):
        def body(i, carry):
            pltpu.make_async_copy(obuf.at[sl, pl.ds(0, 1)], dump_hbm.at[sl, pl.ds(0, 1)], ssem.at[sl]).wait()
            return carry

        lax.fori_loop(0, groups(tile), body, 0)

    def gather_rows(tile, sl, lo, hi):
        n = groups(tile)
        for r in range(lo, hi):
            idx = gsrc_ref[tile * tb + r]

            @pl.when(r // SUBLANES < n)
            def _():
                pltpu.make_async_copy(xc_hbm.at[lax.shift_right_logical(idx, SUBLANES.bit_length() - 1),
                                                pl.ds(idx & (SUBLANES - 1), 1), :],
                                      gbuf.at[sl, r // SUBLANES, pl.ds(r % SUBLANES, 1), :], gsem.at[sl]).start()

    def scatter_rows(tile, sl, lo, hi, enabled):
        n = groups(tile)
        for r in range(lo, hi):
            idx = sdst_ref[tile * tb + r]
            src = obuf.at[sl, r // SUBLANES, pl.ds(r % SUBLANES, 1), :]
            live = enabled & (r // SUBLANES < n)

            @pl.when(live & (idx >= 0))
            def _():
                pltpu.make_async_copy(src, y_hbm.at[lax.shift_right_logical(idx, SUBLANES.bit_length() - 1),
                                                    pl.ds(idx & (SUBLANES - 1), 1), :], ssem.at[sl]).start()

            @pl.when(live & (idx < 0))
            def _():
                pltpu.make_async_copy(src, dump_hbm.at[sl, r // SUBLANES, pl.ds(r % SUBLANES, 1), :],
                                      ssem.at[sl]).start()

    @pl.when(q < n_used)
    def _():
        slot = lax.rem(q, RING)
        ahead = lax.rem(q + 2, RING)
        behind = lax.rem(q + 1, RING)

        @pl.when(q == 0)
        def _():
            gbuf[...] = jnp.zeros(gbuf.shape, F32)
            start_gather(0, 0)

            @pl.when(n_used > 1)
            def _():
                start_gather(1, 1)

        wait_gather(q, slot)

        @pl.when(q >= RING)
        def _():
            wait_scatter(q - RING, slot)

        nxt = jnp.minimum(q + 2, n_used - 1)
        prev, has_prev = jnp.maximum(q - 1, 0), q >= 1
        part = tb // EXPERTS_PER_GROUP
        rows = gbuf[slot].reshape(tb, gbuf.shape[-1])
        x1, comb = rows[:, 0:d], rows[:, d:d + LANES]
        shift2, scale2, gate2 = mod_ref[:, 3 * d:4 * d], mod_ref[:, 4 * d:5 * d], mod_ref[:, 5 * d:6 * d]
        h2 = (_rmsnorm(x1, g2_ref[...]) * (1.0 + scale2) + shift2).astype(BF16)
        first_lane = N_GROUPS + grp_ref[q] * EXPERTS_PER_GROUP
        lane = _iota(comb.shape, 1)
        acc = jnp.zeros((tb, d), F32)
        for e in range(EXPERTS_PER_GROUP):
            w_e = jnp.sum(jnp.where(lane == first_lane + e, comb, 0.0), axis=1, keepdims=True)
            a = _mm(h2, w1_ref[e])
            b = _mm(h2, w3_ref[e])
            acc = acc + _mm(_silu(a) * b * w_e, w2_ref[e])
            gather_rows(nxt, ahead, e * part, (e + 1) * part)
            scatter_rows(prev, ahead, e * part, (e + 1) * part, has_prev)
        obuf[slot] = _rmsnorm(x1 + gate2 * acc, gf_ref[...]).reshape(tb // SUBLANES, SUBLANES, d)

        @pl.when(q == n_used - 1)
        def _():
            start_scatter(q, slot)
            wait_scatter(q, slot)
            wait_gather(q, ahead)

            @pl.when(q >= 1)
            def _():
                wait_scatter(q - 1, ahead)
                wait_gather(q, behind)

            @pl.when(q >= 2)
            def _():
                wait_scatter(q - 2, behind)


def _moe(xc, mod, p, tb):
    s, l, width = xc.shape
    d = D_MODEL
    gsel = xc[:, :, d + GROUP_LANE].astype(jnp.int32)
    row_src, tile_seg, tile_grp, n_used = _route(gsel, tb)
    max_tiles = tile_seg.shape[0]
    tile_rows = jnp.sum((row_src.reshape(max_tiles, tb) >= 0).astype(jnp.int32), axis=1)
    grouped = lambda w: w.reshape((N_GROUPS, EXPERTS_PER_GROUP) + w.shape[1:])
    w1, w3, w2 = grouped(p["moe_w1"]), grouped(p["moe_w3"]), grouped(p["moe_w2"])
    const = lambda shape: pl.BlockSpec(shape, lambda q, *_: (0,) * len(shape))
    by_group = lambda w: pl.BlockSpec((None,) + w.shape[1:],
                                      lambda q, gs, sd, seg, grp, used, cnt: (grp[q], 0, 0, 0))
    hbm = pl.BlockSpec(memory_space=pl.ANY)
    y, _ = pl.pallas_call(
        functools.partial(_moe_kernel, tb=tb),
        grid_spec=pltpu.PrefetchScalarGridSpec(
            num_scalar_prefetch=6,
            grid=(max_tiles,),
            in_specs=[hbm, pl.BlockSpec((None, 1, 6 * d), lambda q, gs, sd, seg, grp, used, cnt: (seg[q], 0, 0)),
                      const((1, d)), by_group(w1), by_group(w3), by_group(w2), const((1, d)), hbm],
            out_specs=[hbm, hbm],
            scratch_shapes=[pltpu.VMEM((RING, tb // SUBLANES, SUBLANES, width), F32),
                            pltpu.VMEM((RING, tb // SUBLANES, SUBLANES, d), F32),
                            pltpu.SemaphoreType.DMA((RING,)), pltpu.SemaphoreType.DMA((RING,))]),
        out_shape=[jax.ShapeDtypeStruct((s * l // SUBLANES, SUBLANES, d), F32),
                   jax.ShapeDtypeStruct((RING, tb // SUBLANES, SUBLANES, d), F32)],
        input_output_aliases={13: 1},
        compiler_params=_cparams(("arbitrary",)),
        name="moe",
    )(jnp.maximum(row_src, 0), row_src, tile_seg, tile_grp, n_used, tile_rows,
      xc.reshape(s * l // SUBLANES, SUBLANES, width), mod, p["norm2_g"], w1, w3, w2, p["final_norm_g"],
      jnp.zeros((RING, tb // SUBLANES, SUBLANES, d), F32))
    return y.reshape(s, l, d)


def _trunk(x, mod_seg, states, conv2d, p, tm, tb, mixer_seqs):
    b, t, d = x.shape
    n_seg = mod_seg.shape[0]
    seg = lambda a: a.reshape(n_seg, b * t // n_seg, a.shape[-1])
    per_seq = lambda a: a.reshape(b, t, a.shape[-1])
    xa, ub, o_pre = (per_seq(a) for a in _in_proj(seg(x), mod_seg, p["norm1_g"], p["w_in"], tm))
    if states is None:
        s_rwkv, mlstm_states = None, None
    else:
        s_rwkv, s_c, s_n, s_m = states
        mlstm_states = (s_c, s_n, jnp.broadcast_to(s_m[..., None], s_m.shape + (LANES,)))
    ya, s_rwkv = _rwkv(xa, s_rwkv, p, mixer_seqs)
    yb, s_c, s_n, s_m = _mlstm(ub, o_pre, mlstm_states, p, conv2d, mixer_seqs)
    xc = _merge(seg(x), seg(ya), seg(yb), mod_seg, p, tm)
    y = _moe(xc, mod_seg, p, tb)
    return y.reshape(b, t, d), s_rwkv, s_c, s_n, s_m[..., 0]


def kernel(x_prompt, x_sample, c, c_ctx, state_rwkv, state_mlstm_C, state_mlstm_n, state_mlstm_m, w_mod, b_mod, norm1_g, norm2_g, w_in, rwkv_mu, rwkv_w_r, rwkv_w_k, rwkv_w_v, rwkv_w0, rwkv_w1, rwkv_w2, rwkv_a0, rwkv_a1, rwkv_a2, rwkv_g1, rwkv_g2, rwkv_k_k, rwkv_k_a, rwkv_r_k, rwkv_ln_g, rwkv_ln_b, rwkv_w_proj, mlstm_conv, mlstm_w_q, mlstm_w_k, mlstm_w_v, mlstm_w_i, mlstm_b_i, mlstm_w_f, mlstm_b_f, mlstm_ln_g, mlstm_w_proj, w_out, moe_w_group, moe_b_group, moe_w_expert, moe_b_expert, moe_w1, moe_w3, moe_w2, final_norm_g):
    assert w_mod.shape[0] == 1, "single trunk layer"
    l = 0
    bp, dec = x_prompt.shape[0], x_sample.shape[0]
    bf = lambda w: w.astype(BF16)
    row = lambda w: w.reshape(1, -1).astype(F32)
    wg = jnp.concatenate([mlstm_w_i[l, 0], mlstm_w_i[l, 1], mlstm_w_f[l, 0], mlstm_w_f[l, 1]], axis=1)
    bg = jnp.concatenate([mlstm_b_i[l, 0], mlstm_b_i[l, 1], mlstm_b_f[l, 0], mlstm_b_f[l, 1]])
    wg = jnp.pad(wg, ((0, 0), (0, LANES - wg.shape[1])))
    bg = jnp.pad(bg, (0, LANES - bg.shape[0]))
    w_router = jnp.pad(jnp.concatenate([moe_w_group[l], moe_w_expert[l]], axis=1),
                       ((0, 0), (0, LANES - N_GROUPS - N_EXPERTS)))
    b_router = jnp.pad(jnp.concatenate([moe_b_group[l], moe_b_expert[l]]), (0, LANES - N_GROUPS - N_EXPERTS))
    p = {
        "norm1_g": row(norm1_g[l]), "norm2_g": row(norm2_g[l]), "w_in": w_in[l],
        "rwkv_mu": rwkv_mu[l], "rwkv_wr": bf(rwkv_w_r[l]), "rwkv_wk": bf(rwkv_w_k[l]), "rwkv_wv": bf(rwkv_w_v[l]),
        "rwkv_w0": rwkv_w0[l], "rwkv_w1": bf(rwkv_w1[l]), "rwkv_w2": bf(rwkv_w2[l]),
        "rwkv_a0": rwkv_a0[l], "rwkv_a1": bf(rwkv_a1[l]), "rwkv_a2": bf(rwkv_a2[l]),
        "rwkv_g1": bf(rwkv_g1[l]), "rwkv_g2": bf(rwkv_g2[l]),
        "rwkv_k_k": row(rwkv_k_k[l]), "rwkv_k_a": row(rwkv_k_a[l]), "rwkv_r_k": row(rwkv_r_k[l]),
        "rwkv_ln_g": row(rwkv_ln_g[l]), "rwkv_ln_b": row(rwkv_ln_b[l]),
        "rwkv_w_proj": rwkv_w_proj[l],
        "mlstm_conv": mlstm_conv[l].reshape(9, B_WIDTH),
        "mlstm_wq": bf(mlstm_w_q[l]), "mlstm_wk": bf(mlstm_w_k[l]), "mlstm_wv": bf(mlstm_w_v[l]),
        "mlstm_wg": bf(wg), "mlstm_bg": row(bg),
        "mlstm_wgt": bf(wg[:, :16].T), "mlstm_bgt": jnp.broadcast_to(bg[:16, None], (16, LANES)),
        "mlstm_ln_g": row(mlstm_ln_g[l]), "mlstm_w_proj": mlstm_w_proj[l], "w_out": w_out[l],
        "moe_w_router": bf(w_router), "moe_b_router": row(b_router),
        "moe_w1": moe_w1[l], "moe_w3": moe_w3[l], "moe_w2": moe_w2[l],
        "final_norm_g": row(final_norm_g),
    }
    cvec = jnp.concatenate([c_ctx[None, :], c, jnp.zeros((8 - 1 - dec, D_MODEL), F32)], axis=0)
    mod = _mod(cvec, w_mod[l], b_mod[l].reshape(1, -1))
    mod_ctx = mod[0:1][:, None, :]
    mod_lat = mod[1:1 + dec][:, None, :]

    yp, n_rwkv, n_c, n_n, n_m = _trunk(x_prompt, mod_ctx, None, False, p, 512, 512, 2)
    lat_states = tuple(s[:, l].astype(F32) for s in (state_rwkv, state_mlstm_C, state_mlstm_n, state_mlstm_m))
    ys, _, _, _, _ = _trunk(x_sample, mod_lat, lat_states, True, p, 512, 256, 1)
    dt = x_prompt.dtype
    return (yp, ys, n_rwkv[:, None].astype(dt), n_c[:, None].astype(dt), n_n[:, None].astype(dt),
            n_m[:, None].astype(dt))
```
